```python
import math
import jax, jax.numpy as jnp
from jax import lax
import numpy as np

D_MODEL = 1024
BATCH = 1
SEQ = 16384
DEPTH = 2
DEC_BATCH = 16
DEC_SEQ = 16
PAST_LEN = 4096

CHUNK = 64
N_EVEN = (DEPTH + 1) // 2
N_ODD = DEPTH // 2
EPS = 1e-6
NEG_INF = -1e30
GMLP_CHUNK = 128
H_A = 4
C_A = D_MODEL // 2 // H_A
W_A = H_A * C_A
W_B = D_MODEL // 2
CONV_W = 3
H_C = 8
DH = 64
W_C = H_C * DH
H_I = 8
D_I = 64
TOPK_MAX = 256
H_D = 4
W_D = H_D * 2 * DH
N_BUCKETS = 32
MAX_DIST = 128
QBLK = 128
D_FF = 2816
N_EXP = 8
TOP_K = 2
D_FF_EXP = 3584

IN0_SIZES = (W_A, W_A, W_B, W_B, W_B)
IN0_WIDTH = sum(IN0_SIZES)
IN1_SIZES = (W_C, W_C, W_C, W_D, W_D, W_D, H_I * D_I, D_I, H_I)
IN1_WIDTH = sum(IN1_SIZES)

kernel_name = "chunk_stream_hybrid_gmlp_conv_dsa_diff_step"


def split_cols(z, sizes):
    return jnp.split(z, [int(s) for s in np.cumsum(sizes)[:-1]], axis=-1)


def rmsnorm(x, g):
    xf = x.astype(jnp.float32)
    y = xf * lax.rsqrt(jnp.mean(xf * xf, axis=-1, keepdims=True) + EPS)
    return (y * g.astype(jnp.float32)).astype(x.dtype)


def layernorm(x, g, b):
    xf = x.astype(jnp.float32)
    mu = jnp.mean(xf, axis=-1, keepdims=True)
    xc = xf - mu
    var = jnp.mean(xc * xc, axis=-1, keepdims=True)
    return (xc * lax.rsqrt(var + EPS) * g.astype(jnp.float32) + b.astype(jnp.float32)).astype(x.dtype)


def chunk_ok(q_pos, k_pos):
    return (k_pos // CHUNK) <= (q_pos // CHUNK)


def t5_bucket(rel):
    half = N_BUCKETS // 2
    max_exact = half // 2
    ret = jnp.where(rel > 0, half, 0)
    n = jnp.abs(rel)
    nf = jnp.maximum(n, 1).astype(jnp.float32)
    large = max_exact + (jnp.log(nf / max_exact) / math.log(MAX_DIST / max_exact)
                         * (half - max_exact)).astype(jnp.int32)
    large = jnp.minimum(large, half - 1)
    return ret + jnp.where(n < max_exact, n, large)


def swiglu(x, wg, wu, wd):
    return (jax.nn.silu(x @ wg) * (x @ wu)) @ wd


def moe(x, router, wg, wu, wd):
    logits = (x @ router).astype(jnp.float32)
    top_v, top_i = lax.top_k(logits, TOP_K)
    g = jax.nn.softmax(top_v, axis=-1)
    gate = jnp.sum(jax.nn.one_hot(top_i, N_EXP, dtype=jnp.float32) * g[..., None], axis=-2)
    y = jnp.zeros_like(x)
    for e in range(N_EXP):
        y = y + gate[..., e:e + 1].astype(x.dtype) * swiglu(x, wg[e], wu[e], wd[e])
    return y


def mixer_ab(h, conv_hist, rows, w_in, ln_g, ln_b, w_s, b_s, conv_k, w_out):
    b, t, _ = h.shape
    u, v, gb, gc, xin = split_cols(h @ w_in, IN0_SIZES)
    u = jax.nn.gelu(u)
    v = layernorm(jax.nn.gelu(v).reshape(b, t, H_A, C_A), ln_g, ln_b)
    r = jnp.arange(rows)
    ws = jnp.where(chunk_ok(r[:, None], r[None, :])[None], w_s[:, :rows, :rows], 0.0)
    vc = v.reshape(b, t // rows, rows, H_A, C_A)
    s = jnp.einsum('hij,bnjhc->bnihc', ws.astype(v.dtype), vc) + b_s[:, :rows].T[None, None, :, :, None]
    y_a = u * s.reshape(b, t, W_A)
    w_ext = jnp.concatenate([conv_hist, gc * xin], axis=1)
    conv = (conv_k[0] * w_ext[:, 0:t] + conv_k[1] * w_ext[:, 1:t + 1]
            + conv_k[2] * w_ext[:, 2:t + 2])
    y_b = gb * conv
    out = jnp.concatenate([y_a, y_b], axis=-1) @ w_out
    return out, v.reshape(b, t, W_A), w_ext[:, -(CONV_W - 1):]


def project_cd(h, w_in):
    b, t, _ = h.shape
    qc, kc, vc, qd, kd, vd, qi, ki, wi = split_cols(h @ w_in, IN1_SIZES)
    return (qc.reshape(b, t, H_C, DH), kc.reshape(b, t, H_C, DH), vc.reshape(b, t, H_C, DH),
            qd.reshape(b, t, H_D, 2, DH), kd.reshape(b, t, H_D, 2, DH), vd.reshape(b, t, H_D, 2 * DH),
            qi.reshape(b, t, H_I, D_I), ki, wi)


def indexer_topk(q_i, w_i, k_i, q_pos, k_pos, topk):
    s = jax.nn.relu(jnp.einsum('bqhd,bld->bqhl', q_i, k_i).astype(jnp.float32))
    score = jnp.einsum('bqh,bqhl->bql', w_i.astype(jnp.float32), s)
    score = jnp.where(chunk_ok(q_pos[:, None], k_pos[None, :])[None], score, NEG_INF)
    _, idx = lax.top_k(score, topk)
    return idx


def dsa_attend(q, k_all, v_all, idx, q_pos, k_pos, bias_tab):
    kg = jax.vmap(lambda kb, ib: kb[ib])(k_all, idx)
    vg = jax.vmap(lambda vb, ib: vb[ib])(v_all, idx)
    kp = k_pos[idx]
    logits = jnp.einsum('bqhd,bqkhd->bhqk', q, kg).astype(jnp.float32) * DH ** -0.5
    bias = bias_tab[t5_bucket(kp - q_pos[None, :, None])].astype(jnp.float32)
    logits = logits + jnp.transpose(bias, (0, 3, 1, 2))
    ok = chunk_ok(q_pos[None, :, None], kp)
    logits = jnp.where(ok[:, None], logits, NEG_INF)
    p = jax.nn.softmax(logits, axis=-1)
    return jnp.einsum('bhqk,bqkhd->bqhd', p.astype(vg.dtype), vg)


def diff_attend(q, k, v, q_pos, k_pos, bias_tab, lam, lam_init, subln_g):
    logits = jnp.einsum('bqhid,blhid->bihql', q, k).astype(jnp.float32) * DH ** -0.5
    bias = bias_tab[t5_bucket(k_pos[None, :] - q_pos[:, None])].astype(jnp.float32)
    logits = logits + jnp.transpose(bias, (2, 0, 1))[None, None]
    logits = jnp.where(chunk_ok(q_pos[:, None], k_pos[None, :]), logits, NEG_INF)
    p = jax.nn.softmax(logits, axis=-1)
    a = p[:, 0] - lam * p[:, 1]
    o = jnp.einsum('bhql,blhe->bqhe', a.astype(v.dtype), v)
    return rmsnorm(o, subln_g) * (1.0 - lam_init)


def attend_block(q_c, q_d, q_i, w_i, q_pos, k_c, v_c, k_d, v_d, k_i, k_pos, topk,
                 rel_bias, lam, lam_init, subln_g):
    b, nq = q_c.shape[:2]
    idx = indexer_topk(q_i, w_i, k_i, q_pos, k_pos, topk)
    o_c = dsa_attend(q_c, k_c, v_c, idx, q_pos, k_pos, rel_bias[:, :H_C])
    o_d = diff_attend(q_d, k_d, v_d, q_pos, k_pos, rel_bias[:, H_C:], lam, lam_init, subln_g)
    return jnp.concatenate([o_c.reshape(b, nq, W_C), o_d.reshape(b, nq, W_D)], axis=-1)


def to_blocks(a):
    b, t = a.shape[:2]
    return jnp.moveaxis(a.reshape((b, t // QBLK, QBLK) + a.shape[2:]), 1, 0)


def diff_lambda(lam_qk, lam_init):
    lf = lam_qk.astype(jnp.float32)
    return jnp.exp(jnp.sum(lf[0] * lf[1])) - jnp.exp(jnp.sum(lf[2] * lf[3])) + lam_init


def mixer_cd_prompt(h, w_in, w_out, rel_bias, lam, lam_init, subln_g):
    b, t, _ = h.shape
    qc, kc, vc, qd, kd, vd, qi, ki, wi = project_cd(h, w_in)
    pos = jnp.arange(t)
    topk = min(TOPK_MAX, t // 4)

    def block(args):
        qc_b, qd_b, qi_b, wi_b, pos_b = args
        return attend_block(qc_b, qd_b, qi_b, wi_b, pos_b, kc, vc, kd, vd, ki, pos, topk,
                            rel_bias, lam, lam_init, subln_g)

    o = lax.map(block, (to_blocks(qc), to_blocks(qd), to_blocks(qi), to_blocks(wi),
                        pos.reshape(t // QBLK, QBLK)))
    o = jnp.moveaxis(o, 0, 1).reshape(b, t, W_C + W_D)
    return o @ w_out, kc, vc, ki, kd, vd


def mixer_cd_sample(h, ck, cv, cik, cdk, cdv, w_in, w_out, rel_bias, lam, lam_init, subln_g):
    b, t, _ = h.shape
    past = ck.shape[1]
    n_keys = past + t
    qc, kc, vc, qd, kd, vd, qi, ki, wi = project_cd(h, w_in)
    cat = lambda old, new: jnp.concatenate([old, new], axis=1)
    o = attend_block(qc, qd, qi, wi, past + jnp.arange(t), cat(ck, kc), cat(cv, vc),
                     cat(cdk, kd), cat(cdv, vd), cat(cik, ki), jnp.arange(n_keys),
                     min(TOPK_MAX, n_keys // 4), rel_bias, lam, lam_init, subln_g)
    return o @ w_out, kc, vc, ki, kd, vd


def setup_inputs(seed: int = 0) -> dict:
    key = jax.random.key(seed)
    keys = list(jax.random.split(key, 40))

    def nrm(shape, scale):
        return jax.random.normal(keys.pop(), shape, jnp.float32) * scale

    def gain(shape):
        return 1.0 + nrm(shape, 0.05)

    d = D_MODEL
    return {
        "x_prompt": nrm((BATCH, SEQ, d), 1.0),
        "x_sample": nrm((DEC_BATCH, DEC_SEQ, d), 1.0),
        "state_b_conv": nrm((N_EVEN, DEC_BATCH, CONV_W - 1, W_B), 1.0),
        "cache_c_k": nrm((N_ODD, DEC_BATCH, PAST_LEN, H_C, DH), 1.0),
        "cache_c_v": nrm((N_ODD, DEC_BATCH, PAST_LEN, H_C, DH), 1.0),
        "cache_idx_k": nrm((N_ODD, DEC_BATCH, PAST_LEN, D_I), 1.0),
        "cache_d_k": nrm((N_ODD, DEC_BATCH, PAST_LEN, H_D, 2, DH), 1.0),
        "cache_d_v": nrm((N_ODD, DEC_BATCH, PAST_LEN, H_D, 2 * DH), 1.0),
        "rel_bias": nrm((N_BUCKETS, H_C + H_D), 0.3),
        "ln_mix": gain((DEPTH, d)),
        "ln_ffn": gain((DEPTH, d)),
        "ln_final": gain((d,)),
        "w_in0": nrm((N_EVEN, d, IN0_WIDTH), d ** -0.5),
        "gmlp_ln_g": gain((N_EVEN, H_A, C_A)),
        "gmlp_ln_b": nrm((N_EVEN, H_A, C_A), 0.02),
        "gmlp_ws": nrm((N_EVEN, H_A, GMLP_CHUNK, GMLP_CHUNK), GMLP_CHUNK ** -0.5),
        "gmlp_bs": gain((N_EVEN, H_A, GMLP_CHUNK)),
        "conv_k": nrm((N_EVEN, CONV_W, W_B), CONV_W ** -0.5),
        "w_out0": nrm((N_EVEN, W_A + W_B, d), (W_A + W_B) ** -0.5),
        "ffn_wg": nrm((N_EVEN, d, D_FF), d ** -0.5),
        "ffn_wu": nrm((N_EVEN, d, D_FF), d ** -0.5),
        "ffn_wd": nrm((N_EVEN, D_FF, d), D_FF ** -0.5),
        "w_in1": nrm((N_ODD, d, IN1_WIDTH), d ** -0.5),
        "lam_qk": nrm((N_ODD, 4, DH), 0.1),
        "subln_g": gain((N_ODD, 2 * DH)),
        "w_out1": nrm((N_ODD, W_C + W_D, d), (W_C + W_D) ** -0.5),
        "router": nrm((N_ODD, d, N_EXP), d ** -0.5),
        "exp_wg": nrm((N_ODD, N_EXP, d, D_FF_EXP), d ** -0.5),
        "exp_wu": nrm((N_ODD, N_EXP, d, D_FF_EXP), d ** -0.5),
        "exp_wd": nrm((N_ODD, N_EXP, D_FF_EXP, d), D_FF_EXP ** -0.5),
    }


def reference(x_prompt, x_sample, state_b_conv, cache_c_k, cache_c_v, cache_idx_k, cache_d_k,
              cache_d_v, rel_bias, ln_mix, ln_ffn, ln_final, w_in0, gmlp_ln_g, gmlp_ln_b, gmlp_ws,
              gmlp_bs, conv_k, w_out0, ffn_wg, ffn_wu, ffn_wd, w_in1, lam_qk, subln_g, w_out1,
              router, exp_wg, exp_wu, exp_wd):
    xp, xs = x_prompt, x_sample
    bp = xp.shape[0]
    ts = xs.shape[1]
    p_conv, s_av, s_conv = [], [], []
    p_ck, p_cv, p_ik, p_dk, p_dv = [], [], [], [], []
    s_ck, s_cv, s_ik, s_dk, s_dv = [], [], [], [], []
    for l in range(DEPTH):
        j = l // 2
        hp = rmsnorm(xp, ln_mix[l])
        hs = rmsnorm(xs, ln_mix[l])
        if l % 2 == 0:
            wts = (w_in0[j], gmlp_ln_g[j], gmlp_ln_b[j], gmlp_ws[j], gmlp_bs[j], conv_k[j], w_out0[j])
            zero_hist = jnp.zeros((bp, CONV_W - 1, W_B), hp.dtype)
            op, _, cp = mixer_ab(hp, zero_hist, GMLP_CHUNK, *wts)
            os_, av, cs = mixer_ab(hs, state_b_conv[j], ts, *wts)
            p_conv.append(cp)
            s_av.append(av)
            s_conv.append(cs)
            xp = xp + op
            xs = xs + os_
            xp = xp + swiglu(rmsnorm(xp, ln_ffn[l]), ffn_wg[j], ffn_wu[j], ffn_wd[j])
            xs = xs + swiglu(rmsnorm(xs, ln_ffn[l]), ffn_wg[j], ffn_wu[j], ffn_wd[j])
        else:
            lam_init = 0.8 - 0.6 * math.exp(-0.3 * l)
            lam = diff_lambda(lam_qk[j], lam_init)
            op, kc, vc, ki, kd, vd = mixer_cd_prompt(hp, w_in1[j], w_out1[j], rel_bias, lam,
                                                     lam_init, subln_g[j])
            p_ck.append(kc); p_cv.append(vc); p_ik.append(ki); p_dk.append(kd); p_dv.append(vd)
            os_, kc, vc, ki, kd, vd = mixer_cd_sample(hs, cache_c_k[j], cache_c_v[j], cache_idx_k[j],
                                                      cache_d_k[j], cache_d_v[j], w_in1[j], w_out1[j],
                                                      rel_bias, lam, lam_init, subln_g[j])
            s_ck.append(kc); s_cv.append(vc); s_ik.append(ki); s_dk.append(kd); s_dv.append(vd)
            xp = xp + op
            xs = xs + os_
            xp = xp + moe(rmsnorm(xp, ln_ffn[l]), router[j], exp_wg[j], exp_wu[j], exp_wd[j])
            xs = xs + moe(rmsnorm(xs, ln_ffn[l]), router[j], exp_wg[j], exp_wu[j], exp_wd[j])
    y_prompt = rmsnorm(xp, ln_final)
    y_sample = rmsnorm(xs, ln_final)
    st = jnp.stack
    return (y_prompt, y_sample,
            st(p_conv), st(p_ck), st(p_cv), st(p_ik), st(p_dk), st(p_dv),
            st(s_av), st(s_conv), st(s_ck), st(s_cv), st(s_ik), st(s_dk), st(s_dv))
```

```python
import functools
import math

import jax
import jax.numpy as jnp
import numpy as np
from jax import lax
from jax.experimental import pallas as pl
from jax.experimental.pallas import tpu as pltpu

F32 = jnp.float32
BF16 = jnp.bfloat16
I32 = jnp.int32

CHUNK = 64
EPS = 1e-6
NEG_INF = -1e30
H_A = 4
C_A = 128
W_A = H_A * C_A
W_B = 512
H_C = 8
DH = 64
W_C = H_C * DH
H_I = 8
D_I = 64
TOPK_MAX = 256
H_D = 4
W_D = H_D * 2 * DH
N_BUCKETS = 32
MAX_DIST = 128
N_EXP = 8
N_HEADS_BIAS = H_C + H_D
N_MAPS = H_C + 2 * H_D

LANES = 128
SUBLANES = 8
VMEM_LIMIT = 56 * 1024 * 1024

INT_MIN = -(2 ** 31)
INT_MAX = 2 ** 31 - 1
ALL_TIES = 2 ** 30

KEY_BLOCK = 512


def _cparams(*sem):
    return pltpu.CompilerParams(dimension_semantics=sem, vmem_limit_bytes=VMEM_LIMIT)


def _largest_divisor(n, target, mult):
    if n <= target:
        return n
    d = (target // mult) * mult
    while d >= mult:
        if n % d == 0:
            return d
        d -= mult
    raise ValueError(f"no block of multiple {mult} divides {n}")


def _rms(x, g):
    return x * lax.rsqrt(jnp.mean(x * x, axis=-1, keepdims=True) + EPS) * g


def _dot(a, b):
    return jnp.dot(a, b, preferred_element_type=F32)


def _dot_nt(a, b):
    return lax.dot_general(a, b, (((1,), (1,)), ((), ())), preferred_element_type=F32)


def _rms_proj_body(x_ref, g_ref, w_ref, *out_refs, plan):
    xn = _rms(x_ref[...], g_ref[...]).astype(BF16)
    for (c0, c1), writes in plan:
        z = _dot(xn, w_ref[:, c0:c1])
        for o_idx, head, z0, z1, scale in writes:
            val = z[:, z0:z1]
            if scale != 1.0:
                val = val * scale
            ref = out_refs[o_idx]
            if head is None:
                ref[...] = val.astype(ref.dtype)
            else:
                ref[head] = val.astype(ref.dtype)


def _rms_proj(x, g, w, plan, out_defs, rows):
    n, d = x.shape
    grid = (n // rows,)
    out_shape, out_specs = [], []
    for width, dtype, heads in out_defs:
        if heads is None:
            out_shape.append(jax.ShapeDtypeStruct((n, width), dtype))
            out_specs.append(pl.BlockSpec((rows, width), lambda i: (i, 0)))
        else:
            out_shape.append(jax.ShapeDtypeStruct((heads, n, width), dtype))
            out_specs.append(pl.BlockSpec((heads, rows, width), lambda i: (0, i, 0)))
    return pl.pallas_call(
        functools.partial(_rms_proj_body, plan=plan),
        grid=grid,
        in_specs=[pl.BlockSpec((rows, d), lambda i: (i, 0)),
                  pl.BlockSpec((1, d), lambda i: (0, 0)),
                  pl.BlockSpec(w.shape, lambda i: (0, 0))],
        out_specs=out_specs,
        out_shape=out_shape,
        compiler_params=_cparams("parallel"),
        name="rms_proj",
    )(x, g, w)


def _mixer_ab_body(*refs, rb, from_prev, emit_v):
    it = iter(refs)
    x_ref, u_ref, v_ref, gb_ref, gc_ref, xin_ref = (next(it) for _ in range(6))
    if from_prev:
        gcp_ref, xinp_ref = next(it), next(it)
    else:
        hist_ref = next(it)
    lng_ref, lnb_ref, ws_ref, bs_ref, ck_ref, wout_ref = (next(it) for _ in range(6))
    x1_ref, tail_ref = next(it), next(it)
    vout_ref = next(it) if emit_v else None
    wext_ref = next(it)

    w = gc_ref[...] * xin_ref[...]
    if from_prev:
        hist = jnp.where(pl.program_id(0) > 0, gcp_ref[...] * xinp_ref[...], 0.0)
    else:
        hist = hist_ref[0]
    wext_ref[0:SUBLANES, :] = hist
    wext_ref[SUBLANES:, :] = w
    ck = ck_ref[...]
    conv = (ck[0:1] * wext_ref[SUBLANES - 2:SUBLANES - 2 + rb, :]
            + ck[1:2] * wext_ref[SUBLANES - 1:SUBLANES - 1 + rb, :]
            + ck[2:3] * w)
    y_b = gb_ref[...] * conv
    tail_ref[0] = w[rb - SUBLANES:, :]

    u = jax.nn.gelu(u_ref[...])
    v = jax.nn.gelu(v_ref[...])
    lng = lng_ref[...]
    lnb = lnb_ref[...]
    bs = bs_ref[...]
    acc = _dot(y_b.astype(BF16), wout_ref[W_A:, :])
    for h in range(H_A):
        sl = slice(h * C_A, (h + 1) * C_A)
        vh = v[:, sl]
        mu = jnp.mean(vh, axis=-1, keepdims=True)
        xc = vh - mu
        var = jnp.mean(xc * xc, axis=-1, keepdims=True)
        vln = xc * lax.rsqrt(var + EPS) * lng[:, sl] + lnb[:, sl]
        if emit_v:
            vout_ref[:, sl] = vln
        s = _dot(ws_ref[h], vln.astype(BF16)) + bs[:, h:h + 1]
        y_a = u[:, sl] * s
        acc = acc + _dot(y_a.astype(BF16), wout_ref[sl, :])
    x1_ref[...] = x_ref[...] + acc


def _mixer_ab(x, z, hist, lng, lnb, ws, bs, ck, wout, rb, emit_v):
    n, d = x.shape
    nb = n // rb
    from_prev = hist is None
    col = lambda c: pl.BlockSpec((rb, 512), lambda i, c=c: (i, c))
    in_specs = [pl.BlockSpec((rb, d), lambda i: (i, 0)), col(0), col(1), col(2), col(3), col(4)]
    args = [x, z, z, z, z, z]
    if from_prev:
        per = rb // SUBLANES
        prev = lambda c: pl.BlockSpec((SUBLANES, 512), lambda i, c=c: (jnp.maximum(i * per - 1, 0), c))
        in_specs += [prev(3), prev(4)]
        args += [z, z]
    else:
        in_specs += [pl.BlockSpec((1, SUBLANES, 512), lambda i: (i, 0, 0))]
        args += [hist]
    const = lambda a: pl.BlockSpec(a.shape, lambda i, nd=a.ndim: (0,) * nd)
    for a in (lng, lnb, ws, bs, ck, wout):
        in_specs.append(const(a))
        args.append(a)
    out_shape = [jax.ShapeDtypeStruct((n, d), F32), jax.ShapeDtypeStruct((nb, SUBLANES, 512), F32)]
    out_specs = [pl.BlockSpec((rb, d), lambda i: (i, 0)), pl.BlockSpec((1, SUBLANES, 512), lambda i: (i, 0, 0))]
    if emit_v:
        out_shape.append(jax.ShapeDtypeStruct((n, W_A), F32))
        out_specs.append(pl.BlockSpec((rb, W_A), lambda i: (i, 0)))
    return pl.pallas_call(
        functools.partial(_mixer_ab_body, rb=rb, from_prev=from_prev, emit_v=emit_v),
        grid=(nb,),
        in_specs=in_specs,
        out_specs=out_specs,
        out_shape=out_shape,
        scratch_shapes=[pltpu.VMEM((rb + SUBLANES, 512), F32)],
        compiler_params=_cparams("arbitrary"),
        name="mixer_ab",
    )(*args)


def _ffn_body(x_ref, g_ref, wg_ref, wu_ref, wd_ref, o_ref, xn_ref, acc_ref):
    k = pl.program_id(1)

    @pl.when(k == 0)
    def _():
        xn_ref[...] = _rms(x_ref[...], g_ref[...]).astype(BF16)
        acc_ref[...] = jnp.zeros_like(acc_ref)

    xn = xn_ref[...]
    h = jax.nn.silu(_dot(xn, wg_ref[...])) * _dot(xn, wu_ref[...])
    acc_ref[...] += _dot(h.astype(BF16), wd_ref[...])

    @pl.when(k == pl.num_programs(1) - 1)
    def _():
        o_ref[...] = x_ref[...] + acc_ref[...]


def _ffn(x, g, wg, wu, wd, rows, fb):
    n, d = x.shape
    dff = wg.shape[1]
    return pl.pallas_call(
        _ffn_body,
        grid=(n // rows, dff // fb),
        in_specs=[pl.BlockSpec((rows, d), lambda i, k: (i, 0)),
                  pl.BlockSpec((1, d), lambda i, k: (0, 0)),
                  pl.BlockSpec((d, fb), lambda i, k: (0, k)),
                  pl.BlockSpec((d, fb), lambda i, k: (0, k)),
                  pl.BlockSpec((fb, d), lambda i, k: (k, 0))],
        out_specs=pl.BlockSpec((rows, d), lambda i, k: (i, 0)),
        out_shape=jax.ShapeDtypeStruct((n, d), F32),
        scratch_shapes=[pltpu.VMEM((rows, d), BF16), pltpu.VMEM((rows, d), F32)],
        compiler_params=_cparams("parallel", "arbitrary"),
        name="ffn",
    )(x, g, wg, wu, wd)


def _select_body(qi_ref, wi_ref, ve_ref, ki_ref, out_ref, keys_ref, *, tq, lb, nkb, topk, nvalid_fn):
    nv = nvalid_fn(pl.program_id(1))
    wi = wi_ref[...]
    ve = ve_ref[...]
    qs = [qi_ref[h] for h in range(H_I)]
    wcols = [wi[:, h:h + 1] for h in range(H_I)]
    lane = lax.broadcasted_iota(I32, (tq, lb), 1)

    def score_block(b, carry):
        kb = ki_ref[0, pl.ds(pl.multiple_of(b * lb, lb), lb), :]
        sc = jnp.zeros((tq, lb), F32)
        for h in range(H_I):
            sc = sc + wcols[h] * jnp.maximum(_dot_nt(qs[h], kb), 0.0)
        bits = lax.bitcast_convert_type(sc, I32)
        key = bits ^ ((bits >> 31) & INT_MAX)
        key = jnp.where(sc == 0.0, 0, key)
        key = jnp.where(lane + b * lb < ve, key, INT_MIN)
        keys_ref[b] = key
        return carry

    lax.fori_loop(0, nv, score_block, 0)

    def count_ge(mid):
        midb = jnp.broadcast_to(mid, (tq, LANES))

        def body(b, acc):
            for c in range(lb // LANES):
                k = keys_ref[b, :, c * LANES:(c + 1) * LANES]
                acc = acc + jnp.where(k >= midb, 1, 0)
            return acc

        acc = lax.fori_loop(0, nv, body, jnp.zeros((tq, LANES), I32))
        return jnp.sum(acc.astype(F32), axis=-1, keepdims=True).astype(I32)

    def bisect(_, st):
        lo, hi, clo, chi = st
        mid = (lo >> 1) + (hi >> 1) + (lo & hi & 1)
        cnt = count_ge(mid)
        active = mid != lo
        up = jnp.logical_and(active, cnt >= topk)
        dn = jnp.logical_and(active, cnt < topk)
        return (jnp.where(up, mid, lo), jnp.where(dn, mid, hi),
                jnp.where(up, cnt, clo), jnp.where(dn, cnt, chi))

    full = lambda v: jnp.full((tq, 1), v, I32)
    lo, hi, clo, chi = lax.fori_loop(0, 32, bisect, (full(INT_MIN), full(INT_MAX), ve, full(0)))

    need = jnp.where(clo > topk, topk - chi, ALL_TIES)
    need = jnp.where(lo == INT_MIN, 0, need).astype(F32)
    lob = jnp.broadcast_to(lo, (tq, LANES))
    needb = jnp.broadcast_to(need, (tq, LANES))
    r = lax.broadcasted_iota(I32, (LANES, LANES), 0)
    c = lax.broadcasted_iota(I32, (LANES, LANES), 1)
    tri = jnp.where(r <= c, 1.0, 0.0).astype(BF16)

    def mask_block(b, seen):
        for cc in range(lb // LANES):
            k = keys_ref[b, :, cc * LANES:(cc + 1) * LANES]
            eq = jnp.where(k == lob, 1.0, 0.0)
            cum = _dot(eq.astype(BF16), tri)
            rank = seen + cum - eq
            take = jnp.where(rank < needb, eq, 0.0)
            sel = jnp.where(k > lob, 1.0, take)
            out_ref[0, b, :, cc * LANES:(cc + 1) * LANES] = jnp.where(sel > 0.5, 0.0, NEG_INF).astype(out_ref.dtype)
            seen = seen + cum[:, LANES - 1:LANES]
        return seen

    lax.fori_loop(0, nv, mask_block, jnp.zeros((tq, 1), F32))

    def fill_block(b, carry):
        out_ref[0, b] = jnp.full((tq, lb), NEG_INF, out_ref.dtype)
        return carry

    lax.fori_loop(nv, nkb, fill_block, 0)


def _select(qi, wi, ve, ki, nbatch, t, tq, topk, nvalid_fn):
    lp = ki.shape[1]
    lb = KEY_BLOCK
    nkb = lp // lb
    nq = t // tq
    row = lambda b, i: b * nq + i
    return pl.pallas_call(
        functools.partial(_select_body, tq=tq, lb=lb, nkb=nkb, topk=topk, nvalid_fn=nvalid_fn),
        grid=(nbatch, nq),
        in_specs=[pl.BlockSpec((H_I, tq, D_I), lambda b, i: (0, row(b, i), 0)),
                  pl.BlockSpec((tq, H_I), lambda b, i: (row(b, i), 0)),
                  pl.BlockSpec((tq, 1), lambda b, i: (row(b, i), 0)),
                  pl.BlockSpec((1, lp, D_I), lambda b, i: (b, 0, 0))],
        out_specs=pl.BlockSpec((1, nkb, tq, lb), lambda b, i: (b, 0, i, 0)),
        out_shape=jax.ShapeDtypeStruct((nbatch, nkb, t, lb), BF16),
        scratch_shapes=[pltpu.VMEM((nkb, tq, lb), I32)],
        compiler_params=_cparams("parallel", "arbitrary"),
        name="index_select",
    )(qi, wi, ve, ki)


def _attn_body(qc_ref, qd_ref, kc_ref, vc_ref, kd_ref, vd_ref, mask_ref, bt_ref, lam_ref, g_ref,
               o_ref, m_ref, l_ref, accc_ref, accd_ref, *, tq, nvalid_fn, lam_init):
    i = pl.program_id(1)
    j = pl.program_id(2)

    @pl.when(j == 0)
    def _():
        m_ref[...] = jnp.full(m_ref.shape, NEG_INF, F32)
        l_ref[...] = jnp.zeros_like(l_ref)
        accc_ref[...] = jnp.zeros_like(accc_ref)
        accd_ref[...] = jnp.zeros_like(accd_ref)

    @pl.when(j < nvalid_fn(i))
    def _():
        sel = mask_ref[0, 0].astype(F32)
        for mp in range(N_MAPS):
            sparse = mp < H_C
            if sparse:
                s = _dot_nt(qc_ref[mp], kc_ref[0, mp]) + bt_ref[0, mp] + sel
                v = vc_ref[0, mp]
                acc_ref, a_idx, dv = accc_ref, mp, DH
            else:
                dm = mp - H_C
                s = _dot_nt(qd_ref[dm], kd_ref[0, dm]) + bt_ref[0, H_C + dm // 2]
                v = vd_ref[0, dm // 2]
                acc_ref, a_idx, dv = accd_ref, dm, 2 * DH
            m_prev = m_ref[mp]
            m_new = jnp.maximum(m_prev, jnp.max(s, axis=-1, keepdims=True))
            alpha = jnp.exp(m_prev - m_new)
            p = jnp.exp(s - m_new[:, 0:1])
            l_ref[mp] = alpha * l_ref[mp] + jnp.sum(p, axis=-1, keepdims=True)
            m_ref[mp] = m_new
            acc_ref[a_idx] = alpha[:, 0:dv] * acc_ref[a_idx] + _dot(p.astype(BF16), v)

    @pl.when(j == pl.num_programs(2) - 1)
    def _():
        for h in range(H_C):
            o_ref[:, h * DH:(h + 1) * DH] = (accc_ref[h] / l_ref[h][:, 0:DH]).astype(o_ref.dtype)
        lam = lam_ref[...]
        g = g_ref[...]
        for h in range(H_D):
            a0 = accd_ref[2 * h] / l_ref[H_C + 2 * h]
            a1 = accd_ref[2 * h + 1] / l_ref[H_C + 2 * h + 1]
            od = _rms(a0 - lam * a1, g) * (1.0 - lam_init)
            o_ref[:, W_C + h * 2 * DH:W_C + (h + 1) * 2 * DH] = od.astype(o_ref.dtype)


def _attend(qc, qd, kc, vc, kd, vd, mask, btiles, lam, g, nbatch, t, tq, nvalid_fn, tile_fn, lam_init):
    lp = kc.shape[2]
    lb = KEY_BLOCK
    nkb = lp // lb
    nq = t // tq
    row = lambda b, i: b * nq + i
    kblk = lambda i, j: jnp.minimum(j, nvalid_fn(i) - 1)
    qspec = pl.BlockSpec((H_C, tq, DH), lambda b, i, j: (0, row(b, i), 0))
    kspec = lambda nh, w: pl.BlockSpec((1, nh, lb, w), lambda b, i, j: (b, 0, kblk(i, j), 0))
    return pl.pallas_call(
        functools.partial(_attn_body, tq=tq, nvalid_fn=nvalid_fn, lam_init=lam_init),
        grid=(nbatch, nq, nkb),
        in_specs=[qspec, qspec, kspec(H_C, DH), kspec(H_C, DH), kspec(2 * H_D, DH), kspec(H_D, 2 * DH),
                  pl.BlockSpec((1, 1, tq, lb), lambda b, i, j: (b, kblk(i, j), i, 0)),
                  pl.BlockSpec((1, N_HEADS_BIAS, tq, lb), lambda b, i, j: (tile_fn(i, kblk(i, j)), 0, 0, 0)),
                  pl.BlockSpec((1, 1), lambda b, i, j: (0, 0)),
                  pl.BlockSpec((1, 2 * DH), lambda b, i, j: (0, 0))],
        out_specs=pl.BlockSpec((tq, W_C + W_D), lambda b, i, j: (row(b, i), 0)),
        out_shape=jax.ShapeDtypeStruct((nbatch * t, W_C + W_D), BF16),
        scratch_shapes=[pltpu.VMEM((N_MAPS, tq, LANES), F32), pltpu.VMEM((N_MAPS, tq, LANES), F32),
                        pltpu.VMEM((H_C, tq, DH), F32), pltpu.VMEM((2 * H_D, tq, 2 * DH), F32)],
        compiler_params=_cparams("parallel", "parallel", "arbitrary"),
        name="attend",
    )(qc, qd, kc, vc, kd, vd, mask, btiles, lam, g)


def _out_router_body(x_ref, o_ref, w_ref, g_ref, r_ref, x3_ref, xn_ref, gate_ref):
    x3 = x_ref[...] + _dot(o_ref[...], w_ref[...])
    x3_ref[...] = x3
    xn = _rms(x3, g_ref[...]).astype(BF16)
    xn_ref[...] = xn
    logits = _dot(xn, r_ref[...])
    lane = lax.broadcasted_iota(I32, logits.shape, 1)
    logits = jnp.where(lane < N_EXP, logits, -jnp.inf)
    m1 = jnp.max(logits, axis=-1, keepdims=True)
    i1 = jnp.min(jnp.where(logits == m1, lane, LANES), axis=-1, keepdims=True)
    rest = jnp.where(lane == i1, -jnp.inf, logits)
    m2 = jnp.max(rest, axis=-1, keepdims=True)
    i2 = jnp.min(jnp.where(rest == m2, lane, LANES), axis=-1, keepdims=True)
    e = jnp.exp(m2 - m1)
    g1 = 1.0 / (1.0 + e)
    g2 = e / (1.0 + e)
    gate_ref[...] = jnp.where(lane == i1, g1, 0.0) + jnp.where(lane == i2, g2, 0.0)


def _out_router(x, o, w, g, router, rows):
    n, d = x.shape
    return pl.pallas_call(
        _out_router_body,
        grid=(n // rows,),
        in_specs=[pl.BlockSpec((rows, d), lambda i: (i, 0)),
                  pl.BlockSpec((rows, d), lambda i: (i, 0)),
                  pl.BlockSpec(w.shape, lambda i: (0, 0)),
                  pl.BlockSpec((1, d), lambda i: (0, 0)),
                  pl.BlockSpec(router.shape, lambda i: (0, 0))],
        out_specs=[pl.BlockSpec((rows, d), lambda i: (i, 0)),
                   pl.BlockSpec((rows, d), lambda i: (i, 0)),
                   pl.BlockSpec((rows, LANES), lambda i: (i, 0))],
        out_shape=[jax.ShapeDtypeStruct((n, d), F32), jax.ShapeDtypeStruct((n, d), BF16),
                   jax.ShapeDtypeStruct((n, LANES), F32)],
        compiler_params=_cparams("parallel"),
        name="out_router",
    )(x, o, w, g, router)


def _moe_body(x_ref, xn_ref, gate_ref, gf_ref, wg_ref, wu_ref, wd_ref, y_ref, acc_ref):
    e = pl.program_id(1)
    k = pl.program_id(2)

    @pl.when(jnp.logical_and(e == 0, k == 0))
    def _():
        acc_ref[...] = jnp.zeros_like(acc_ref)

    gate = gate_ref[...]
    lane = lax.broadcasted_iota(I32, gate.shape, 1)
    ge = jnp.sum(jnp.where(lane == e, gate, 0.0), axis=-1, keepdims=True)
    xn = xn_ref[...]
    h = jax.nn.silu(_dot(xn, wg_ref[0])) * _dot(xn, wu_ref[0])
    acc_ref[...] += _dot((ge * h).astype(BF16), wd_ref[0])

    @pl.when(jnp.logical_and(e == pl.num_programs(1) - 1, k == pl.num_programs(2) - 1))
    def _():
        y_ref[...] = _rms(x_ref[...] + acc_ref[...], gf_ref[...])


def _moe(x, xn, gate, gf, wg, wu, wd, rows, fb):
    n, d = x.shape
    ne, _, dff = wg.shape
    return pl.pallas_call(
        _moe_body,
        grid=(n // rows, ne, dff // fb),
        in_specs=[pl.BlockSpec((rows, d), lambda i, e, k: (i, 0)),
                  pl.BlockSpec((rows, d), lambda i, e, k: (i, 0)),
                  pl.BlockSpec((rows, LANES), lambda i, e, k: (i, 0)),
                  pl.BlockSpec((1, d), lambda i, e, k: (0, 0)),
                  pl.BlockSpec((1, d, fb), lambda i, e, k: (e, 0, k)),
                  pl.BlockSpec((1, d, fb), lambda i, e, k: (e, 0, k)),
                  pl.BlockSpec((1, fb, d), lambda i, e, k: (e, k, 0))],
        out_specs=pl.BlockSpec((rows, d), lambda i, e, k: (i, 0)),
        out_shape=jax.ShapeDtypeStruct((n, d), F32),
        scratch_shapes=[pltpu.VMEM((rows, d), F32)],
        compiler_params=_cparams("parallel", "arbitrary", "arbitrary"),
        name="moe",
    )(x, xn, gate, gf, wg, wu, wd)


def _t5_bucket(rel):
    half = N_BUCKETS // 2
    max_exact = half // 2
    ret = np.where(rel > 0, half, 0)
    n = np.abs(rel)
    nf = np.maximum(n, 1).astype(np.float32)
    large = max_exact + (np.log(nf / np.float32(max_exact)) / np.float32(math.log(MAX_DIST / max_exact))
                         * np.float32(half - max_exact)).astype(np.int32)
    large = np.minimum(large, half - 1)
    return (ret + np.where(n < max_exact, n, large)).astype(np.int32)


def _bias_tile(rel_bias, q_pos, k_pos, k_real):
    rel = k_pos[None, :] - q_pos[:, None]
    bias = jnp.transpose(rel_bias[_t5_bucket(rel)].astype(F32), (2, 0, 1))
    ok = np.logical_and(k_pos[None, :] // CHUNK <= q_pos[:, None] // CHUNK, k_pos[None, :] < k_real)
    return jnp.where(ok[None], bias, NEG_INF)


def _in1_plan():
    scale = DH ** -0.5
    heads64 = lambda o, sc=1.0: [(o, h, h * DH, (h + 1) * DH, sc) for h in range(8)]
    plan = [
        ((0, 512), heads64(0, scale)),
        ((512, 1024), [(1, None, 0, 512, 1.0)] + heads64(2)),
        ((1024, 1536), [(3, None, 0, 512, 1.0)] + heads64(4)),
        ((1536, 2048), heads64(5, scale)),
        ((2048, 2560), [(6, None, 0, 512, 1.0)] + heads64(7)),
        ((2560, 3072), [(8, None, 0, 512, 1.0)] + [(9, h, h * 128, (h + 1) * 128, 1.0) for h in range(H_D)]),
        ((3072, 3584), heads64(10)),
        ((3584, 3712), [(11, None, 0, D_I, 1.0), (12, None, 0, D_I, 1.0), (13, None, D_I, D_I + H_I, 1.0)]),
    ]
    out_defs = [(DH, BF16, 8), (512, F32, None), (DH, BF16, 8), (512, F32, None), (DH, BF16, 8),
                (DH, BF16, 8), (512, F32, None), (DH, BF16, 8), (512, F32, None), (2 * DH, BF16, H_D),
                (D_I, BF16, 8), (D_I, F32, None), (D_I, BF16, None), (H_I, F32, None)]
    return plan, out_defs


def _layer0(x, hist, rb, emit_v, p):
    n = x.shape[0]
    rows = _largest_divisor(n, 512, 16)
    plan = [((0, p["w_in0"].shape[1]), [(0, None, 0, p["w_in0"].shape[1], 1.0)])]
    (z,) = _rms_proj(x, p["ln_mix0"], p["w_in0"], plan, [(p["w_in0"].shape[1], F32, None)], rows)
    ws = p["ws_prompt"] if hist is None else p["ws_sample"]
    bs = p["bs_prompt"] if hist is None else p["bs_sample"]
    outs = _mixer_ab(x, z, hist, p["gmlp_ln_g"], p["gmlp_ln_b"], ws, bs, p["conv_k"], p["w_out0"], rb, emit_v)
    x1 = outs[0]
    x2 = _ffn(x1, p["ln_ffn0"], p["ffn_wg"], p["ffn_wu"], p["ffn_wd"], rows, p["ffn_fb"])
    return (x2,) + tuple(outs[1:])


def _layer1_tail(x, o, p):
    n = x.shape[0]
    rows = _largest_divisor(n, 512, 16)
    x3, xn, gate = _out_router(x, o, p["w_out1"], p["ln_ffn1"], p["router"], rows)
    return _moe(x3, xn, gate, p["ln_final"], p["exp_wg"], p["exp_wu"], p["exp_wd"], rows, p["exp_fb"])


def kernel(x_prompt, x_sample, state_b_conv, cache_c_k, cache_c_v, cache_idx_k, cache_d_k, cache_d_v, rel_bias, ln_mix, ln_ffn, ln_final, w_in0, gmlp_ln_g, gmlp_ln_b, gmlp_ws, gmlp_bs, conv_k, w_out0, ffn_wg, ffn_wu, ffn_wd, w_in1, lam_qk, subln_g, w_out1, router, exp_wg, exp_wu, exp_wd):
    bp, seq, d = x_prompt.shape
    bs_, ts, _ = x_sample.shape
    past = cache_c_k.shape[2]
    assert bp == 1 and ln_mix.shape[0] == 2 and seq % 256 == 0 and ts % SUBLANES == 0 and ts <= CHUNK
    gmlp_chunk = gmlp_ws.shape[-1]
    lam_init = 0.8 - 0.6 * math.exp(-0.3 * 1)

    def ws_masked(rows):
        r = jnp.arange(rows)
        ok = (r[None, :] // CHUNK) <= (r[:, None] // CHUNK)
        return jnp.where(ok[None], gmlp_ws[0][:, :rows, :rows], 0.0).astype(BF16)

    in1_pad = (-w_in1.shape[2]) % LANES
    lf = lam_qk[0].astype(F32)
    lam = (jnp.exp(jnp.sum(lf[0] * lf[1])) - jnp.exp(jnp.sum(lf[2] * lf[3])) + lam_init).reshape(1, 1)
    p = {
        "ln_mix0": ln_mix[0:1], "ln_ffn0": ln_ffn[0:1], "ln_mix1": ln_mix[1:2], "ln_ffn1": ln_ffn[1:2],
        "ln_final": ln_final.reshape(1, d),
        "w_in0": w_in0[0].astype(BF16),
        "gmlp_ln_g": gmlp_ln_g[0].reshape(1, W_A), "gmlp_ln_b": gmlp_ln_b[0].reshape(1, W_A),
        "ws_prompt": ws_masked(gmlp_chunk), "ws_sample": ws_masked(ts),
        "bs_prompt": gmlp_bs[0][:, :gmlp_chunk].T, "bs_sample": gmlp_bs[0][:, :ts].T,
        "conv_k": conv_k[0], "w_out0": w_out0[0].astype(BF16),
        "ffn_wg": ffn_wg[0].astype(BF16), "ffn_wu": ffn_wu[0].astype(BF16), "ffn_wd": ffn_wd[0].astype(BF16),
        "ffn_fb": _largest_divisor(ffn_wg.shape[2], 1408, LANES),
        "w_in1": jnp.pad(w_in1[0], ((0, 0), (0, in1_pad))).astype(BF16),
        "w_out1": w_out1[0].astype(BF16),
        "router": jnp.pad(router[0], ((0, 0), (0, LANES - N_EXP))).astype(BF16),
        "exp_wg": exp_wg[0].astype(BF16), "exp_wu": exp_wu[0].astype(BF16), "exp_wd": exp_wd[0].astype(BF16),
        "exp_fb": _largest_divisor(exp_wg.shape[3], 896, LANES),
    }
    g_sub = subln_g[0].reshape(1, 2 * DH)
    plan1, out_defs1 = _in1_plan()

    xp = x_prompt.reshape(seq, d)
    xs = x_sample.reshape(bs_ * ts, d)
    xp, p_tail = _layer0(xp, None, gmlp_chunk, False, p)
    hist = jnp.pad(state_b_conv[0], ((0, 0), (SUBLANES - 2, 0), (0, 0)))
    xs, s_tail, s_av = _layer0(xs, hist, ts, True, p)
    p_b_conv = p_tail[-1, SUBLANES - 2:, :].reshape(1, 1, 2, W_B)
    s_b_conv = s_tail[:, SUBLANES - 2:, :].reshape(1, bs_, 2, W_B)
    s_a_v = s_av.reshape(1, bs_, ts, W_A)

    lb = KEY_BLOCK
    pr = _rms_proj(xp, p["ln_mix1"], p["w_in1"], plan1, out_defs1, _largest_divisor(seq, 256, 16))
    qc, kc32, kc, vc32, vc, qd, kd32, kd, vd32, vd, qi, ki32, ki, wi = pr
    lp = -(-seq // lb) * lb
    padk = lambda a, axis: jnp.pad(a, [(0, lp - seq) if ax == axis else (0, 0) for ax in range(a.ndim)])
    pos = jnp.arange(seq, dtype=I32)
    ve = ((pos // CHUNK + 1) * CHUNK).reshape(seq, 1)
    tq_sel = _largest_divisor(seq, 128, CHUNK)
    mask = _select(qi, wi, ve, padk(ki, 0)[None], 1, seq, tq_sel, min(TOPK_MAX, seq // 4),
                   lambda i: ((i + 1) * tq_sel + lb - 1) // lb)
    tq = 256
    e_far = -(-(lb - 1 + MAX_DIST) // tq)
    kpos = np.arange(lb)
    qpos = np.arange(tq)
    base = (e_far + 1) * tq
    btiles = jnp.stack([_bias_tile(rel_bias, base + e * tq + qpos, base + kpos, base + lb) for e in range(e_far + 1)])
    per = lb // tq
    op = _attend(qc, qd, padk(kc, 1)[None], padk(vc, 1)[None], padk(kd, 1)[None], padk(vd, 1)[None],
                 mask, btiles, lam, g_sub, 1, seq, tq,
                 lambda i: (i + per) // per,
                 lambda i, j: jnp.minimum(i - j * per, e_far), lam_init)
    y_prompt = _layer1_tail(xp, op, p).reshape(1, seq, d)

    sr = _rms_proj(xs, p["ln_mix1"], p["w_in1"], plan1, out_defs1, _largest_divisor(bs_ * ts, 256, 16))
    sqc, skc32, skc, svc32, svc, sqd, skd32, skd, svd32, svd, sqi, ski32, ski, swi = sr
    nk = past + ts
    lps = -(-nk // lb) * lb

    def cat_heads(cache, new, nh, w):
        old = jnp.transpose(cache.reshape(bs_, past, nh, w), (0, 2, 1, 3)).astype(BF16)
        new = jnp.transpose(new.reshape(nh, bs_, ts, w), (1, 0, 2, 3))
        return jnp.pad(jnp.concatenate([old, new], axis=2), ((0, 0), (0, 0), (0, lps - nk), (0, 0)))

    kis = jnp.pad(jnp.concatenate([cache_idx_k[0].astype(BF16), ski.reshape(bs_, ts, D_I)], axis=1),
                  ((0, 0), (0, lps - nk), (0, 0)))
    ves = jnp.full((bs_ * ts, 1), nk, I32)
    smask = _select(sqi, swi, ves, kis, bs_, ts, ts, min(TOPK_MAX, nk // 4), lambda i: lps // lb)
    sq_pos = past + np.arange(ts)
    sbt = jnp.stack([_bias_tile(rel_bias, sq_pos, j * lb + kpos, nk) for j in range(lps // lb)])
    os_ = _attend(sqc, sqd, cat_heads(cache_c_k[0], skc, H_C, DH), cat_heads(cache_c_v[0], svc, H_C, DH),
                  cat_heads(cache_d_k[0], skd, 2 * H_D, DH), cat_heads(cache_d_v[0], svd, H_D, 2 * DH),
                  smask, sbt, lam, g_sub, bs_, ts, ts, lambda i: lps // lb, lambda i, j: j, lam_init)
    y_sample = _layer1_tail(xs, os_, p).reshape(bs_, ts, d)

    r5 = lambda a, n, t, *tail: a.reshape((1, n, t) + tail)
    return (y_prompt, y_sample, p_b_conv,
            r5(kc32, 1, seq, H_C, DH), r5(vc32, 1, seq, H_C, DH), r5(ki32, 1, seq, D_I),
            r5(kd32, 1, seq, H_D, 2, DH), r5(vd32, 1, seq, H_D, 2 * DH),
            s_a_v, s_b_conv,
            r5(skc32, bs_, ts, H_C, DH), r5(svc32, bs_, ts, H_C, DH), r5(ski32, bs_, ts, D_I),
            r5(skd32, bs_, ts, H_D, 2, DH), r5(svd32, bs_, ts, H_D, 2 * DH))
```

```python
import functools
import math

import jax
import jax.numpy as jnp
import numpy as np
from jax import lax
from jax.experimental import pallas as pl
from jax.experimental.pallas import tpu as pltpu

F32 = jnp.float32
BF16 = jnp.bfloat16
I32 = jnp.int32

CHUNK = 64
EPS = 1e-6
NEG_INF = -1e30
H_A = 4
C_A = 128
W_A = H_A * C_A
W_B = 512
H_C = 8
DH = 64
W_C = H_C * DH
H_I = 8
D_I = 64
TOPK_MAX = 256
H_D = 4
W_D = H_D * 2 * DH
N_BUCKETS = 32
MAX_DIST = 128
N_EXP = 8
IN1_SIZES = (W_C, W_C, W_C, W_D, W_D, W_D, H_I * D_I, D_I, H_I)
N_HEADS_BIAS = H_C + H_D
N_MAPS = H_C + 2 * H_D

LANES = 128
SUBLANES = 8
VMEM_LIMIT = 56 * 1024 * 1024

INT_MIN = -(2 ** 31)
INT_MAX = 2 ** 31 - 1
ALL_TIES = 2 ** 30

KEY_BLOCK = 512


def _cparams(*sem):
    return pltpu.CompilerParams(dimension_semantics=sem, vmem_limit_bytes=VMEM_LIMIT)


def _largest_divisor(n, target, mult):
    if n <= target:
        return n
    d = (target // mult) * mult
    while d >= mult:
        if n % d == 0:
            return d
        d -= mult
    raise ValueError(f"no block of multiple {mult} divides {n}")


def _rms(x, g):
    return x * lax.rsqrt(jnp.mean(x * x, axis=-1, keepdims=True) + EPS) * g


def _dot(a, b):
    return jnp.dot(a, b, preferred_element_type=F32)


def _dot_nt(a, b):
    return lax.dot_general(a, b, (((1,), (1,)), ((), ())), preferred_element_type=F32)


def _rms_proj_body(x_ref, g_ref, w_ref, *out_refs, plan):
    xn = _rms(x_ref[...], g_ref[...]).astype(BF16)
    for (c0, c1), writes in plan:
        z = _dot(xn, w_ref[:, c0:c1])
        for o_idx, head, z0, z1, scale in writes:
            val = z[:, z0:z1]
            if scale != 1.0:
                val = val * scale
            ref = out_refs[o_idx]
            if head is None:
                ref[...] = val.astype(ref.dtype)
            else:
                ref[head] = val.astype(ref.dtype)


def _rms_proj(x, g, w, plan, out_defs, rows):
    n, d = x.shape
    grid = (n // rows,)
    out_shape, out_specs = [], []
    for width, dtype, heads in out_defs:
        if heads is None:
            out_shape.append(jax.ShapeDtypeStruct((n, width), dtype))
            out_specs.append(pl.BlockSpec((rows, width), lambda i: (i, 0)))
        else:
            out_shape.append(jax.ShapeDtypeStruct((heads, n, width), dtype))
            out_specs.append(pl.BlockSpec((heads, rows, width), lambda i: (0, i, 0)))
    return pl.pallas_call(
        functools.partial(_rms_proj_body, plan=plan),
        grid=grid,
        in_specs=[pl.BlockSpec((rows, d), lambda i: (i, 0)),
                  pl.BlockSpec((1, d), lambda i: (0, 0)),
                  pl.BlockSpec(w.shape, lambda i: (0, 0))],
        out_specs=out_specs,
        out_shape=out_shape,
        compiler_params=_cparams("parallel"),
        name="rms_proj",
    )(x, g, w)


def _mixer_ab_body(*refs, rb, from_prev, emit_v):
    it = iter(refs)
    x_ref, u_ref, v_ref, gb_ref, gc_ref, xin_ref = (next(it) for _ in range(6))
    if from_prev:
        gcp_ref, xinp_ref = next(it), next(it)
    else:
        hist_ref = next(it)
    lng_ref, lnb_ref, ws_ref, bs_ref, ck_ref, wout_ref = (next(it) for _ in range(6))
    x1_ref, tail_ref = next(it), next(it)
    vout_ref = next(it) if emit_v else None
    wext_ref = next(it)

    w = gc_ref[...] * xin_ref[...]
    if from_prev:
        hist = jnp.where(pl.program_id(0) > 0, gcp_ref[...] * xinp_ref[...], 0.0)
    else:
        hist = hist_ref[0]
    wext_ref[0:SUBLANES, :] = hist
    wext_ref[SUBLANES:, :] = w
    ck = ck_ref[...]
    conv = (ck[0:1] * wext_ref[SUBLANES - 2:SUBLANES - 2 + rb, :]
            + ck[1:2] * wext_ref[SUBLANES - 1:SUBLANES - 1 + rb, :]
            + ck[2:3] * w)
    y_b = gb_ref[...] * conv
    tail_ref[0] = w[rb - SUBLANES:, :]

    u = jax.nn.gelu(u_ref[...])
    v = jax.nn.gelu(v_ref[...])
    lng = lng_ref[...]
    lnb = lnb_ref[...]
    bs = bs_ref[...]
    acc = _dot(y_b.astype(BF16), wout_ref[W_A:, :])
    for h in range(H_A):
        sl = slice(h * C_A, (h + 1) * C_A)
        vh = v[:, sl]
        mu = jnp.mean(vh, axis=-1, keepdims=True)
        xc = vh - mu
        var = jnp.mean(xc * xc, axis=-1, keepdims=True)
        vln = xc * lax.rsqrt(var + EPS) * lng[:, sl] + lnb[:, sl]
        if emit_v:
            vout_ref[:, sl] = vln
        s = _dot(ws_ref[h], vln.astype(BF16)) + bs[:, h:h + 1]
        y_a = u[:, sl] * s
        acc = acc + _dot(y_a.astype(BF16), wout_ref[sl, :])
    x1_ref[...] = x_ref[...] + acc


def _mixer_ab(x, z, hist, lng, lnb, ws, bs, ck, wout, rb, emit_v):
    n, d = x.shape
    nb = n // rb
    from_prev = hist is None
    col = lambda c: pl.BlockSpec((rb, 512), lambda i, c=c: (i, c))
    in_specs = [pl.BlockSpec((rb, d), lambda i: (i, 0)), col(0), col(1), col(2), col(3), col(4)]
    args = [x, z, z, z, z, z]
    if from_prev:
        per = rb // SUBLANES
        prev = lambda c: pl.BlockSpec((SUBLANES, 512), lambda i, c=c: (jnp.maximum(i * per - 1, 0), c))
        in_specs += [prev(3), prev(4)]
        args += [z, z]
    else:
        in_specs += [pl.BlockSpec((1, SUBLANES, 512), lambda i: (i, 0, 0))]
        args += [hist]
    const = lambda a: pl.BlockSpec(a.shape, lambda i, nd=a.ndim: (0,) * nd)
    for a in (lng, lnb, ws, bs, ck, wout):
        in_specs.append(const(a))
        args.append(a)
    out_shape = [jax.ShapeDtypeStruct((n, d), F32), jax.ShapeDtypeStruct((nb, SUBLANES, 512), F32)]
    out_specs = [pl.BlockSpec((rb, d), lambda i: (i, 0)), pl.BlockSpec((1, SUBLANES, 512), lambda i: (i, 0, 0))]
    if emit_v:
        out_shape.append(jax.ShapeDtypeStruct((n, W_A), F32))
        out_specs.append(pl.BlockSpec((rb, W_A), lambda i: (i, 0)))
    return pl.pallas_call(
        functools.partial(_mixer_ab_body, rb=rb, from_prev=from_prev, emit_v=emit_v),
        grid=(nb,),
        in_specs=in_specs,
        out_specs=out_specs,
        out_shape=out_shape,
        scratch_shapes=[pltpu.VMEM((rb + SUBLANES, 512), F32)],
        compiler_params=_cparams("arbitrary"),
        name="mixer_ab",
    )(*args)


def _ffn_body(x_ref, g_ref, wg_ref, wu_ref, wd_ref, o_ref, xn_ref, acc_ref):
    k = pl.program_id(1)

    @pl.when(k == 0)
    def _():
        xn_ref[...] = _rms(x_ref[...], g_ref[...]).astype(BF16)
        acc_ref[...] = jnp.zeros_like(acc_ref)

    xn = xn_ref[...]
    h = jax.nn.silu(_dot(xn, wg_ref[...])) * _dot(xn, wu_ref[...])
    acc_ref[...] += _dot(h.astype(BF16), wd_ref[...])

    @pl.when(k == pl.num_programs(1) - 1)
    def _():
        o_ref[...] = x_ref[...] + acc_ref[...]


def _ffn(x, g, wg, wu, wd, rows, fb):
    n, d = x.shape
    dff = wg.shape[1]
    return pl.pallas_call(
        _ffn_body,
        grid=(n // rows, dff // fb),
        in_specs=[pl.BlockSpec((rows, d), lambda i, k: (i, 0)),
                  pl.BlockSpec((1, d), lambda i, k: (0, 0)),
                  pl.BlockSpec((d, fb), lambda i, k: (0, k)),
                  pl.BlockSpec((d, fb), lambda i, k: (0, k)),
                  pl.BlockSpec((fb, d), lambda i, k: (k, 0))],
        out_specs=pl.BlockSpec((rows, d), lambda i, k: (i, 0)),
        out_shape=jax.ShapeDtypeStruct((n, d), F32),
        scratch_shapes=[pltpu.VMEM((rows, d), BF16), pltpu.VMEM((rows, d), F32)],
        compiler_params=_cparams("parallel", "arbitrary"),
        name="ffn",
    )(x, g, wg, wu, wd)


def _select_body(qi_ref, wi_ref, ve_ref, ki_ref, out_ref, keys_ref, *, tq, lb, nkb, topk, nvalid_fn):
    nv = nvalid_fn(pl.program_id(1))
    wi = wi_ref[...]
    ve = ve_ref[...]
    qs = [qi_ref[h] for h in range(H_I)]
    wcols = [wi[:, h:h + 1] for h in range(H_I)]
    lane = lax.broadcasted_iota(I32, (tq, lb), 1)

    def score_block(b, carry):
        kb = ki_ref[0, pl.ds(pl.multiple_of(b * lb, lb), lb), :]
        sc = jnp.zeros((tq, lb), F32)
        for h in range(H_I):
            sc = sc + wcols[h] * jnp.maximum(_dot_nt(qs[h], kb), 0.0)
        bits = lax.bitcast_convert_type(sc, I32)
        key = bits ^ ((bits >> 31) & INT_MAX)
        key = jnp.where(sc == 0.0, 0, key)
        key = jnp.where(lane + b * lb < ve, key, INT_MIN)
        keys_ref[b] = key
        return carry

    lax.fori_loop(0, nv, score_block, 0)

    def count_ge(mid):
        midb = jnp.broadcast_to(mid, (tq, LANES))

        def body(b, acc):
            for c in range(lb // LANES):
                k = keys_ref[b, :, c * LANES:(c + 1) * LANES]
                acc = acc + jnp.where(k >= midb, 1, 0)
            return acc

        acc = lax.fori_loop(0, nv, body, jnp.zeros((tq, LANES), I32))
        return jnp.sum(acc.astype(F32), axis=-1, keepdims=True).astype(I32)

    def bisect(_, st):
        lo, hi, clo, chi = st
        mid = (lo >> 1) + (hi >> 1) + (lo & hi & 1)
        cnt = count_ge(mid)
        active = mid != lo
        up = jnp.logical_and(active, cnt >= topk)
        dn = jnp.logical_and(active, cnt < topk)
        return (jnp.where(up, mid, lo), jnp.where(dn, mid, hi),
                jnp.where(up, cnt, clo), jnp.where(dn, cnt, chi))

    full = lambda v: jnp.full((tq, 1), v, I32)
    lo, hi, clo, chi = lax.fori_loop(0, 32, bisect, (full(INT_MIN), full(INT_MAX), ve, full(0)))

    need = jnp.where(clo > topk, topk - chi, ALL_TIES)
    need = jnp.where(lo == INT_MIN, 0, need).astype(F32)
    lob = jnp.broadcast_to(lo, (tq, LANES))
    needb = jnp.broadcast_to(need, (tq, LANES))
    r = lax.broadcasted_iota(I32, (LANES, LANES), 0)
    c = lax.broadcasted_iota(I32, (LANES, LANES), 1)
    tri = jnp.where(r <= c, 1.0, 0.0).astype(BF16)

    def mask_block(b, seen):
        for cc in range(lb // LANES):
            k = keys_ref[b, :, cc * LANES:(cc + 1) * LANES]
            eq = jnp.where(k == lob, 1.0, 0.0)
            cum = _dot(eq.astype(BF16), tri)
            rank = seen + cum - eq
            take = jnp.where(rank < needb, eq, 0.0)
            sel = jnp.where(k > lob, 1.0, take)
            out_ref[0, b, :, cc * LANES:(cc + 1) * LANES] = jnp.where(sel > 0.5, 0.0, NEG_INF).astype(out_ref.dtype)
            seen = seen + cum[:, LANES - 1:LANES]
        return seen

    lax.fori_loop(0, nv, mask_block, jnp.zeros((tq, 1), F32))

    def fill_block(b, carry):
        out_ref[0, b] = jnp.full((tq, lb), NEG_INF, out_ref.dtype)
        return carry

    lax.fori_loop(nv, nkb, fill_block, 0)


def _select(qi, wi, ve, ki, nbatch, t, tq, topk, nvalid_fn):
    lp = ki.shape[1]
    lb = KEY_BLOCK
    nkb = lp // lb
    nq = t // tq
    row = lambda b, i: b * nq + i
    return pl.pallas_call(
        functools.partial(_select_body, tq=tq, lb=lb, nkb=nkb, topk=topk, nvalid_fn=nvalid_fn),
        grid=(nbatch, nq),
        in_specs=[pl.BlockSpec((H_I, tq, D_I), lambda b, i: (0, row(b, i), 0)),
                  pl.BlockSpec((tq, H_I), lambda b, i: (row(b, i), 0)),
                  pl.BlockSpec((tq, 1), lambda b, i: (row(b, i), 0)),
                  pl.BlockSpec((1, lp, D_I), lambda b, i: (b, 0, 0))],
        out_specs=pl.BlockSpec((1, nkb, tq, lb), lambda b, i: (b, 0, i, 0)),
        out_shape=jax.ShapeDtypeStruct((nbatch, nkb, t, lb), BF16),
        scratch_shapes=[pltpu.VMEM((nkb, tq, lb), I32)],
        compiler_params=_cparams("parallel", "arbitrary"),
        name="index_select",
    )(qi, wi, ve, ki)


def _attn_body(qc_ref, qd_ref, kc_ref, vc_ref, kd_ref, vd_ref, mask_ref, bt_ref, lam_ref, g_ref,
               o_ref, m_ref, l_ref, accc_ref, accd_ref, *, tq, nvalid_fn, lam_init):
    i = pl.program_id(1)
    j = pl.program_id(2)

    @pl.when(j == 0)
    def _():
        m_ref[...] = jnp.full(m_ref.shape, NEG_INF, F32)
        l_ref[...] = jnp.zeros_like(l_ref)
        accc_ref[...] = jnp.zeros_like(accc_ref)
        accd_ref[...] = jnp.zeros_like(accd_ref)

    @pl.when(j < nvalid_fn(i))
    def _():
        sel = mask_ref[0, 0].astype(F32)
        for mp in range(N_MAPS):
            sparse = mp < H_C
            if sparse:
                s = _dot_nt(qc_ref[mp], kc_ref[0, mp]) + bt_ref[0, mp] + sel
                v = vc_ref[0, mp]
                acc_ref, a_idx, dv = accc_ref, mp, DH
            else:
                dm = mp - H_C
                s = _dot_nt(qd_ref[dm], kd_ref[0, dm]) + bt_ref[0, H_C + dm // 2]
                v = vd_ref[0, dm // 2]
                acc_ref, a_idx, dv = accd_ref, dm, 2 * DH
            m_prev = m_ref[mp]
            m_new = jnp.maximum(m_prev, jnp.max(s, axis=-1, keepdims=True))
            alpha = jnp.exp(m_prev - m_new)
            p = jnp.exp(s - m_new[:, 0:1])
            l_ref[mp] = alpha * l_ref[mp] + jnp.sum(p, axis=-1, keepdims=True)
            m_ref[mp] = m_new
            acc_ref[a_idx] = alpha[:, 0:dv] * acc_ref[a_idx] + _dot(p.astype(BF16), v)

    @pl.when(j == pl.num_programs(2) - 1)
    def _():
        for h in range(H_C):
            o_ref[:, h * DH:(h + 1) * DH] = (accc_ref[h] / l_ref[h][:, 0:DH]).astype(o_ref.dtype)
        lam = lam_ref[...]
        g = g_ref[...]
        for h in range(H_D):
            a0 = accd_ref[2 * h] / l_ref[H_C + 2 * h]
            a1 = accd_ref[2 * h + 1] / l_ref[H_C + 2 * h + 1]
            od = _rms(a0 - lam * a1, g) * (1.0 - lam_init)
            o_ref[:, W_C + h * 2 * DH:W_C + (h + 1) * 2 * DH] = od.astype(o_ref.dtype)


def _attend(qc, qd, kc, vc, kd, vd, mask, btiles, lam, g, nbatch, t, tq, nvalid_fn, tile_fn, lam_init):
    lp = kc.shape[2]
    lb = KEY_BLOCK
    nkb = lp // lb
    nq = t // tq
    row = lambda b, i: b * nq + i
    kblk = lambda i, j: jnp.minimum(j, nvalid_fn(i) - 1)
    qspec = pl.BlockSpec((H_C, tq, DH), lambda b, i, j: (0, row(b, i), 0))
    kspec = lambda nh, w: pl.BlockSpec((1, nh, lb, w), lambda b, i, j: (b, 0, kblk(i, j), 0))
    return pl.pallas_call(
        functools.partial(_attn_body, tq=tq, nvalid_fn=nvalid_fn, lam_init=lam_init),
        grid=(nbatch, nq, nkb),
        in_specs=[qspec, qspec, kspec(H_C, DH), kspec(H_C, DH), kspec(2 * H_D, DH), kspec(H_D, 2 * DH),
                  pl.BlockSpec((1, 1, tq, lb), lambda b, i, j: (b, kblk(i, j), i, 0)),
                  pl.BlockSpec((1, N_HEADS_BIAS, tq, lb), lambda b, i, j: (tile_fn(i, kblk(i, j)), 0, 0, 0)),
                  pl.BlockSpec((1, 1), lambda b, i, j: (0, 0)),
                  pl.BlockSpec((1, 2 * DH), lambda b, i, j: (0, 0))],
        out_specs=pl.BlockSpec((tq, W_C + W_D), lambda b, i, j: (row(b, i), 0)),
        out_shape=jax.ShapeDtypeStruct((nbatch * t, W_C + W_D), BF16),
        scratch_shapes=[pltpu.VMEM((N_MAPS, tq, LANES), F32), pltpu.VMEM((N_MAPS, tq, LANES), F32),
                        pltpu.VMEM((H_C, tq, DH), F32), pltpu.VMEM((2 * H_D, tq, 2 * DH), F32)],
        compiler_params=_cparams("parallel", "parallel", "arbitrary"),
        name="attend",
    )(qc, qd, kc, vc, kd, vd, mask, btiles, lam, g)


PROMPT_TQ = 256
PROMPT_LB = 512
SEL_ROWS = 256
TIE_ROWS = 128
MAPS_PER_DOT = 4


def _proj1_t_body(x_ref, g_ref, w_ref, wt_ref, kc32_ref, vc32_ref, kd32_ref, vd32_ref, ki32_ref,
                  kc_ref, kd_ref, ki_ref, qct_ref, qdt_ref, qit_ref, vct_ref, vdt_ref, wit_ref):
    xn32 = _rms(x_ref[...], g_ref[...])
    xn = xn32.astype(BF16)
    xnt = xn32.T.astype(BF16)
    for c, (full_ref, head_ref) in enumerate(((kc32_ref, kc_ref), (vc32_ref, None), (kd32_ref, kd_ref), (vd32_ref, None))):
        z = _dot(xn, w_ref[:, c * 512:(c + 1) * 512])
        full_ref[...] = z
        if head_ref is not None:
            head_ref[...] = z.astype(BF16)
    z = _dot(xn, w_ref[:, 2048:2048 + LANES])
    ki32_ref[...] = z[:, 0:D_I]
    ki_ref[...] = z[:, 0:D_I].astype(BF16)
    for c, t_ref in enumerate((qct_ref, qdt_ref, qit_ref, vct_ref)):
        zt = _dot(wt_ref[c * 512:(c + 1) * 512, :], xnt)
        for h in range(8):
            t_ref[h] = zt[h * DH:(h + 1) * DH, :].astype(BF16)
    zt = _dot(wt_ref[2048:2560, :], xnt)
    for h in range(H_D):
        vdt_ref[h] = zt[h * 2 * DH:(h + 1) * 2 * DH, :].astype(BF16)
    zt = _dot(wt_ref[2560:2560 + LANES, :], xnt)
    wit_ref[...] = zt[0:H_I, :]


def _proj1_t(x, g, w, wt, rows):
    n, d = x.shape
    full = lambda width: (jax.ShapeDtypeStruct((n, width), F32), pl.BlockSpec((rows, width), lambda i: (i, 0)))
    packed = lambda width: (jax.ShapeDtypeStruct((n, width), BF16), pl.BlockSpec((rows, width), lambda i: (i, 0)))
    heads_t = lambda nh, width: (jax.ShapeDtypeStruct((nh, width, n), BF16),
                                 pl.BlockSpec((nh, width, rows), lambda i: (0, 0, i)))
    outs = [full(512), full(512), full(512), full(512), full(D_I),
            packed(512), packed(512), packed(D_I),
            heads_t(8, DH), heads_t(8, DH), heads_t(8, DH), heads_t(8, DH), heads_t(H_D, 2 * DH),
            (jax.ShapeDtypeStruct((H_I, n), F32), pl.BlockSpec((H_I, rows), lambda i: (0, i)))]
    return pl.pallas_call(
        _proj1_t_body,
        grid=(n // rows,),
        in_specs=[pl.BlockSpec((rows, d), lambda i: (i, 0)),
                  pl.BlockSpec((1, d), lambda i: (0, 0)),
                  pl.BlockSpec(w.shape, lambda i: (0, 0)),
                  pl.BlockSpec(wt.shape, lambda i: (0, 0))],
        out_specs=[o[1] for o in outs],
        out_shape=[o[0] for o in outs],
        compiler_params=_cparams("parallel"),
        name="proj1_t",
    )(x, g, w, wt)


def _sublane_all(x8, op):
    for shift in (4, 2, 1):
        x8 = op(x8, pltpu.roll(x8, shift, 0))
    return x8


def _select_t_body(qit_ref, wit_ref, ve_ref, ki_ref, out_ref, keys_ref, gmax_ref, *, tq, lp, topk):
    rb = SEL_ROWS
    nv = (pl.program_id(0) + 1) * (tq // rb)
    wit = wit_ref[...]
    ve = ve_ref[...]
    rows = lax.broadcasted_iota(I32, (rb, tq), 0)
    gmax_ref[...] = jnp.full((rb, tq), INT_MIN, I32)

    def score_chunk(c, carry):
        r0 = pl.multiple_of(c * rb, rb)
        kb = ki_ref[pl.ds(r0, rb), :]
        sc = jnp.zeros((rb, tq), F32)
        for h in range(H_I):
            sc = sc + wit[h:h + 1, :] * jnp.maximum(_dot(kb, qit_ref[h]), 0.0)
        bits = lax.bitcast_convert_type(sc, I32)
        key = bits ^ ((bits >> 31) & INT_MAX)
        key = jnp.where(sc == 0.0, 0, key)
        key = jnp.where(rows + r0 < ve, key, INT_MIN)
        keys_ref[pl.ds(r0, rb), :] = key
        gmax_ref[...] = jnp.maximum(gmax_ref[...], key)
        return carry

    lax.fori_loop(0, nv, score_chunk, 0)

    def count_ge(mid):
        midb = jnp.broadcast_to(mid, (SUBLANES, tq))

        def body(c, accs):
            kc = keys_ref[pl.ds(pl.multiple_of(c * rb, rb), rb), :]
            accs = list(accs)
            for g in range(rb // SUBLANES):
                a = g % len(accs)
                accs[a] = accs[a] + jnp.where(kc[g * SUBLANES:(g + 1) * SUBLANES, :] >= midb, 1, 0)
            return tuple(accs)

        accs = lax.fori_loop(0, nv, body, (jnp.zeros((SUBLANES, tq), I32),) * 4)
        return _sublane_all(accs[0] + accs[1] + accs[2] + accs[3], jnp.add)[0:1, :]

    floor_avg = lambda a, b: (a >> 1) + (b >> 1) + (a & b & 1)

    def pending(lo, hi, clo):
        open_ = jnp.logical_and(floor_avg(lo, hi) != lo, clo > topk)
        return jnp.max(jnp.where(open_, 1.0, 0.0))

    def bisect(st):
        lo, hi, clo, chi, _ = st
        mid = floor_avg(lo, hi)
        cnt = count_ge(mid)
        active = jnp.logical_and(mid != lo, clo > topk)
        up = jnp.logical_and(active, cnt >= topk)
        dn = jnp.logical_and(active, cnt < topk)
        lo, hi = jnp.where(up, mid, lo), jnp.where(dn, mid, hi)
        clo, chi = jnp.where(up, cnt, clo), jnp.where(dn, cnt, chi)
        return lo, hi, clo, chi, pending(lo, hi, clo)

    g8 = gmax_ref[0:SUBLANES, :]
    h8 = g8
    for g in range(1, rb // SUBLANES):
        blk = gmax_ref[g * SUBLANES:(g + 1) * SUBLANES, :]
        g8 = jnp.minimum(g8, blk)
        h8 = jnp.maximum(h8, blk)
    lo0 = _sublane_all(g8, jnp.minimum)[0:1, :]
    hi0 = _sublane_all(h8, jnp.maximum)[0:1, :] + 1
    clo0 = jnp.where(lo0 == INT_MIN, ve, count_ge(lo0))
    lo, hi, clo, chi, _ = lax.while_loop(
        lambda st: st[4] > 0.5, bisect, (lo0, hi0, clo0, jnp.zeros((1, tq), I32), pending(lo0, hi0, clo0)))

    need = jnp.where(clo > topk, topk - chi, ALL_TIES)
    need = jnp.where(lo == INT_MIN, 0, need).astype(F32)
    tr = TIE_ROWS
    lob = jnp.broadcast_to(lo, (tr, tq))
    needb = jnp.broadcast_to(need, (tr, tq))
    r = lax.broadcasted_iota(I32, (tr, tr), 0)
    c = lax.broadcasted_iota(I32, (tr, tr), 1)
    tri = jnp.where(c <= r, 1.0, 0.0).astype(BF16)

    def mask_chunk(cidx, seen):
        r0 = pl.multiple_of(cidx * tr, tr)
        k = keys_ref[pl.ds(r0, tr), :]
        eq = jnp.where(k == lob, 1.0, 0.0)
        cum = _dot(tri, eq.astype(BF16))
        take = jnp.where(seen + cum - eq < needb, eq, 0.0)
        sel = jnp.where(k > lob, 1.0, take)
        out_ref[pl.ds(r0, tr), :] = jnp.where(sel > 0.5, 0.0, NEG_INF).astype(out_ref.dtype)
        return seen + cum[tr - 1:tr, :]

    lax.fori_loop(0, nv * (rb // tr), mask_chunk, jnp.zeros((1, tq), F32))

    def fill_chunk(cidx, carry):
        out_ref[pl.ds(pl.multiple_of(cidx * rb, rb), rb), :] = jnp.full((rb, tq), NEG_INF, out_ref.dtype)
        return carry

    lax.fori_loop(nv, lp // rb, fill_chunk, 0)


def _select_t(qit, wit, ve, ki, topk):
    t = qit.shape[2]
    tq = PROMPT_TQ
    return pl.pallas_call(
        functools.partial(_select_t_body, tq=tq, lp=t, topk=topk),
        grid=(t // tq,),
        in_specs=[pl.BlockSpec((H_I, D_I, tq), lambda i: (0, 0, i)),
                  pl.BlockSpec((H_I, tq), lambda i: (0, i)),
                  pl.BlockSpec((1, tq), lambda i: (0, i)),
                  pl.BlockSpec((t, D_I), lambda i: (0, 0))],
        out_specs=pl.BlockSpec((t, tq), lambda i: (0, i)),
        out_shape=jax.ShapeDtypeStruct((t, t), BF16),
        scratch_shapes=[pltpu.VMEM((t, tq), I32), pltpu.VMEM((SEL_ROWS, tq), I32)],
        compiler_params=_cparams("parallel"),
        name="index_select_t",
    )(qit, wit, ve, ki)


def _attn_t_body(qct_ref, qdt_ref, kc_ref, kd_ref, vct_ref, vdt_ref, mask_ref, bt_ref, lam_ref, g_ref,
                 o_ref, m_ref, l_ref, accc_ref, accd_ref, ot_ref, qbd_ref, *, tq, per, e_far, lam_init):
    i = pl.program_id(0)
    j = pl.program_id(1)
    e = i - j * per
    grp = MAPS_PER_DOT

    @pl.when(j == 0)
    def _():
        m_ref[...] = jnp.full(m_ref.shape, NEG_INF, F32)
        l_ref[...] = jnp.zeros_like(l_ref)
        accc_ref[...] = jnp.zeros_like(accc_ref)
        accd_ref[...] = jnp.zeros_like(accd_ref)
        qbd_ref[...] = jnp.zeros_like(qbd_ref)
        for mp in range(N_MAPS):
            q = qct_ref[mp] if mp < H_C else qdt_ref[mp - H_C]
            a = mp % grp
            qbd_ref[mp // grp, a * DH:(a + 1) * DH, a * tq:(a + 1) * tq] = q

    def logits(g):
        k_ref = kc_ref if g < H_C // grp else kd_ref
        half = g % (H_C // grp)
        return _dot(k_ref[:, half * grp * DH:(half + 1) * grp * DH], qbd_ref[g])

    def step(near):
        sel = mask_ref[...].astype(F32)
        s_grp = None
        for mp in range(N_MAPS):
            if mp % grp == 0:
                s_grp = logits(mp // grp)
            s = s_grp[:, (mp % grp) * tq:(mp % grp + 1) * tq]
            if mp < H_C:
                s = s + sel
                bias_idx, vt, acc_ref, a_idx = mp, vct_ref[mp], accc_ref, mp
            else:
                dm = mp - H_C
                bias_idx, vt, acc_ref, a_idx = H_C + dm // 2, vdt_ref[dm // 2], accd_ref, dm
            if near:
                s = s + bt_ref[0, bias_idx]
            m_prev = m_ref[mp]
            m_new = jnp.maximum(m_prev, jnp.max(s, axis=0, keepdims=True))
            alpha = jnp.exp(m_prev - m_new)
            p = jnp.exp(s - m_new)
            l_ref[mp] = alpha * l_ref[mp] + jnp.sum(p, axis=0, keepdims=True)
            m_ref[mp] = m_new
            acc_ref[a_idx] = alpha * acc_ref[a_idx] + _dot(vt, p.astype(BF16))

    valid = j * per <= i
    pl.when(jnp.logical_and(valid, e < e_far))(functools.partial(step, True))
    pl.when(jnp.logical_and(valid, e >= e_far))(functools.partial(step, False))

    @pl.when(j == pl.num_programs(1) - 1)
    def _():
        for h in range(H_C):
            ot_ref[h * DH:(h + 1) * DH, :] = accc_ref[h] / l_ref[h]
        lam = lam_ref[...]
        g = g_ref[...]
        for h in range(H_D):
            a0 = accd_ref[2 * h] / l_ref[H_C + 2 * h]
            a1 = accd_ref[2 * h + 1] / l_ref[H_C + 2 * h + 1]
            od = a0 - lam * a1
            od = od * lax.rsqrt(jnp.mean(od * od, axis=0, keepdims=True) + EPS) * g * (1.0 - lam_init)
            ot_ref[W_C + h * 2 * DH:W_C + (h + 1) * 2 * DH, :] = od
        o_ref[...] = ot_ref[...].T.astype(o_ref.dtype)


def _attend_t(qct, qdt, kc, kd, vct, vdt, mask, btiles, lam, g, lam_init):
    t = qct.shape[2]
    tq, lb = PROMPT_TQ, PROMPT_LB
    per = lb // tq
    e_far = btiles.shape[0]
    kblk = lambda i, j: jnp.minimum(j, i // per)
    qspec = pl.BlockSpec((H_C, DH, tq), lambda i, j: (0, 0, i))
    kspec = pl.BlockSpec((lb, H_C * DH), lambda i, j: (kblk(i, j), 0))
    return pl.pallas_call(
        functools.partial(_attn_t_body, tq=tq, per=per, e_far=e_far, lam_init=lam_init),
        grid=(t // tq, t // lb),
        in_specs=[qspec, qspec, kspec, kspec,
                  pl.BlockSpec((H_C, DH, lb), lambda i, j: (0, 0, kblk(i, j))),
                  pl.BlockSpec((H_D, 2 * DH, lb), lambda i, j: (0, 0, kblk(i, j))),
                  pl.BlockSpec((lb, tq), lambda i, j: (kblk(i, j), i)),
                  pl.BlockSpec((1, N_HEADS_BIAS, lb, tq),
                               lambda i, j: (jnp.clip(i - kblk(i, j) * per, 0, e_far - 1), 0, 0, 0)),
                  pl.BlockSpec((1, 1), lambda i, j: (0, 0)),
                  pl.BlockSpec((2 * DH, 1), lambda i, j: (0, 0))],
        out_specs=pl.BlockSpec((tq, W_C + W_D), lambda i, j: (i, 0)),
        out_shape=jax.ShapeDtypeStruct((t, W_C + W_D), BF16),
        scratch_shapes=[pltpu.VMEM((N_MAPS, 1, tq), F32), pltpu.VMEM((N_MAPS, 1, tq), F32),
                        pltpu.VMEM((H_C, DH, tq), F32), pltpu.VMEM((2 * H_D, 2 * DH, tq), F32),
                        pltpu.VMEM((W_C + W_D, tq), F32),
                        pltpu.VMEM((N_MAPS // MAPS_PER_DOT, MAPS_PER_DOT * DH, MAPS_PER_DOT * tq), BF16)],
        compiler_params=_cparams("parallel", "arbitrary"),
        name="attend_t",
    )(qct, qdt, kc, kd, vct, vdt, mask, btiles, lam, g)


def _out_router_body(x_ref, o_ref, w_ref, g_ref, r_ref, x3_ref, xn_ref, gate_ref):
    x3 = x_ref[...] + _dot(o_ref[...], w_ref[...])
    x3_ref[...] = x3
    xn = _rms(x3, g_ref[...]).astype(BF16)
    xn_ref[...] = xn
    logits = _dot(xn, r_ref[...])
    lane = lax.broadcasted_iota(I32, logits.shape, 1)
    logits = jnp.where(lane < N_EXP, logits, -jnp.inf)
    m1 = jnp.max(logits, axis=-1, keepdims=True)
    i1 = jnp.min(jnp.where(logits == m1, lane, LANES), axis=-1, keepdims=True)
    rest = jnp.where(lane == i1, -jnp.inf, logits)
    m2 = jnp.max(rest, axis=-1, keepdims=True)
    i2 = jnp.min(jnp.where(rest == m2, lane, LANES), axis=-1, keepdims=True)
    e = jnp.exp(m2 - m1)
    g1 = 1.0 / (1.0 + e)
    g2 = e / (1.0 + e)
    gate_ref[...] = jnp.where(lane == i1, g1, 0.0) + jnp.where(lane == i2, g2, 0.0)


def _out_router(x, o, w, g, router, rows):
    n, d = x.shape
    return pl.pallas_call(
        _out_router_body,
        grid=(n // rows,),
        in_specs=[pl.BlockSpec((rows, d), lambda i: (i, 0)),
                  pl.BlockSpec((rows, d), lambda i: (i, 0)),
                  pl.BlockSpec(w.shape, lambda i: (0, 0)),
                  pl.BlockSpec((1, d), lambda i: (0, 0)),
                  pl.BlockSpec(router.shape, lambda i: (0, 0))],
        out_specs=[pl.BlockSpec((rows, d), lambda i: (i, 0)),
                   pl.BlockSpec((rows, d), lambda i: (i, 0)),
                   pl.BlockSpec((rows, LANES), lambda i: (i, 0))],
        out_shape=[jax.ShapeDtypeStruct((n, d), F32), jax.ShapeDtypeStruct((n, d), BF16),
                   jax.ShapeDtypeStruct((n, LANES), F32)],
        compiler_params=_cparams("parallel"),
        name="out_router",
    )(x, o, w, g, router)


def _moe_body(x_ref, xn_ref, gate_ref, gf_ref, wg_ref, wu_ref, wd_ref, y_ref, acc_ref):
    e = pl.program_id(1)
    k = pl.program_id(2)

    @pl.when(jnp.logical_and(e == 0, k == 0))
    def _():
        acc_ref[...] = jnp.zeros_like(acc_ref)

    gate = gate_ref[...]
    lane = lax.broadcasted_iota(I32, gate.shape, 1)
    ge = jnp.sum(jnp.where(lane == e, gate, 0.0), axis=-1, keepdims=True)
    xn = xn_ref[...]
    h = jax.nn.silu(_dot(xn, wg_ref[0])) * _dot(xn, wu_ref[0])
    acc_ref[...] += _dot((ge * h).astype(BF16), wd_ref[0])

    @pl.when(jnp.logical_and(e == pl.num_programs(1) - 1, k == pl.num_programs(2) - 1))
    def _():
        y_ref[...] = _rms(x_ref[...] + acc_ref[...], gf_ref[...])


def _moe(x, xn, gate, gf, wg, wu, wd, rows, fb):
    n, d = x.shape
    ne, _, dff = wg.shape
    return pl.pallas_call(
        _moe_body,
        grid=(n // rows, ne, dff // fb),
        in_specs=[pl.BlockSpec((rows, d), lambda i, e, k: (i, 0)),
                  pl.BlockSpec((rows, d), lambda i, e, k: (i, 0)),
                  pl.BlockSpec((rows, LANES), lambda i, e, k: (i, 0)),
                  pl.BlockSpec((1, d), lambda i, e, k: (0, 0)),
                  pl.BlockSpec((1, d, fb), lambda i, e, k: (e, 0, k)),
                  pl.BlockSpec((1, d, fb), lambda i, e, k: (e, 0, k)),
                  pl.BlockSpec((1, fb, d), lambda i, e, k: (e, k, 0))],
        out_specs=pl.BlockSpec((rows, d), lambda i, e, k: (i, 0)),
        out_shape=jax.ShapeDtypeStruct((n, d), F32),
        scratch_shapes=[pltpu.VMEM((rows, d), F32)],
        compiler_params=_cparams("parallel", "arbitrary", "arbitrary"),
        name="moe",
    )(x, xn, gate, gf, wg, wu, wd)


def _t5_bucket(rel):
    half = N_BUCKETS // 2
    max_exact = half // 2
    ret = np.where(rel > 0, half, 0)
    n = np.abs(rel)
    nf = np.maximum(n, 1).astype(np.float32)
    large = max_exact + (np.log(nf / np.float32(max_exact)) / np.float32(math.log(MAX_DIST / max_exact))
                         * np.float32(half - max_exact)).astype(np.int32)
    large = np.minimum(large, half - 1)
    return (ret + np.where(n < max_exact, n, large)).astype(np.int32)


def _bias_tile(rel_bias, q_pos, k_pos, k_real):
    rel = k_pos[None, :] - q_pos[:, None]
    bias = jnp.transpose(rel_bias[_t5_bucket(rel)].astype(F32), (2, 0, 1))
    ok = np.logical_and(k_pos[None, :] // CHUNK <= q_pos[:, None] // CHUNK, k_pos[None, :] < k_real)
    return jnp.where(ok[None], bias, NEG_INF)


def _in1_plan():
    scale = DH ** -0.5
    heads64 = lambda o, sc=1.0: [(o, h, h * DH, (h + 1) * DH, sc) for h in range(8)]
    plan = [
        ((0, 512), heads64(0, scale)),
        ((512, 1024), [(1, None, 0, 512, 1.0)] + heads64(2)),
        ((1024, 1536), [(3, None, 0, 512, 1.0)] + heads64(4)),
        ((1536, 2048), heads64(5, scale)),
        ((2048, 2560), [(6, None, 0, 512, 1.0)] + heads64(7)),
        ((2560, 3072), [(8, None, 0, 512, 1.0)] + [(9, h, h * 128, (h + 1) * 128, 1.0) for h in range(H_D)]),
        ((3072, 3584), heads64(10)),
        ((3584, 3712), [(11, None, 0, D_I, 1.0), (12, None, 0, D_I, 1.0), (13, None, D_I, D_I + H_I, 1.0)]),
    ]
    out_defs = [(DH, BF16, 8), (512, F32, None), (DH, BF16, 8), (512, F32, None), (DH, BF16, 8),
                (DH, BF16, 8), (512, F32, None), (DH, BF16, 8), (512, F32, None), (2 * DH, BF16, H_D),
                (D_I, BF16, 8), (D_I, F32, None), (D_I, BF16, None), (H_I, F32, None)]
    return plan, out_defs


def _layer0(x, hist, rb, emit_v, p):
    n = x.shape[0]
    rows = _largest_divisor(n, 512, 16)
    plan = [((0, p["w_in0"].shape[1]), [(0, None, 0, p["w_in0"].shape[1], 1.0)])]
    (z,) = _rms_proj(x, p["ln_mix0"], p["w_in0"], plan, [(p["w_in0"].shape[1], F32, None)], rows)
    ws = p["ws_prompt"] if hist is None else p["ws_sample"]
    bs = p["bs_prompt"] if hist is None else p["bs_sample"]
    outs = _mixer_ab(x, z, hist, p["gmlp_ln_g"], p["gmlp_ln_b"], ws, bs, p["conv_k"], p["w_out0"], rb, emit_v)
    x1 = outs[0]
    x2 = _ffn(x1, p["ln_ffn0"], p["ffn_wg"], p["ffn_wu"], p["ffn_wd"], rows, p["ffn_fb"])
    return (x2,) + tuple(outs[1:])


def _layer1_tail(x, o, p):
    n = x.shape[0]
    rows = _largest_divisor(n, 512, 16)
    x3, xn, gate = _out_router(x, o, p["w_out1"], p["ln_ffn1"], p["router"], rows)
    return _moe(x3, xn, gate, p["ln_final"], p["exp_wg"], p["exp_wu"], p["exp_wd"], rows, p["exp_fb"])


def kernel(x_prompt, x_sample, state_b_conv, cache_c_k, cache_c_v, cache_idx_k, cache_d_k, cache_d_v, rel_bias, ln_mix, ln_ffn, ln_final, w_in0, gmlp_ln_g, gmlp_ln_b, gmlp_ws, gmlp_bs, conv_k, w_out0, ffn_wg, ffn_wu, ffn_wd, w_in1, lam_qk, subln_g, w_out1, router, exp_wg, exp_wu, exp_wd):
    bp, seq, d = x_prompt.shape
    bs_, ts, _ = x_sample.shape
    past = cache_c_k.shape[2]
    assert bp == 1 and ln_mix.shape[0] == 2 and seq % PROMPT_LB == 0 and ts % SUBLANES == 0 and ts <= CHUNK
    gmlp_chunk = gmlp_ws.shape[-1]
    lam_init = 0.8 - 0.6 * math.exp(-0.3 * 1)

    def ws_masked(rows):
        r = jnp.arange(rows)
        ok = (r[None, :] // CHUNK) <= (r[:, None] // CHUNK)
        return jnp.where(ok[None], gmlp_ws[0][:, :rows, :rows], 0.0).astype(BF16)

    in1_pad = (-w_in1.shape[2]) % LANES
    lf = lam_qk[0].astype(F32)
    lam = (jnp.exp(jnp.sum(lf[0] * lf[1])) - jnp.exp(jnp.sum(lf[2] * lf[3])) + lam_init).reshape(1, 1)
    p = {
        "ln_mix0": ln_mix[0:1], "ln_ffn0": ln_ffn[0:1], "ln_mix1": ln_mix[1:2], "ln_ffn1": ln_ffn[1:2],
        "ln_final": ln_final.reshape(1, d),
        "w_in0": w_in0[0].astype(BF16),
        "gmlp_ln_g": gmlp_ln_g[0].reshape(1, W_A), "gmlp_ln_b": gmlp_ln_b[0].reshape(1, W_A),
        "ws_prompt": ws_masked(gmlp_chunk), "ws_sample": ws_masked(ts),
        "bs_prompt": gmlp_bs[0][:, :gmlp_chunk].T, "bs_sample": gmlp_bs[0][:, :ts].T,
        "conv_k": conv_k[0], "w_out0": w_out0[0].astype(BF16),
        "ffn_wg": ffn_wg[0].astype(BF16), "ffn_wu": ffn_wu[0].astype(BF16), "ffn_wd": ffn_wd[0].astype(BF16),
        "ffn_fb": _largest_divisor(ffn_wg.shape[2], 1408, LANES),
        "w_in1": jnp.pad(w_in1[0], ((0, 0), (0, in1_pad))).astype(BF16),
        "w_out1": w_out1[0].astype(BF16),
        "router": jnp.pad(router[0], ((0, 0), (0, LANES - N_EXP))).astype(BF16),
        "exp_wg": exp_wg[0].astype(BF16), "exp_wu": exp_wu[0].astype(BF16), "exp_wd": exp_wd[0].astype(BF16),
        "exp_fb": _largest_divisor(exp_wg.shape[3], 896, LANES),
    }
    g_sub = subln_g[0].reshape(1, 2 * DH)
    plan1, out_defs1 = _in1_plan()

    xp = x_prompt.reshape(seq, d)
    xs = x_sample.reshape(bs_ * ts, d)
    xp, p_tail = _layer0(xp, None, gmlp_chunk, False, p)
    hist = jnp.pad(state_b_conv[0], ((0, 0), (SUBLANES - 2, 0), (0, 0)))
    xs, s_tail, s_av = _layer0(xs, hist, ts, True, p)
    p_b_conv = p_tail[-1, SUBLANES - 2:, :].reshape(1, 1, 2, W_B)
    s_b_conv = s_tail[:, SUBLANES - 2:, :].reshape(1, bs_, 2, W_B)
    s_a_v = s_av.reshape(1, bs_, ts, W_A)

    lb = KEY_BLOCK
    w1 = w_in1[0]
    off = np.concatenate([[0], np.cumsum(IN1_SIZES)])
    field = lambda k: w1[:, off[k]:off[k + 1]]
    padc = lambda a: jnp.pad(a, ((0, 0), (0, LANES - a.shape[1])))
    qscale = DH ** -0.5
    w_norm = jnp.concatenate([field(1), field(2), field(4), field(5), padc(field(7))], axis=1).astype(BF16)
    w_tran = jnp.concatenate([field(0) * qscale, field(3) * qscale, field(6), field(2), field(5),
                              padc(field(8))], axis=1).T.astype(BF16)
    (kc32, vc32, kd32, vd32, ki32, kc, kd, ki, qct, qdt, qit, vct, vdt, wit) = _proj1_t(
        xp, p["ln_mix1"], w_norm, w_tran, PROMPT_TQ)
    pos = jnp.arange(seq, dtype=I32)
    ve = ((pos // CHUNK + 1) * CHUNK).reshape(1, seq)
    mask = _select_t(qit, wit, ve, ki, min(TOPK_MAX, seq // 4))
    e_far = -(-(PROMPT_LB - 1 + MAX_DIST) // PROMPT_TQ)
    kpos = np.arange(PROMPT_LB)
    qpos = np.arange(PROMPT_TQ)
    base = e_far * PROMPT_TQ
    rel_near = rel_bias - rel_bias[N_BUCKETS // 2 - 1:N_BUCKETS // 2]
    btiles = jnp.stack([jnp.transpose(_bias_tile(rel_near, base + e * PROMPT_TQ + qpos, base + kpos, base + PROMPT_LB),
                                      (0, 2, 1)) for e in range(e_far)])
    op = _attend_t(qct, qdt, kc, kd, vct, vdt, mask, btiles, lam, subln_g[0].reshape(2 * DH, 1), lam_init)
    y_prompt = _layer1_tail(xp, op, p).reshape(1, seq, d)

    sr = _rms_proj(xs, p["ln_mix1"], p["w_in1"], plan1, out_defs1, _largest_divisor(bs_ * ts, 256, 16))
    sqc, skc32, skc, svc32, svc, sqd, skd32, skd, svd32, svd, sqi, ski32, ski, swi = sr
    nk = past + ts
    lps = -(-nk // lb) * lb

    def cat_heads(cache, new, nh, w):
        old = jnp.transpose(cache.reshape(bs_, past, nh, w), (0, 2, 1, 3)).astype(BF16)
        new = jnp.transpose(new.reshape(nh, bs_, ts, w), (1, 0, 2, 3))
        return jnp.pad(jnp.concatenate([old, new], axis=2), ((0, 0), (0, 0), (0, lps - nk), (0, 0)))

    kis = jnp.pad(jnp.concatenate([cache_idx_k[0].astype(BF16), ski.reshape(bs_, ts, D_I)], axis=1),
                  ((0, 0), (0, lps - nk), (0, 0)))
    ves = jnp.full((bs_ * ts, 1), nk, I32)
    smask = _select(sqi, swi, ves, kis, bs_, ts, ts, min(TOPK_MAX, nk // 4), lambda i: lps // lb)
    sq_pos = past + np.arange(ts)
    sbt = jnp.stack([_bias_tile(rel_bias, sq_pos, j * lb + np.arange(lb), nk) for j in range(lps // lb)])
    os_ = _attend(sqc, sqd, cat_heads(cache_c_k[0], skc, H_C, DH), cat_heads(cache_c_v[0], svc, H_C, DH),
                  cat_heads(cache_d_k[0], skd, 2 * H_D, DH), cat_heads(cache_d_v[0], svd, H_D, 2 * DH),
                  smask, sbt, lam, g_sub, bs_, ts, ts, lambda i: lps // lb, lambda i, j: j, lam_init)
    y_sample = _layer1_tail(xs, os_, p).reshape(bs_, ts, d)

    r5 = lambda a, n, t, *tail: a.reshape((1, n, t) + tail)
    return (y_prompt, y_sample, p_b_conv,
            r5(kc32, 1, seq, H_C, DH), r5(vc32, 1, seq, H_C, DH), r5(ki32, 1, seq, D_I),
            r5(kd32, 1, seq, H_D, 2, DH), r5(vd32, 1, seq, H_D, 2 * DH),
            s_a_v, s_b_conv,
            r5(skc32, bs_, ts, H_C, DH), r5(svc32, bs_, ts, H_C, DH), r5(ski32, bs_, ts, D_I),
            r5(skd32, bs_, ts, H_D, 2, DH), r5(svd32, bs_, ts, H_D, 2 * DH))
```

```python
import functools
import math

import jax
import jax.numpy as jnp
import numpy as np
from jax import lax
from jax.experimental import pallas as pl
from jax.experimental.pallas import tpu as pltpu

F32 = jnp.float32
BF16 = jnp.bfloat16
I32 = jnp.int32

CHUNK = 64
EPS = 1e-6
NEG_INF = -1e30
H_A = 4
C_A = 128
W_A = H_A * C_A
W_B = 512
H_C = 8
DH = 64
W_C = H_C * DH
H_I = 8
D_I = 64
TOPK_MAX = 256
H_D = 4
W_D = H_D * 2 * DH
N_BUCKETS = 32
MAX_DIST = 128
N_EXP = 8
IN1_SIZES = (W_C, W_C, W_C, W_D, W_D, W_D, H_I * D_I, D_I, H_I)
N_HEADS_BIAS = H_C + H_D
N_MAPS = H_C + 2 * H_D

LANES = 128
SUBLANES = 8
VMEM_LIMIT = 56 * 1024 * 1024

INT_MIN = -(2 ** 31)
INT_MAX = 2 ** 31 - 1
ALL_TIES = 2 ** 30

KEY_BLOCK = 512


def _cparams(*sem):
    return pltpu.CompilerParams(dimension_semantics=sem, vmem_limit_bytes=VMEM_LIMIT)


def _largest_divisor(n, target, mult):
    if n <= target:
        return n
    d = (target // mult) * mult
    while d >= mult:
        if n % d == 0:
            return d
        d -= mult
    raise ValueError(f"no block of multiple {mult} divides {n}")


def _rms(x, g):
    return x * lax.rsqrt(jnp.mean(x * x, axis=-1, keepdims=True) + EPS) * g


def _dot(a, b):
    return jnp.dot(a, b, preferred_element_type=F32)


def _dot_nt(a, b):
    return lax.dot_general(a, b, (((1,), (1,)), ((), ())), preferred_element_type=F32)


def _rms_proj_body(x_ref, g_ref, w_ref, *out_refs, plan):
    xn = _rms(x_ref[...], g_ref[...]).astype(BF16)
    for (c0, c1), writes in plan:
        z = _dot(xn, w_ref[:, c0:c1])
        for o_idx, head, z0, z1, scale in writes:
            val = z[:, z0:z1]
            if scale != 1.0:
                val = val * scale
            ref = out_refs[o_idx]
            if head is None:
                ref[...] = val.astype(ref.dtype)
            else:
                ref[head] = val.astype(ref.dtype)


def _rms_proj(x, g, w, plan, out_defs, rows):
    n, d = x.shape
    grid = (n // rows,)
    out_shape, out_specs = [], []
    for width, dtype, heads in out_defs:
        if heads is None:
            out_shape.append(jax.ShapeDtypeStruct((n, width), dtype))
            out_specs.append(pl.BlockSpec((rows, width), lambda i: (i, 0)))
        else:
            out_shape.append(jax.ShapeDtypeStruct((heads, n, width), dtype))
            out_specs.append(pl.BlockSpec((heads, rows, width), lambda i: (0, i, 0)))
    return pl.pallas_call(
        functools.partial(_rms_proj_body, plan=plan),
        grid=grid,
        in_specs=[pl.BlockSpec((rows, d), lambda i: (i, 0)),
                  pl.BlockSpec((1, d), lambda i: (0, 0)),
                  pl.BlockSpec(w.shape, lambda i: (0, 0))],
        out_specs=out_specs,
        out_shape=out_shape,
        compiler_params=_cparams("parallel"),
        name="rms_proj",
    )(x, g, w)


def _mixer_ab_body(*refs, rb, from_prev, emit_v):
    it = iter(refs)
    x_ref, u_ref, v_ref, gb_ref, gc_ref, xin_ref = (next(it) for _ in range(6))
    if from_prev:
        gcp_ref, xinp_ref = next(it), next(it)
    else:
        hist_ref = next(it)
    lng_ref, lnb_ref, ws_ref, bs_ref, ck_ref, wout_ref = (next(it) for _ in range(6))
    x1_ref, tail_ref = next(it), next(it)
    vout_ref = next(it) if emit_v else None
    wext_ref = next(it)

    w = gc_ref[...] * xin_ref[...]
    if from_prev:
        hist = jnp.where(pl.program_id(0) > 0, gcp_ref[...] * xinp_ref[...], 0.0)
    else:
        hist = hist_ref[0]
    wext_ref[0:SUBLANES, :] = hist
    wext_ref[SUBLANES:, :] = w
    ck = ck_ref[...]
    conv = (ck[0:1] * wext_ref[SUBLANES - 2:SUBLANES - 2 + rb, :]
            + ck[1:2] * wext_ref[SUBLANES - 1:SUBLANES - 1 + rb, :]
            + ck[2:3] * w)
    y_b = gb_ref[...] * conv
    tail_ref[0] = w[rb - SUBLANES:, :]

    u = jax.nn.gelu(u_ref[...])
    v = jax.nn.gelu(v_ref[...])
    lng = lng_ref[...]
    lnb = lnb_ref[...]
    bs = bs_ref[...]
    acc = _dot(y_b.astype(BF16), wout_ref[W_A:, :])
    for h in range(H_A):
        sl = slice(h * C_A, (h + 1) * C_A)
        vh = v[:, sl]
        mu = jnp.mean(vh, axis=-1, keepdims=True)
        xc = vh - mu
        var = jnp.mean(xc * xc, axis=-1, keepdims=True)
        vln = xc * lax.rsqrt(var + EPS) * lng[:, sl] + lnb[:, sl]
        if emit_v:
            vout_ref[:, sl] = vln
        s = _dot(ws_ref[h], vln.astype(BF16)) + bs[:, h:h + 1]
        y_a = u[:, sl] * s
        acc = acc + _dot(y_a.astype(BF16), wout_ref[sl, :])
    x1_ref[...] = x_ref[...] + acc


def _mixer_ab(x, z, hist, lng, lnb, ws, bs, ck, wout, rb, emit_v):
    n, d = x.shape
    nb = n // rb
    from_prev = hist is None
    col = lambda c: pl.BlockSpec((rb, 512), lambda i, c=c: (i, c))
    in_specs = [pl.BlockSpec((rb, d), lambda i: (i, 0)), col(0), col(1), col(2), col(3), col(4)]
    args = [x, z, z, z, z, z]
    if from_prev:
        per = rb // SUBLANES
        prev = lambda c: pl.BlockSpec((SUBLANES, 512), lambda i, c=c: (jnp.maximum(i * per - 1, 0), c))
        in_specs += [prev(3), prev(4)]
        args += [z, z]
    else:
        in_specs += [pl.BlockSpec((1, SUBLANES, 512), lambda i: (i, 0, 0))]
        args += [hist]
    const = lambda a: pl.BlockSpec(a.shape, lambda i, nd=a.ndim: (0,) * nd)
    for a in (lng, lnb, ws, bs, ck, wout):
        in_specs.append(const(a))
        args.append(a)
    out_shape = [jax.ShapeDtypeStruct((n, d), F32), jax.ShapeDtypeStruct((nb, SUBLANES, 512), F32)]
    out_specs = [pl.BlockSpec((rb, d), lambda i: (i, 0)), pl.BlockSpec((1, SUBLANES, 512), lambda i: (i, 0, 0))]
    if emit_v:
        out_shape.append(jax.ShapeDtypeStruct((n, W_A), F32))
        out_specs.append(pl.BlockSpec((rb, W_A), lambda i: (i, 0)))
    return pl.pallas_call(
        functools.partial(_mixer_ab_body, rb=rb, from_prev=from_prev, emit_v=emit_v),
        grid=(nb,),
        in_specs=in_specs,
        out_specs=out_specs,
        out_shape=out_shape,
        scratch_shapes=[pltpu.VMEM((rb + SUBLANES, 512), F32)],
        compiler_params=_cparams("arbitrary"),
        name="mixer_ab",
    )(*args)


def _ffn_body(x_ref, g_ref, wg_ref, wu_ref, wd_ref, o_ref, xn_ref, acc_ref):
    k = pl.program_id(1)

    @pl.when(k == 0)
    def _():
        xn_ref[...] = _rms(x_ref[...], g_ref[...]).astype(BF16)
        acc_ref[...] = jnp.zeros_like(acc_ref)

    xn = xn_ref[...]
    h = jax.nn.silu(_dot(xn, wg_ref[...])) * _dot(xn, wu_ref[...])
    acc_ref[...] += _dot(h.astype(BF16), wd_ref[...])

    @pl.when(k == pl.num_programs(1) - 1)
    def _():
        o_ref[...] = x_ref[...] + acc_ref[...]


def _ffn(x, g, wg, wu, wd, rows, fb):
    n, d = x.shape
    dff = wg.shape[1]
    return pl.pallas_call(
        _ffn_body,
        grid=(n // rows, dff // fb),
        in_specs=[pl.BlockSpec((rows, d), lambda i, k: (i, 0)),
                  pl.BlockSpec((1, d), lambda i, k: (0, 0)),
                  pl.BlockSpec((d, fb), lambda i, k: (0, k)),
                  pl.BlockSpec((d, fb), lambda i, k: (0, k)),
                  pl.BlockSpec((fb, d), lambda i, k: (k, 0))],
        out_specs=pl.BlockSpec((rows, d), lambda i, k: (i, 0)),
        out_shape=jax.ShapeDtypeStruct((n, d), F32),
        scratch_shapes=[pltpu.VMEM((rows, d), BF16), pltpu.VMEM((rows, d), F32)],
        compiler_params=_cparams("parallel", "arbitrary"),
        name="ffn",
    )(x, g, wg, wu, wd)


def _select_body(qi_ref, wi_ref, ve_ref, ki_ref, out_ref, keys_ref, *, tq, lb, nkb, topk, nvalid_fn):
    nv = nvalid_fn(pl.program_id(1))
    wi = wi_ref[...]
    ve = ve_ref[...]
    qs = [qi_ref[h] for h in range(H_I)]
    wcols = [wi[:, h:h + 1] for h in range(H_I)]
    lane = lax.broadcasted_iota(I32, (tq, lb), 1)

    def score_block(b, carry):
        kb = ki_ref[0, pl.ds(pl.multiple_of(b * lb, lb), lb), :]
        sc = jnp.zeros((tq, lb), F32)
        for h in range(H_I):
            sc = sc + wcols[h] * jnp.maximum(_dot_nt(qs[h], kb), 0.0)
        bits = lax.bitcast_convert_type(sc, I32)
        key = bits ^ ((bits >> 31) & INT_MAX)
        key = jnp.where(sc == 0.0, 0, key)
        key = jnp.where(lane + b * lb < ve, key, INT_MIN)
        keys_ref[b] = key
        return carry

    lax.fori_loop(0, nv, score_block, 0)

    def count_ge(mid):
        midb = jnp.broadcast_to(mid, (tq, LANES))

        def body(b, acc):
            for c in range(lb // LANES):
                k = keys_ref[b, :, c * LANES:(c + 1) * LANES]
                acc = acc + jnp.where(k >= midb, 1, 0)
            return acc

        acc = lax.fori_loop(0, nv, body, jnp.zeros((tq, LANES), I32))
        return jnp.sum(acc.astype(F32), axis=-1, keepdims=True).astype(I32)

    def bisect(_, st):
        lo, hi, clo, chi = st
        mid = (lo >> 1) + (hi >> 1) + (lo & hi & 1)
        cnt = count_ge(mid)
        active = mid != lo
        up = jnp.logical_and(active, cnt >= topk)
        dn = jnp.logical_and(active, cnt < topk)
        return (jnp.where(up, mid, lo), jnp.where(dn, mid, hi),
                jnp.where(up, cnt, clo), jnp.where(dn, cnt, chi))

    full = lambda v: jnp.full((tq, 1), v, I32)
    lo, hi, clo, chi = lax.fori_loop(0, 32, bisect, (full(INT_MIN), full(INT_MAX), ve, full(0)))

    need = jnp.where(clo > topk, topk - chi, ALL_TIES)
    need = jnp.where(lo == INT_MIN, 0, need).astype(F32)
    lob = jnp.broadcast_to(lo, (tq, LANES))
    needb = jnp.broadcast_to(need, (tq, LANES))
    r = lax.broadcasted_iota(I32, (LANES, LANES), 0)
    c = lax.broadcasted_iota(I32, (LANES, LANES), 1)
    tri = jnp.where(r <= c, 1.0, 0.0).astype(BF16)

    def mask_block(b, seen):
        for cc in range(lb // LANES):
            k = keys_ref[b, :, cc * LANES:(cc + 1) * LANES]
            eq = jnp.where(k == lob, 1.0, 0.0)
            cum = _dot(eq.astype(BF16), tri)
            rank = seen + cum - eq
            take = jnp.where(rank < needb, eq, 0.0)
            sel = jnp.where(k > lob, 1.0, take)
            out_ref[0, b, :, cc * LANES:(cc + 1) * LANES] = jnp.where(sel > 0.5, 0.0, NEG_INF).astype(out_ref.dtype)
            seen = seen + cum[:, LANES - 1:LANES]
        return seen

    lax.fori_loop(0, nv, mask_block, jnp.zeros((tq, 1), F32))

    def fill_block(b, carry):
        out_ref[0, b] = jnp.full((tq, lb), NEG_INF, out_ref.dtype)
        return carry

    lax.fori_loop(nv, nkb, fill_block, 0)


def _select(qi, wi, ve, ki, nbatch, t, tq, topk, nvalid_fn):
    lp = ki.shape[1]
    lb = KEY_BLOCK
    nkb = lp // lb
    nq = t // tq
    row = lambda b, i: b * nq + i
    return pl.pallas_call(
        functools.partial(_select_body, tq=tq, lb=lb, nkb=nkb, topk=topk, nvalid_fn=nvalid_fn),
        grid=(nbatch, nq),
        in_specs=[pl.BlockSpec((H_I, tq, D_I), lambda b, i: (0, row(b, i), 0)),
                  pl.BlockSpec((tq, H_I), lambda b, i: (row(b, i), 0)),
                  pl.BlockSpec((tq, 1), lambda b, i: (row(b, i), 0)),
                  pl.BlockSpec((1, lp, D_I), lambda b, i: (b, 0, 0))],
        out_specs=pl.BlockSpec((1, nkb, tq, lb), lambda b, i: (b, 0, i, 0)),
        out_shape=jax.ShapeDtypeStruct((nbatch, nkb, t, lb), BF16),
        scratch_shapes=[pltpu.VMEM((nkb, tq, lb), I32)],
        compiler_params=_cparams("parallel", "arbitrary"),
        name="index_select",
    )(qi, wi, ve, ki)


def _attn_body(qc_ref, qd_ref, kc_ref, vc_ref, kd_ref, vd_ref, mask_ref, bt_ref, lam_ref, g_ref,
               o_ref, m_ref, l_ref, accc_ref, accd_ref, *, tq, nvalid_fn, lam_init):
    i = pl.program_id(1)
    j = pl.program_id(2)

    @pl.when(j == 0)
    def _():
        m_ref[...] = jnp.full(m_ref.shape, NEG_INF, F32)
        l_ref[...] = jnp.zeros_like(l_ref)
        accc_ref[...] = jnp.zeros_like(accc_ref)
        accd_ref[...] = jnp.zeros_like(accd_ref)

    @pl.when(j < nvalid_fn(i))
    def _():
        sel = mask_ref[0, 0].astype(F32)
        for mp in range(N_MAPS):
            sparse = mp < H_C
            if sparse:
                s = _dot_nt(qc_ref[mp], kc_ref[0, mp]) + bt_ref[0, mp] + sel
                v = vc_ref[0, mp]
                acc_ref, a_idx, dv = accc_ref, mp, DH
            else:
                dm = mp - H_C
                s = _dot_nt(qd_ref[dm], kd_ref[0, dm]) + bt_ref[0, H_C + dm // 2]
                v = vd_ref[0, dm // 2]
                acc_ref, a_idx, dv = accd_ref, dm, 2 * DH
            m_prev = m_ref[mp]
            m_new = jnp.maximum(m_prev, jnp.max(s, axis=-1, keepdims=True))
            alpha = jnp.exp(m_prev - m_new)
            p = jnp.exp(s - m_new[:, 0:1])
            l_ref[mp] = alpha * l_ref[mp] + jnp.sum(p, axis=-1, keepdims=True)
            m_ref[mp] = m_new
            acc_ref[a_idx] = alpha[:, 0:dv] * acc_ref[a_idx] + _dot(p.astype(BF16), v)

    @pl.when(j == pl.num_programs(2) - 1)
    def _():
        for h in range(H_C):
            o_ref[:, h * DH:(h + 1) * DH] = (accc_ref[h] / l_ref[h][:, 0:DH]).astype(o_ref.dtype)
        lam = lam_ref[...]
        g = g_ref[...]
        for h in range(H_D):
            a0 = accd_ref[2 * h] / l_ref[H_C + 2 * h]
            a1 = accd_ref[2 * h + 1] / l_ref[H_C + 2 * h + 1]
            od = _rms(a0 - lam * a1, g) * (1.0 - lam_init)
            o_ref[:, W_C + h * 2 * DH:W_C + (h + 1) * 2 * DH] = od.astype(o_ref.dtype)


def _attend(qc, qd, kc, vc, kd, vd, mask, btiles, lam, g, nbatch, t, tq, nvalid_fn, tile_fn, lam_init):
    lp = kc.shape[2]
    lb = KEY_BLOCK
    nkb = lp // lb
    nq = t // tq
    row = lambda b, i: b * nq + i
    kblk = lambda i, j: jnp.minimum(j, nvalid_fn(i) - 1)
    qspec = pl.BlockSpec((H_C, tq, DH), lambda b, i, j: (0, row(b, i), 0))
    kspec = lambda nh, w: pl.BlockSpec((1, nh, lb, w), lambda b, i, j: (b, 0, kblk(i, j), 0))
    return pl.pallas_call(
        functools.partial(_attn_body, tq=tq, nvalid_fn=nvalid_fn, lam_init=lam_init),
        grid=(nbatch, nq, nkb),
        in_specs=[qspec, qspec, kspec(H_C, DH), kspec(H_C, DH), kspec(2 * H_D, DH), kspec(H_D, 2 * DH),
                  pl.BlockSpec((1, 1, tq, lb), lambda b, i, j: (b, kblk(i, j), i, 0)),
                  pl.BlockSpec((1, N_HEADS_BIAS, tq, lb), lambda b, i, j: (tile_fn(i, kblk(i, j)), 0, 0, 0)),
                  pl.BlockSpec((1, 1), lambda b, i, j: (0, 0)),
                  pl.BlockSpec((1, 2 * DH), lambda b, i, j: (0, 0))],
        out_specs=pl.BlockSpec((tq, W_C + W_D), lambda b, i, j: (row(b, i), 0)),
        out_shape=jax.ShapeDtypeStruct((nbatch * t, W_C + W_D), BF16),
        scratch_shapes=[pltpu.VMEM((N_MAPS, tq, LANES), F32), pltpu.VMEM((N_MAPS, tq, LANES), F32),
                        pltpu.VMEM((H_C, tq, DH), F32), pltpu.VMEM((2 * H_D, tq, 2 * DH), F32)],
        compiler_params=_cparams("parallel", "parallel", "arbitrary"),
        name="attend",
    )(qc, qd, kc, vc, kd, vd, mask, btiles, lam, g)


PROMPT_TQ = 256
PROMPT_LB = 512
SEL_ROWS = 256
TIE_ROWS = 128
MAPS_PER_DOT = 4


def _proj1_t_body(x_ref, g_ref, w_ref, wt_ref, kc32_ref, vc32_ref, kd32_ref, vd32_ref, ki32_ref,
                  kc_ref, kd_ref, ki_ref, qct_ref, qdt_ref, qit_ref, vct_ref, vdt_ref, wit_ref):
    xn32 = _rms(x_ref[...], g_ref[...])
    xn = xn32.astype(BF16)
    xnt = xn32.T.astype(BF16)
    for c, (full_ref, head_ref) in enumerate(((kc32_ref, kc_ref), (vc32_ref, None), (kd32_ref, kd_ref), (vd32_ref, None))):
        z = _dot(xn, w_ref[:, c * 512:(c + 1) * 512])
        full_ref[...] = z
        if head_ref is not None:
            head_ref[...] = z.astype(BF16)
    z = _dot(xn, w_ref[:, 2048:2048 + LANES])
    ki32_ref[...] = z[:, 0:D_I]
    ki_ref[...] = z[:, 0:D_I].astype(BF16)
    for c, t_ref in enumerate((qct_ref, qdt_ref, qit_ref, vct_ref)):
        zt = _dot(wt_ref[c * 512:(c + 1) * 512, :], xnt)
        for h in range(8):
            t_ref[h] = zt[h * DH:(h + 1) * DH, :].astype(BF16)
    zt = _dot(wt_ref[2048:2560, :], xnt)
    for h in range(H_D):
        vdt_ref[h] = zt[h * 2 * DH:(h + 1) * 2 * DH, :].astype(BF16)
    zt = _dot(wt_ref[2560:2560 + LANES, :], xnt)
    wit_ref[...] = zt[0:H_I, :]


def _proj1_t(x, g, w, wt, rows):
    n, d = x.shape
    full = lambda width: (jax.ShapeDtypeStruct((n, width), F32), pl.BlockSpec((rows, width), lambda i: (i, 0)))
    packed = lambda width: (jax.ShapeDtypeStruct((n, width), BF16), pl.BlockSpec((rows, width), lambda i: (i, 0)))
    heads_t = lambda nh, width: (jax.ShapeDtypeStruct((nh, width, n), BF16),
                                 pl.BlockSpec((nh, width, rows), lambda i: (0, 0, i)))
    outs = [full(512), full(512), full(512), full(512), full(D_I),
            packed(512), packed(512), packed(D_I),
            heads_t(8, DH), heads_t(8, DH), heads_t(8, DH), heads_t(8, DH), heads_t(H_D, 2 * DH),
            (jax.ShapeDtypeStruct((H_I, n), F32), pl.BlockSpec((H_I, rows), lambda i: (0, i)))]
    return pl.pallas_call(
        _proj1_t_body,
        grid=(n // rows,),
        in_specs=[pl.BlockSpec((rows, d), lambda i: (i, 0)),
                  pl.BlockSpec((1, d), lambda i: (0, 0)),
                  pl.BlockSpec(w.shape, lambda i: (0, 0)),
                  pl.BlockSpec(wt.shape, lambda i: (0, 0))],
        out_specs=[o[1] for o in outs],
        out_shape=[o[0] for o in outs],
        compiler_params=_cparams("parallel"),
        name="proj1_t",
    )(x, g, w, wt)


def _sublane_all(x8, op):
    for shift in (4, 2, 1):
        x8 = op(x8, pltpu.roll(x8, shift, 0))
    return x8


def _select_t_body(qit_ref, wit_ref, ve_ref, ki_ref, out_ref, keys_ref, gmax_ref, *, tq, lp, topk):
    rb = SEL_ROWS
    nv = (pl.program_id(0) + 1) * (tq // rb)
    wit = wit_ref[...]
    ve = ve_ref[...]
    rows = lax.broadcasted_iota(I32, (rb, tq), 0)
    gmax_ref[...] = jnp.full((rb, tq), INT_MIN, I32)

    def score_chunk(c, carry):
        r0 = pl.multiple_of(c * rb, rb)
        kb = ki_ref[pl.ds(r0, rb), :]
        sc = jnp.zeros((rb, tq), F32)
        for h in range(H_I):
            sc = sc + wit[h:h + 1, :] * jnp.maximum(_dot(kb, qit_ref[h]), 0.0)
        bits = lax.bitcast_convert_type(sc, I32)
        key = bits ^ ((bits >> 31) & INT_MAX)
        key = jnp.where(sc == 0.0, 0, key)
        key = jnp.where(rows + r0 < ve, key, INT_MIN)
        keys_ref[pl.ds(r0, rb), :] = key
        gmax_ref[...] = jnp.maximum(gmax_ref[...], key)
        return carry

    lax.fori_loop(0, nv, score_chunk, 0)

    def count_ge(mid):
        midb = jnp.broadcast_to(mid, (SUBLANES, tq))

        def body(c, accs):
            kc = keys_ref[pl.ds(pl.multiple_of(c * rb, rb), rb), :]
            accs = list(accs)
            for g in range(rb // SUBLANES):
                a = g % len(accs)
                accs[a] = accs[a] + jnp.where(kc[g * SUBLANES:(g + 1) * SUBLANES, :] >= midb, 1, 0)
            return tuple(accs)

        accs = lax.fori_loop(0, nv, body, (jnp.zeros((SUBLANES, tq), I32),) * 4)
        return _sublane_all(accs[0] + accs[1] + accs[2] + accs[3], jnp.add)[0:1, :]

    floor_avg = lambda a, b: (a >> 1) + (b >> 1) + (a & b & 1)

    def pending(lo, hi, clo):
        open_ = jnp.logical_and(floor_avg(lo, hi) != lo, clo > topk)
        return jnp.max(jnp.where(open_, 1.0, 0.0))

    def bisect(st):
        lo, hi, clo, chi, _ = st
        mid = floor_avg(lo, hi)
        cnt = count_ge(mid)
        active = jnp.logical_and(mid != lo, clo > topk)
        up = jnp.logical_and(active, cnt >= topk)
        dn = jnp.logical_and(active, cnt < topk)
        lo, hi = jnp.where(up, mid, lo), jnp.where(dn, mid, hi)
        clo, chi = jnp.where(up, cnt, clo), jnp.where(dn, cnt, chi)
        return lo, hi, clo, chi, pending(lo, hi, clo)

    g8 = gmax_ref[0:SUBLANES, :]
    h8 = g8
    for g in range(1, rb // SUBLANES):
        blk = gmax_ref[g * SUBLANES:(g + 1) * SUBLANES, :]
        g8 = jnp.minimum(g8, blk)
        h8 = jnp.maximum(h8, blk)
    lo0 = _sublane_all(g8, jnp.minimum)[0:1, :]
    hi0 = _sublane_all(h8, jnp.maximum)[0:1, :] + 1
    clo0 = jnp.where(lo0 == INT_MIN, ve, count_ge(lo0))
    lo, hi, clo, chi, _ = lax.while_loop(
        lambda st: st[4] > 0.5, bisect, (lo0, hi0, clo0, jnp.zeros((1, tq), I32), pending(lo0, hi0, clo0)))

    need = jnp.where(clo > topk, topk - chi, ALL_TIES)
    need = jnp.where(lo == INT_MIN, 0, need).astype(F32)
    tr = TIE_ROWS
    lob = jnp.broadcast_to(lo, (tr, tq))
    needb = jnp.broadcast_to(need, (tr, tq))
    r = lax.broadcasted_iota(I32, (tr, tr), 0)
    c = lax.broadcasted_iota(I32, (tr, tr), 1)
    tri = jnp.where(c <= r, 1.0, 0.0).astype(BF16)

    def mask_chunk(cidx, seen):
        r0 = pl.multiple_of(cidx * tr, tr)
        k = keys_ref[pl.ds(r0, tr), :]
        eq = jnp.where(k == lob, 1.0, 0.0)
        cum = _dot(tri, eq.astype(BF16))
        take = jnp.where(seen + cum - eq < needb, eq, 0.0)
        sel = jnp.where(k > lob, 1.0, take)
        out_ref[pl.ds(r0, tr), :] = jnp.where(sel > 0.5, 0.0, NEG_INF).astype(out_ref.dtype)
        return seen + cum[tr - 1:tr, :]

    def plain_chunk(cidx, carry):
        r0 = pl.multiple_of(cidx * rb, rb)
        k = keys_ref[pl.ds(r0, rb), :]
        hit = jnp.where(k == INT_MIN, NEG_INF, 0.0)
        out_ref[pl.ds(r0, rb), :] = jnp.where(k >= lo, hit, NEG_INF).astype(out_ref.dtype)
        return carry

    any_tie = jnp.max(jnp.where(clo > topk, 1.0, 0.0)) > 0.5
    lax.cond(any_tie,
             lambda: lax.fori_loop(0, nv * (rb // tr), mask_chunk, jnp.zeros((1, tq), F32)),
             lambda: lax.fori_loop(0, nv, plain_chunk, jnp.zeros((1, tq), F32)))

    def fill_chunk(cidx, carry):
        out_ref[pl.ds(pl.multiple_of(cidx * rb, rb), rb), :] = jnp.full((rb, tq), NEG_INF, out_ref.dtype)
        return carry

    lax.fori_loop(nv, lp // rb, fill_chunk, 0)


def _select_t(qit, wit, ve, ki, topk):
    t = qit.shape[2]
    tq = PROMPT_TQ
    return pl.pallas_call(
        functools.partial(_select_t_body, tq=tq, lp=t, topk=topk),
        grid=(t // tq,),
        in_specs=[pl.BlockSpec((H_I, D_I, tq), lambda i: (0, 0, i)),
                  pl.BlockSpec((H_I, tq), lambda i: (0, i)),
                  pl.BlockSpec((1, tq), lambda i: (0, i)),
                  pl.BlockSpec((t, D_I), lambda i: (0, 0))],
        out_specs=pl.BlockSpec((t, tq), lambda i: (0, i)),
        out_shape=jax.ShapeDtypeStruct((t, t), BF16),
        scratch_shapes=[pltpu.VMEM((t, tq), I32), pltpu.VMEM((SEL_ROWS, tq), I32)],
        compiler_params=_cparams("parallel"),
        name="index_select_t",
    )(qit, wit, ve, ki)


def _attn_t_body(qct_ref, qdt_ref, kc_ref, kd_ref, vct_ref, vdt_ref, mask_ref, bt_ref, lam_ref, g_ref,
                 o_ref, m_ref, l_ref, accc_ref, accd_ref, ot_ref, qbd_ref, s_ref, *, tq, per, e_far, lam_init):
    i = pl.program_id(0)
    j = pl.program_id(1)
    e = i - j * per
    grp = MAPS_PER_DOT

    @pl.when(j == 0)
    def _():
        m_ref[...] = jnp.full(m_ref.shape, NEG_INF, F32)
        l_ref[...] = jnp.zeros_like(l_ref)
        accc_ref[...] = jnp.zeros_like(accc_ref)
        accd_ref[...] = jnp.zeros_like(accd_ref)
        qbd_ref[...] = jnp.zeros_like(qbd_ref)
        for mp in range(N_MAPS):
            q = qct_ref[mp] if mp < H_C else qdt_ref[mp - H_C]
            a = mp % grp
            qbd_ref[mp // grp, a * DH:(a + 1) * DH, a * tq:(a + 1) * tq] = q

    def logits(g):
        k_ref = kc_ref if g < H_C // grp else kd_ref
        half = g % (H_C // grp)
        return _dot(k_ref[:, half * grp * DH:(half + 1) * grp * DH], qbd_ref[g])

    def step(near):
        for g in range(N_MAPS // grp):
            s_ref[g] = logits(g)
        for mp in range(N_MAPS):
            g, a = mp // grp, mp % grp
            cols = slice(a * tq, (a + 1) * tq)
            if mp < H_C:
                bias_idx, vt, acc_ref, a_idx = mp, vct_ref[mp], accc_ref, mp
            else:
                dm = mp - H_C
                bias_idx, vt, acc_ref, a_idx = H_C + dm // 2, vdt_ref[dm // 2], accd_ref, dm

            def biased():
                s = s_ref[g, :, cols]
                if mp < H_C:
                    s = s + mask_ref[...].astype(F32)
                if near:
                    s = s + bt_ref[0, bias_idx]
                return s

            m_prev = m_ref[mp]
            m_new = jnp.maximum(m_prev, jnp.max(biased(), axis=0, keepdims=True))
            alpha = jnp.exp2(m_prev - m_new)
            p = jnp.exp2(biased() - m_new)
            l_ref[mp] = alpha * l_ref[mp] + jnp.sum(p, axis=0, keepdims=True)
            m_ref[mp] = m_new
            acc_ref[a_idx] = alpha * acc_ref[a_idx] + _dot(vt, p.astype(BF16))

    valid = j * per <= i
    pl.when(jnp.logical_and(valid, e < e_far))(functools.partial(step, True))
    pl.when(jnp.logical_and(valid, e >= e_far))(functools.partial(step, False))

    @pl.when(j == pl.num_programs(1) - 1)
    def _():
        for h in range(H_C):
            ot_ref[h * DH:(h + 1) * DH, :] = accc_ref[h] / l_ref[h]
        lam = lam_ref[...]
        g = g_ref[...]
        for h in range(H_D):
            a0 = accd_ref[2 * h] / l_ref[H_C + 2 * h]
            a1 = accd_ref[2 * h + 1] / l_ref[H_C + 2 * h + 1]
            od = a0 - lam * a1
            od = od * lax.rsqrt(jnp.mean(od * od, axis=0, keepdims=True) + EPS) * g * (1.0 - lam_init)
            ot_ref[W_C + h * 2 * DH:W_C + (h + 1) * 2 * DH, :] = od
        o_ref[...] = ot_ref[...].T.astype(o_ref.dtype)


def _attend_t(qct, qdt, kc, kd, vct, vdt, mask, btiles, lam, g, lam_init):
    t = qct.shape[2]
    tq, lb = PROMPT_TQ, PROMPT_LB
    per = lb // tq
    e_far = btiles.shape[0]
    kblk = lambda i, j: jnp.minimum(j, i // per)
    qspec = pl.BlockSpec((H_C, DH, tq), lambda i, j: (0, 0, i))
    kspec = pl.BlockSpec((lb, H_C * DH), lambda i, j: (kblk(i, j), 0))
    return pl.pallas_call(
        functools.partial(_attn_t_body, tq=tq, per=per, e_far=e_far, lam_init=lam_init),
        grid=(t // tq, t // lb),
        in_specs=[qspec, qspec, kspec, kspec,
                  pl.BlockSpec((H_C, DH, lb), lambda i, j: (0, 0, kblk(i, j))),
                  pl.BlockSpec((H_D, 2 * DH, lb), lambda i, j: (0, 0, kblk(i, j))),
                  pl.BlockSpec((lb, tq), lambda i, j: (kblk(i, j), i)),
                  pl.BlockSpec((1, N_HEADS_BIAS, lb, tq),
                               lambda i, j: (jnp.clip(i - kblk(i, j) * per, 0, e_far - 1), 0, 0, 0)),
                  pl.BlockSpec((1, 1), lambda i, j: (0, 0)),
                  pl.BlockSpec((2 * DH, 1), lambda i, j: (0, 0))],
        out_specs=pl.BlockSpec((tq, W_C + W_D), lambda i, j: (i, 0)),
        out_shape=jax.ShapeDtypeStruct((t, W_C + W_D), BF16),
        scratch_shapes=[pltpu.VMEM((N_MAPS, 1, tq), F32), pltpu.VMEM((N_MAPS, 1, tq), F32),
                        pltpu.VMEM((H_C, DH, tq), F32), pltpu.VMEM((2 * H_D, 2 * DH, tq), F32),
                        pltpu.VMEM((W_C + W_D, tq), F32),
                        pltpu.VMEM((N_MAPS // MAPS_PER_DOT, MAPS_PER_DOT * DH, MAPS_PER_DOT * tq), BF16),
                        pltpu.VMEM((N_MAPS // MAPS_PER_DOT, lb, MAPS_PER_DOT * tq), F32)],
        compiler_params=_cparams("parallel", "arbitrary"),
        name="attend_t",
    )(qct, qdt, kc, kd, vct, vdt, mask, btiles, lam, g)


def _attn_s_body(qc_ref, qd_ref, ck_ref, cv_ref, dk_ref, dv_ref, nck_ref, ncv_ref, ndk_ref, ndv_ref,
                 mask_ref, bias_ref, lam_ref, g_ref, o_ref, m_ref, l_ref, acc_ref, *, ts, ncache, lam_init):
    j = pl.program_id(1)
    rows = H_C * ts

    @pl.when(j == 0)
    def _():
        m_ref[...] = jnp.full(m_ref.shape, NEG_INF, F32)
        l_ref[...] = jnp.zeros_like(l_ref)
        acc_ref[...] = jnp.zeros_like(acc_ref)

    def step(kc, vc, kd, vd):
        sel = jnp.tile(mask_ref[0, 0].astype(F32), (H_C, 1))
        bias = bias_ref[0]
        s_c = _dot_nt(qc_ref[0], kc.astype(BF16)) + sel + bias[0:rows]
        s_d = _dot_nt(qd_ref[0], kd.astype(BF16)) + bias[rows:2 * rows]
        for idx, (s, v) in enumerate(((s_c, vc), (s_d, vd))):
            m_prev = m_ref[idx]
            m_new = jnp.maximum(m_prev, jnp.max(s, axis=-1, keepdims=True))
            alpha = jnp.exp(m_prev - m_new)
            p = jnp.exp(s - m_new)
            l_ref[idx] = alpha * l_ref[idx] + jnp.sum(p, axis=-1, keepdims=True)
            m_ref[idx] = m_new
            acc_ref[idx] = alpha * acc_ref[idx] + _dot(p.astype(BF16), v.astype(BF16))

    @pl.when(j < ncache)
    def _():
        step(ck_ref[0], cv_ref[0], dk_ref[0], dv_ref[0])

    @pl.when(j == ncache)
    def _():
        step(nck_ref[0], ncv_ref[0], ndk_ref[0], ndv_ref[0])
        for h in range(H_C):
            r = slice(h * ts, (h + 1) * ts)
            o_ref[:, h * DH:(h + 1) * DH] = (acc_ref[0, r, h * DH:(h + 1) * DH] / l_ref[0, r, :]).astype(o_ref.dtype)
        lam = lam_ref[...]
        g = g_ref[...]
        for h in range(H_D):
            r0 = slice(2 * h * ts, (2 * h + 1) * ts)
            r1 = slice((2 * h + 1) * ts, (2 * h + 2) * ts)
            c = slice(h * 2 * DH, (h + 1) * 2 * DH)
            od = acc_ref[1, r0, c] / l_ref[1, r0, :] - lam * (acc_ref[1, r1, c] / l_ref[1, r1, :])
            od = _rms(od, g) * (1.0 - lam_init)
            o_ref[:, W_C + h * 2 * DH:W_C + (h + 1) * 2 * DH] = od.astype(o_ref.dtype)


def _attend_s(qbd_c, qbd_d, caches, news, mask, bias, lam, g, lam_init):
    nb, rows, _ = qbd_c.shape
    ts = rows // H_C
    lb = KEY_BLOCK
    ncache = caches[0].shape[1] // lb
    qspec = pl.BlockSpec((1, rows, W_C), lambda b, j: (b, 0, 0))
    cspec = pl.BlockSpec((1, lb, W_C), lambda b, j: (b, jnp.minimum(j, ncache - 1), 0))
    nspec = pl.BlockSpec((1, lb, W_C), lambda b, j: (b, 0, 0))
    return pl.pallas_call(
        functools.partial(_attn_s_body, ts=ts, ncache=ncache, lam_init=lam_init),
        grid=(nb, ncache + 1),
        in_specs=[qspec, qspec, cspec, cspec, cspec, cspec, nspec, nspec, nspec, nspec,
                  pl.BlockSpec((1, 1, ts, lb), lambda b, j: (b, j, 0, 0)),
                  pl.BlockSpec((1, 2 * rows, lb), lambda b, j: (j, 0, 0)),
                  pl.BlockSpec((1, 1), lambda b, j: (0, 0)),
                  pl.BlockSpec((1, 2 * DH), lambda b, j: (0, 0))],
        out_specs=pl.BlockSpec((ts, W_C + W_D), lambda b, j: (b, 0)),
        out_shape=jax.ShapeDtypeStruct((nb * ts, W_C + W_D), BF16),
        scratch_shapes=[pltpu.VMEM((2, rows, 1), F32), pltpu.VMEM((2, rows, 1), F32),
                        pltpu.VMEM((2, rows, W_C), F32)],
        compiler_params=_cparams("parallel", "arbitrary"),
        name="attend_s",
    )(qbd_c, qbd_d, *caches, *news, mask, bias, lam, g)


def _out_router_body(x_ref, o_ref, w_ref, g_ref, r_ref, x3_ref, xn_ref, gate_ref):
    x3 = x_ref[...] + _dot(o_ref[...], w_ref[...])
    x3_ref[...] = x3
    xn = _rms(x3, g_ref[...]).astype(BF16)
    xn_ref[...] = xn
    logits = _dot(xn, r_ref[...])
    lane = lax.broadcasted_iota(I32, logits.shape, 1)
    logits = jnp.where(lane < N_EXP, logits, -jnp.inf)
    m1 = jnp.max(logits, axis=-1, keepdims=True)
    i1 = jnp.min(jnp.where(logits == m1, lane, LANES), axis=-1, keepdims=True)
    rest = jnp.where(lane == i1, -jnp.inf, logits)
    m2 = jnp.max(rest, axis=-1, keepdims=True)
    i2 = jnp.min(jnp.where(rest == m2, lane, LANES), axis=-1, keepdims=True)
    e = jnp.exp(m2 - m1)
    g1 = 1.0 / (1.0 + e)
    g2 = e / (1.0 + e)
    gate_ref[...] = jnp.where(lane == i1, g1, 0.0) + jnp.where(lane == i2, g2, 0.0)


def _out_router(x, o, w, g, router, rows):
    n, d = x.shape
    return pl.pallas_call(
        _out_router_body,
        grid=(n // rows,),
        in_specs=[pl.BlockSpec((rows, d), lambda i: (i, 0)),
                  pl.BlockSpec((rows, d), lambda i: (i, 0)),
                  pl.BlockSpec(w.shape, lambda i: (0, 0)),
                  pl.BlockSpec((1, d), lambda i: (0, 0)),
                  pl.BlockSpec(router.shape, lambda i: (0, 0))],
        out_specs=[pl.BlockSpec((rows, d), lambda i: (i, 0)),
                   pl.BlockSpec((rows, d), lambda i: (i, 0)),
                   pl.BlockSpec((rows, LANES), lambda i: (i, 0))],
        out_shape=[jax.ShapeDtypeStruct((n, d), F32), jax.ShapeDtypeStruct((n, d), BF16),
                   jax.ShapeDtypeStruct((n, LANES), F32)],
        compiler_params=_cparams("parallel"),
        name="out_router",
    )(x, o, w, g, router)


def _moe_body(x_ref, xn_ref, gate_ref, gf_ref, wg_ref, wu_ref, wd_ref, y_ref, acc_ref):
    e = pl.program_id(1)
    k = pl.program_id(2)

    @pl.when(jnp.logical_and(e == 0, k == 0))
    def _():
        acc_ref[...] = jnp.zeros_like(acc_ref)

    gate = gate_ref[...]
    lane = lax.broadcasted_iota(I32, gate.shape, 1)
    ge = jnp.sum(jnp.where(lane == e, gate, 0.0), axis=-1, keepdims=True)
    xn = xn_ref[...]
    h = jax.nn.silu(_dot(xn, wg_ref[0])) * _dot(xn, wu_ref[0])
    acc_ref[...] += _dot((ge * h).astype(BF16), wd_ref[0])

    @pl.when(jnp.logical_and(e == pl.num_programs(1) - 1, k == pl.num_programs(2) - 1))
    def _():
        y_ref[...] = _rms(x_ref[...] + acc_ref[...], gf_ref[...])


def _moe(x, xn, gate, gf, wg, wu, wd, rows, fb):
    n, d = x.shape
    ne, _, dff = wg.shape
    return pl.pallas_call(
        _moe_body,
        grid=(n // rows, ne, dff // fb),
        in_specs=[pl.BlockSpec((rows, d), lambda i, e, k: (i, 0)),
                  pl.BlockSpec((rows, d), lambda i, e, k: (i, 0)),
                  pl.BlockSpec((rows, LANES), lambda i, e, k: (i, 0)),
                  pl.BlockSpec((1, d), lambda i, e, k: (0, 0)),
                  pl.BlockSpec((1, d, fb), lambda i, e, k: (e, 0, k)),
                  pl.BlockSpec((1, d, fb), lambda i, e, k: (e, 0, k)),
                  pl.BlockSpec((1, fb, d), lambda i, e, k: (e, k, 0))],
        out_specs=pl.BlockSpec((rows, d), lambda i, e, k: (i, 0)),
        out_shape=jax.ShapeDtypeStruct((n, d), F32),
        scratch_shapes=[pltpu.VMEM((rows, d), F32)],
        compiler_params=_cparams("parallel", "arbitrary", "arbitrary"),
        name="moe",
    )(x, xn, gate, gf, wg, wu, wd)


def _t5_bucket(rel):
    half = N_BUCKETS // 2
    max_exact = half // 2
    ret = np.where(rel > 0, half, 0)
    n = np.abs(rel)
    nf = np.maximum(n, 1).astype(np.float32)
    large = max_exact + (np.log(nf / np.float32(max_exact)) / np.float32(math.log(MAX_DIST / max_exact))
                         * np.float32(half - max_exact)).astype(np.int32)
    large = np.minimum(large, half - 1)
    return (ret + np.where(n < max_exact, n, large)).astype(np.int32)


def _bias_tile(rel_bias, q_pos, k_pos, k_real):
    rel = k_pos[None, :] - q_pos[:, None]
    bias = jnp.transpose(rel_bias[_t5_bucket(rel)].astype(F32), (2, 0, 1))
    ok = np.logical_and(k_pos[None, :] // CHUNK <= q_pos[:, None] // CHUNK, k_pos[None, :] < k_real)
    return jnp.where(ok[None], bias, NEG_INF)


def _in1_plan():
    scale = DH ** -0.5
    heads64 = lambda o, sc=1.0: [(o, h, h * DH, (h + 1) * DH, sc) for h in range(8)]
    plan = [
        ((0, 512), heads64(0, scale)),
        ((512, 1024), [(1, None, 0, 512, 1.0)] + heads64(2)),
        ((1024, 1536), [(3, None, 0, 512, 1.0)] + heads64(4)),
        ((1536, 2048), heads64(5, scale)),
        ((2048, 2560), [(6, None, 0, 512, 1.0)] + heads64(7)),
        ((2560, 3072), [(8, None, 0, 512, 1.0)] + [(9, h, h * 128, (h + 1) * 128, 1.0) for h in range(H_D)]),
        ((3072, 3584), heads64(10)),
        ((3584, 3712), [(11, None, 0, D_I, 1.0), (12, None, 0, D_I, 1.0), (13, None, D_I, D_I + H_I, 1.0)]),
    ]
    out_defs = [(DH, BF16, 8), (512, F32, None), (DH, BF16, 8), (512, F32, None), (DH, BF16, 8),
                (DH, BF16, 8), (512, F32, None), (DH, BF16, 8), (512, F32, None), (2 * DH, BF16, H_D),
                (D_I, BF16, 8), (D_I, F32, None), (D_I, BF16, None), (H_I, F32, None)]
    return plan, out_defs


def _layer0(x, hist, rb, emit_v, p):
    n = x.shape[0]
    rows = _largest_divisor(n, 512, 16)
    plan = [((0, p["w_in0"].shape[1]), [(0, None, 0, p["w_in0"].shape[1], 1.0)])]
    (z,) = _rms_proj(x, p["ln_mix0"], p["w_in0"], plan, [(p["w_in0"].shape[1], F32, None)], rows)
    ws = p["ws_prompt"] if hist is None else p["ws_sample"]
    bs = p["bs_prompt"] if hist is None else p["bs_sample"]
    outs = _mixer_ab(x, z, hist, p["gmlp_ln_g"], p["gmlp_ln_b"], ws, bs, p["conv_k"], p["w_out0"], rb, emit_v)
    x1 = outs[0]
    x2 = _ffn(x1, p["ln_ffn0"], p["ffn_wg"], p["ffn_wu"], p["ffn_wd"], rows, p["ffn_fb"])
    return (x2,) + tuple(outs[1:])


def _layer1_tail(x, o, p):
    n = x.shape[0]
    rows = _largest_divisor(n, 512, 16)
    x3, xn, gate = _out_router(x, o, p["w_out1"], p["ln_ffn1"], p["router"], rows)
    return _moe(x3, xn, gate, p["ln_final"], p["exp_wg"], p["exp_wu"], p["exp_wd"], rows, p["exp_fb"])


def kernel(x_prompt, x_sample, state_b_conv, cache_c_k, cache_c_v, cache_idx_k, cache_d_k, cache_d_v, rel_bias, ln_mix, ln_ffn, ln_final, w_in0, gmlp_ln_g, gmlp_ln_b, gmlp_ws, gmlp_bs, conv_k, w_out0, ffn_wg, ffn_wu, ffn_wd, w_in1, lam_qk, subln_g, w_out1, router, exp_wg, exp_wu, exp_wd):
    bp, seq, d = x_prompt.shape
    bs_, ts, _ = x_sample.shape
    past = cache_c_k.shape[2]
    assert bp == 1 and ln_mix.shape[0] == 2 and seq % PROMPT_LB == 0 and ts % SUBLANES == 0 and ts <= CHUNK
    assert past % KEY_BLOCK == 0
    gmlp_chunk = gmlp_ws.shape[-1]
    lam_init = 0.8 - 0.6 * math.exp(-0.3 * 1)

    def ws_masked(rows):
        r = jnp.arange(rows)
        ok = (r[None, :] // CHUNK) <= (r[:, None] // CHUNK)
        return jnp.where(ok[None], gmlp_ws[0][:, :rows, :rows], 0.0).astype(BF16)

    in1_pad = (-w_in1.shape[2]) % LANES
    lf = lam_qk[0].astype(F32)
    lam = (jnp.exp(jnp.sum(lf[0] * lf[1])) - jnp.exp(jnp.sum(lf[2] * lf[3])) + lam_init).reshape(1, 1)
    p = {
        "ln_mix0": ln_mix[0:1], "ln_ffn0": ln_ffn[0:1], "ln_mix1": ln_mix[1:2], "ln_ffn1": ln_ffn[1:2],
        "ln_final": ln_final.reshape(1, d),
        "w_in0": w_in0[0].astype(BF16),
        "gmlp_ln_g": gmlp_ln_g[0].reshape(1, W_A), "gmlp_ln_b": gmlp_ln_b[0].reshape(1, W_A),
        "ws_prompt": ws_masked(gmlp_chunk), "ws_sample": ws_masked(ts),
        "bs_prompt": gmlp_bs[0][:, :gmlp_chunk].T, "bs_sample": gmlp_bs[0][:, :ts].T,
        "conv_k": conv_k[0], "w_out0": w_out0[0].astype(BF16),
        "ffn_wg": ffn_wg[0].astype(BF16), "ffn_wu": ffn_wu[0].astype(BF16), "ffn_wd": ffn_wd[0].astype(BF16),
        "ffn_fb": _largest_divisor(ffn_wg.shape[2], 1408, LANES),
        "w_in1": jnp.pad(w_in1[0], ((0, 0), (0, in1_pad))).astype(BF16),
        "w_out1": w_out1[0].astype(BF16),
        "router": jnp.pad(router[0], ((0, 0), (0, LANES - N_EXP))).astype(BF16),
        "exp_wg": exp_wg[0].astype(BF16), "exp_wu": exp_wu[0].astype(BF16), "exp_wd": exp_wd[0].astype(BF16),
        "exp_fb": _largest_divisor(exp_wg.shape[3], 896, LANES),
    }
    g_sub = subln_g[0].reshape(1, 2 * DH)
    plan1, out_defs1 = _in1_plan()

    xp = x_prompt.reshape(seq, d)
    xs = x_sample.reshape(bs_ * ts, d)
    xp, p_tail = _layer0(xp, None, gmlp_chunk, False, p)
    hist = jnp.pad(state_b_conv[0], ((0, 0), (SUBLANES - 2, 0), (0, 0)))
    xs, s_tail, s_av = _layer0(xs, hist, ts, True, p)
    p_b_conv = p_tail[-1, SUBLANES - 2:, :].reshape(1, 1, 2, W_B)
    s_b_conv = s_tail[:, SUBLANES - 2:, :].reshape(1, bs_, 2, W_B)
    s_a_v = s_av.reshape(1, bs_, ts, W_A)

    lb = KEY_BLOCK
    w1 = w_in1[0]
    off = np.concatenate([[0], np.cumsum(IN1_SIZES)])
    field = lambda k: w1[:, off[k]:off[k + 1]]
    padc = lambda a: jnp.pad(a, ((0, 0), (0, LANES - a.shape[1])))
    log2e = math.log2(math.e)
    qscale = DH ** -0.5 * log2e
    w_norm = jnp.concatenate([field(1), field(2), field(4), field(5), padc(field(7))], axis=1).astype(BF16)
    w_tran = jnp.concatenate([field(0) * qscale, field(3) * qscale, field(6), field(2), field(5),
                              padc(field(8))], axis=1).T.astype(BF16)
    (kc32, vc32, kd32, vd32, ki32, kc, kd, ki, qct, qdt, qit, vct, vdt, wit) = _proj1_t(
        xp, p["ln_mix1"], w_norm, w_tran, PROMPT_TQ)
    pos = jnp.arange(seq, dtype=I32)
    ve = ((pos // CHUNK + 1) * CHUNK).reshape(1, seq)
    mask = _select_t(qit, wit, ve, ki, min(TOPK_MAX, seq // 4))
    e_far = -(-(PROMPT_LB - 1 + MAX_DIST) // PROMPT_TQ)
    kpos = np.arange(PROMPT_LB)
    qpos = np.arange(PROMPT_TQ)
    base = e_far * PROMPT_TQ
    rel_near = (rel_bias - rel_bias[N_BUCKETS // 2 - 1:N_BUCKETS // 2]) * log2e
    btiles = jnp.stack([jnp.transpose(_bias_tile(rel_near, base + e * PROMPT_TQ + qpos, base + kpos, base + PROMPT_LB),
                                      (0, 2, 1)) for e in range(e_far)])
    op = _attend_t(qct, qdt, kc, kd, vct, vdt, mask, btiles, lam, subln_g[0].reshape(2 * DH, 1), lam_init)
    y_prompt = _layer1_tail(xp, op, p).reshape(1, seq, d)

    sr = _rms_proj(xs, p["ln_mix1"], p["w_in1"], plan1, out_defs1, _largest_divisor(bs_ * ts, 256, 16))
    sqc, skc32, skc, svc32, svc, sqd, skd32, skd, svd32, svd, sqi, ski32, ski, swi = sr
    nk = past + ts
    lps = -(-nk // lb) * lb

    kis = jnp.pad(jnp.concatenate([cache_idx_k[0].astype(BF16), ski.reshape(bs_, ts, D_I)], axis=1),
                  ((0, 0), (0, lps - nk), (0, 0)))
    ves = jnp.full((bs_ * ts, 1), nk, I32)
    smask = _select(sqi, swi, ves, kis, bs_, ts, ts, min(TOPK_MAX, nk // 4), lambda i: lps // lb)
    sq_pos = past + np.arange(ts)
    sbt = jnp.stack([_bias_tile(rel_bias, sq_pos, j * lb + np.arange(lb), nk) for j in range(lps // lb)])
    sbias = jnp.concatenate([sbt[:, :H_C], jnp.repeat(sbt[:, H_C:], 2, axis=1)], axis=1).reshape(lps // lb, N_MAPS * ts, lb)

    def block_diag(q):
        qb = jnp.transpose(q.reshape(H_C, bs_, ts, DH), (1, 0, 2, 3))
        eye = jnp.eye(H_C, dtype=q.dtype)
        return (qb[:, :, :, None, :] * eye[None, :, None, :, None]).reshape(bs_, H_C * ts, W_C)

    cache_rows = lambda c: c.reshape(bs_, past, W_C)
    new_rows = lambda a: jnp.pad(a.reshape(bs_, ts, W_C), ((0, 0), (0, lb - ts), (0, 0)))
    os_ = _attend_s(block_diag(sqc), block_diag(sqd),
                    [cache_rows(c[0]) for c in (cache_c_k, cache_c_v, cache_d_k, cache_d_v)],
                    [new_rows(a) for a in (skc32, svc32, skd32, svd32)],
                    smask, sbias, lam, g_sub, lam_init)
    y_sample = _layer1_tail(xs, os_, p).reshape(bs_, ts, d)

    r5 = lambda a, n, t, *tail: a.reshape((1, n, t) + tail)
    return (y_prompt, y_sample, p_b_conv,
            r5(kc32, 1, seq, H_C, DH), r5(vc32, 1, seq, H_C, DH), r5(ki32, 1, seq, D_I),
            r5(kd32, 1, seq, H_D, 2, DH), r5(vd32, 1, seq, H_D, 2 * DH),
            s_a_v, s_b_conv,
            r5(skc32, bs_, ts, H_C, DH), r5(svc32, bs_, ts, H_C, DH), r5(ski32, bs_, ts, D_I),
            r5(skd32, bs_, ts, H_D, 2, DH), r5(svd32, bs_, ts, H_D, 2 * DH))
```

```python
import functools
import math

import jax
import jax.numpy as jnp
import numpy as np
from jax import lax
from jax.experimental import pallas as pl
from jax.experimental.pallas import tpu as pltpu

F32 = jnp.float32
BF16 = jnp.bfloat16
I32 = jnp.int32

CHUNK = 64
EPS = 1e-6
NEG_INF = -1e30
H_A = 4
C_A = 128
W_A = H_A * C_A
W_B = 512
H_C = 8
DH = 64
W_C = H_C * DH
H_I = 8
D_I = 64
TOPK_MAX = 256
H_D = 4
W_D = H_D * 2 * DH
N_BUCKETS = 32
MAX_DIST = 128
N_EXP = 8
IN1_SIZES = (W_C, W_C, W_C, W_D, W_D, W_D, H_I * D_I, D_I, H_I)
N_HEADS_BIAS = H_C + H_D
N_MAPS = H_C + 2 * H_D

LANES = 128
SUBLANES = 8
VMEM_LIMIT = 56 * 1024 * 1024

INT_MIN = -(2 ** 31)
INT_MAX = 2 ** 31 - 1
ALL_TIES = 2 ** 30

KEY_BLOCK = 512


def _cparams(*sem):
    return pltpu.CompilerParams(dimension_semantics=sem, vmem_limit_bytes=VMEM_LIMIT)


def _largest_divisor(n, target, mult):
    if n <= target:
        return n
    d = (target // mult) * mult
    while d >= mult:
        if n % d == 0:
            return d
        d -= mult
    raise ValueError(f"no block of multiple {mult} divides {n}")


def _rms(x, g):
    return x * lax.rsqrt(jnp.mean(x * x, axis=-1, keepdims=True) + EPS) * g


def _dot(a, b):
    return jnp.dot(a, b, preferred_element_type=F32)


def _dot_nt(a, b):
    return lax.dot_general(a, b, (((1,), (1,)), ((), ())), preferred_element_type=F32)


def _rms_proj_body(x_ref, g_ref, w_ref, *out_refs, plan):
    xn = _rms(x_ref[...], g_ref[...]).astype(BF16)
    for (c0, c1), writes in plan:
        z = _dot(xn, w_ref[:, c0:c1])
        for o_idx, head, z0, z1, scale in writes:
            val = z[:, z0:z1]
            if scale != 1.0:
                val = val * scale
            ref = out_refs[o_idx]
            if head is None:
                ref[...] = val.astype(ref.dtype)
            else:
                ref[head] = val.astype(ref.dtype)


def _rms_proj(x, g, w, plan, out_defs, rows):
    n, d = x.shape
    grid = (n // rows,)
    out_shape, out_specs = [], []
    for width, dtype, heads in out_defs:
        if heads is None:
            out_shape.append(jax.ShapeDtypeStruct((n, width), dtype))
            out_specs.append(pl.BlockSpec((rows, width), lambda i: (i, 0)))
        else:
            out_shape.append(jax.ShapeDtypeStruct((heads, n, width), dtype))
            out_specs.append(pl.BlockSpec((heads, rows, width), lambda i: (0, i, 0)))
    return pl.pallas_call(
        functools.partial(_rms_proj_body, plan=plan),
        grid=grid,
        in_specs=[pl.BlockSpec((rows, d), lambda i: (i, 0)),
                  pl.BlockSpec((1, d), lambda i: (0, 0)),
                  pl.BlockSpec(w.shape, lambda i: (0, 0))],
        out_specs=out_specs,
        out_shape=out_shape,
        compiler_params=_cparams("parallel"),
        name="rms_proj",
    )(x, g, w)


def _mixer_ab_body(*refs, rb, from_prev, emit_v):
    it = iter(refs)
    x_ref, u_ref, v_ref, gb_ref, gc_ref, xin_ref = (next(it) for _ in range(6))
    if from_prev:
        gcp_ref, xinp_ref = next(it), next(it)
    else:
        hist_ref = next(it)
    lng_ref, lnb_ref, ws_ref, bs_ref, ck_ref, wout_ref = (next(it) for _ in range(6))
    x1_ref, tail_ref = next(it), next(it)
    vout_ref = next(it) if emit_v else None
    wext_ref = next(it)

    w = gc_ref[...] * xin_ref[...]
    if from_prev:
        hist = jnp.where(pl.program_id(0) > 0, gcp_ref[...] * xinp_ref[...], 0.0)
    else:
        hist = hist_ref[0]
    wext_ref[0:SUBLANES, :] = hist
    wext_ref[SUBLANES:, :] = w
    ck = ck_ref[...]
    conv = (ck[0:1] * wext_ref[SUBLANES - 2:SUBLANES - 2 + rb, :]
            + ck[1:2] * wext_ref[SUBLANES - 1:SUBLANES - 1 + rb, :]
            + ck[2:3] * w)
    y_b = gb_ref[...] * conv
    tail_ref[0] = w[rb - SUBLANES:, :]

    u = jax.nn.gelu(u_ref[...])
    v = jax.nn.gelu(v_ref[...])
    lng = lng_ref[...]
    lnb = lnb_ref[...]
    bs = bs_ref[...]
    acc = _dot(y_b.astype(BF16), wout_ref[W_A:, :])
    for h in range(H_A):
        sl = slice(h * C_A, (h + 1) * C_A)
        vh = v[:, sl]
        mu = jnp.mean(vh, axis=-1, keepdims=True)
        xc = vh - mu
        var = jnp.mean(xc * xc, axis=-1, keepdims=True)
        vln = xc * lax.rsqrt(var + EPS) * lng[:, sl] + lnb[:, sl]
        if emit_v:
            vout_ref[:, sl] = vln
        s = _dot(ws_ref[h], vln.astype(BF16)) + bs[:, h:h + 1]
        y_a = u[:, sl] * s
        acc = acc + _dot(y_a.astype(BF16), wout_ref[sl, :])
    x1_ref[...] = x_ref[...] + acc


def _mixer_ab(x, z, hist, lng, lnb, ws, bs, ck, wout, rb, emit_v):
    n, d = x.shape
    nb = n // rb
    from_prev = hist is None
    col = lambda c: pl.BlockSpec((rb, 512), lambda i, c=c: (i, c))
    in_specs = [pl.BlockSpec((rb, d), lambda i: (i, 0)), col(0), col(1), col(2), col(3), col(4)]
    args = [x, z, z, z, z, z]
    if from_prev:
        per = rb // SUBLANES
        prev = lambda c: pl.BlockSpec((SUBLANES, 512), lambda i, c=c: (jnp.maximum(i * per - 1, 0), c))
        in_specs += [prev(3), prev(4)]
        args += [z, z]
    else:
        in_specs += [pl.BlockSpec((1, SUBLANES, 512), lambda i: (i, 0, 0))]
        args += [hist]
    const = lambda a: pl.BlockSpec(a.shape, lambda i, nd=a.ndim: (0,) * nd)
    for a in (lng, lnb, ws, bs, ck, wout):
        in_specs.append(const(a))
        args.append(a)
    out_shape = [jax.ShapeDtypeStruct((n, d), F32), jax.ShapeDtypeStruct((nb, SUBLANES, 512), F32)]
    out_specs = [pl.BlockSpec((rb, d), lambda i: (i, 0)), pl.BlockSpec((1, SUBLANES, 512), lambda i: (i, 0, 0))]
    if emit_v:
        out_shape.append(jax.ShapeDtypeStruct((n, W_A), F32))
        out_specs.append(pl.BlockSpec((rb, W_A), lambda i: (i, 0)))
    return pl.pallas_call(
        functools.partial(_mixer_ab_body, rb=rb, from_prev=from_prev, emit_v=emit_v),
        grid=(nb,),
        in_specs=in_specs,
        out_specs=out_specs,
        out_shape=out_shape,
        scratch_shapes=[pltpu.VMEM((rb + SUBLANES, 512), F32)],
        compiler_params=_cparams("arbitrary"),
        name="mixer_ab",
    )(*args)


def _ffn_body(x_ref, g_ref, wg_ref, wu_ref, wd_ref, o_ref, xn_ref, acc_ref):
    k = pl.program_id(1)

    @pl.when(k == 0)
    def _():
        xn_ref[...] = _rms(x_ref[...], g_ref[...]).astype(BF16)
        acc_ref[...] = jnp.zeros_like(acc_ref)

    xn = xn_ref[...]
    h = jax.nn.silu(_dot(xn, wg_ref[...])) * _dot(xn, wu_ref[...])
    acc_ref[...] += _dot(h.astype(BF16), wd_ref[...])

    @pl.when(k == pl.num_programs(1) - 1)
    def _():
        o_ref[...] = x_ref[...] + acc_ref[...]


def _ffn(x, g, wg, wu, wd, rows, fb):
    n, d = x.shape
    dff = wg.shape[1]
    return pl.pallas_call(
        _ffn_body,
        grid=(n // rows, dff // fb),
        in_specs=[pl.BlockSpec((rows, d), lambda i, k: (i, 0)),
                  pl.BlockSpec((1, d), lambda i, k: (0, 0)),
                  pl.BlockSpec((d, fb), lambda i, k: (0, k)),
                  pl.BlockSpec((d, fb), lambda i, k: (0, k)),
                  pl.BlockSpec((fb, d), lambda i, k: (k, 0))],
        out_specs=pl.BlockSpec((rows, d), lambda i, k: (i, 0)),
        out_shape=jax.ShapeDtypeStruct((n, d), F32),
        scratch_shapes=[pltpu.VMEM((rows, d), BF16), pltpu.VMEM((rows, d), F32)],
        compiler_params=_cparams("parallel", "arbitrary"),
        name="ffn",
    )(x, g, wg, wu, wd)


def _select_body(qi_ref, wi_ref, ve_ref, ki_ref, out_ref, keys_ref, *, tq, lb, nkb, topk, nvalid_fn):
    nv = nvalid_fn(pl.program_id(1))
    wi = wi_ref[...]
    ve = ve_ref[...]
    qs = [qi_ref[h] for h in range(H_I)]
    wcols = [wi[:, h:h + 1] for h in range(H_I)]
    lane = lax.broadcasted_iota(I32, (tq, lb), 1)

    def score_block(b, carry):
        kb = ki_ref[0, pl.ds(pl.multiple_of(b * lb, lb), lb), :]
        sc = jnp.zeros((tq, lb), F32)
        for h in range(H_I):
            sc = sc + wcols[h] * jnp.maximum(_dot_nt(qs[h], kb), 0.0)
        bits = lax.bitcast_convert_type(sc, I32)
        key = bits ^ ((bits >> 31) & INT_MAX)
        key = jnp.where(sc == 0.0, 0, key)
        key = jnp.where(lane + b * lb < ve, key, INT_MIN)
        keys_ref[b] = key
        return carry

    lax.fori_loop(0, nv, score_block, 0)

    def count_ge(mid):
        midb = jnp.broadcast_to(mid, (tq, LANES))

        def body(b, acc):
            for c in range(lb // LANES):
                k = keys_ref[b, :, c * LANES:(c + 1) * LANES]
                acc = acc + jnp.where(k >= midb, 1, 0)
            return acc

        acc = lax.fori_loop(0, nv, body, jnp.zeros((tq, LANES), I32))
        return jnp.sum(acc.astype(F32), axis=-1, keepdims=True).astype(I32)

    def bisect(_, st):
        lo, hi, clo, chi = st
        mid = (lo >> 1) + (hi >> 1) + (lo & hi & 1)
        cnt = count_ge(mid)
        active = mid != lo
        up = jnp.logical_and(active, cnt >= topk)
        dn = jnp.logical_and(active, cnt < topk)
        return (jnp.where(up, mid, lo), jnp.where(dn, mid, hi),
                jnp.where(up, cnt, clo), jnp.where(dn, cnt, chi))

    full = lambda v: jnp.full((tq, 1), v, I32)
    lo, hi, clo, chi = lax.fori_loop(0, 32, bisect, (full(INT_MIN), full(INT_MAX), ve, full(0)))

    need = jnp.where(clo > topk, topk - chi, ALL_TIES)
    need = jnp.where(lo == INT_MIN, 0, need).astype(F32)
    lob = jnp.broadcast_to(lo, (tq, LANES))
    needb = jnp.broadcast_to(need, (tq, LANES))
    r = lax.broadcasted_iota(I32, (LANES, LANES), 0)
    c = lax.broadcasted_iota(I32, (LANES, LANES), 1)
    tri = jnp.where(r <= c, 1.0, 0.0).astype(BF16)

    def mask_block(b, seen):
        for cc in range(lb // LANES):
            k = keys_ref[b, :, cc * LANES:(cc + 1) * LANES]
            eq = jnp.where(k == lob, 1.0, 0.0)
            cum = _dot(eq.astype(BF16), tri)
            rank = seen + cum - eq
            take = jnp.where(rank < needb, eq, 0.0)
            sel = jnp.where(k > lob, 1.0, take)
            out_ref[0, b, :, cc * LANES:(cc + 1) * LANES] = jnp.where(sel > 0.5, 0.0, NEG_INF).astype(out_ref.dtype)
            seen = seen + cum[:, LANES - 1:LANES]
        return seen

    lax.fori_loop(0, nv, mask_block, jnp.zeros((tq, 1), F32))

    def fill_block(b, carry):
        out_ref[0, b] = jnp.full((tq, lb), NEG_INF, out_ref.dtype)
        return carry

    lax.fori_loop(nv, nkb, fill_block, 0)


def _select(qi, wi, ve, ki, nbatch, t, tq, topk, nvalid_fn):
    lp = ki.shape[1]
    lb = KEY_BLOCK
    nkb = lp // lb
    nq = t // tq
    row = lambda b, i: b * nq + i
    return pl.pallas_call(
        functools.partial(_select_body, tq=tq, lb=lb, nkb=nkb, topk=topk, nvalid_fn=nvalid_fn),
        grid=(nbatch, nq),
        in_specs=[pl.BlockSpec((H_I, tq, D_I), lambda b, i: (0, row(b, i), 0)),
                  pl.BlockSpec((tq, H_I), lambda b, i: (row(b, i), 0)),
                  pl.BlockSpec((tq, 1), lambda b, i: (row(b, i), 0)),
                  pl.BlockSpec((1, lp, D_I), lambda b, i: (b, 0, 0))],
        out_specs=pl.BlockSpec((1, nkb, tq, lb), lambda b, i: (b, 0, i, 0)),
        out_shape=jax.ShapeDtypeStruct((nbatch, nkb, t, lb), BF16),
        scratch_shapes=[pltpu.VMEM((nkb, tq, lb), I32)],
        compiler_params=_cparams("parallel", "arbitrary"),
        name="index_select",
    )(qi, wi, ve, ki)


def _attn_body(qc_ref, qd_ref, kc_ref, vc_ref, kd_ref, vd_ref, mask_ref, bt_ref, lam_ref, g_ref,
               o_ref, m_ref, l_ref, accc_ref, accd_ref, *, tq, nvalid_fn, lam_init):
    i = pl.program_id(1)
    j = pl.program_id(2)

    @pl.when(j == 0)
    def _():
        m_ref[...] = jnp.full(m_ref.shape, NEG_INF, F32)
        l_ref[...] = jnp.zeros_like(l_ref)
        accc_ref[...] = jnp.zeros_like(accc_ref)
        accd_ref[...] = jnp.zeros_like(accd_ref)

    @pl.when(j < nvalid_fn(i))
    def _():
        sel = mask_ref[0, 0].astype(F32)
        for mp in range(N_MAPS):
            sparse = mp < H_C
            if sparse:
                s = _dot_nt(qc_ref[mp], kc_ref[0, mp]) + bt_ref[0, mp] + sel
                v = vc_ref[0, mp]
                acc_ref, a_idx, dv = accc_ref, mp, DH
            else:
                dm = mp - H_C
                s = _dot_nt(qd_ref[dm], kd_ref[0, dm]) + bt_ref[0, H_C + dm // 2]
                v = vd_ref[0, dm // 2]
                acc_ref, a_idx, dv = accd_ref, dm, 2 * DH
            m_prev = m_ref[mp]
            m_new = jnp.maximum(m_prev, jnp.max(s, axis=-1, keepdims=True))
            alpha = jnp.exp(m_prev - m_new)
            p = jnp.exp(s - m_new[:, 0:1])
            l_ref[mp] = alpha * l_ref[mp] + jnp.sum(p, axis=-1, keepdims=True)
            m_ref[mp] = m_new
            acc_ref[a_idx] = alpha[:, 0:dv] * acc_ref[a_idx] + _dot(p.astype(BF16), v)

    @pl.when(j == pl.num_programs(2) - 1)
    def _():
        for h in range(H_C):
            o_ref[:, h * DH:(h + 1) * DH] = (accc_ref[h] / l_ref[h][:, 0:DH]).astype(o_ref.dtype)
        lam = lam_ref[...]
        g = g_ref[...]
        for h in range(H_D):
            a0 = accd_ref[2 * h] / l_ref[H_C + 2 * h]
            a1 = accd_ref[2 * h + 1] / l_ref[H_C + 2 * h + 1]
            od = _rms(a0 - lam * a1, g) * (1.0 - lam_init)
            o_ref[:, W_C + h * 2 * DH:W_C + (h + 1) * 2 * DH] = od.astype(o_ref.dtype)


def _attend(qc, qd, kc, vc, kd, vd, mask, btiles, lam, g, nbatch, t, tq, nvalid_fn, tile_fn, lam_init):
    lp = kc.shape[2]
    lb = KEY_BLOCK
    nkb = lp // lb
    nq = t // tq
    row = lambda b, i: b * nq + i
    kblk = lambda i, j: jnp.minimum(j, nvalid_fn(i) - 1)
    qspec = pl.BlockSpec((H_C, tq, DH), lambda b, i, j: (0, row(b, i), 0))
    kspec = lambda nh, w: pl.BlockSpec((1, nh, lb, w), lambda b, i, j: (b, 0, kblk(i, j), 0))
    return pl.pallas_call(
        functools.partial(_attn_body, tq=tq, nvalid_fn=nvalid_fn, lam_init=lam_init),
        grid=(nbatch, nq, nkb),
        in_specs=[qspec, qspec, kspec(H_C, DH), kspec(H_C, DH), kspec(2 * H_D, DH), kspec(H_D, 2 * DH),
                  pl.BlockSpec((1, 1, tq, lb), lambda b, i, j: (b, kblk(i, j), i, 0)),
                  pl.BlockSpec((1, N_HEADS_BIAS, tq, lb), lambda b, i, j: (tile_fn(i, kblk(i, j)), 0, 0, 0)),
                  pl.BlockSpec((1, 1), lambda b, i, j: (0, 0)),
                  pl.BlockSpec((1, 2 * DH), lambda b, i, j: (0, 0))],
        out_specs=pl.BlockSpec((tq, W_C + W_D), lambda b, i, j: (row(b, i), 0)),
        out_shape=jax.ShapeDtypeStruct((nbatch * t, W_C + W_D), BF16),
        scratch_shapes=[pltpu.VMEM((N_MAPS, tq, LANES), F32), pltpu.VMEM((N_MAPS, tq, LANES), F32),
                        pltpu.VMEM((H_C, tq, DH), F32), pltpu.VMEM((2 * H_D, tq, 2 * DH), F32)],
        compiler_params=_cparams("parallel", "parallel", "arbitrary"),
        name="attend",
    )(qc, qd, kc, vc, kd, vd, mask, btiles, lam, g)


PROMPT_TQ = 256
PROMPT_LB = 512
SEL_ROWS = 256
TIE_ROWS = 128
MAPS_PER_DOT = 4


def _proj1_t_body(x_ref, g_ref, w_ref, wt_ref, vd32_ref, ki32_ref, kc_ref, kd_ref, ki_ref,
                  qct_ref, qdt_ref, qit_ref, vct_ref, vdt_ref, vc32t_ref, kc32t_ref, kd32t_ref, wit_ref):
    xn32 = _rms(x_ref[...], g_ref[...])
    xn = xn32.astype(BF16)
    xnt = xn32.T.astype(BF16)
    kc_ref[...] = _dot(xn, w_ref[:, 0:512]).astype(BF16)
    kd_ref[...] = _dot(xn, w_ref[:, 512:1024]).astype(BF16)
    vd32_ref[...] = _dot(xn, w_ref[:, 1024:1536])
    z = _dot(xn, w_ref[:, 1536:1536 + LANES])
    ki32_ref[...] = z[:, 0:D_I]
    ki_ref[...] = z[:, 0:D_I].astype(BF16)
    for c, (head_ref, full_ref) in enumerate(((qct_ref, None), (qdt_ref, None), (qit_ref, None), (vct_ref, vc32t_ref),
                                              (vdt_ref, None), (None, kc32t_ref), (None, kd32t_ref))):
        zt = _dot(wt_ref[c * 512:(c + 1) * 512, :], xnt)
        if full_ref is not None:
            full_ref[...] = zt
        if head_ref is not None:
            nh = head_ref.shape[0]
            for h in range(nh):
                head_ref[h] = zt[h * (512 // nh):(h + 1) * (512 // nh), :].astype(BF16)
    zt = _dot(wt_ref[3584:3584 + LANES, :], xnt)
    wit_ref[...] = zt[0:H_I, :]


def _proj1_t(x, g, w, wt, rows):
    n, d = x.shape
    full = lambda width: (jax.ShapeDtypeStruct((n, width), F32), pl.BlockSpec((rows, width), lambda i: (i, 0)))
    packed = lambda width: (jax.ShapeDtypeStruct((n, width), BF16), pl.BlockSpec((rows, width), lambda i: (i, 0)))
    heads_t = lambda nh, width: (jax.ShapeDtypeStruct((nh, width, n), BF16),
                                 pl.BlockSpec((nh, width, rows), lambda i: (0, 0, i)))
    full_t = lambda: (jax.ShapeDtypeStruct((512, n), F32), pl.BlockSpec((512, rows), lambda i: (0, i)))
    outs = [full(512), full(D_I), packed(512), packed(512), packed(D_I),
            heads_t(8, DH), heads_t(8, DH), heads_t(8, DH), heads_t(8, DH), heads_t(H_D, 2 * DH),
            full_t(), full_t(), full_t(),
            (jax.ShapeDtypeStruct((H_I, n), F32), pl.BlockSpec((H_I, rows), lambda i: (0, i)))]
    return pl.pallas_call(
        _proj1_t_body,
        grid=(n // rows,),
        in_specs=[pl.BlockSpec((rows, d), lambda i: (i, 0)),
                  pl.BlockSpec((1, d), lambda i: (0, 0)),
                  pl.BlockSpec(w.shape, lambda i: (0, 0)),
                  pl.BlockSpec(wt.shape, lambda i: (0, 0))],
        out_specs=[o[1] for o in outs],
        out_shape=[o[0] for o in outs],
        compiler_params=_cparams("parallel"),
        name="proj1_t",
    )(x, g, w, wt)


def _sublane_all(x8, op):
    for shift in (4, 2, 1):
        x8 = op(x8, pltpu.roll(x8, shift, 0))
    return x8


def _select_t_body(qit_ref, wit_ref, ve_ref, ki_ref, out_ref, keys_ref, gmax_ref, *, tq, lp, topk):
    rb = SEL_ROWS
    nv = (pl.program_id(0) + 1) * (tq // rb)
    wit = wit_ref[...]
    ve = ve_ref[...]
    rows = lax.broadcasted_iota(I32, (rb, tq), 0)
    gmax_ref[...] = jnp.full((rb, tq), INT_MIN, I32)

    def score_chunk(c, carry):
        r0 = pl.multiple_of(c * rb, rb)
        kb = ki_ref[pl.ds(r0, rb), :]
        sc = jnp.zeros((rb, tq), F32)
        for h in range(H_I):
            sc = sc + wit[h:h + 1, :] * jnp.maximum(_dot(kb, qit_ref[h]), 0.0)
        bits = lax.bitcast_convert_type(sc, I32)
        key = bits ^ ((bits >> 31) & INT_MAX)
        key = jnp.where(sc == 0.0, 0, key)
        key = jnp.where(rows + r0 < ve, key, INT_MIN)
        keys_ref[pl.ds(r0, rb), :] = key
        gmax_ref[...] = jnp.maximum(gmax_ref[...], key)
        return carry

    lax.fori_loop(0, nv, score_chunk, 0)

    def count_ge(mid):
        midb = jnp.broadcast_to(mid, (SUBLANES, tq))

        def body(c, accs):
            kc = keys_ref[pl.ds(pl.multiple_of(c * rb, rb), rb), :]
            accs = list(accs)
            for g in range(rb // SUBLANES):
                a = g % len(accs)
                accs[a] = accs[a] + jnp.where(kc[g * SUBLANES:(g + 1) * SUBLANES, :] >= midb, 1, 0)
            return tuple(accs)

        accs = lax.fori_loop(0, nv, body, (jnp.zeros((SUBLANES, tq), I32),) * 4)
        return _sublane_all(accs[0] + accs[1] + accs[2] + accs[3], jnp.add)[0:1, :]

    floor_avg = lambda a, b: (a >> 1) + (b >> 1) + (a & b & 1)

    def pending(lo, hi, clo):
        open_ = jnp.logical_and(floor_avg(lo, hi) != lo, clo > topk)
        return jnp.max(jnp.where(open_, 1.0, 0.0))

    def bisect(st):
        lo, hi, clo, chi, _ = st
        mid = floor_avg(lo, hi)
        cnt = count_ge(mid)
        active = jnp.logical_and(mid != lo, clo > topk)
        up = jnp.logical_and(active, cnt >= topk)
        dn = jnp.logical_and(active, cnt < topk)
        lo, hi = jnp.where(up, mid, lo), jnp.where(dn, mid, hi)
        clo, chi = jnp.where(up, cnt, clo), jnp.where(dn, cnt, chi)
        return lo, hi, clo, chi, pending(lo, hi, clo)

    g8 = gmax_ref[0:SUBLANES, :]
    h8 = g8
    for g in range(1, rb // SUBLANES):
        blk = gmax_ref[g * SUBLANES:(g + 1) * SUBLANES, :]
        g8 = jnp.minimum(g8, blk)
        h8 = jnp.maximum(h8, blk)
    lo0 = _sublane_all(g8, jnp.minimum)[0:1, :]
    hi0 = _sublane_all(h8, jnp.maximum)[0:1, :] + 1
    clo0 = jnp.where(lo0 == INT_MIN, ve, count_ge(lo0))
    lo, hi, clo, chi, _ = lax.while_loop(
        lambda st: st[4] > 0.5, bisect, (lo0, hi0, clo0, jnp.zeros((1, tq), I32), pending(lo0, hi0, clo0)))

    need = jnp.where(clo > topk, topk - chi, ALL_TIES)
    need = jnp.where(lo == INT_MIN, 0, need).astype(F32)
    tr = TIE_ROWS
    lob = jnp.broadcast_to(lo, (tr, tq))
    needb = jnp.broadcast_to(need, (tr, tq))
    r = lax.broadcasted_iota(I32, (tr, tr), 0)
    c = lax.broadcasted_iota(I32, (tr, tr), 1)
    tri = jnp.where(c <= r, 1.0, 0.0).astype(BF16)

    def mask_chunk(cidx, seen):
        r0 = pl.multiple_of(cidx * tr, tr)
        k = keys_ref[pl.ds(r0, tr), :]
        eq = jnp.where(k == lob, 1.0, 0.0)
        cum = _dot(tri, eq.astype(BF16))
        take = jnp.where(seen + cum - eq < needb, eq, 0.0)
        sel = jnp.where(k > lob, 1.0, take)
        out_ref[pl.ds(r0, tr), :] = jnp.where(sel > 0.5, 0.0, NEG_INF).astype(out_ref.dtype)
        return seen + cum[tr - 1:tr, :]

    def plain_chunk(cidx, carry):
        r0 = pl.multiple_of(cidx * rb, rb)
        k = keys_ref[pl.ds(r0, rb), :]
        hit = jnp.where(k == INT_MIN, NEG_INF, 0.0)
        out_ref[pl.ds(r0, rb), :] = jnp.where(k >= lo, hit, NEG_INF).astype(out_ref.dtype)
        return carry

    any_tie = jnp.max(jnp.where(clo > topk, 1.0, 0.0)) > 0.5
    lax.cond(any_tie,
             lambda: lax.fori_loop(0, nv * (rb // tr), mask_chunk, jnp.zeros((1, tq), F32)),
             lambda: lax.fori_loop(0, nv, plain_chunk, jnp.zeros((1, tq), F32)))

    def fill_chunk(cidx, carry):
        out_ref[pl.ds(pl.multiple_of(cidx * rb, rb), rb), :] = jnp.full((rb, tq), NEG_INF, out_ref.dtype)
        return carry

    lax.fori_loop(nv, lp // rb, fill_chunk, 0)


def _select_t(qit, wit, ve, ki, topk):
    t = qit.shape[2]
    tq = PROMPT_TQ
    return pl.pallas_call(
        functools.partial(_select_t_body, tq=tq, lp=t, topk=topk),
        grid=(t // tq,),
        in_specs=[pl.BlockSpec((H_I, D_I, tq), lambda i: (0, 0, i)),
                  pl.BlockSpec((H_I, tq), lambda i: (0, i)),
                  pl.BlockSpec((1, tq), lambda i: (0, i)),
                  pl.BlockSpec((t, D_I), lambda i: (0, 0))],
        out_specs=pl.BlockSpec((t, tq), lambda i: (0, i)),
        out_shape=jax.ShapeDtypeStruct((t, t), BF16),
        scratch_shapes=[pltpu.VMEM((t, tq), I32), pltpu.VMEM((SEL_ROWS, tq), I32)],
        compiler_params=_cparams("parallel"),
        name="index_select_t",
    )(qit, wit, ve, ki)


def _attn_t_body(qct_ref, qdt_ref, kc_ref, kd_ref, vct_ref, vdt_ref, mask_ref, bt_ref, lam_ref, g_ref,
                 o_ref, m_ref, l_ref, accc_ref, accd_ref, ot_ref, qbd_ref, s_ref, *, tq, per, e_far, lam_init):
    i = pl.program_id(0)
    j = pl.program_id(1)
    e = i - j * per
    grp = MAPS_PER_DOT

    @pl.when(j == 0)
    def _():
        m_ref[...] = jnp.full(m_ref.shape, NEG_INF, F32)
        l_ref[...] = jnp.zeros_like(l_ref)
        accc_ref[...] = jnp.zeros_like(accc_ref)
        accd_ref[...] = jnp.zeros_like(accd_ref)
        qbd_ref[...] = jnp.zeros_like(qbd_ref)
        for mp in range(N_MAPS):
            q = qct_ref[mp] if mp < H_C else qdt_ref[mp - H_C]
            a = mp % grp
            qbd_ref[mp // grp, a * DH:(a + 1) * DH, a * tq:(a + 1) * tq] = q

    def logits(g):
        k_ref = kc_ref if g < H_C // grp else kd_ref
        half = g % (H_C // grp)
        return _dot(k_ref[:, half * grp * DH:(half + 1) * grp * DH], qbd_ref[g])

    def step(near):
        for g in range(N_MAPS // grp):
            s_ref[g] = logits(g)
        for mp in range(N_MAPS):
            g, a = mp // grp, mp % grp
            cols = slice(a * tq, (a + 1) * tq)
            if mp < H_C:
                bias_idx, vt, acc_ref, a_idx = mp, vct_ref[mp], accc_ref, mp
            else:
                dm = mp - H_C
                bias_idx, vt, acc_ref, a_idx = H_C + dm // 2, vdt_ref[dm // 2], accd_ref, dm

            def biased():
                s = s_ref[g, :, cols]
                if mp < H_C:
                    s = s + mask_ref[...].astype(F32)
                if near:
                    s = s + bt_ref[0, bias_idx]
                return s

            m_prev = m_ref[mp]
            m_new = jnp.maximum(m_prev, jnp.max(biased(), axis=0, keepdims=True))
            alpha = jnp.exp2(m_prev - m_new)
            p = jnp.exp2(biased() - m_new)
            l_ref[mp] = alpha * l_ref[mp] + jnp.sum(p, axis=0, keepdims=True)
            m_ref[mp] = m_new
            acc_ref[a_idx] = alpha * acc_ref[a_idx] + _dot(vt, p.astype(BF16))

    valid = j * per <= i
    pl.when(jnp.logical_and(valid, e < e_far))(functools.partial(step, True))
    pl.when(jnp.logical_and(valid, e >= e_far))(functools.partial(step, False))

    @pl.when(j == pl.num_programs(1) - 1)
    def _():
        for h in range(H_C):
            ot_ref[h * DH:(h + 1) * DH, :] = accc_ref[h] / l_ref[h]
        lam = lam_ref[...]
        g = g_ref[...]
        for h in range(H_D):
            a0 = accd_ref[2 * h] / l_ref[H_C + 2 * h]
            a1 = accd_ref[2 * h + 1] / l_ref[H_C + 2 * h + 1]
            od = a0 - lam * a1
            od = od * lax.rsqrt(jnp.mean(od * od, axis=0, keepdims=True) + EPS) * g * (1.0 - lam_init)
            ot_ref[W_C + h * 2 * DH:W_C + (h + 1) * 2 * DH, :] = od
        o_ref[...] = ot_ref[...].T.astype(o_ref.dtype)


def _attend_t(qct, qdt, kc, kd, vct, vdt, mask, btiles, lam, g, lam_init):
    t = qct.shape[2]
    tq, lb = PROMPT_TQ, PROMPT_LB
    per = lb // tq
    e_far = btiles.shape[0]
    kblk = lambda i, j: jnp.minimum(j, i // per)
    qspec = pl.BlockSpec((H_C, DH, tq), lambda i, j: (0, 0, i))
    kspec = pl.BlockSpec((lb, H_C * DH), lambda i, j: (kblk(i, j), 0))
    return pl.pallas_call(
        functools.partial(_attn_t_body, tq=tq, per=per, e_far=e_far, lam_init=lam_init),
        grid=(t // tq, t // lb),
        in_specs=[qspec, qspec, kspec, kspec,
                  pl.BlockSpec((H_C, DH, lb), lambda i, j: (0, 0, kblk(i, j))),
                  pl.BlockSpec((H_D, 2 * DH, lb), lambda i, j: (0, 0, kblk(i, j))),
                  pl.BlockSpec((lb, tq), lambda i, j: (kblk(i, j), i)),
                  pl.BlockSpec((1, N_HEADS_BIAS, lb, tq),
                               lambda i, j: (jnp.clip(i - kblk(i, j) * per, 0, e_far - 1), 0, 0, 0)),
                  pl.BlockSpec((1, 1), lambda i, j: (0, 0)),
                  pl.BlockSpec((2 * DH, 1), lambda i, j: (0, 0))],
        out_specs=pl.BlockSpec((tq, W_C + W_D), lambda i, j: (i, 0)),
        out_shape=jax.ShapeDtypeStruct((t, W_C + W_D), BF16),
        scratch_shapes=[pltpu.VMEM((N_MAPS, 1, tq), F32), pltpu.VMEM((N_MAPS, 1, tq), F32),
                        pltpu.VMEM((H_C, DH, tq), F32), pltpu.VMEM((2 * H_D, 2 * DH, tq), F32),
                        pltpu.VMEM((W_C + W_D, tq), F32),
                        pltpu.VMEM((N_MAPS // MAPS_PER_DOT, MAPS_PER_DOT * DH, MAPS_PER_DOT * tq), BF16),
                        pltpu.VMEM((N_MAPS // MAPS_PER_DOT, lb, MAPS_PER_DOT * tq), F32)],
        compiler_params=_cparams("parallel", "arbitrary"),
        name="attend_t",
    )(qct, qdt, kc, kd, vct, vdt, mask, btiles, lam, g)


def _attn_s_body(qc_ref, qd_ref, ck_ref, cv_ref, dk_ref, dv_ref, nck_ref, ncv_ref, ndk_ref, ndv_ref,
                 mask_ref, bias_ref, lam_ref, g_ref, o_ref, m_ref, l_ref, acc_ref, *, ts, ncache, lam_init):
    j = pl.program_id(1)
    rows = H_C * ts

    @pl.when(j == 0)
    def _():
        m_ref[...] = jnp.full(m_ref.shape, NEG_INF, F32)
        l_ref[...] = jnp.zeros_like(l_ref)
        acc_ref[...] = jnp.zeros_like(acc_ref)

    def step(kc_t, vc_t, kd_t, vd):
        sel = jnp.tile(mask_ref[0, 0].astype(F32), (H_C, 1))
        bias = bias_ref[0]
        s_c = _dot(qc_ref[0], kc_t.astype(BF16)) + sel + bias[0:rows]
        s_d = _dot(qd_ref[0], kd_t.astype(BF16)) + bias[rows:2 * rows]
        for idx, (s, v) in enumerate(((s_c, vc_t), (s_d, vd))):
            m_prev = m_ref[idx]
            m_new = jnp.maximum(m_prev, jnp.max(s, axis=-1, keepdims=True))
            alpha = jnp.exp(m_prev - m_new)
            p = jnp.exp(s - m_new)
            l_ref[idx] = alpha * l_ref[idx] + jnp.sum(p, axis=-1, keepdims=True)
            m_ref[idx] = m_new
            p = p.astype(BF16)
            pv = _dot_nt(p, v.astype(BF16)) if idx == 0 else _dot(p, v.astype(BF16))
            acc_ref[idx] = alpha * acc_ref[idx] + pv

    @pl.when(j < ncache)
    def _():
        step(ck_ref[0], cv_ref[0], dk_ref[0], dv_ref[0])

    @pl.when(j == ncache)
    def _():
        step(nck_ref[0], ncv_ref[0], ndk_ref[0], ndv_ref[0])
        for h in range(H_C):
            r = slice(h * ts, (h + 1) * ts)
            o_ref[:, h * DH:(h + 1) * DH] = (acc_ref[0, r, h * DH:(h + 1) * DH] / l_ref[0, r, :]).astype(o_ref.dtype)
        lam = lam_ref[...]
        g = g_ref[...]
        for h in range(H_D):
            r0 = slice(2 * h * ts, (2 * h + 1) * ts)
            r1 = slice((2 * h + 1) * ts, (2 * h + 2) * ts)
            c = slice(h * 2 * DH, (h + 1) * 2 * DH)
            od = acc_ref[1, r0, c] / l_ref[1, r0, :] - lam * (acc_ref[1, r1, c] / l_ref[1, r1, :])
            od = _rms(od, g) * (1.0 - lam_init)
            o_ref[:, W_C + h * 2 * DH:W_C + (h + 1) * 2 * DH] = od.astype(o_ref.dtype)


def _attend_s(qbd_c, qbd_d, caches, news, mask, bias, lam, g, lam_init):
    nb, rows, _ = qbd_c.shape
    ts = rows // H_C
    lb = KEY_BLOCK
    ncache = caches[3].shape[1] // lb
    qspec = pl.BlockSpec((1, rows, W_C), lambda b, j: (b, 0, 0))
    ctspec = pl.BlockSpec((1, W_C, lb), lambda b, j: (b, 0, jnp.minimum(j, ncache - 1)))
    cspec = pl.BlockSpec((1, lb, W_C), lambda b, j: (b, jnp.minimum(j, ncache - 1), 0))
    ntspec = pl.BlockSpec((1, W_C, lb), lambda b, j: (b, 0, 0))
    nspec = pl.BlockSpec((1, lb, W_C), lambda b, j: (b, 0, 0))
    return pl.pallas_call(
        functools.partial(_attn_s_body, ts=ts, ncache=ncache, lam_init=lam_init),
        grid=(nb, ncache + 1),
        in_specs=[qspec, qspec, ctspec, ctspec, ctspec, cspec, ntspec, ntspec, ntspec, nspec,
                  pl.BlockSpec((1, 1, ts, lb), lambda b, j: (b, j, 0, 0)),
                  pl.BlockSpec((1, 2 * rows, lb), lambda b, j: (j, 0, 0)),
                  pl.BlockSpec((1, 1), lambda b, j: (0, 0)),
                  pl.BlockSpec((1, 2 * DH), lambda b, j: (0, 0))],
        out_specs=pl.BlockSpec((ts, W_C + W_D), lambda b, j: (b, 0)),
        out_shape=jax.ShapeDtypeStruct((nb * ts, W_C + W_D), BF16),
        scratch_shapes=[pltpu.VMEM((2, rows, 1), F32), pltpu.VMEM((2, rows, 1), F32),
                        pltpu.VMEM((2, rows, W_C), F32)],
        compiler_params=_cparams("parallel", "arbitrary"),
        name="attend_s",
    )(qbd_c, qbd_d, *caches, *news, mask, bias, lam, g)


def _out_router_body(x_ref, o_ref, w_ref, g_ref, r_ref, x3_ref, xn_ref, gate_ref):
    x3 = x_ref[...] + _dot(o_ref[...], w_ref[...])
    x3_ref[...] = x3
    xn = _rms(x3, g_ref[...]).astype(BF16)
    xn_ref[...] = xn
    logits = _dot(xn, r_ref[...])
    lane = lax.broadcasted_iota(I32, logits.shape, 1)
    logits = jnp.where(lane < N_EXP, logits, -jnp.inf)
    m1 = jnp.max(logits, axis=-1, keepdims=True)
    i1 = jnp.min(jnp.where(logits == m1, lane, LANES), axis=-1, keepdims=True)
    rest = jnp.where(lane == i1, -jnp.inf, logits)
    m2 = jnp.max(rest, axis=-1, keepdims=True)
    i2 = jnp.min(jnp.where(rest == m2, lane, LANES), axis=-1, keepdims=True)
    e = jnp.exp(m2 - m1)
    g1 = 1.0 / (1.0 + e)
    g2 = e / (1.0 + e)
    gate_ref[...] = jnp.where(lane == i1, g1, 0.0) + jnp.where(lane == i2, g2, 0.0)


def _out_router(x, o, w, g, router, rows):
    n, d = x.shape
    return pl.pallas_call(
        _out_router_body,
        grid=(n // rows,),
        in_specs=[pl.BlockSpec((rows, d), lambda i: (i, 0)),
                  pl.BlockSpec((rows, d), lambda i: (i, 0)),
                  pl.BlockSpec(w.shape, lambda i: (0, 0)),
                  pl.BlockSpec((1, d), lambda i: (0, 0)),
                  pl.BlockSpec(router.shape, lambda i: (0, 0))],
        out_specs=[pl.BlockSpec((rows, d), lambda i: (i, 0)),
                   pl.BlockSpec((rows, d), lambda i: (i, 0)),
                   pl.BlockSpec((rows, LANES), lambda i: (i, 0))],
        out_shape=[jax.ShapeDtypeStruct((n, d), F32), jax.ShapeDtypeStruct((n, d), BF16),
                   jax.ShapeDtypeStruct((n, LANES), F32)],
        compiler_params=_cparams("parallel"),
        name="out_router",
    )(x, o, w, g, router)


def _moe_body(x_ref, xn_ref, gate_ref, gf_ref, wg_ref, wu_ref, wd_ref, y_ref, acc_ref):
    e = pl.program_id(1)
    k = pl.program_id(2)

    @pl.when(jnp.logical_and(e == 0, k == 0))
    def _():
        acc_ref[...] = jnp.zeros_like(acc_ref)

    gate = gate_ref[...]
    lane = lax.broadcasted_iota(I32, gate.shape, 1)
    ge = jnp.sum(jnp.where(lane == e, gate, 0.0), axis=-1, keepdims=True)
    xn = xn_ref[...]
    h = jax.nn.silu(_dot(xn, wg_ref[0])) * _dot(xn, wu_ref[0])
    acc_ref[...] += _dot((ge * h).astype(BF16), wd_ref[0])

    @pl.when(jnp.logical_and(e == pl.num_programs(1) - 1, k == pl.num_programs(2) - 1))
    def _():
        y_ref[...] = _rms(x_ref[...] + acc_ref[...], gf_ref[...])


def _moe(x, xn, gate, gf, wg, wu, wd, rows, fb):
    n, d = x.shape
    ne, _, dff = wg.shape
    return pl.pallas_call(
        _moe_body,
        grid=(n // rows, ne, dff // fb),
        in_specs=[pl.BlockSpec((rows, d), lambda i, e, k: (i, 0)),
                  pl.BlockSpec((rows, d), lambda i, e, k: (i, 0)),
                  pl.BlockSpec((rows, LANES), lambda i, e, k: (i, 0)),
                  pl.BlockSpec((1, d), lambda i, e, k: (0, 0)),
                  pl.BlockSpec((1, d, fb), lambda i, e, k: (e, 0, k)),
                  pl.BlockSpec((1, d, fb), lambda i, e, k: (e, 0, k)),
                  pl.BlockSpec((1, fb, d), lambda i, e, k: (e, k, 0))],
        out_specs=pl.BlockSpec((rows, d), lambda i, e, k: (i, 0)),
        out_shape=jax.ShapeDtypeStruct((n, d), F32),
        scratch_shapes=[pltpu.VMEM((rows, d), F32)],
        compiler_params=_cparams("parallel", "arbitrary", "arbitrary"),
        name="moe",
    )(x, xn, gate, gf, wg, wu, wd)


def _t5_bucket(rel):
    half = N_BUCKETS // 2
    max_exact = half // 2
    ret = np.where(rel > 0, half, 0)
    n = np.abs(rel)
    nf = np.maximum(n, 1).astype(np.float32)
    large = max_exact + (np.log(nf / np.float32(max_exact)) / np.float32(math.log(MAX_DIST / max_exact))
                         * np.float32(half - max_exact)).astype(np.int32)
    large = np.minimum(large, half - 1)
    return (ret + np.where(n < max_exact, n, large)).astype(np.int32)


def _bias_tile(rel_bias, q_pos, k_pos, k_real):
    rel = k_pos[None, :] - q_pos[:, None]
    onehot = (jnp.asarray(_t5_bucket(rel).astype(np.int8))[:, :, None]
              == jnp.arange(N_BUCKETS, dtype=jnp.int8)).astype(F32)
    bias = jnp.einsum("qkb,bh->hqk", onehot, rel_bias.astype(F32), precision=lax.Precision.HIGHEST)
    ok = np.logical_and(k_pos[None, :] // CHUNK <= q_pos[:, None] // CHUNK, k_pos[None, :] < k_real)
    return jnp.where(ok[None], bias, NEG_INF)


def _in1_plan():
    scale = DH ** -0.5
    heads64 = lambda o, sc=1.0: [(o, h, h * DH, (h + 1) * DH, sc) for h in range(8)]
    plan = [
        ((0, 512), heads64(0, scale)),
        ((512, 1024), [(1, None, 0, 512, 1.0)] + heads64(2)),
        ((1024, 1536), [(3, None, 0, 512, 1.0)] + heads64(4)),
        ((1536, 2048), heads64(5, scale)),
        ((2048, 2560), [(6, None, 0, 512, 1.0)] + heads64(7)),
        ((2560, 3072), [(8, None, 0, 512, 1.0)] + [(9, h, h * 128, (h + 1) * 128, 1.0) for h in range(H_D)]),
        ((3072, 3584), heads64(10)),
        ((3584, 3712), [(11, None, 0, D_I, 1.0), (12, None, 0, D_I, 1.0), (13, None, D_I, D_I + H_I, 1.0)]),
    ]
    out_defs = [(DH, BF16, 8), (512, F32, None), (DH, BF16, 8), (512, F32, None), (DH, BF16, 8),
                (DH, BF16, 8), (512, F32, None), (DH, BF16, 8), (512, F32, None), (2 * DH, BF16, H_D),
                (D_I, BF16, 8), (D_I, F32, None), (D_I, BF16, None), (H_I, F32, None)]
    return plan, out_defs


def _layer0(x, hist, rb, emit_v, p):
    n = x.shape[0]
    rows = _largest_divisor(n, 512, 16)
    plan = [((0, p["w_in0"].shape[1]), [(0, None, 0, p["w_in0"].shape[1], 1.0)])]
    (z,) = _rms_proj(x, p["ln_mix0"], p["w_in0"], plan, [(p["w_in0"].shape[1], F32, None)], rows)
    ws = p["ws_prompt"] if hist is None else p["ws_sample"]
    bs = p["bs_prompt"] if hist is None else p["bs_sample"]
    outs = _mixer_ab(x, z, hist, p["gmlp_ln_g"], p["gmlp_ln_b"], ws, bs, p["conv_k"], p["w_out0"], rb, emit_v)
    x1 = outs[0]
    x2 = _ffn(x1, p["ln_ffn0"], p["ffn_wg"], p["ffn_wu"], p["ffn_wd"], rows, p["ffn_fb"])
    return (x2,) + tuple(outs[1:])


def _layer1_tail(x, o, p):
    n = x.shape[0]
    rows = _largest_divisor(n, 512, 16)
    x3, xn, gate = _out_router(x, o, p["w_out1"], p["ln_ffn1"], p["router"], rows)
    return _moe(x3, xn, gate, p["ln_final"], p["exp_wg"], p["exp_wu"], p["exp_wd"], rows, p["exp_fb"])


def kernel(x_prompt, x_sample, state_b_conv, cache_c_k, cache_c_v, cache_idx_k, cache_d_k, cache_d_v, rel_bias, ln_mix, ln_ffn, ln_final, w_in0, gmlp_ln_g, gmlp_ln_b, gmlp_ws, gmlp_bs, conv_k, w_out0, ffn_wg, ffn_wu, ffn_wd, w_in1, lam_qk, subln_g, w_out1, router, exp_wg, exp_wu, exp_wd):
    bp, seq, d = x_prompt.shape
    bs_, ts, _ = x_sample.shape
    past = cache_c_k.shape[2]
    assert bp == 1 and ln_mix.shape[0] == 2 and seq % PROMPT_LB == 0 and ts % SUBLANES == 0 and ts <= CHUNK
    assert past % KEY_BLOCK == 0
    gmlp_chunk = gmlp_ws.shape[-1]
    lam_init = 0.8 - 0.6 * math.exp(-0.3 * 1)

    def ws_masked(rows):
        r = jnp.arange(rows)
        ok = (r[None, :] // CHUNK) <= (r[:, None] // CHUNK)
        return jnp.where(ok[None], gmlp_ws[0][:, :rows, :rows], 0.0).astype(BF16)

    in1_pad = (-w_in1.shape[2]) % LANES
    lf = lam_qk[0].astype(F32)
    lam = (jnp.exp(jnp.sum(lf[0] * lf[1])) - jnp.exp(jnp.sum(lf[2] * lf[3])) + lam_init).reshape(1, 1)
    p = {
        "ln_mix0": ln_mix[0:1], "ln_ffn0": ln_ffn[0:1], "ln_mix1": ln_mix[1:2], "ln_ffn1": ln_ffn[1:2],
        "ln_final": ln_final.reshape(1, d),
        "w_in0": w_in0[0].astype(BF16),
        "gmlp_ln_g": gmlp_ln_g[0].reshape(1, W_A), "gmlp_ln_b": gmlp_ln_b[0].reshape(1, W_A),
        "ws_prompt": ws_masked(gmlp_chunk), "ws_sample": ws_masked(ts),
        "bs_prompt": gmlp_bs[0][:, :gmlp_chunk].T, "bs_sample": gmlp_bs[0][:, :ts].T,
        "conv_k": conv_k[0], "w_out0": w_out0[0].astype(BF16),
        "ffn_wg": ffn_wg[0].astype(BF16), "ffn_wu": ffn_wu[0].astype(BF16), "ffn_wd": ffn_wd[0].astype(BF16),
        "ffn_fb": _largest_divisor(ffn_wg.shape[2], 1408, LANES),
        "w_in1": jnp.pad(w_in1[0], ((0, 0), (0, in1_pad))).astype(BF16),
        "w_out1": w_out1[0].astype(BF16),
        "router": jnp.pad(router[0], ((0, 0), (0, LANES - N_EXP))).astype(BF16),
        "exp_wg": exp_wg[0].astype(BF16), "exp_wu": exp_wu[0].astype(BF16), "exp_wd": exp_wd[0].astype(BF16),
        "exp_fb": _largest_divisor(exp_wg.shape[3], 896, LANES),
    }
    g_sub = subln_g[0].reshape(1, 2 * DH)
    plan1, out_defs1 = _in1_plan()

    xp = x_prompt.reshape(seq, d)
    xs = x_sample.reshape(bs_ * ts, d)
    xp, p_tail = _layer0(xp, None, gmlp_chunk, False, p)
    hist = jnp.pad(state_b_conv[0], ((0, 0), (SUBLANES - 2, 0), (0, 0)))
    xs, s_tail, s_av = _layer0(xs, hist, ts, True, p)
    p_b_conv = p_tail[-1, SUBLANES - 2:, :].reshape(1, 1, 2, W_B)
    s_b_conv = s_tail[:, SUBLANES - 2:, :].reshape(1, bs_, 2, W_B)
    s_a_v = s_av.reshape(1, bs_, ts, W_A)

    lb = KEY_BLOCK
    w1 = w_in1[0]
    off = np.concatenate([[0], np.cumsum(IN1_SIZES)])
    field = lambda k: w1[:, off[k]:off[k + 1]]
    padc = lambda a: jnp.pad(a, ((0, 0), (0, LANES - a.shape[1])))
    log2e = math.log2(math.e)
    qscale = DH ** -0.5 * log2e
    w_norm = jnp.concatenate([field(1), field(4), field(5), padc(field(7))], axis=1).astype(BF16)
    w_tran = jnp.concatenate([field(0) * qscale, field(3) * qscale, field(6), field(2), field(5),
                              field(1), field(4), padc(field(8))], axis=1).T.astype(BF16)
    (vd32, ki32, kc, kd, ki, qct, qdt, qit, vct, vdt, vc32t, kc32t, kd32t, wit) = _proj1_t(
        xp, p["ln_mix1"], w_norm, w_tran, PROMPT_TQ)
    token_major = lambda a, *dims: jnp.moveaxis(a.reshape(dims + (seq,)), -1, 0).reshape((1, 1, seq) + dims)
    p_c_k, p_c_v = token_major(kc32t, H_C, DH), token_major(vc32t, H_C, DH)
    p_d_k = token_major(kd32t, H_D, 2, DH)
    pos = jnp.arange(seq, dtype=I32)
    ve = ((pos // CHUNK + 1) * CHUNK).reshape(1, seq)
    mask = _select_t(qit, wit, ve, ki, min(TOPK_MAX, seq // 4))
    e_far = -(-(PROMPT_LB - 1 + MAX_DIST) // PROMPT_TQ)
    kpos = np.arange(PROMPT_LB)
    qpos = np.arange(PROMPT_TQ)
    base = e_far * PROMPT_TQ
    rel_near = (rel_bias - rel_bias[N_BUCKETS // 2 - 1:N_BUCKETS // 2]) * log2e
    btiles = jnp.stack([jnp.transpose(_bias_tile(rel_near, base + e * PROMPT_TQ + qpos, base + kpos, base + PROMPT_LB),
                                      (0, 2, 1)) for e in range(e_far)])
    op = _attend_t(qct, qdt, kc, kd, vct, vdt, mask, btiles, lam, subln_g[0].reshape(2 * DH, 1), lam_init)
    y_prompt = _layer1_tail(xp, op, p).reshape(1, seq, d)

    sr = _rms_proj(xs, p["ln_mix1"], p["w_in1"], plan1, out_defs1, _largest_divisor(bs_ * ts, 256, 16))
    sqc, skc32, skc, svc32, svc, sqd, skd32, skd, svd32, svd, sqi, ski32, ski, swi = sr
    nk = past + ts
    lps = -(-nk // lb) * lb

    kis = jnp.pad(jnp.concatenate([cache_idx_k[0].astype(BF16), ski.reshape(bs_, ts, D_I)], axis=1),
                  ((0, 0), (0, lps - nk), (0, 0)))
    ves = jnp.full((bs_ * ts, 1), nk, I32)
    smask = _select(sqi, swi, ves, kis, bs_, ts, ts, min(TOPK_MAX, nk // 4), lambda i: lps // lb)
    sq_pos = past + np.arange(ts)
    sbt = jnp.stack([_bias_tile(rel_bias, sq_pos, j * lb + np.arange(lb), nk) for j in range(lps // lb)])
    sbias = jnp.concatenate([sbt[:, :H_C], jnp.repeat(sbt[:, H_C:], 2, axis=1)], axis=1).reshape(lps // lb, N_MAPS * ts, lb)

    def block_diag(q):
        qb = jnp.transpose(q.reshape(H_C, bs_, ts, DH), (1, 0, 2, 3))
        eye = jnp.eye(H_C, dtype=q.dtype)
        return (qb[:, :, :, None, :] * eye[None, :, None, :, None]).reshape(bs_, H_C * ts, W_C)

    def cache_rows(c, feature_major):
        c = c.reshape(bs_, past, W_C)
        return jnp.transpose(c, (0, 2, 1)) if feature_major else c

    def new_rows(a, feature_major):
        a = jnp.pad(a.reshape(bs_, ts, W_C), ((0, 0), (0, lb - ts), (0, 0)))
        return jnp.transpose(a, (0, 2, 1)) if feature_major else a
    os_ = _attend_s(block_diag(sqc), block_diag(sqd),
                    [cache_rows(c[0], fm) for c, fm in ((cache_c_k, True), (cache_c_v, True), (cache_d_k, True), (cache_d_v, False))],
                    [new_rows(a, fm) for a, fm in ((skc32, True), (svc32, True), (skd32, True), (svd32, False))],
                    smask, sbias, lam, g_sub, lam_init)
    y_sample = _layer1_tail(xs, os_, p).reshape(bs_, ts, d)

    r5 = lambda a, n, t, *tail: a.reshape((1, n, t) + tail)
    return (y_prompt, y_sample, p_b_conv,
            p_c_k, p_c_v, r5(ki32, 1, seq, D_I), p_d_k, r5(vd32, 1, seq, H_D, 2 * DH),
            s_a_v, s_b_conv,
            r5(skc32, bs_, ts, H_C, DH), r5(svc32, bs_, ts, H_C, DH), r5(ski32, bs_, ts, D_I),
            r5(skd32, bs_, ts, H_D, 2, DH), r5(svd32, bs_, ts, H_D, 2 * DH))
```

```python
import functools
import math

import jax
import jax.numpy as jnp
import numpy as np
from jax import lax
from jax.experimental import pallas as pl
from jax.experimental.pallas import tpu as pltpu

F32 = jnp.float32
BF16 = jnp.bfloat16
I32 = jnp.int32

CHUNK = 64
EPS = 1e-6
NEG_INF = -1e30
H_A = 4
C_A = 128
W_A = H_A * C_A
W_B = 512
H_C = 8
DH = 64
W_C = H_C * DH
H_I = 8
D_I = 64
TOPK_MAX = 256
H_D = 4
W_D = H_D * 2 * DH
N_BUCKETS = 32
MAX_DIST = 128
N_EXP = 8
IN1_SIZES = (W_C, W_C, W_C, W_D, W_D, W_D, H_I * D_I, D_I, H_I)
N_HEADS_BIAS = H_C + H_D
N_MAPS = H_C + 2 * H_D

LANES = 128
SUBLANES = 8
VMEM_LIMIT = 56 * 1024 * 1024

INT_MIN = -(2 ** 31)
INT_MAX = 2 ** 31 - 1
ALL_TIES = 2 ** 30

KEY_BLOCK = 512


def _cparams(*sem):
    return pltpu.CompilerParams(dimension_semantics=sem, vmem_limit_bytes=VMEM_LIMIT)


def _largest_divisor(n, target, mult):
    if n <= target:
        return n
    d = (target // mult) * mult
    while d >= mult:
        if n % d == 0:
            return d
        d -= mult
    raise ValueError(f"no block of multiple {mult} divides {n}")


def _rms(x, g):
    return x * lax.rsqrt(jnp.mean(x * x, axis=-1, keepdims=True) + EPS) * g


def _dot(a, b):
    return jnp.dot(a, b, preferred_element_type=F32)


def _dot_nt(a, b):
    return lax.dot_general(a, b, (((1,), (1,)), ((), ())), preferred_element_type=F32)


def _rms_proj_body(x_ref, g_ref, w_ref, *out_refs, plan):
    xn = _rms(x_ref[...], g_ref[...]).astype(BF16)
    for (c0, c1), writes in plan:
        z = _dot(xn, w_ref[:, c0:c1])
        for o_idx, head, z0, z1, scale in writes:
            val = z[:, z0:z1]
            if scale != 1.0:
                val = val * scale
            ref = out_refs[o_idx]
            if head is None:
                ref[...] = val.astype(ref.dtype)
            else:
                ref[head] = val.astype(ref.dtype)


def _rms_proj(x, g, w, plan, out_defs, rows):
    n, d = x.shape
    grid = (n // rows,)
    out_shape, out_specs = [], []
    for width, dtype, heads in out_defs:
        if heads is None:
            out_shape.append(jax.ShapeDtypeStruct((n, width), dtype))
            out_specs.append(pl.BlockSpec((rows, width), lambda i: (i, 0)))
        else:
            out_shape.append(jax.ShapeDtypeStruct((heads, n, width), dtype))
            out_specs.append(pl.BlockSpec((heads, rows, width), lambda i: (0, i, 0)))
    return pl.pallas_call(
        functools.partial(_rms_proj_body, plan=plan),
        grid=grid,
        in_specs=[pl.BlockSpec((rows, d), lambda i: (i, 0)),
                  pl.BlockSpec((1, d), lambda i: (0, 0)),
                  pl.BlockSpec(w.shape, lambda i: (0, 0))],
        out_specs=out_specs,
        out_shape=out_shape,
        compiler_params=_cparams("parallel"),
        name="rms_proj",
    )(x, g, w)


def _mixer_ab_body(*refs, rb, from_prev, emit_v):
    it = iter(refs)
    x_ref, u_ref, v_ref, gb_ref, gc_ref, xin_ref = (next(it) for _ in range(6))
    if from_prev:
        gcp_ref, xinp_ref = next(it), next(it)
    else:
        hist_ref = next(it)
    lng_ref, lnb_ref, ws_ref, bs_ref, ck_ref, wout_ref = (next(it) for _ in range(6))
    x1_ref, tail_ref = next(it), next(it)
    vout_ref = next(it) if emit_v else None
    wext_ref = next(it)

    w = gc_ref[...] * xin_ref[...]
    if from_prev:
        hist = jnp.where(pl.program_id(0) > 0, gcp_ref[...] * xinp_ref[...], 0.0)
    else:
        hist = hist_ref[0]
    wext_ref[0:SUBLANES, :] = hist
    wext_ref[SUBLANES:, :] = w
    ck = ck_ref[...]
    conv = (ck[0:1] * wext_ref[SUBLANES - 2:SUBLANES - 2 + rb, :]
            + ck[1:2] * wext_ref[SUBLANES - 1:SUBLANES - 1 + rb, :]
            + ck[2:3] * w)
    y_b = gb_ref[...] * conv
    tail_ref[0] = w[rb - SUBLANES:, :]

    u = jax.nn.gelu(u_ref[...])
    v = jax.nn.gelu(v_ref[...])
    lng = lng_ref[...]
    lnb = lnb_ref[...]
    bs = bs_ref[...]
    acc = _dot(y_b.astype(BF16), wout_ref[W_A:, :])
    for h in range(H_A):
        sl = slice(h * C_A, (h + 1) * C_A)
        vh = v[:, sl]
        mu = jnp.mean(vh, axis=-1, keepdims=True)
        xc = vh - mu
        var = jnp.mean(xc * xc, axis=-1, keepdims=True)
        vln = xc * lax.rsqrt(var + EPS) * lng[:, sl] + lnb[:, sl]
        if emit_v:
            vout_ref[:, sl] = vln
        s = _dot(ws_ref[h], vln.astype(BF16)) + bs[:, h:h + 1]
        y_a = u[:, sl] * s
        acc = acc + _dot(y_a.astype(BF16), wout_ref[sl, :])
    x1_ref[...] = x_ref[...] + acc


def _mixer_ab(x, z, hist, lng, lnb, ws, bs, ck, wout, rb, emit_v):
    n, d = x.shape
    nb = n // rb
    from_prev = hist is None
    col = lambda c: pl.BlockSpec((rb, 512), lambda i, c=c: (i, c))
    in_specs = [pl.BlockSpec((rb, d), lambda i: (i, 0)), col(0), col(1), col(2), col(3), col(4)]
    args = [x, z, z, z, z, z]
    if from_prev:
        per = rb // SUBLANES
        prev = lambda c: pl.BlockSpec((SUBLANES, 512), lambda i, c=c: (jnp.maximum(i * per - 1, 0), c))
        in_specs += [prev(3), prev(4)]
        args += [z, z]
    else:
        in_specs += [pl.BlockSpec((1, SUBLANES, 512), lambda i: (i, 0, 0))]
        args += [hist]
    const = lambda a: pl.BlockSpec(a.shape, lambda i, nd=a.ndim: (0,) * nd)
    for a in (lng, lnb, ws, bs, ck, wout):
        in_specs.append(const(a))
        args.append(a)
    out_shape = [jax.ShapeDtypeStruct((n, d), F32), jax.ShapeDtypeStruct((nb, SUBLANES, 512), F32)]
    out_specs = [pl.BlockSpec((rb, d), lambda i: (i, 0)), pl.BlockSpec((1, SUBLANES, 512), lambda i: (i, 0, 0))]
    if emit_v:
        out_shape.append(jax.ShapeDtypeStruct((n, W_A), F32))
        out_specs.append(pl.BlockSpec((rb, W_A), lambda i: (i, 0)))
    return pl.pallas_call(
        functools.partial(_mixer_ab_body, rb=rb, from_prev=from_prev, emit_v=emit_v),
        grid=(nb,),
        in_specs=in_specs,
        out_specs=out_specs,
        out_shape=out_shape,
        scratch_shapes=[pltpu.VMEM((rb + SUBLANES, 512), F32)],
        compiler_params=_cparams("arbitrary"),
        name="mixer_ab",
    )(*args)


def _ffn_body(x_ref, g_ref, wg_ref, wu_ref, wd_ref, o_ref, xn_ref, acc_ref):
    k = pl.program_id(1)

    @pl.when(k == 0)
    def _():
        xn_ref[...] = _rms(x_ref[...], g_ref[...]).astype(BF16)
        acc_ref[...] = jnp.zeros_like(acc_ref)

    xn = xn_ref[...]
    h = jax.nn.silu(_dot(xn, wg_ref[...])) * _dot(xn, wu_ref[...])
    acc_ref[...] += _dot(h.astype(BF16), wd_ref[...])

    @pl.when(k == pl.num_programs(1) - 1)
    def _():
        o_ref[...] = x_ref[...] + acc_ref[...]


def _ffn(x, g, wg, wu, wd, rows, fb):
    n, d = x.shape
    dff = wg.shape[1]
    return pl.pallas_call(
        _ffn_body,
        grid=(n // rows, dff // fb),
        in_specs=[pl.BlockSpec((rows, d), lambda i, k: (i, 0)),
                  pl.BlockSpec((1, d), lambda i, k: (0, 0)),
                  pl.BlockSpec((d, fb), lambda i, k: (0, k)),
                  pl.BlockSpec((d, fb), lambda i, k: (0, k)),
                  pl.BlockSpec((fb, d), lambda i, k: (k, 0))],
        out_specs=pl.BlockSpec((rows, d), lambda i, k: (i, 0)),
        out_shape=jax.ShapeDtypeStruct((n, d), F32),
        scratch_shapes=[pltpu.VMEM((rows, d), BF16), pltpu.VMEM((rows, d), F32)],
        compiler_params=_cparams("parallel", "arbitrary"),
        name="ffn",
    )(x, g, wg, wu, wd)


def _select_body(qi_ref, wi_ref, ve_ref, ki_ref, out_ref, keys_ref, *, tq, lb, nkb, topk, nvalid_fn):
    nv = nvalid_fn(pl.program_id(1))
    wi = wi_ref[...]
    ve = ve_ref[...]
    qs = [qi_ref[h] for h in range(H_I)]
    wcols = [wi[:, h:h + 1] for h in range(H_I)]
    lane = lax.broadcasted_iota(I32, (tq, lb), 1)

    def score_block(b, carry):
        kb = ki_ref[0, pl.ds(pl.multiple_of(b * lb, lb), lb), :]
        sc = jnp.zeros((tq, lb), F32)
        for h in range(H_I):
            sc = sc + wcols[h] * jnp.maximum(_dot_nt(qs[h], kb), 0.0)
        bits = lax.bitcast_convert_type(sc, I32)
        key = bits ^ ((bits >> 31) & INT_MAX)
        key = jnp.where(sc == 0.0, 0, key)
        key = jnp.where(lane + b * lb < ve, key, INT_MIN)
        keys_ref[b] = key
        return carry

    lax.fori_loop(0, nv, score_block, 0)

    def count_ge(mid):
        midb = jnp.broadcast_to(mid, (tq, LANES))

        def body(b, acc):
            for c in range(lb // LANES):
                k = keys_ref[b, :, c * LANES:(c + 1) * LANES]
                acc = acc + jnp.where(k >= midb, 1, 0)
            return acc

        acc = lax.fori_loop(0, nv, body, jnp.zeros((tq, LANES), I32))
        return jnp.sum(acc.astype(F32), axis=-1, keepdims=True).astype(I32)

    def bisect(_, st):
        lo, hi, clo, chi = st
        mid = (lo >> 1) + (hi >> 1) + (lo & hi & 1)
        cnt = count_ge(mid)
        active = mid != lo
        up = jnp.logical_and(active, cnt >= topk)
        dn = jnp.logical_and(active, cnt < topk)
        return (jnp.where(up, mid, lo), jnp.where(dn, mid, hi),
                jnp.where(up, cnt, clo), jnp.where(dn, cnt, chi))

    full = lambda v: jnp.full((tq, 1), v, I32)
    lo, hi, clo, chi = lax.fori_loop(0, 32, bisect, (full(INT_MIN), full(INT_MAX), ve, full(0)))

    need = jnp.where(clo > topk, topk - chi, ALL_TIES)
    need = jnp.where(lo == INT_MIN, 0, need).astype(F32)
    lob = jnp.broadcast_to(lo, (tq, LANES))
    needb = jnp.broadcast_to(need, (tq, LANES))
    r = lax.broadcasted_iota(I32, (LANES, LANES), 0)
    c = lax.broadcasted_iota(I32, (LANES, LANES), 1)
    tri = jnp.where(r <= c, 1.0, 0.0).astype(BF16)

    def mask_block(b, seen):
        for cc in range(lb // LANES):
            k = keys_ref[b, :, cc * LANES:(cc + 1) * LANES]
            eq = jnp.where(k == lob, 1.0, 0.0)
            cum = _dot(eq.astype(BF16), tri)
            rank = seen + cum - eq
            take = jnp.where(rank < needb, eq, 0.0)
            sel = jnp.where(k > lob, 1.0, take)
            out_ref[0, b, :, cc * LANES:(cc + 1) * LANES] = jnp.where(sel > 0.5, 0.0, NEG_INF).astype(out_ref.dtype)
            seen = seen + cum[:, LANES - 1:LANES]
        return seen

    lax.fori_loop(0, nv, mask_block, jnp.zeros((tq, 1), F32))

    def fill_block(b, carry):
        out_ref[0, b] = jnp.full((tq, lb), NEG_INF, out_ref.dtype)
        return carry

    lax.fori_loop(nv, nkb, fill_block, 0)


def _select(qi, wi, ve, ki, nbatch, t, tq, topk, nvalid_fn):
    lp = ki.shape[1]
    lb = KEY_BLOCK
    nkb = lp // lb
    nq = t // tq
    row = lambda b, i: b * nq + i
    return pl.pallas_call(
        functools.partial(_select_body, tq=tq, lb=lb, nkb=nkb, topk=topk, nvalid_fn=nvalid_fn),
        grid=(nbatch, nq),
        in_specs=[pl.BlockSpec((H_I, tq, D_I), lambda b, i: (0, row(b, i), 0)),
                  pl.BlockSpec((tq, H_I), lambda b, i: (row(b, i), 0)),
                  pl.BlockSpec((tq, 1), lambda b, i: (row(b, i), 0)),
                  pl.BlockSpec((1, lp, D_I), lambda b, i: (b, 0, 0))],
        out_specs=pl.BlockSpec((1, nkb, tq, lb), lambda b, i: (b, 0, i, 0)),
        out_shape=jax.ShapeDtypeStruct((nbatch, nkb, t, lb), BF16),
        scratch_shapes=[pltpu.VMEM((nkb, tq, lb), I32)],
        compiler_params=_cparams("parallel", "arbitrary"),
        name="index_select",
    )(qi, wi, ve, ki)


def _attn_body(qc_ref, qd_ref, kc_ref, vc_ref, kd_ref, vd_ref, mask_ref, bt_ref, lam_ref, g_ref,
               o_ref, m_ref, l_ref, accc_ref, accd_ref, *, tq, nvalid_fn, lam_init):
    i = pl.program_id(1)
    j = pl.program_id(2)

    @pl.when(j == 0)
    def _():
        m_ref[...] = jnp.full(m_ref.shape, NEG_INF, F32)
        l_ref[...] = jnp.zeros_like(l_ref)
        accc_ref[...] = jnp.zeros_like(accc_ref)
        accd_ref[...] = jnp.zeros_like(accd_ref)

    @pl.when(j < nvalid_fn(i))
    def _():
        sel = mask_ref[0, 0].astype(F32)
        for mp in range(N_MAPS):
            sparse = mp < H_C
            if sparse:
                s = _dot_nt(qc_ref[mp], kc_ref[0, mp]) + bt_ref[0, mp] + sel
                v = vc_ref[0, mp]
                acc_ref, a_idx, dv = accc_ref, mp, DH
            else:
                dm = mp - H_C
                s = _dot_nt(qd_ref[dm], kd_ref[0, dm]) + bt_ref[0, H_C + dm // 2]
                v = vd_ref[0, dm // 2]
                acc_ref, a_idx, dv = accd_ref, dm, 2 * DH
            m_prev = m_ref[mp]
            m_new = jnp.maximum(m_prev, jnp.max(s, axis=-1, keepdims=True))
            alpha = jnp.exp(m_prev - m_new)
            p = jnp.exp(s - m_new[:, 0:1])
            l_ref[mp] = alpha * l_ref[mp] + jnp.sum(p, axis=-1, keepdims=True)
            m_ref[mp] = m_new
            acc_ref[a_idx] = alpha[:, 0:dv] * acc_ref[a_idx] + _dot(p.astype(BF16), v)

    @pl.when(j == pl.num_programs(2) - 1)
    def _():
        for h in range(H_C):
            o_ref[:, h * DH:(h + 1) * DH] = (accc_ref[h] / l_ref[h][:, 0:DH]).astype(o_ref.dtype)
        lam = lam_ref[...]
        g = g_ref[...]
        for h in range(H_D):
            a0 = accd_ref[2 * h] / l_ref[H_C + 2 * h]
            a1 = accd_ref[2 * h + 1] / l_ref[H_C + 2 * h + 1]
            od = _rms(a0 - lam * a1, g) * (1.0 - lam_init)
            o_ref[:, W_C + h * 2 * DH:W_C + (h + 1) * 2 * DH] = od.astype(o_ref.dtype)


def _attend(qc, qd, kc, vc, kd, vd, mask, btiles, lam, g, nbatch, t, tq, nvalid_fn, tile_fn, lam_init):
    lp = kc.shape[2]
    lb = KEY_BLOCK
    nkb = lp // lb
    nq = t // tq
    row = lambda b, i: b * nq + i
    kblk = lambda i, j: jnp.minimum(j, nvalid_fn(i) - 1)
    qspec = pl.BlockSpec((H_C, tq, DH), lambda b, i, j: (0, row(b, i), 0))
    kspec = lambda nh, w: pl.BlockSpec((1, nh, lb, w), lambda b, i, j: (b, 0, kblk(i, j), 0))
    return pl.pallas_call(
        functools.partial(_attn_body, tq=tq, nvalid_fn=nvalid_fn, lam_init=lam_init),
        grid=(nbatch, nq, nkb),
        in_specs=[qspec, qspec, kspec(H_C, DH), kspec(H_C, DH), kspec(2 * H_D, DH), kspec(H_D, 2 * DH),
                  pl.BlockSpec((1, 1, tq, lb), lambda b, i, j: (b, kblk(i, j), i, 0)),
                  pl.BlockSpec((1, N_HEADS_BIAS, tq, lb), lambda b, i, j: (tile_fn(i, kblk(i, j)), 0, 0, 0)),
                  pl.BlockSpec((1, 1), lambda b, i, j: (0, 0)),
                  pl.BlockSpec((1, 2 * DH), lambda b, i, j: (0, 0))],
        out_specs=pl.BlockSpec((tq, W_C + W_D), lambda b, i, j: (row(b, i), 0)),
        out_shape=jax.ShapeDtypeStruct((nbatch * t, W_C + W_D), BF16),
        scratch_shapes=[pltpu.VMEM((N_MAPS, tq, LANES), F32), pltpu.VMEM((N_MAPS, tq, LANES), F32),
                        pltpu.VMEM((H_C, tq, DH), F32), pltpu.VMEM((2 * H_D, tq, 2 * DH), F32)],
        compiler_params=_cparams("parallel", "parallel", "arbitrary"),
        name="attend",
    )(qc, qd, kc, vc, kd, vd, mask, btiles, lam, g)


PROMPT_TQ = 256
PROMPT_LB = 512
SEL_ROWS = 256
TIE_ROWS = 128
MAPS_PER_DOT = 4


def _proj1_t_body(x_ref, g_ref, w_ref, wt_ref, vd32_ref, ki32_ref, kc_ref, kd_ref, ki_ref,
                  qct_ref, qdt_ref, qit_ref, vct_ref, vdt_ref, vc32t_ref, kc32t_ref, kd32t_ref, wit_ref):
    xn32 = _rms(x_ref[...], g_ref[...])
    xn = xn32.astype(BF16)
    xnt = xn32.T.astype(BF16)
    kc_ref[...] = _dot(xn, w_ref[:, 0:512]).astype(BF16)
    kd_ref[...] = _dot(xn, w_ref[:, 512:1024]).astype(BF16)
    vd32_ref[...] = _dot(xn, w_ref[:, 1024:1536])
    z = _dot(xn, w_ref[:, 1536:1536 + LANES])
    ki32_ref[...] = z[:, 0:D_I]
    ki_ref[...] = z[:, 0:D_I].astype(BF16)
    for c, (head_ref, full_ref) in enumerate(((qct_ref, None), (qdt_ref, None), (qit_ref, None), (vct_ref, vc32t_ref),
                                              (vdt_ref, None), (None, kc32t_ref), (None, kd32t_ref))):
        zt = _dot(wt_ref[c * 512:(c + 1) * 512, :], xnt)
        if full_ref is not None:
            full_ref[...] = zt
        if head_ref is not None:
            nh = head_ref.shape[0]
            for h in range(nh):
                head_ref[h] = zt[h * (512 // nh):(h + 1) * (512 // nh), :].astype(BF16)
    zt = _dot(wt_ref[3584:3584 + LANES, :], xnt)
    wit_ref[...] = zt[0:H_I, :]


def _proj1_t(x, g, w, wt, rows):
    n, d = x.shape
    full = lambda width: (jax.ShapeDtypeStruct((n, width), F32), pl.BlockSpec((rows, width), lambda i: (i, 0)))
    packed = lambda width: (jax.ShapeDtypeStruct((n, width), BF16), pl.BlockSpec((rows, width), lambda i: (i, 0)))
    heads_t = lambda nh, width: (jax.ShapeDtypeStruct((nh, width, n), BF16),
                                 pl.BlockSpec((nh, width, rows), lambda i: (0, 0, i)))
    full_t = lambda: (jax.ShapeDtypeStruct((512, n), F32), pl.BlockSpec((512, rows), lambda i: (0, i)))
    outs = [full(512), full(D_I), packed(512), packed(512), packed(D_I),
            heads_t(8, DH), heads_t(8, DH), heads_t(8, DH), heads_t(8, DH), heads_t(H_D, 2 * DH),
            full_t(), full_t(), full_t(),
            (jax.ShapeDtypeStruct((H_I, n), F32), pl.BlockSpec((H_I, rows), lambda i: (0, i)))]
    return pl.pallas_call(
        _proj1_t_body,
        grid=(n // rows,),
        in_specs=[pl.BlockSpec((rows, d), lambda i: (i, 0)),
                  pl.BlockSpec((1, d), lambda i: (0, 0)),
                  pl.BlockSpec(w.shape, lambda i: (0, 0)),
                  pl.BlockSpec(wt.shape, lambda i: (0, 0))],
        out_specs=[o[1] for o in outs],
        out_shape=[o[0] for o in outs],
        compiler_params=_cparams("parallel"),
        name="proj1_t",
    )(x, g, w, wt)


def _sublane_all(x8, op):
    for shift in (4, 2, 1):
        x8 = op(x8, pltpu.roll(x8, shift, 0))
    return x8


def _select_t_body(qit_ref, wit_ref, ve_ref, ki_ref, out_ref, keys_ref, gmax_ref, *, tq, lp, topk):
    rb = SEL_ROWS
    nv = (pl.program_id(0) + 1) * (tq // rb)
    wit = wit_ref[...]
    ve = ve_ref[...]
    rows = lax.broadcasted_iota(I32, (rb, tq), 0)
    gmax_ref[...] = jnp.full((rb, tq), INT_MIN, I32)

    def score_chunk(c, carry):
        r0 = pl.multiple_of(c * rb, rb)
        kb = ki_ref[pl.ds(r0, rb), :]
        sc = jnp.zeros((rb, tq), F32)
        for h in range(H_I):
            sc = sc + wit[h:h + 1, :] * jnp.maximum(_dot(kb, qit_ref[h]), 0.0)
        bits = lax.bitcast_convert_type(sc, I32)
        key = bits ^ ((bits >> 31) & INT_MAX)
        key = jnp.where(sc == 0.0, 0, key)
        key = jnp.where(rows + r0 < ve, key, INT_MIN)
        keys_ref[pl.ds(r0, rb), :] = key
        gmax_ref[...] = jnp.maximum(gmax_ref[...], key)
        return carry

    lax.fori_loop(0, nv, score_chunk, 0)

    def count_ge(mid):
        midb = jnp.broadcast_to(mid, (SUBLANES, tq))

        def body(c, accs):
            kc = keys_ref[pl.ds(pl.multiple_of(c * rb, rb), rb), :]
            accs = list(accs)
            for g in range(rb // SUBLANES):
                a = g % len(accs)
                accs[a] = accs[a] + jnp.where(kc[g * SUBLANES:(g + 1) * SUBLANES, :] >= midb, 1, 0)
            return tuple(accs)

        accs = lax.fori_loop(0, nv, body, (jnp.zeros((SUBLANES, tq), I32),) * 4)
        return _sublane_all(accs[0] + accs[1] + accs[2] + accs[3], jnp.add)[0:1, :]

    floor_avg = lambda a, b: (a >> 1) + (b >> 1) + (a & b & 1)

    def pending(lo, hi, clo):
        open_ = jnp.logical_and(floor_avg(lo, hi) != lo, clo > topk)
        return jnp.max(jnp.where(open_, 1.0, 0.0))

    def bisect(st):
        lo, hi, clo, chi, _ = st
        mid = floor_avg(lo, hi)
        cnt = count_ge(mid)
        active = jnp.logical_and(mid != lo, clo > topk)
        up = jnp.logical_and(active, cnt >= topk)
        dn = jnp.logical_and(active, cnt < topk)
        lo, hi = jnp.where(up, mid, lo), jnp.where(dn, mid, hi)
        clo, chi = jnp.where(up, cnt, clo), jnp.where(dn, cnt, chi)
        return lo, hi, clo, chi, pending(lo, hi, clo)

    g8 = gmax_ref[0:SUBLANES, :]
    h8 = g8
    for g in range(1, rb // SUBLANES):
        blk = gmax_ref[g * SUBLANES:(g + 1) * SUBLANES, :]
        g8 = jnp.minimum(g8, blk)
        h8 = jnp.maximum(h8, blk)
    lo0 = _sublane_all(g8, jnp.minimum)[0:1, :]
    hi0 = _sublane_all(h8, jnp.maximum)[0:1, :] + 1
    clo0 = jnp.where(lo0 == INT_MIN, ve, count_ge(lo0))
    lo, hi, clo, chi, _ = lax.while_loop(
        lambda st: st[4] > 0.5, bisect, (lo0, hi0, clo0, jnp.zeros((1, tq), I32), pending(lo0, hi0, clo0)))

    need = jnp.where(clo > topk, topk - chi, ALL_TIES)
    need = jnp.where(lo == INT_MIN, 0, need).astype(F32)
    tr = TIE_ROWS
    lob = jnp.broadcast_to(lo, (tr, tq))
    needb = jnp.broadcast_to(need, (tr, tq))
    r = lax.broadcasted_iota(I32, (tr, tr), 0)
    c = lax.broadcasted_iota(I32, (tr, tr), 1)
    tri = jnp.where(c <= r, 1.0, 0.0).astype(BF16)

    def mask_chunk(cidx, seen):
        r0 = pl.multiple_of(cidx * tr, tr)
        k = keys_ref[pl.ds(r0, tr), :]
        eq = jnp.where(k == lob, 1.0, 0.0)
        cum = _dot(tri, eq.astype(BF16))
        take = jnp.where(seen + cum - eq < needb, eq, 0.0)
        sel = jnp.where(k > lob, 1.0, take)
        out_ref[pl.ds(r0, tr), :] = jnp.where(sel > 0.5, 0.0, NEG_INF).astype(out_ref.dtype)
        return seen + cum[tr - 1:tr, :]

    def plain_chunk(cidx, carry):
        r0 = pl.multiple_of(cidx * rb, rb)
        k = keys_ref[pl.ds(r0, rb), :]
        hit = jnp.where(k == INT_MIN, NEG_INF, 0.0)
        out_ref[pl.ds(r0, rb), :] = jnp.where(k >= lo, hit, NEG_INF).astype(out_ref.dtype)
        return carry

    any_tie = jnp.max(jnp.where(clo > topk, 1.0, 0.0)) > 0.5
    lax.cond(any_tie,
             lambda: lax.fori_loop(0, nv * (rb // tr), mask_chunk, jnp.zeros((1, tq), F32)),
             lambda: lax.fori_loop(0, nv, plain_chunk, jnp.zeros((1, tq), F32)))

    def fill_chunk(cidx, carry):
        out_ref[pl.ds(pl.multiple_of(cidx * rb, rb), rb), :] = jnp.full((rb, tq), NEG_INF, out_ref.dtype)
        return carry

    lax.fori_loop(nv, lp // rb, fill_chunk, 0)


def _select_t(qit, wit, ve, ki, topk):
    t = qit.shape[2]
    tq = PROMPT_TQ
    return pl.pallas_call(
        functools.partial(_select_t_body, tq=tq, lp=t, topk=topk),
        grid=(t // tq,),
        in_specs=[pl.BlockSpec((H_I, D_I, tq), lambda i: (0, 0, i)),
                  pl.BlockSpec((H_I, tq), lambda i: (0, i)),
                  pl.BlockSpec((1, tq), lambda i: (0, i)),
                  pl.BlockSpec((t, D_I), lambda i: (0, 0))],
        out_specs=pl.BlockSpec((t, tq), lambda i: (0, i)),
        out_shape=jax.ShapeDtypeStruct((t, t), BF16),
        scratch_shapes=[pltpu.VMEM((t, tq), I32), pltpu.VMEM((SEL_ROWS, tq), I32)],
        compiler_params=_cparams("parallel"),
        name="index_select_t",
    )(qit, wit, ve, ki)


def _attn_t_body(qct_ref, qdt_ref, kc_ref, kd_ref, vct_ref, vdt_ref, mask_ref, bt_ref, lam_ref, g_ref,
                 o_ref, m_ref, l_ref, accc_ref, accd_ref, ot_ref, qbd_ref, s_ref, *, tq, per, e_far, lam_init):
    i = pl.program_id(0)
    j = pl.program_id(1)
    e = i - j * per
    grp = MAPS_PER_DOT

    @pl.when(j == 0)
    def _():
        m_ref[...] = jnp.full(m_ref.shape, NEG_INF, F32)
        l_ref[...] = jnp.zeros_like(l_ref)
        accc_ref[...] = jnp.zeros_like(accc_ref)
        accd_ref[...] = jnp.zeros_like(accd_ref)
        qbd_ref[...] = jnp.zeros_like(qbd_ref)
        for mp in range(N_MAPS):
            q = qct_ref[mp] if mp < H_C else qdt_ref[mp - H_C]
            a = mp % grp
            qbd_ref[mp // grp, a * DH:(a + 1) * DH, a * tq:(a + 1) * tq] = q

    def logits(g):
        k_ref = kc_ref if g < H_C // grp else kd_ref
        half = g % (H_C // grp)
        return _dot(k_ref[:, half * grp * DH:(half + 1) * grp * DH], qbd_ref[g])

    def step(near):
        for g in range(N_MAPS // grp):
            s_ref[g] = logits(g)
        for mp in range(N_MAPS):
            g, a = mp // grp, mp % grp
            cols = slice(a * tq, (a + 1) * tq)
            if mp < H_C:
                bias_idx, vt, acc_ref, a_idx = mp, vct_ref[mp], accc_ref, mp
            else:
                dm = mp - H_C
                bias_idx, vt, acc_ref, a_idx = H_C + dm // 2, vdt_ref[dm // 2], accd_ref, dm

            def biased():
                s = s_ref[g, :, cols]
                if mp < H_C:
                    s = s + mask_ref[...].astype(F32)
                if near:
                    s = s + bt_ref[0, bias_idx]
                return s

            m_prev = m_ref[mp]
            m_new = jnp.maximum(m_prev, jnp.max(biased(), axis=0, keepdims=True))
            alpha = jnp.exp2(m_prev - m_new)
            p = jnp.exp2(biased() - m_new)
            l_ref[mp] = alpha * l_ref[mp] + jnp.sum(p, axis=0, keepdims=True)
            m_ref[mp] = m_new
            acc_ref[a_idx] = alpha * acc_ref[a_idx] + _dot(vt, p.astype(BF16))

    valid = j * per <= i
    pl.when(jnp.logical_and(valid, e < e_far))(functools.partial(step, True))
    pl.when(jnp.logical_and(valid, e >= e_far))(functools.partial(step, False))

    @pl.when(j == pl.num_programs(1) - 1)
    def _():
        for h in range(H_C):
            ot_ref[h * DH:(h + 1) * DH, :] = accc_ref[h] / l_ref[h]
        lam = lam_ref[...]
        g = g_ref[...]
        for h in range(H_D):
            a0 = accd_ref[2 * h] / l_ref[H_C + 2 * h]
            a1 = accd_ref[2 * h + 1] / l_ref[H_C + 2 * h + 1]
            od = a0 - lam * a1
            od = od * lax.rsqrt(jnp.mean(od * od, axis=0, keepdims=True) + EPS) * g * (1.0 - lam_init)
            ot_ref[W_C + h * 2 * DH:W_C + (h + 1) * 2 * DH, :] = od
        o_ref[...] = ot_ref[...].T.astype(o_ref.dtype)


def _attend_t(qct, qdt, kc, kd, vct, vdt, mask, btiles, lam, g, lam_init):
    t = qct.shape[2]
    tq, lb = PROMPT_TQ, PROMPT_LB
    per = lb // tq
    e_far = btiles.shape[0]
    kblk = lambda i, j: jnp.minimum(j, i // per)
    qspec = pl.BlockSpec((H_C, DH, tq), lambda i, j: (0, 0, i))
    kspec = pl.BlockSpec((lb, H_C * DH), lambda i, j: (kblk(i, j), 0))
    return pl.pallas_call(
        functools.partial(_attn_t_body, tq=tq, per=per, e_far=e_far, lam_init=lam_init),
        grid=(t // tq, t // lb),
        in_specs=[qspec, qspec, kspec, kspec,
                  pl.BlockSpec((H_C, DH, lb), lambda i, j: (0, 0, kblk(i, j))),
                  pl.BlockSpec((H_D, 2 * DH, lb), lambda i, j: (0, 0, kblk(i, j))),
                  pl.BlockSpec((lb, tq), lambda i, j: (kblk(i, j), i)),
                  pl.BlockSpec((1, N_HEADS_BIAS, lb, tq),
                               lambda i, j: (jnp.clip(i - kblk(i, j) * per, 0, e_far - 1), 0, 0, 0)),
                  pl.BlockSpec((1, 1), lambda i, j: (0, 0)),
                  pl.BlockSpec((2 * DH, 1), lambda i, j: (0, 0))],
        out_specs=pl.BlockSpec((tq, W_C + W_D), lambda i, j: (i, 0)),
        out_shape=jax.ShapeDtypeStruct((t, W_C + W_D), BF16),
        scratch_shapes=[pltpu.VMEM((N_MAPS, 1, tq), F32), pltpu.VMEM((N_MAPS, 1, tq), F32),
                        pltpu.VMEM((H_C, DH, tq), F32), pltpu.VMEM((2 * H_D, 2 * DH, tq), F32),
                        pltpu.VMEM((W_C + W_D, tq), F32),
                        pltpu.VMEM((N_MAPS // MAPS_PER_DOT, MAPS_PER_DOT * DH, MAPS_PER_DOT * tq), BF16),
                        pltpu.VMEM((N_MAPS // MAPS_PER_DOT, lb, MAPS_PER_DOT * tq), F32)],
        compiler_params=_cparams("parallel", "arbitrary"),
        name="attend_t",
    )(qct, qdt, kc, kd, vct, vdt, mask, btiles, lam, g)


def _attn_s_body(qc_ref, qd_ref, ck_ref, cv_ref, dk_ref, dv_ref, nck_ref, ncv_ref, ndk_ref, ndv_ref,
                 mask_ref, bias_ref, lam_ref, g_ref, o_ref, m_ref, l_ref, acc_ref, *, ts, ncache, lam_init):
    j = pl.program_id(1)
    rows = H_C * ts

    @pl.when(j == 0)
    def _():
        m_ref[...] = jnp.full(m_ref.shape, NEG_INF, F32)
        l_ref[...] = jnp.zeros_like(l_ref)
        acc_ref[...] = jnp.zeros_like(acc_ref)

    def step(kc_t, vc_t, kd_t, vd):
        sel = jnp.tile(mask_ref[0, 0].astype(F32), (H_C, 1))
        bias = bias_ref[0]
        s_c = _dot(qc_ref[0], kc_t.astype(BF16)) + sel + bias[0:rows]
        s_d = _dot(qd_ref[0], kd_t.astype(BF16)) + bias[rows:2 * rows]
        for idx, (s, v) in enumerate(((s_c, vc_t), (s_d, vd))):
            m_prev = m_ref[idx]
            m_new = jnp.maximum(m_prev, jnp.max(s, axis=-1, keepdims=True))
            alpha = jnp.exp(m_prev - m_new)
            p = jnp.exp(s - m_new)
            l_ref[idx] = alpha * l_ref[idx] + jnp.sum(p, axis=-1, keepdims=True)
            m_ref[idx] = m_new
            p = p.astype(BF16)
            pv = _dot_nt(p, v.astype(BF16)) if idx == 0 else _dot(p, v.astype(BF16))
            acc_ref[idx] = alpha * acc_ref[idx] + pv

    @pl.when(j < ncache)
    def _():
        step(ck_ref[0], cv_ref[0], dk_ref[0], dv_ref[0])

    @pl.when(j == ncache)
    def _():
        step(nck_ref[0], ncv_ref[0], ndk_ref[0], ndv_ref[0])
        for h in range(H_C):
            r = slice(h * ts, (h + 1) * ts)
            o_ref[:, h * DH:(h + 1) * DH] = (acc_ref[0, r, h * DH:(h + 1) * DH] / l_ref[0, r, :]).astype(o_ref.dtype)
        lam = lam_ref[...]
        g = g_ref[...]
        for h in range(H_D):
            r0 = slice(2 * h * ts, (2 * h + 1) * ts)
            r1 = slice((2 * h + 1) * ts, (2 * h + 2) * ts)
            c = slice(h * 2 * DH, (h + 1) * 2 * DH)
            od = acc_ref[1, r0, c] / l_ref[1, r0, :] - lam * (acc_ref[1, r1, c] / l_ref[1, r1, :])
            od = _rms(od, g) * (1.0 - lam_init)
            o_ref[:, W_C + h * 2 * DH:W_C + (h + 1) * 2 * DH] = od.astype(o_ref.dtype)


def _attend_s(qbd_c, qbd_d, caches, news, mask, bias, lam, g, lam_init):
    nb, rows, _ = qbd_c.shape
    ts = rows // H_C
    lb = KEY_BLOCK
    ncache = caches[3].shape[1] // lb
    qspec = pl.BlockSpec((1, rows, W_C), lambda b, j: (b, 0, 0))
    ctspec = pl.BlockSpec((1, W_C, lb), lambda b, j: (b, 0, jnp.minimum(j, ncache - 1)))
    cspec = pl.BlockSpec((1, lb, W_C), lambda b, j: (b, jnp.minimum(j, ncache - 1), 0))
    ntspec = pl.BlockSpec((1, W_C, lb), lambda b, j: (b, 0, 0))
    nspec = pl.BlockSpec((1, lb, W_C), lambda b, j: (b, 0, 0))
    return pl.pallas_call(
        functools.partial(_attn_s_body, ts=ts, ncache=ncache, lam_init=lam_init),
        grid=(nb, ncache + 1),
        in_specs=[qspec, qspec, ctspec, ctspec, ctspec, cspec, ntspec, ntspec, ntspec, nspec,
                  pl.BlockSpec((1, 1, ts, lb), lambda b, j: (b, j, 0, 0)),
                  pl.BlockSpec((1, 2 * rows, lb), lambda b, j: (j, 0, 0)),
                  pl.BlockSpec((1, 1), lambda b, j: (0, 0)),
                  pl.BlockSpec((1, 2 * DH), lambda b, j: (0, 0))],
        out_specs=pl.BlockSpec((ts, W_C + W_D), lambda b, j: (b, 0)),
        out_shape=jax.ShapeDtypeStruct((nb * ts, W_C + W_D), BF16),
        scratch_shapes=[pltpu.VMEM((2, rows, 1), F32), pltpu.VMEM((2, rows, 1), F32),
                        pltpu.VMEM((2, rows, W_C), F32)],
        compiler_params=_cparams("parallel", "arbitrary"),
        name="attend_s",
    )(qbd_c, qbd_d, *caches, *news, mask, bias, lam, g)


def _out_router_body(x_ref, o_ref, w_ref, g_ref, r_ref, x3_ref, xn_ref, gate_ref, gatet_ref):
    x3 = x_ref[...] + _dot(o_ref[...], w_ref[...])
    x3_ref[...] = x3
    xn = _rms(x3, g_ref[...]).astype(BF16)
    xn_ref[...] = xn
    logits = _dot(xn, r_ref[...])
    lane = lax.broadcasted_iota(I32, logits.shape, 1)
    logits = jnp.where(lane < N_EXP, logits, -jnp.inf)
    m1 = jnp.max(logits, axis=-1, keepdims=True)
    i1 = jnp.min(jnp.where(logits == m1, lane, LANES), axis=-1, keepdims=True)
    rest = jnp.where(lane == i1, -jnp.inf, logits)
    m2 = jnp.max(rest, axis=-1, keepdims=True)
    i2 = jnp.min(jnp.where(rest == m2, lane, LANES), axis=-1, keepdims=True)
    e = jnp.exp(m2 - m1)
    g1 = 1.0 / (1.0 + e)
    g2 = e / (1.0 + e)
    gate = jnp.where(lane == i1, g1, 0.0) + jnp.where(lane == i2, g2, 0.0)
    gate_ref[...] = gate
    gatet_ref[...] = gate.T[0:GATE_ROWS, :]


def _out_router(x, o, w, g, router, rows):
    n, d = x.shape
    return pl.pallas_call(
        _out_router_body,
        grid=(n // rows,),
        in_specs=[pl.BlockSpec((rows, d), lambda i: (i, 0)),
                  pl.BlockSpec((rows, d), lambda i: (i, 0)),
                  pl.BlockSpec(w.shape, lambda i: (0, 0)),
                  pl.BlockSpec((1, d), lambda i: (0, 0)),
                  pl.BlockSpec(router.shape, lambda i: (0, 0))],
        out_specs=[pl.BlockSpec((rows, d), lambda i: (i, 0)),
                   pl.BlockSpec((rows, d), lambda i: (i, 0)),
                   pl.BlockSpec((rows, LANES), lambda i: (i, 0)),
                   pl.BlockSpec((GATE_ROWS, rows), lambda i: (0, i))],
        out_shape=[jax.ShapeDtypeStruct((n, d), F32), jax.ShapeDtypeStruct((n, d), BF16),
                   jax.ShapeDtypeStruct((n, LANES), F32), jax.ShapeDtypeStruct((GATE_ROWS, n), F32)],
        compiler_params=_cparams("parallel"),
        name="out_router",
    )(x, o, w, g, router)


def _moe_body(x_ref, xn_ref, gate_ref, gf_ref, wg_ref, wu_ref, wd_ref, y_ref, acc_ref):
    e = pl.program_id(1)
    k = pl.program_id(2)

    @pl.when(jnp.logical_and(e == 0, k == 0))
    def _():
        acc_ref[...] = jnp.zeros_like(acc_ref)

    gate = gate_ref[...]
    lane = lax.broadcasted_iota(I32, gate.shape, 1)
    ge = jnp.sum(jnp.where(lane == e, gate, 0.0), axis=-1, keepdims=True)
    xn = xn_ref[...]
    h = jax.nn.silu(_dot(xn, wg_ref[0])) * _dot(xn, wu_ref[0])
    acc_ref[...] += _dot((ge * h).astype(BF16), wd_ref[0])

    @pl.when(jnp.logical_and(e == pl.num_programs(1) - 1, k == pl.num_programs(2) - 1))
    def _():
        y_ref[...] = _rms(x_ref[...] + acc_ref[...], gf_ref[...])


def _moe(x, xn, gate, gf, wg, wu, wd, rows, fb):
    n, d = x.shape
    ne, _, dff = wg.shape
    return pl.pallas_call(
        _moe_body,
        grid=(n // rows, ne, dff // fb),
        in_specs=[pl.BlockSpec((rows, d), lambda i, e, k: (i, 0)),
                  pl.BlockSpec((rows, d), lambda i, e, k: (i, 0)),
                  pl.BlockSpec((rows, LANES), lambda i, e, k: (i, 0)),
                  pl.BlockSpec((1, d), lambda i, e, k: (0, 0)),
                  pl.BlockSpec((1, d, fb), lambda i, e, k: (e, 0, k)),
                  pl.BlockSpec((1, d, fb), lambda i, e, k: (e, 0, k)),
                  pl.BlockSpec((1, fb, d), lambda i, e, k: (e, k, 0))],
        out_specs=pl.BlockSpec((rows, d), lambda i, e, k: (i, 0)),
        out_shape=jax.ShapeDtypeStruct((n, d), F32),
        scratch_shapes=[pltpu.VMEM((rows, d), F32)],
        compiler_params=_cparams("parallel", "arbitrary", "arbitrary"),
        name="moe",
    )(x, xn, gate, gf, wg, wu, wd)


def _t5_bucket(rel):
    half = N_BUCKETS // 2
    max_exact = half // 2
    ret = np.where(rel > 0, half, 0)
    n = np.abs(rel)
    nf = np.maximum(n, 1).astype(np.float32)
    large = max_exact + (np.log(nf / np.float32(max_exact)) / np.float32(math.log(MAX_DIST / max_exact))
                         * np.float32(half - max_exact)).astype(np.int32)
    large = np.minimum(large, half - 1)
    return (ret + np.where(n < max_exact, n, large)).astype(np.int32)


def _bias_tile(rel_bias, q_pos, k_pos, k_real):
    rel = k_pos[None, :] - q_pos[:, None]
    onehot = (jnp.asarray(_t5_bucket(rel).astype(np.int8))[:, :, None]
              == jnp.arange(N_BUCKETS, dtype=jnp.int8)).astype(F32)
    bias = jnp.einsum("qkb,bh->hqk", onehot, rel_bias.astype(F32), precision=lax.Precision.HIGHEST)
    ok = np.logical_and(k_pos[None, :] // CHUNK <= q_pos[:, None] // CHUNK, k_pos[None, :] < k_real)
    return jnp.where(ok[None], bias, NEG_INF)


def _in1_plan():
    scale = DH ** -0.5
    heads64 = lambda o, sc=1.0: [(o, h, h * DH, (h + 1) * DH, sc) for h in range(8)]
    plan = [
        ((0, 512), heads64(0, scale)),
        ((512, 1024), [(1, None, 0, 512, 1.0)] + heads64(2)),
        ((1024, 1536), [(3, None, 0, 512, 1.0)] + heads64(4)),
        ((1536, 2048), heads64(5, scale)),
        ((2048, 2560), [(6, None, 0, 512, 1.0)] + heads64(7)),
        ((2560, 3072), [(8, None, 0, 512, 1.0)] + [(9, h, h * 128, (h + 1) * 128, 1.0) for h in range(H_D)]),
        ((3072, 3584), heads64(10)),
        ((3584, 3712), [(11, None, 0, D_I, 1.0), (12, None, 0, D_I, 1.0), (13, None, D_I, D_I + H_I, 1.0)]),
    ]
    out_defs = [(DH, BF16, 8), (512, F32, None), (DH, BF16, 8), (512, F32, None), (DH, BF16, 8),
                (DH, BF16, 8), (512, F32, None), (DH, BF16, 8), (512, F32, None), (2 * DH, BF16, H_D),
                (D_I, BF16, 8), (D_I, F32, None), (D_I, BF16, None), (H_I, F32, None)]
    return plan, out_defs


def _layer0(x, hist, rb, emit_v, p):
    n = x.shape[0]
    rows = _largest_divisor(n, 512, 16)
    plan = [((0, p["w_in0"].shape[1]), [(0, None, 0, p["w_in0"].shape[1], 1.0)])]
    (z,) = _rms_proj(x, p["ln_mix0"], p["w_in0"], plan, [(p["w_in0"].shape[1], F32, None)], rows)
    ws = p["ws_prompt"] if hist is None else p["ws_sample"]
    bs = p["bs_prompt"] if hist is None else p["bs_sample"]
    outs = _mixer_ab(x, z, hist, p["gmlp_ln_g"], p["gmlp_ln_b"], ws, bs, p["conv_k"], p["w_out0"], rb, emit_v)
    x1 = outs[0]
    x2 = _ffn(x1, p["ln_ffn0"], p["ffn_wg"], p["ffn_wu"], p["ffn_wd"], rows, p["ffn_fb"])
    return (x2,) + tuple(outs[1:])


GATE_ROWS = 16
MOE_ROWS = 128


def _moe_routed_body(x_ref, xn_ref, gate_ref, gatet_ref, gf_ref, wg_ref, wu_ref, wd_ref, y_ref,
                     triu_ref, tril_ref, crow_ref, ccol_ref, xs_ref, ge_ref, acc_ref, yblk_ref, nsub_ref,
                     *, tb, sub):
    e = pl.program_id(1)
    k = pl.program_id(2)
    first_k = k == 0
    last_k = k == pl.num_programs(2) - 1

    @pl.when(jnp.logical_and(e == 0, first_k))
    def _():
        r = lax.broadcasted_iota(I32, (tb, tb), 0)
        c = lax.broadcasted_iota(I32, (tb, tb), 1)
        triu_ref[...] = jnp.where(r < c, 1.0, 0.0).astype(BF16)
        tril_ref[...] = jnp.where(c < r, 1.0, 0.0).astype(BF16)
        crow_ref[...] = _dot(jnp.where(gatet_ref[...] > 0.0, 1.0, 0.0).astype(BF16), triu_ref[...])
        ccol_ref[...] = _dot(tril_ref[...], jnp.where(gate_ref[...] > 0.0, 1.0, 0.0).astype(BF16))
        yblk_ref[...] = jnp.zeros_like(yblk_ref)
        acc_ref[...] = jnp.zeros_like(acc_ref)

    @pl.when(first_k)
    def _():
        g_e = gatet_ref[pl.ds(e, 1), :]
        m_e = g_e > 0.0
        c_e = crow_ref[pl.ds(e, 1), :]
        nsub = (jnp.sum(jnp.where(m_e, 1.0, 0.0)).astype(I32) + sub - 1) // sub
        nsub_ref[0] = nsub
        xn = xn_ref[...]
        slot = lax.broadcasted_iota(I32, (sub, tb), 0).astype(F32)

        def pack(s, carry):
            r0 = pl.multiple_of(s * sub, sub)
            hit = jnp.logical_and(m_e, c_e == slot + (s * sub).astype(F32))
            onehot = jnp.where(hit, 1.0, 0.0)
            xs_ref[pl.ds(r0, sub), :] = _dot(onehot.astype(BF16), xn).astype(BF16)
            ge_ref[pl.ds(r0, sub), :] = jnp.sum(onehot * g_e, axis=-1, keepdims=True)
            acc_ref[pl.ds(r0, sub), :] = jnp.zeros((sub, acc_ref.shape[1]), F32)
            return carry

        lax.fori_loop(0, nsub, pack, 0)

    nsub = nsub_ref[0]

    def experts(s, carry):
        rows = pl.ds(pl.multiple_of(s * sub, sub), sub)
        xs = xs_ref[rows, :]
        h = jax.nn.silu(_dot(xs, wg_ref[0])) * _dot(xs, wu_ref[0])
        acc_ref[rows, :] += _dot((ge_ref[rows, :] * h).astype(BF16), wd_ref[0])
        return carry

    lax.fori_loop(0, nsub, experts, 0)

    @pl.when(last_k)
    def _():
        lane = lax.broadcasted_iota(I32, (tb, LANES), 1)
        pick = lambda a: jnp.sum(jnp.where(lane == e, a, 0.0), axis=-1, keepdims=True)
        m_e = pick(gate_ref[...]) > 0.0
        c_e = pick(ccol_ref[...])
        slot = lax.broadcasted_iota(I32, (tb, sub), 1).astype(F32)

        def unpack(s, carry):
            rows = pl.ds(pl.multiple_of(s * sub, sub), sub)
            hit = jnp.logical_and(m_e, c_e == slot + (s * sub).astype(F32))
            onehot = jnp.where(hit, 1.0, 0.0).astype(BF16)
            y = acc_ref[rows, :]
            hi = y.astype(BF16)
            lo = (y - hi.astype(F32)).astype(BF16)
            yblk_ref[...] += _dot(jnp.concatenate([onehot, onehot], axis=1), jnp.concatenate([hi, lo], axis=0))
            return carry

        lax.fori_loop(0, nsub, unpack, 0)

    @pl.when(jnp.logical_and(e == pl.num_programs(1) - 1, last_k))
    def _():
        y_ref[...] = _rms(x_ref[...] + yblk_ref[...], gf_ref[...])


def _moe_routed(x, xn, gate, gatet, gf, wg, wu, wd, tb, fb):
    n, d = x.shape
    ne, _, dff = wg.shape
    return pl.pallas_call(
        functools.partial(_moe_routed_body, tb=tb, sub=min(MOE_ROWS, tb)),
        grid=(n // tb, ne, dff // fb),
        in_specs=[pl.BlockSpec((tb, d), lambda i, e, k: (i, 0)),
                  pl.BlockSpec((tb, d), lambda i, e, k: (i, 0)),
                  pl.BlockSpec((tb, LANES), lambda i, e, k: (i, 0)),
                  pl.BlockSpec((GATE_ROWS, tb), lambda i, e, k: (0, i)),
                  pl.BlockSpec((1, d), lambda i, e, k: (0, 0)),
                  pl.BlockSpec((1, d, fb), lambda i, e, k: (e, 0, k)),
                  pl.BlockSpec((1, d, fb), lambda i, e, k: (e, 0, k)),
                  pl.BlockSpec((1, fb, d), lambda i, e, k: (e, k, 0))],
        out_specs=pl.BlockSpec((tb, d), lambda i, e, k: (i, 0)),
        out_shape=jax.ShapeDtypeStruct((n, d), F32),
        scratch_shapes=[pltpu.VMEM((tb, tb), BF16), pltpu.VMEM((tb, tb), BF16),
                        pltpu.VMEM((GATE_ROWS, tb), F32), pltpu.VMEM((tb, LANES), F32),
                        pltpu.VMEM((tb, d), BF16), pltpu.VMEM((tb, 1), F32),
                        pltpu.VMEM((tb, d), F32), pltpu.VMEM((tb, d), F32),
                        pltpu.SMEM((1,), I32)],
        compiler_params=_cparams("parallel", "arbitrary", "arbitrary"),
        name="moe_routed",
    )(x, xn, gate, gatet, gf, wg, wu, wd)


def _layer1_tail(x, o, p):
    n = x.shape[0]
    rows = _largest_divisor(n, 512, 16)
    x3, xn, gate, gatet = _out_router(x, o, p["w_out1"], p["ln_ffn1"], p["router"], rows)
    return _moe_routed(x3, xn, gate, gatet, p["ln_final"], p["exp_wg"], p["exp_wu"], p["exp_wd"],
                       _largest_divisor(n, 1024, 128), p["exp_fb"])


def kernel(x_prompt, x_sample, state_b_conv, cache_c_k, cache_c_v, cache_idx_k, cache_d_k, cache_d_v, rel_bias, ln_mix, ln_ffn, ln_final, w_in0, gmlp_ln_g, gmlp_ln_b, gmlp_ws, gmlp_bs, conv_k, w_out0, ffn_wg, ffn_wu, ffn_wd, w_in1, lam_qk, subln_g, w_out1, router, exp_wg, exp_wu, exp_wd):
    bp, seq, d = x_prompt.shape
    bs_, ts, _ = x_sample.shape
    past = cache_c_k.shape[2]
    assert bp == 1 and ln_mix.shape[0] == 2 and seq % PROMPT_LB == 0 and ts % SUBLANES == 0 and ts <= CHUNK
    assert past % KEY_BLOCK == 0
    gmlp_chunk = gmlp_ws.shape[-1]
    lam_init = 0.8 - 0.6 * math.exp(-0.3 * 1)

    def ws_masked(rows):
        r = jnp.arange(rows)
        ok = (r[None, :] // CHUNK) <= (r[:, None] // CHUNK)
        return jnp.where(ok[None], gmlp_ws[0][:, :rows, :rows], 0.0).astype(BF16)

    in1_pad = (-w_in1.shape[2]) % LANES
    lf = lam_qk[0].astype(F32)
    lam = (jnp.exp(jnp.sum(lf[0] * lf[1])) - jnp.exp(jnp.sum(lf[2] * lf[3])) + lam_init).reshape(1, 1)
    p = {
        "ln_mix0": ln_mix[0:1], "ln_ffn0": ln_ffn[0:1], "ln_mix1": ln_mix[1:2], "ln_ffn1": ln_ffn[1:2],
        "ln_final": ln_final.reshape(1, d),
        "w_in0": w_in0[0].astype(BF16),
        "gmlp_ln_g": gmlp_ln_g[0].reshape(1, W_A), "gmlp_ln_b": gmlp_ln_b[0].reshape(1, W_A),
        "ws_prompt": ws_masked(gmlp_chunk), "ws_sample": ws_masked(ts),
        "bs_prompt": gmlp_bs[0][:, :gmlp_chunk].T, "bs_sample": gmlp_bs[0][:, :ts].T,
        "conv_k": conv_k[0], "w_out0": w_out0[0].astype(BF16),
        "ffn_wg": ffn_wg[0].astype(BF16), "ffn_wu": ffn_wu[0].astype(BF16), "ffn_wd": ffn_wd[0].astype(BF16),
        "ffn_fb": _largest_divisor(ffn_wg.shape[2], 1408, LANES),
        "w_in1": jnp.pad(w_in1[0], ((0, 0), (0, in1_pad))).astype(BF16),
        "w_out1": w_out1[0].astype(BF16),
        "router": jnp.pad(router[0], ((0, 0), (0, LANES - N_EXP))).astype(BF16),
        "exp_wg": exp_wg[0].astype(BF16), "exp_wu": exp_wu[0].astype(BF16), "exp_wd": exp_wd[0].astype(BF16),
        "exp_fb": _largest_divisor(exp_wg.shape[3], 896, LANES),
    }
    g_sub = subln_g[0].reshape(1, 2 * DH)
    plan1, out_defs1 = _in1_plan()

    xp = x_prompt.reshape(seq, d)
    xs = x_sample.reshape(bs_ * ts, d)
    xp, p_tail = _layer0(xp, None, gmlp_chunk, False, p)
    hist = jnp.pad(state_b_conv[0], ((0, 0), (SUBLANES - 2, 0), (0, 0)))
    xs, s_tail, s_av = _layer0(xs, hist, ts, True, p)
    p_b_conv = p_tail[-1, SUBLANES - 2:, :].reshape(1, 1, 2, W_B)
    s_b_conv = s_tail[:, SUBLANES - 2:, :].reshape(1, bs_, 2, W_B)
    s_a_v = s_av.reshape(1, bs_, ts, W_A)

    lb = KEY_BLOCK
    w1 = w_in1[0]
    off = np.concatenate([[0], np.cumsum(IN1_SIZES)])
    field = lambda k: w1[:, off[k]:off[k + 1]]
    padc = lambda a: jnp.pad(a, ((0, 0), (0, LANES - a.shape[1])))
    log2e = math.log2(math.e)
    qscale = DH ** -0.5 * log2e
    w_norm = jnp.concatenate([field(1), field(4), field(5), padc(field(7))], axis=1).astype(BF16)
    w_tran = jnp.concatenate([field(0) * qscale, field(3) * qscale, field(6), field(2), field(5),
                              field(1), field(4), padc(field(8))], axis=1).T.astype(BF16)
    (vd32, ki32, kc, kd, ki, qct, qdt, qit, vct, vdt, vc32t, kc32t, kd32t, wit) = _proj1_t(
        xp, p["ln_mix1"], w_norm, w_tran, PROMPT_TQ)
    token_major = lambda a, *dims: jnp.moveaxis(a.reshape(dims + (seq,)), -1, 0).reshape((1, 1, seq) + dims)
    p_c_k, p_c_v = token_major(kc32t, H_C, DH), token_major(vc32t, H_C, DH)
    p_d_k = token_major(kd32t, H_D, 2, DH)
    pos = jnp.arange(seq, dtype=I32)
    ve = ((pos // CHUNK + 1) * CHUNK).reshape(1, seq)
    mask = _select_t(qit, wit, ve, ki, min(TOPK_MAX, seq // 4))
    e_far = -(-(PROMPT_LB - 1 + MAX_DIST) // PROMPT_TQ)
    kpos = np.arange(PROMPT_LB)
    qpos = np.arange(PROMPT_TQ)
    base = e_far * PROMPT_TQ
    rel_near = (rel_bias - rel_bias[N_BUCKETS // 2 - 1:N_BUCKETS // 2]) * log2e
    btiles = jnp.stack([jnp.transpose(_bias_tile(rel_near, base + e * PROMPT_TQ + qpos, base + kpos, base + PROMPT_LB),
                                      (0, 2, 1)) for e in range(e_far)])
    op = _attend_t(qct, qdt, kc, kd, vct, vdt, mask, btiles, lam, subln_g[0].reshape(2 * DH, 1), lam_init)
    y_prompt = _layer1_tail(xp, op, p).reshape(1, seq, d)

    sr = _rms_proj(xs, p["ln_mix1"], p["w_in1"], plan1, out_defs1, _largest_divisor(bs_ * ts, 256, 16))
    sqc, skc32, skc, svc32, svc, sqd, skd32, skd, svd32, svd, sqi, ski32, ski, swi = sr
    nk = past + ts
    lps = -(-nk // lb) * lb

    kis = jnp.pad(jnp.concatenate([cache_idx_k[0].astype(BF16), ski.reshape(bs_, ts, D_I)], axis=1),
                  ((0, 0), (0, lps - nk), (0, 0)))
    ves = jnp.full((bs_ * ts, 1), nk, I32)
    smask = _select(sqi, swi, ves, kis, bs_, ts, ts, min(TOPK_MAX, nk // 4), lambda i: lps // lb)
    sq_pos = past + np.arange(ts)
    sbt = jnp.stack([_bias_tile(rel_bias, sq_pos, j * lb + np.arange(lb), nk) for j in range(lps // lb)])
    sbias = jnp.concatenate([sbt[:, :H_C], jnp.repeat(sbt[:, H_C:], 2, axis=1)], axis=1).reshape(lps // lb, N_MAPS * ts, lb)

    def block_diag(q):
        qb = jnp.transpose(q.reshape(H_C, bs_, ts, DH), (1, 0, 2, 3))
        eye = jnp.eye(H_C, dtype=q.dtype)
        return (qb[:, :, :, None, :] * eye[None, :, None, :, None]).reshape(bs_, H_C * ts, W_C)

    def cache_rows(c, feature_major):
        c = c.reshape(bs_, past, W_C)
        return jnp.transpose(c, (0, 2, 1)) if feature_major else c

    def new_rows(a, feature_major):
        a = jnp.pad(a.reshape(bs_, ts, W_C), ((0, 0), (0, lb - ts), (0, 0)))
        return jnp.transpose(a, (0, 2, 1)) if feature_major else a
    os_ = _attend_s(block_diag(sqc), block_diag(sqd),
                    [cache_rows(c[0], fm) for c, fm in ((cache_c_k, True), (cache_c_v, True), (cache_d_k, True), (cache_d_v, False))],
                    [new_rows(a, fm) for a, fm in ((skc32, True), (svc32, True), (skd32, True), (svd32, False))],
                    smask, sbias, lam, g_sub, lam_init)
    y_sample = _layer1_tail(xs, os_, p).reshape(bs_, ts, d)

    r5 = lambda a, n, t, *tail: a.reshape((1, n, t) + tail)
    return (y_prompt, y_sample, p_b_conv,
            p_c_k, p_c_v, r5(ki32, 1, seq, D_I), p_d_k, r5(vd32, 1, seq, H_D, 2 * DH),
            s_a_v, s_b_conv,
            r5(skc32, bs_, ts, H_C, DH), r5(svc32, bs_, ts, H_C, DH), r5(ski32, bs_, ts, D_I),
            r5(skd32, bs_, ts, H_D, 2, DH), r5(svd32, bs_, ts, H_D, 2 * DH))
```

```python
import functools
import math

import jax
import jax.numpy as jnp
import numpy as np
from jax import lax
from jax.experimental import pallas as pl
from jax.experimental.pallas import tpu as pltpu

F32 = jnp.float32
BF16 = jnp.bfloat16
I32 = jnp.int32

CHUNK = 64
EPS = 1e-6
NEG_INF = -1e30
H_A = 4
C_A = 128
W_A = H_A * C_A
W_B = 512
H_C = 8
DH = 64
W_C = H_C * DH
H_I = 8
D_I = 64
TOPK_MAX = 256
H_D = 4
W_D = H_D * 2 * DH
N_BUCKETS = 32
MAX_DIST = 128
N_EXP = 8
IN1_SIZES = (W_C, W_C, W_C, W_D, W_D, W_D, H_I * D_I, D_I, H_I)
N_HEADS_BIAS = H_C + H_D
N_MAPS = H_C + 2 * H_D

LANES = 128
SUBLANES = 8
VMEM_LIMIT = 56 * 1024 * 1024

INT_MIN = -(2 ** 31)
INT_MAX = 2 ** 31 - 1
ALL_TIES = 2 ** 30

KEY_BLOCK = 512


def _cparams(*sem):
    return pltpu.CompilerParams(dimension_semantics=sem, vmem_limit_bytes=VMEM_LIMIT)


def _largest_divisor(n, target, mult):
    if n <= target:
        return n
    d = (target // mult) * mult
    while d >= mult:
        if n % d == 0:
            return d
        d -= mult
    raise ValueError(f"no block of multiple {mult} divides {n}")


def _rms(x, g):
    return x * lax.rsqrt(jnp.mean(x * x, axis=-1, keepdims=True) + EPS) * g


def _dot(a, b):
    return jnp.dot(a, b, preferred_element_type=F32)


def _dot_nt(a, b):
    return lax.dot_general(a, b, (((1,), (1,)), ((), ())), preferred_element_type=F32)


def _rms_proj_body(x_ref, g_ref, w_ref, *out_refs, plan):
    xn = _rms(x_ref[...], g_ref[...]).astype(BF16)
    for (c0, c1), writes in plan:
        z = _dot(xn, w_ref[:, c0:c1])
        for o_idx, head, z0, z1, scale in writes:
            val = z[:, z0:z1]
            if scale != 1.0:
                val = val * scale
            ref = out_refs[o_idx]
            if head is None:
                ref[...] = val.astype(ref.dtype)
            else:
                ref[head] = val.astype(ref.dtype)


def _rms_proj(x, g, w, plan, out_defs, rows):
    n, d = x.shape
    grid = (n // rows,)
    out_shape, out_specs = [], []
    for width, dtype, heads in out_defs:
        if heads is None:
            out_shape.append(jax.ShapeDtypeStruct((n, width), dtype))
            out_specs.append(pl.BlockSpec((rows, width), lambda i: (i, 0)))
        else:
            out_shape.append(jax.ShapeDtypeStruct((heads, n, width), dtype))
            out_specs.append(pl.BlockSpec((heads, rows, width), lambda i: (0, i, 0)))
    return pl.pallas_call(
        functools.partial(_rms_proj_body, plan=plan),
        grid=grid,
        in_specs=[pl.BlockSpec((rows, d), lambda i: (i, 0)),
                  pl.BlockSpec((1, d), lambda i: (0, 0)),
                  pl.BlockSpec(w.shape, lambda i: (0, 0))],
        out_specs=out_specs,
        out_shape=out_shape,
        compiler_params=_cparams("parallel"),
        name="rms_proj",
    )(x, g, w)


def _mixer_ab_body(*refs, rb, from_prev, emit_v):
    it = iter(refs)
    x_ref, u_ref, v_ref, gb_ref, gc_ref, xin_ref = (next(it) for _ in range(6))
    if from_prev:
        gcp_ref, xinp_ref = next(it), next(it)
    else:
        hist_ref = next(it)
    lng_ref, lnb_ref, ws_ref, bs_ref, ck_ref, wout_ref = (next(it) for _ in range(6))
    x1_ref, tail_ref = next(it), next(it)
    vout_ref = next(it) if emit_v else None
    wext_ref = next(it)

    w = gc_ref[...] * xin_ref[...]
    if from_prev:
        hist = jnp.where(pl.program_id(0) > 0, gcp_ref[...] * xinp_ref[...], 0.0)
    else:
        hist = hist_ref[0]
    wext_ref[0:SUBLANES, :] = hist
    wext_ref[SUBLANES:, :] = w
    ck = ck_ref[...]
    conv = (ck[0:1] * wext_ref[SUBLANES - 2:SUBLANES - 2 + rb, :]
            + ck[1:2] * wext_ref[SUBLANES - 1:SUBLANES - 1 + rb, :]
            + ck[2:3] * w)
    y_b = gb_ref[...] * conv
    tail_ref[0] = w[rb - SUBLANES:, :]

    u = jax.nn.gelu(u_ref[...])
    v = jax.nn.gelu(v_ref[...])
    lng = lng_ref[...]
    lnb = lnb_ref[...]
    bs = bs_ref[...]
    acc = _dot(y_b.astype(BF16), wout_ref[W_A:, :])
    for h in range(H_A):
        sl = slice(h * C_A, (h + 1) * C_A)
        vh = v[:, sl]
        mu = jnp.mean(vh, axis=-1, keepdims=True)
        xc = vh - mu
        var = jnp.mean(xc * xc, axis=-1, keepdims=True)
        vln = xc * lax.rsqrt(var + EPS) * lng[:, sl] + lnb[:, sl]
        if emit_v:
            vout_ref[:, sl] = vln
        s = _dot(ws_ref[h], vln.astype(BF16)) + bs[:, h:h + 1]
        y_a = u[:, sl] * s
        acc = acc + _dot(y_a.astype(BF16), wout_ref[sl, :])
    x1_ref[...] = x_ref[...] + acc


def _mixer_ab(x, z, hist, lng, lnb, ws, bs, ck, wout, rb, emit_v):
    n, d = x.shape
    nb = n // rb
    from_prev = hist is None
    col = lambda c: pl.BlockSpec((rb, 512), lambda i, c=c: (i, c))
    in_specs = [pl.BlockSpec((rb, d), lambda i: (i, 0)), col(0), col(1), col(2), col(3), col(4)]
    args = [x, z, z, z, z, z]
    if from_prev:
        per = rb // SUBLANES
        prev = lambda c: pl.BlockSpec((SUBLANES, 512), lambda i, c=c: (jnp.maximum(i * per - 1, 0), c))
        in_specs += [prev(3), prev(4)]
        args += [z, z]
    else:
        in_specs += [pl.BlockSpec((1, SUBLANES, 512), lambda i: (i, 0, 0))]
        args += [hist]
    const = lambda a: pl.BlockSpec(a.shape, lambda i, nd=a.ndim: (0,) * nd)
    for a in (lng, lnb, ws, bs, ck, wout):
        in_specs.append(const(a))
        args.append(a)
    out_shape = [jax.ShapeDtypeStruct((n, d), F32), jax.ShapeDtypeStruct((nb, SUBLANES, 512), F32)]
    out_specs = [pl.BlockSpec((rb, d), lambda i: (i, 0)), pl.BlockSpec((1, SUBLANES, 512), lambda i: (i, 0, 0))]
    if emit_v:
        out_shape.append(jax.ShapeDtypeStruct((n, W_A), F32))
        out_specs.append(pl.BlockSpec((rb, W_A), lambda i: (i, 0)))
    return pl.pallas_call(
        functools.partial(_mixer_ab_body, rb=rb, from_prev=from_prev, emit_v=emit_v),
        grid=(nb,),
        in_specs=in_specs,
        out_specs=out_specs,
        out_shape=out_shape,
        scratch_shapes=[pltpu.VMEM((rb + SUBLANES, 512), F32)],
        compiler_params=_cparams("arbitrary"),
        name="mixer_ab",
    )(*args)


def _ffn_body(x_ref, g_ref, wg_ref, wu_ref, wd_ref, o_ref, xn_ref, acc_ref):
    k = pl.program_id(1)

    @pl.when(k == 0)
    def _():
        xn_ref[...] = _rms(x_ref[...], g_ref[...]).astype(BF16)
        acc_ref[...] = jnp.zeros_like(acc_ref)

    xn = xn_ref[...]
    h = jax.nn.silu(_dot(xn, wg_ref[...])) * _dot(xn, wu_ref[...])
    acc_ref[...] += _dot(h.astype(BF16), wd_ref[...])

    @pl.when(k == pl.num_programs(1) - 1)
    def _():
        o_ref[...] = x_ref[...] + acc_ref[...]


def _ffn(x, g, wg, wu, wd, rows, fb):
    n, d = x.shape
    dff = wg.shape[1]
    return pl.pallas_call(
        _ffn_body,
        grid=(n // rows, dff // fb),
        in_specs=[pl.BlockSpec((rows, d), lambda i, k: (i, 0)),
                  pl.BlockSpec((1, d), lambda i, k: (0, 0)),
                  pl.BlockSpec((d, fb), lambda i, k: (0, k)),
                  pl.BlockSpec((d, fb), lambda i, k: (0, k)),
                  pl.BlockSpec((fb, d), lambda i, k: (k, 0))],
        out_specs=pl.BlockSpec((rows, d), lambda i, k: (i, 0)),
        out_shape=jax.ShapeDtypeStruct((n, d), F32),
        scratch_shapes=[pltpu.VMEM((rows, d), BF16), pltpu.VMEM((rows, d), F32)],
        compiler_params=_cparams("parallel", "arbitrary"),
        name="ffn",
    )(x, g, wg, wu, wd)


def _select_body(qi_ref, wi_ref, ve_ref, ki_ref, out_ref, keys_ref, *, tq, lb, nkb, topk, nvalid_fn):
    nv = nvalid_fn(pl.program_id(1))
    wi = wi_ref[...]
    ve = ve_ref[...]
    qs = [qi_ref[h] for h in range(H_I)]
    wcols = [wi[:, h:h + 1] for h in range(H_I)]
    lane = lax.broadcasted_iota(I32, (tq, lb), 1)

    def score_block(b, carry):
        kb = ki_ref[0, pl.ds(pl.multiple_of(b * lb, lb), lb), :]
        sc = jnp.zeros((tq, lb), F32)
        for h in range(H_I):
            sc = sc + wcols[h] * jnp.maximum(_dot_nt(qs[h], kb), 0.0)
        bits = lax.bitcast_convert_type(sc, I32)
        key = bits ^ ((bits >> 31) & INT_MAX)
        key = jnp.where(sc == 0.0, 0, key)
        key = jnp.where(lane + b * lb < ve, key, INT_MIN)
        keys_ref[b] = key
        return carry

    lax.fori_loop(0, nv, score_block, 0)

    def count_ge(mid):
        midb = jnp.broadcast_to(mid, (tq, LANES))

        def body(b, acc):
            for c in range(lb // LANES):
                k = keys_ref[b, :, c * LANES:(c + 1) * LANES]
                acc = acc + jnp.where(k >= midb, 1, 0)
            return acc

        acc = lax.fori_loop(0, nv, body, jnp.zeros((tq, LANES), I32))
        return jnp.sum(acc.astype(F32), axis=-1, keepdims=True).astype(I32)

    def bisect(_, st):
        lo, hi, clo, chi = st
        mid = (lo >> 1) + (hi >> 1) + (lo & hi & 1)
        cnt = count_ge(mid)
        active = mid != lo
        up = jnp.logical_and(active, cnt >= topk)
        dn = jnp.logical_and(active, cnt < topk)
        return (jnp.where(up, mid, lo), jnp.where(dn, mid, hi),
                jnp.where(up, cnt, clo), jnp.where(dn, cnt, chi))

    full = lambda v: jnp.full((tq, 1), v, I32)
    lo, hi, clo, chi = lax.fori_loop(0, 32, bisect, (full(INT_MIN), full(INT_MAX), ve, full(0)))

    need = jnp.where(clo > topk, topk - chi, ALL_TIES)
    need = jnp.where(lo == INT_MIN, 0, need).astype(F32)
    lob = jnp.broadcast_to(lo, (tq, LANES))
    needb = jnp.broadcast_to(need, (tq, LANES))
    r = lax.broadcasted_iota(I32, (LANES, LANES), 0)
    c = lax.broadcasted_iota(I32, (LANES, LANES), 1)
    tri = jnp.where(r <= c, 1.0, 0.0).astype(BF16)

    def mask_block(b, seen):
        for cc in range(lb // LANES):
            k = keys_ref[b, :, cc * LANES:(cc + 1) * LANES]
            eq = jnp.where(k == lob, 1.0, 0.0)
            cum = _dot(eq.astype(BF16), tri)
            rank = seen + cum - eq
            take = jnp.where(rank < needb, eq, 0.0)
            sel = jnp.where(k > lob, 1.0, take)
            out_ref[0, b, :, cc * LANES:(cc + 1) * LANES] = jnp.where(sel > 0.5, 0.0, NEG_INF).astype(out_ref.dtype)
            seen = seen + cum[:, LANES - 1:LANES]
        return seen

    lax.fori_loop(0, nv, mask_block, jnp.zeros((tq, 1), F32))

    def fill_block(b, carry):
        out_ref[0, b] = jnp.full((tq, lb), NEG_INF, out_ref.dtype)
        return carry

    lax.fori_loop(nv, nkb, fill_block, 0)


def _select(qi, wi, ve, ki, nbatch, t, tq, topk, nvalid_fn):
    lp = ki.shape[1]
    lb = KEY_BLOCK
    nkb = lp // lb
    nq = t // tq
    row = lambda b, i: b * nq + i
    return pl.pallas_call(
        functools.partial(_select_body, tq=tq, lb=lb, nkb=nkb, topk=topk, nvalid_fn=nvalid_fn),
        grid=(nbatch, nq),
        in_specs=[pl.BlockSpec((H_I, tq, D_I), lambda b, i: (0, row(b, i), 0)),
                  pl.BlockSpec((tq, H_I), lambda b, i: (row(b, i), 0)),
                  pl.BlockSpec((tq, 1), lambda b, i: (row(b, i), 0)),
                  pl.BlockSpec((1, lp, D_I), lambda b, i: (b, 0, 0))],
        out_specs=pl.BlockSpec((1, nkb, tq, lb), lambda b, i: (b, 0, i, 0)),
        out_shape=jax.ShapeDtypeStruct((nbatch, nkb, t, lb), BF16),
        scratch_shapes=[pltpu.VMEM((nkb, tq, lb), I32)],
        compiler_params=_cparams("parallel", "arbitrary"),
        name="index_select",
    )(qi, wi, ve, ki)


def _attn_body(qc_ref, qd_ref, kc_ref, vc_ref, kd_ref, vd_ref, mask_ref, bt_ref, lam_ref, g_ref,
               o_ref, m_ref, l_ref, accc_ref, accd_ref, *, tq, nvalid_fn, lam_init):
    i = pl.program_id(1)
    j = pl.program_id(2)

    @pl.when(j == 0)
    def _():
        m_ref[...] = jnp.full(m_ref.shape, NEG_INF, F32)
        l_ref[...] = jnp.zeros_like(l_ref)
        accc_ref[...] = jnp.zeros_like(accc_ref)
        accd_ref[...] = jnp.zeros_like(accd_ref)

    @pl.when(j < nvalid_fn(i))
    def _():
        sel = mask_ref[0, 0].astype(F32)
        for mp in range(N_MAPS):
            sparse = mp < H_C
            if sparse:
                s = _dot_nt(qc_ref[mp], kc_ref[0, mp]) + bt_ref[0, mp] + sel
                v = vc_ref[0, mp]
                acc_ref, a_idx, dv = accc_ref, mp, DH
            else:
                dm = mp - H_C
                s = _dot_nt(qd_ref[dm], kd_ref[0, dm]) + bt_ref[0, H_C + dm // 2]
                v = vd_ref[0, dm // 2]
                acc_ref, a_idx, dv = accd_ref, dm, 2 * DH
            m_prev = m_ref[mp]
            m_new = jnp.maximum(m_prev, jnp.max(s, axis=-1, keepdims=True))
            alpha = jnp.exp(m_prev - m_new)
            p = jnp.exp(s - m_new[:, 0:1])
            l_ref[mp] = alpha * l_ref[mp] + jnp.sum(p, axis=-1, keepdims=True)
            m_ref[mp] = m_new
            acc_ref[a_idx] = alpha[:, 0:dv] * acc_ref[a_idx] + _dot(p.astype(BF16), v)

    @pl.when(j == pl.num_programs(2) - 1)
    def _():
        for h in range(H_C):
            o_ref[:, h * DH:(h + 1) * DH] = (accc_ref[h] / l_ref[h][:, 0:DH]).astype(o_ref.dtype)
        lam = lam_ref[...]
        g = g_ref[...]
        for h in range(H_D):
            a0 = accd_ref[2 * h] / l_ref[H_C + 2 * h]
            a1 = accd_ref[2 * h + 1] / l_ref[H_C + 2 * h + 1]
            od = _rms(a0 - lam * a1, g) * (1.0 - lam_init)
            o_ref[:, W_C + h * 2 * DH:W_C + (h + 1) * 2 * DH] = od.astype(o_ref.dtype)


def _attend(qc, qd, kc, vc, kd, vd, mask, btiles, lam, g, nbatch, t, tq, nvalid_fn, tile_fn, lam_init):
    lp = kc.shape[2]
    lb = KEY_BLOCK
    nkb = lp // lb
    nq = t // tq
    row = lambda b, i: b * nq + i
    kblk = lambda i, j: jnp.minimum(j, nvalid_fn(i) - 1)
    qspec = pl.BlockSpec((H_C, tq, DH), lambda b, i, j: (0, row(b, i), 0))
    kspec = lambda nh, w: pl.BlockSpec((1, nh, lb, w), lambda b, i, j: (b, 0, kblk(i, j), 0))
    return pl.pallas_call(
        functools.partial(_attn_body, tq=tq, nvalid_fn=nvalid_fn, lam_init=lam_init),
        grid=(nbatch, nq, nkb),
        in_specs=[qspec, qspec, kspec(H_C, DH), kspec(H_C, DH), kspec(2 * H_D, DH), kspec(H_D, 2 * DH),
                  pl.BlockSpec((1, 1, tq, lb), lambda b, i, j: (b, kblk(i, j), i, 0)),
                  pl.BlockSpec((1, N_HEADS_BIAS, tq, lb), lambda b, i, j: (tile_fn(i, kblk(i, j)), 0, 0, 0)),
                  pl.BlockSpec((1, 1), lambda b, i, j: (0, 0)),
                  pl.BlockSpec((1, 2 * DH), lambda b, i, j: (0, 0))],
        out_specs=pl.BlockSpec((tq, W_C + W_D), lambda b, i, j: (row(b, i), 0)),
        out_shape=jax.ShapeDtypeStruct((nbatch * t, W_C + W_D), BF16),
        scratch_shapes=[pltpu.VMEM((N_MAPS, tq, LANES), F32), pltpu.VMEM((N_MAPS, tq, LANES), F32),
                        pltpu.VMEM((H_C, tq, DH), F32), pltpu.VMEM((2 * H_D, tq, 2 * DH), F32)],
        compiler_params=_cparams("parallel", "parallel", "arbitrary"),
        name="attend",
    )(qc, qd, kc, vc, kd, vd, mask, btiles, lam, g)


PROMPT_TQ = 256
PROMPT_LB = 512
SEL_ROWS = 256
TIE_ROWS = 128
MAPS_PER_DOT = 4


def _proj1_t_body(x_ref, g_ref, w_ref, wt_ref, vd32_ref, ki32_ref, kc_ref, kd_ref, ki_ref,
                  qct_ref, qdt_ref, qit_ref, vct_ref, vdt_ref, vc32t_ref, kc32t_ref, kd32t_ref, wit_ref):
    xn32 = _rms(x_ref[...], g_ref[...])
    xn = xn32.astype(BF16)
    xnt = xn32.T.astype(BF16)
    kc_ref[...] = _dot(xn, w_ref[:, 0:512]).astype(BF16)
    kd_ref[...] = _dot(xn, w_ref[:, 512:1024]).astype(BF16)
    vd32_ref[...] = _dot(xn, w_ref[:, 1024:1536])
    z = _dot(xn, w_ref[:, 1536:1536 + LANES])
    ki32_ref[...] = z[:, 0:D_I]
    ki_ref[...] = z[:, 0:D_I].astype(BF16)
    for c, (head_ref, full_ref) in enumerate(((qct_ref, None), (qdt_ref, None), (qit_ref, None), (vct_ref, vc32t_ref),
                                              (vdt_ref, None), (None, kc32t_ref), (None, kd32t_ref))):
        zt = _dot(wt_ref[c * 512:(c + 1) * 512, :], xnt)
        if full_ref is not None:
            full_ref[...] = zt
        if head_ref is not None:
            nh = head_ref.shape[0]
            for h in range(nh):
                head_ref[h] = zt[h * (512 // nh):(h + 1) * (512 // nh), :].astype(BF16)
    zt = _dot(wt_ref[3584:3584 + LANES, :], xnt)
    wit_ref[...] = zt[0:H_I, :]


def _proj1_t(x, g, w, wt, rows):
    n, d = x.shape
    full = lambda width: (jax.ShapeDtypeStruct((n, width), F32), pl.BlockSpec((rows, width), lambda i: (i, 0)))
    packed = lambda width: (jax.ShapeDtypeStruct((n, width), BF16), pl.BlockSpec((rows, width), lambda i: (i, 0)))
    heads_t = lambda nh, width: (jax.ShapeDtypeStruct((nh, width, n), BF16),
                                 pl.BlockSpec((nh, width, rows), lambda i: (0, 0, i)))
    full_t = lambda: (jax.ShapeDtypeStruct((512, n), F32), pl.BlockSpec((512, rows), lambda i: (0, i)))
    outs = [full(512), full(D_I), packed(512), packed(512), packed(D_I),
            heads_t(8, DH), heads_t(8, DH), heads_t(8, DH), heads_t(8, DH), heads_t(H_D, 2 * DH),
            full_t(), full_t(), full_t(),
            (jax.ShapeDtypeStruct((H_I, n), F32), pl.BlockSpec((H_I, rows), lambda i: (0, i)))]
    return pl.pallas_call(
        _proj1_t_body,
        grid=(n // rows,),
        in_specs=[pl.BlockSpec((rows, d), lambda i: (i, 0)),
                  pl.BlockSpec((1, d), lambda i: (0, 0)),
                  pl.BlockSpec(w.shape, lambda i: (0, 0)),
                  pl.BlockSpec(wt.shape, lambda i: (0, 0))],
        out_specs=[o[1] for o in outs],
        out_shape=[o[0] for o in outs],
        compiler_params=_cparams("parallel"),
        name="proj1_t",
    )(x, g, w, wt)


def _sublane_all(x8, op):
    for shift in (4, 2, 1):
        x8 = op(x8, pltpu.roll(x8, shift, 0))
    return x8


def _select_t_body(qit_ref, wit_ref, ve_ref, ki_ref, out_ref, keys_ref, gmax_ref, *, tq, lp, topk):
    rb = SEL_ROWS
    nv = (pl.program_id(0) + 1) * (tq // rb)
    wit = wit_ref[...]
    ve = ve_ref[...]
    rows = lax.broadcasted_iota(I32, (rb, tq), 0)
    gmax_ref[...] = jnp.full((rb, tq), INT_MIN, I32)

    def score_chunk(c, carry):
        r0 = pl.multiple_of(c * rb, rb)
        kb = ki_ref[pl.ds(r0, rb), :]
        sc = jnp.zeros((rb, tq), F32)
        for h in range(H_I):
            sc = sc + wit[h:h + 1, :] * jnp.maximum(_dot(kb, qit_ref[h]), 0.0)
        bits = lax.bitcast_convert_type(sc, I32)
        key = bits ^ ((bits >> 31) & INT_MAX)
        key = jnp.where(sc == 0.0, 0, key)
        key = jnp.where(rows + r0 < ve, key, INT_MIN)
        keys_ref[pl.ds(r0, rb), :] = key
        gmax_ref[...] = jnp.maximum(gmax_ref[...], key)
        return carry

    lax.fori_loop(0, nv, score_chunk, 0)

    def count_ge(mid):
        midb = jnp.broadcast_to(mid, (SUBLANES, tq))

        def body(c, accs):
            kc = keys_ref[pl.ds(pl.multiple_of(c * rb, rb), rb), :]
            accs = list(accs)
            for g in range(rb // SUBLANES):
                a = g % len(accs)
                accs[a] = accs[a] + jnp.where(kc[g * SUBLANES:(g + 1) * SUBLANES, :] >= midb, 1, 0)
            return tuple(accs)

        accs = lax.fori_loop(0, nv, body, (jnp.zeros((SUBLANES, tq), I32),) * 4)
        return _sublane_all(accs[0] + accs[1] + accs[2] + accs[3], jnp.add)[0:1, :]

    floor_avg = lambda a, b: (a >> 1) + (b >> 1) + (a & b & 1)

    def pending(lo, hi, clo):
        open_ = jnp.logical_and(floor_avg(lo, hi) != lo, clo > topk)
        return jnp.max(jnp.where(open_, 1.0, 0.0))

    def key_as_score(k):
        return lax.bitcast_convert_type(k ^ ((k >> 31) & INT_MAX), F32)

    def score_as_key(v):
        bits = lax.bitcast_convert_type(v, I32)
        return bits ^ ((bits >> 31) & INT_MAX)

    def bisect(st):
        lo, hi, clo, chi, it, _ = st
        half = floor_avg(lo, hi)
        lclo = jnp.log2(clo.astype(F32))
        frac = (lclo - math.log2(topk + 0.5)) / (lclo - jnp.log2(jnp.maximum(chi.astype(F32), 0.5)))
        slo, shi = key_as_score(lo), key_as_score(hi)
        guess = jnp.clip(score_as_key(slo + frac * (shi - slo)), lo + 1, hi - 1)
        use_guess = jnp.logical_and(it % 2 == 0, lo != INT_MIN)
        mid = jnp.where(use_guess, guess, half)
        cnt = count_ge(mid)
        active = jnp.logical_and(half != lo, clo > topk)
        up = jnp.logical_and(active, cnt >= topk)
        dn = jnp.logical_and(active, cnt < topk)
        lo, hi = jnp.where(up, mid, lo), jnp.where(dn, mid, hi)
        clo, chi = jnp.where(up, cnt, clo), jnp.where(dn, cnt, chi)
        return lo, hi, clo, chi, it + 1, pending(lo, hi, clo)

    g8 = gmax_ref[0:SUBLANES, :]
    h8 = g8
    for g in range(1, rb // SUBLANES):
        blk = gmax_ref[g * SUBLANES:(g + 1) * SUBLANES, :]
        g8 = jnp.minimum(g8, blk)
        h8 = jnp.maximum(h8, blk)
    lo0 = _sublane_all(g8, jnp.minimum)[0:1, :]
    hi0 = _sublane_all(h8, jnp.maximum)[0:1, :] + 1
    clo0 = jnp.where(lo0 == INT_MIN, ve, count_ge(lo0))
    lo, hi, clo, chi, _, _ = lax.while_loop(
        lambda st: st[5] > 0.5, bisect,
        (lo0, hi0, clo0, jnp.zeros((1, tq), I32), jnp.int32(0), pending(lo0, hi0, clo0)))

    need = jnp.where(clo > topk, topk - chi, ALL_TIES)
    need = jnp.where(lo == INT_MIN, 0, need).astype(F32)
    tr = TIE_ROWS
    lob = jnp.broadcast_to(lo, (tr, tq))
    needb = jnp.broadcast_to(need, (tr, tq))
    r = lax.broadcasted_iota(I32, (tr, tr), 0)
    c = lax.broadcasted_iota(I32, (tr, tr), 1)
    tri = jnp.where(c <= r, 1.0, 0.0).astype(BF16)

    def mask_chunk(cidx, seen):
        r0 = pl.multiple_of(cidx * tr, tr)
        k = keys_ref[pl.ds(r0, tr), :]
        eq = jnp.where(k == lob, 1.0, 0.0)
        cum = _dot(tri, eq.astype(BF16))
        take = jnp.where(seen + cum - eq < needb, eq, 0.0)
        sel = jnp.where(k > lob, 1.0, take)
        out_ref[pl.ds(r0, tr), :] = jnp.where(sel > 0.5, 0.0, NEG_INF).astype(out_ref.dtype)
        return seen + cum[tr - 1:tr, :]

    def plain_chunk(cidx, carry):
        r0 = pl.multiple_of(cidx * rb, rb)
        k = keys_ref[pl.ds(r0, rb), :]
        hit = jnp.where(k == INT_MIN, NEG_INF, 0.0)
        out_ref[pl.ds(r0, rb), :] = jnp.where(k >= lo, hit, NEG_INF).astype(out_ref.dtype)
        return carry

    any_tie = jnp.max(jnp.where(clo > topk, 1.0, 0.0)) > 0.5
    lax.cond(any_tie,
             lambda: lax.fori_loop(0, nv * (rb // tr), mask_chunk, jnp.zeros((1, tq), F32)),
             lambda: lax.fori_loop(0, nv, plain_chunk, jnp.zeros((1, tq), F32)))

    def fill_chunk(cidx, carry):
        out_ref[pl.ds(pl.multiple_of(cidx * rb, rb), rb), :] = jnp.full((rb, tq), NEG_INF, out_ref.dtype)
        return carry

    lax.fori_loop(nv, lp // rb, fill_chunk, 0)


def _select_t(qit, wit, ve, ki, topk):
    t = qit.shape[2]
    tq = PROMPT_TQ
    return pl.pallas_call(
        functools.partial(_select_t_body, tq=tq, lp=t, topk=topk),
        grid=(t // tq,),
        in_specs=[pl.BlockSpec((H_I, D_I, tq), lambda i: (0, 0, i)),
                  pl.BlockSpec((H_I, tq), lambda i: (0, i)),
                  pl.BlockSpec((1, tq), lambda i: (0, i)),
                  pl.BlockSpec((t, D_I), lambda i: (0, 0))],
        out_specs=pl.BlockSpec((t, tq), lambda i: (0, i)),
        out_shape=jax.ShapeDtypeStruct((t, t), BF16),
        scratch_shapes=[pltpu.VMEM((t, tq), I32), pltpu.VMEM((SEL_ROWS, tq), I32)],
        compiler_params=_cparams("parallel"),
        name="index_select_t",
    )(qit, wit, ve, ki)


def _attn_t_body(qi_ref, kj_ref, qct_ref, qdt_ref, kc_ref, kd_ref, vct_ref, vdt_ref, mask_ref, bt_ref, lam_ref,
                 g_ref, o_ref, m_ref, l_ref, accc_ref, accd_ref, ot_ref, qbd_ref, s_ref, *, tq, per, e_far,
                 lam_init):
    i = qi_ref[pl.program_id(0)]
    j = kj_ref[pl.program_id(0)]
    e = i - j * per
    grp = MAPS_PER_DOT

    @pl.when(j == 0)
    def _():
        m_ref[...] = jnp.full(m_ref.shape, NEG_INF, F32)
        l_ref[...] = jnp.zeros_like(l_ref)
        accc_ref[...] = jnp.zeros_like(accc_ref)
        accd_ref[...] = jnp.zeros_like(accd_ref)
        qbd_ref[...] = jnp.zeros_like(qbd_ref)
        for mp in range(N_MAPS):
            q = qct_ref[mp] if mp < H_C else qdt_ref[mp - H_C]
            a = mp % grp
            qbd_ref[mp // grp, a * DH:(a + 1) * DH, a * tq:(a + 1) * tq] = q

    def logits(g):
        k_ref = kc_ref if g < H_C // grp else kd_ref
        half = g % (H_C // grp)
        return _dot(k_ref[:, half * grp * DH:(half + 1) * grp * DH], qbd_ref[g])

    def step(near):
        for g in range(N_MAPS // grp):
            s_ref[g] = logits(g)
        for mp in range(N_MAPS):
            g, a = mp // grp, mp % grp
            cols = slice(a * tq, (a + 1) * tq)
            if mp < H_C:
                bias_idx, vt, acc_ref, a_idx = mp, vct_ref[mp], accc_ref, mp
            else:
                dm = mp - H_C
                bias_idx, vt, acc_ref, a_idx = H_C + dm // 2, vdt_ref[dm // 2], accd_ref, dm

            def biased():
                s = s_ref[g, :, cols]
                if mp < H_C:
                    s = s + mask_ref[...].astype(F32)
                if near:
                    s = s + bt_ref[0, bias_idx]
                return s

            m_prev = m_ref[mp]
            m_new = jnp.maximum(m_prev, jnp.max(biased(), axis=0, keepdims=True))
            alpha = jnp.exp2(m_prev - m_new)
            p = jnp.exp2(biased() - m_new)
            l_ref[mp] = alpha * l_ref[mp] + jnp.sum(p, axis=0, keepdims=True)
            m_ref[mp] = m_new
            acc_ref[a_idx] = alpha * acc_ref[a_idx] + _dot(vt, p.astype(BF16))

    pl.when(e < e_far)(functools.partial(step, True))
    pl.when(e >= e_far)(functools.partial(step, False))

    @pl.when(j == i // per)
    def _():
        for h in range(H_C):
            ot_ref[h * DH:(h + 1) * DH, :] = accc_ref[h] / l_ref[h]
        lam = lam_ref[...]
        g = g_ref[...]
        for h in range(H_D):
            a0 = accd_ref[2 * h] / l_ref[H_C + 2 * h]
            a1 = accd_ref[2 * h + 1] / l_ref[H_C + 2 * h + 1]
            od = a0 - lam * a1
            od = od * lax.rsqrt(jnp.mean(od * od, axis=0, keepdims=True) + EPS) * g * (1.0 - lam_init)
            ot_ref[W_C + h * 2 * DH:W_C + (h + 1) * 2 * DH, :] = od
        o_ref[...] = ot_ref[...].T.astype(o_ref.dtype)


def _attend_t(qct, qdt, kc, kd, vct, vdt, mask, btiles, lam, g, lam_init):
    t = qct.shape[2]
    tq, lb = PROMPT_TQ, PROMPT_LB
    per = lb // tq
    e_far = btiles.shape[0]
    pairs = [(i, j) for i in range(t // tq) for j in range(i // per + 1)]
    qi_tab = jnp.asarray(np.array([p[0] for p in pairs], np.int32))
    kj_tab = jnp.asarray(np.array([p[1] for p in pairs], np.int32))
    qspec = pl.BlockSpec((H_C, DH, tq), lambda s, qi, kj: (0, 0, qi[s]))
    kspec = pl.BlockSpec((lb, H_C * DH), lambda s, qi, kj: (kj[s], 0))
    grid_spec = pltpu.PrefetchScalarGridSpec(
        num_scalar_prefetch=2,
        grid=(len(pairs),),
        in_specs=[qspec, qspec, kspec, kspec,
                  pl.BlockSpec((H_C, DH, lb), lambda s, qi, kj: (0, 0, kj[s])),
                  pl.BlockSpec((H_D, 2 * DH, lb), lambda s, qi, kj: (0, 0, kj[s])),
                  pl.BlockSpec((lb, tq), lambda s, qi, kj: (kj[s], qi[s])),
                  pl.BlockSpec((1, N_HEADS_BIAS, lb, tq),
                               lambda s, qi, kj: (jnp.minimum(qi[s] - kj[s] * per, e_far - 1), 0, 0, 0)),
                  pl.BlockSpec((1, 1), lambda s, qi, kj: (0, 0)),
                  pl.BlockSpec((2 * DH, 1), lambda s, qi, kj: (0, 0))],
        out_specs=pl.BlockSpec((tq, W_C + W_D), lambda s, qi, kj: (qi[s], 0)),
        scratch_shapes=[pltpu.VMEM((N_MAPS, 1, tq), F32), pltpu.VMEM((N_MAPS, 1, tq), F32),
                        pltpu.VMEM((H_C, DH, tq), F32), pltpu.VMEM((2 * H_D, 2 * DH, tq), F32),
                        pltpu.VMEM((W_C + W_D, tq), F32),
                        pltpu.VMEM((N_MAPS // MAPS_PER_DOT, MAPS_PER_DOT * DH, MAPS_PER_DOT * tq), BF16),
                        pltpu.VMEM((N_MAPS // MAPS_PER_DOT, lb, MAPS_PER_DOT * tq), F32)])
    return pl.pallas_call(
        functools.partial(_attn_t_body, tq=tq, per=per, e_far=e_far, lam_init=lam_init),
        grid_spec=grid_spec,
        out_shape=jax.ShapeDtypeStruct((t, W_C + W_D), BF16),
        compiler_params=_cparams("arbitrary"),
        name="attend_t",
    )(qi_tab, kj_tab, qct, qdt, kc, kd, vct, vdt, mask, btiles, lam, g)


def _attn_s_body(qc_ref, qd_ref, ck_ref, cv_ref, dk_ref, dv_ref, nck_ref, ncv_ref, ndk_ref, ndv_ref,
                 mask_ref, bias_ref, lam_ref, g_ref, o_ref, m_ref, l_ref, acc_ref, *, ts, ncache, lam_init):
    j = pl.program_id(1)
    rows = H_C * ts

    @pl.when(j == 0)
    def _():
        m_ref[...] = jnp.full(m_ref.shape, NEG_INF, F32)
        l_ref[...] = jnp.zeros_like(l_ref)
        acc_ref[...] = jnp.zeros_like(acc_ref)

    def step(kc_t, vc_t, kd_t, vd):
        sel = jnp.tile(mask_ref[0, 0].astype(F32), (H_C, 1))
        bias = bias_ref[0]
        s_c = _dot(qc_ref[0], kc_t.astype(BF16)) + sel + bias[0:rows]
        s_d = _dot(qd_ref[0], kd_t.astype(BF16)) + bias[rows:2 * rows]
        for idx, (s, v) in enumerate(((s_c, vc_t), (s_d, vd))):
            m_prev = m_ref[idx]
            m_new = jnp.maximum(m_prev, jnp.max(s, axis=-1, keepdims=True))
            alpha = jnp.exp(m_prev - m_new)
            p = jnp.exp(s - m_new)
            l_ref[idx] = alpha * l_ref[idx] + jnp.sum(p, axis=-1, keepdims=True)
            m_ref[idx] = m_new
            p = p.astype(BF16)
            pv = _dot_nt(p, v.astype(BF16)) if idx == 0 else _dot(p, v.astype(BF16))
            acc_ref[idx] = alpha * acc_ref[idx] + pv

    @pl.when(j < ncache)
    def _():
        step(ck_ref[0], cv_ref[0], dk_ref[0], dv_ref[0])

    @pl.when(j == ncache)
    def _():
        step(nck_ref[0], ncv_ref[0], ndk_ref[0], ndv_ref[0])
        for h in range(H_C):
            r = slice(h * ts, (h + 1) * ts)
            o_ref[:, h * DH:(h + 1) * DH] = (acc_ref[0, r, h * DH:(h + 1) * DH] / l_ref[0, r, :]).astype(o_ref.dtype)
        lam = lam_ref[...]
        g = g_ref[...]
        for h in range(H_D):
            r0 = slice(2 * h * ts, (2 * h + 1) * ts)
            r1 = slice((2 * h + 1) * ts, (2 * h + 2) * ts)
            c = slice(h * 2 * DH, (h + 1) * 2 * DH)
            od = acc_ref[1, r0, c] / l_ref[1, r0, :] - lam * (acc_ref[1, r1, c] / l_ref[1, r1, :])
            od = _rms(od, g) * (1.0 - lam_init)
            o_ref[:, W_C + h * 2 * DH:W_C + (h + 1) * 2 * DH] = od.astype(o_ref.dtype)


def _attend_s(qbd_c, qbd_d, caches, news, mask, bias, lam, g, lam_init):
    nb, rows, _ = qbd_c.shape
    ts = rows // H_C
    lb = KEY_BLOCK
    ncache = caches[3].shape[1] // lb
    qspec = pl.BlockSpec((1, rows, W_C), lambda b, j: (b, 0, 0))
    ctspec = pl.BlockSpec((1, W_C, lb), lambda b, j: (b, 0, jnp.minimum(j, ncache - 1)))
    cspec = pl.BlockSpec((1, lb, W_C), lambda b, j: (b, jnp.minimum(j, ncache - 1), 0))
    ntspec = pl.BlockSpec((1, W_C, lb), lambda b, j: (b, 0, 0))
    nspec = pl.BlockSpec((1, lb, W_C), lambda b, j: (b, 0, 0))
    return pl.pallas_call(
        functools.partial(_attn_s_body, ts=ts, ncache=ncache, lam_init=lam_init),
        grid=(nb, ncache + 1),
        in_specs=[qspec, qspec, ctspec, ctspec, ctspec, cspec, ntspec, ntspec, ntspec, nspec,
                  pl.BlockSpec((1, 1, ts, lb), lambda b, j: (b, j, 0, 0)),
                  pl.BlockSpec((1, 2 * rows, lb), lambda b, j: (j, 0, 0)),
                  pl.BlockSpec((1, 1), lambda b, j: (0, 0)),
                  pl.BlockSpec((1, 2 * DH), lambda b, j: (0, 0))],
        out_specs=pl.BlockSpec((ts, W_C + W_D), lambda b, j: (b, 0)),
        out_shape=jax.ShapeDtypeStruct((nb * ts, W_C + W_D), BF16),
        scratch_shapes=[pltpu.VMEM((2, rows, 1), F32), pltpu.VMEM((2, rows, 1), F32),
                        pltpu.VMEM((2, rows, W_C), F32)],
        compiler_params=_cparams("parallel", "arbitrary"),
        name="attend_s",
    )(qbd_c, qbd_d, *caches, *news, mask, bias, lam, g)


def _out_router_body(x_ref, o_ref, w_ref, g_ref, r_ref, x3_ref, xn_ref, gate_ref, gatet_ref):
    x3 = x_ref[...] + _dot(o_ref[...], w_ref[...])
    x3_ref[...] = x3
    xn = _rms(x3, g_ref[...]).astype(BF16)
    xn_ref[...] = xn
    logits = _dot(xn, r_ref[...])
    lane = lax.broadcasted_iota(I32, logits.shape, 1)
    logits = jnp.where(lane < N_EXP, logits, -jnp.inf)
    m1 = jnp.max(logits, axis=-1, keepdims=True)
    i1 = jnp.min(jnp.where(logits == m1, lane, LANES), axis=-1, keepdims=True)
    rest = jnp.where(lane == i1, -jnp.inf, logits)
    m2 = jnp.max(rest, axis=-1, keepdims=True)
    i2 = jnp.min(jnp.where(rest == m2, lane, LANES), axis=-1, keepdims=True)
    e = jnp.exp(m2 - m1)
    g1 = 1.0 / (1.0 + e)
    g2 = e / (1.0 + e)
    gate = jnp.where(lane == i1, g1, 0.0) + jnp.where(lane == i2, g2, 0.0)
    gate_ref[...] = gate
    gatet_ref[...] = gate.T[0:GATE_ROWS, :]


def _out_router(x, o, w, g, router, rows):
    n, d = x.shape
    return pl.pallas_call(
        _out_router_body,
        grid=(n // rows,),
        in_specs=[pl.BlockSpec((rows, d), lambda i: (i, 0)),
                  pl.BlockSpec((rows, d), lambda i: (i, 0)),
                  pl.BlockSpec(w.shape, lambda i: (0, 0)),
                  pl.BlockSpec((1, d), lambda i: (0, 0)),
                  pl.BlockSpec(router.shape, lambda i: (0, 0))],
        out_specs=[pl.BlockSpec((rows, d), lambda i: (i, 0)),
                   pl.BlockSpec((rows, d), lambda i: (i, 0)),
                   pl.BlockSpec((rows, LANES), lambda i: (i, 0)),
                   pl.BlockSpec((GATE_ROWS, rows), lambda i: (0, i))],
        out_shape=[jax.ShapeDtypeStruct((n, d), F32), jax.ShapeDtypeStruct((n, d), BF16),
                   jax.ShapeDtypeStruct((n, LANES), F32), jax.ShapeDtypeStruct((GATE_ROWS, n), F32)],
        compiler_params=_cparams("parallel"),
        name="out_router",
    )(x, o, w, g, router)


def _moe_body(x_ref, xn_ref, gate_ref, gf_ref, wg_ref, wu_ref, wd_ref, y_ref, acc_ref):
    e = pl.program_id(1)
    k = pl.program_id(2)

    @pl.when(jnp.logical_and(e == 0, k == 0))
    def _():
        acc_ref[...] = jnp.zeros_like(acc_ref)

    gate = gate_ref[...]
    lane = lax.broadcasted_iota(I32, gate.shape, 1)
    ge = jnp.sum(jnp.where(lane == e, gate, 0.0), axis=-1, keepdims=True)
    xn = xn_ref[...]
    h = jax.nn.silu(_dot(xn, wg_ref[0])) * _dot(xn, wu_ref[0])
    acc_ref[...] += _dot((ge * h).astype(BF16), wd_ref[0])

    @pl.when(jnp.logical_and(e == pl.num_programs(1) - 1, k == pl.num_programs(2) - 1))
    def _():
        y_ref[...] = _rms(x_ref[...] + acc_ref[...], gf_ref[...])


def _moe(x, xn, gate, gf, wg, wu, wd, rows, fb):
    n, d = x.shape
    ne, _, dff = wg.shape
    return pl.pallas_call(
        _moe_body,
        grid=(n // rows, ne, dff // fb),
        in_specs=[pl.BlockSpec((rows, d), lambda i, e, k: (i, 0)),
                  pl.BlockSpec((rows, d), lambda i, e, k: (i, 0)),
                  pl.BlockSpec((rows, LANES), lambda i, e, k: (i, 0)),
                  pl.BlockSpec((1, d), lambda i, e, k: (0, 0)),
                  pl.BlockSpec((1, d, fb), lambda i, e, k: (e, 0, k)),
                  pl.BlockSpec((1, d, fb), lambda i, e, k: (e, 0, k)),
                  pl.BlockSpec((1, fb, d), lambda i, e, k: (e, k, 0))],
        out_specs=pl.BlockSpec((rows, d), lambda i, e, k: (i, 0)),
        out_shape=jax.ShapeDtypeStruct((n, d), F32),
        scratch_shapes=[pltpu.VMEM((rows, d), F32)],
        compiler_params=_cparams("parallel", "arbitrary", "arbitrary"),
        name="moe",
    )(x, xn, gate, gf, wg, wu, wd)


def _t5_bucket(rel):
    half = N_BUCKETS // 2
    max_exact = half // 2
    ret = np.where(rel > 0, half, 0)
    n = np.abs(rel)
    nf = np.maximum(n, 1).astype(np.float32)
    large = max_exact + (np.log(nf / np.float32(max_exact)) / np.float32(math.log(MAX_DIST / max_exact))
                         * np.float32(half - max_exact)).astype(np.int32)
    large = np.minimum(large, half - 1)
    return (ret + np.where(n < max_exact, n, large)).astype(np.int32)


def _bias_tile(rel_bias, q_pos, k_pos, k_real):
    rel = k_pos[None, :] - q_pos[:, None]
    onehot = (jnp.asarray(_t5_bucket(rel).astype(np.int8))[:, :, None]
              == jnp.arange(N_BUCKETS, dtype=jnp.int8)).astype(F32)
    bias = jnp.einsum("qkb,bh->hqk", onehot, rel_bias.astype(F32), precision=lax.Precision.HIGHEST)
    ok = np.logical_and(k_pos[None, :] // CHUNK <= q_pos[:, None] // CHUNK, k_pos[None, :] < k_real)
    return jnp.where(ok[None], bias, NEG_INF)


def _in1_plan():
    scale = DH ** -0.5
    heads64 = lambda o, sc=1.0: [(o, h, h * DH, (h + 1) * DH, sc) for h in range(8)]
    plan = [
        ((0, 512), heads64(0, scale)),
        ((512, 1024), [(1, None, 0, 512, 1.0)] + heads64(2)),
        ((1024, 1536), [(3, None, 0, 512, 1.0)] + heads64(4)),
        ((1536, 2048), heads64(5, scale)),
        ((2048, 2560), [(6, None, 0, 512, 1.0)] + heads64(7)),
        ((2560, 3072), [(8, None, 0, 512, 1.0)] + [(9, h, h * 128, (h + 1) * 128, 1.0) for h in range(H_D)]),
        ((3072, 3584), heads64(10)),
        ((3584, 3712), [(11, None, 0, D_I, 1.0), (12, None, 0, D_I, 1.0), (13, None, D_I, D_I + H_I, 1.0)]),
    ]
    out_defs = [(DH, BF16, 8), (512, F32, None), (DH, BF16, 8), (512, F32, None), (DH, BF16, 8),
                (DH, BF16, 8), (512, F32, None), (DH, BF16, 8), (512, F32, None), (2 * DH, BF16, H_D),
                (D_I, BF16, 8), (D_I, F32, None), (D_I, BF16, None), (H_I, F32, None)]
    return plan, out_defs


def _layer0(x, hist, rb, emit_v, p):
    n = x.shape[0]
    rows = _largest_divisor(n, 512, 16)
    plan = [((0, p["w_in0"].shape[1]), [(0, None, 0, p["w_in0"].shape[1], 1.0)])]
    (z,) = _rms_proj(x, p["ln_mix0"], p["w_in0"], plan, [(p["w_in0"].shape[1], F32, None)], rows)
    ws = p["ws_prompt"] if hist is None else p["ws_sample"]
    bs = p["bs_prompt"] if hist is None else p["bs_sample"]
    outs = _mixer_ab(x, z, hist, p["gmlp_ln_g"], p["gmlp_ln_b"], ws, bs, p["conv_k"], p["w_out0"], rb, emit_v)
    x1 = outs[0]
    x2 = _ffn(x1, p["ln_ffn0"], p["ffn_wg"], p["ffn_wu"], p["ffn_wd"], rows, p["ffn_fb"])
    return (x2,) + tuple(outs[1:])


GATE_ROWS = 16
MOE_ROWS = 128


def _moe_routed_body(x_ref, xn_ref, gate_ref, gatet_ref, gf_ref, wg_ref, wu_ref, wd_ref, y_ref,
                     triu_ref, tril_ref, crow_ref, ccol_ref, xs_ref, ge_ref, acc_ref, yblk_ref, nsub_ref,
                     *, tb, sub):
    e = pl.program_id(1)
    k = pl.program_id(2)
    first_k = k == 0
    last_k = k == pl.num_programs(2) - 1

    @pl.when(jnp.logical_and(e == 0, first_k))
    def _():
        r = lax.broadcasted_iota(I32, (tb, tb), 0)
        c = lax.broadcasted_iota(I32, (tb, tb), 1)
        triu_ref[...] = jnp.where(r < c, 1.0, 0.0).astype(BF16)
        tril_ref[...] = jnp.where(c < r, 1.0, 0.0).astype(BF16)
        crow_ref[...] = _dot(jnp.where(gatet_ref[...] > 0.0, 1.0, 0.0).astype(BF16), triu_ref[...])
        ccol_ref[...] = _dot(tril_ref[...], jnp.where(gate_ref[...] > 0.0, 1.0, 0.0).astype(BF16))
        yblk_ref[...] = jnp.zeros_like(yblk_ref)
        acc_ref[...] = jnp.zeros_like(acc_ref)

    @pl.when(first_k)
    def _():
        g_e = gatet_ref[pl.ds(e, 1), :]
        m_e = g_e > 0.0
        c_e = crow_ref[pl.ds(e, 1), :]
        nsub = (jnp.sum(jnp.where(m_e, 1.0, 0.0)).astype(I32) + sub - 1) // sub
        nsub_ref[0] = nsub
        xn = xn_ref[...]
        slot = lax.broadcasted_iota(I32, (sub, tb), 0).astype(F32)

        def pack(s, carry):
            r0 = pl.multiple_of(s * sub, sub)
            hit = jnp.logical_and(m_e, c_e == slot + (s * sub).astype(F32))
            onehot = jnp.where(hit, 1.0, 0.0)
            xs_ref[pl.ds(r0, sub), :] = _dot(onehot.astype(BF16), xn).astype(BF16)
            ge_ref[pl.ds(r0, sub), :] = jnp.sum(onehot * g_e, axis=-1, keepdims=True)
            acc_ref[pl.ds(r0, sub), :] = jnp.zeros((sub, acc_ref.shape[1]), F32)
            return carry

        lax.fori_loop(0, nsub, pack, 0)

    nsub = nsub_ref[0]

    def experts(s, carry):
        rows = pl.ds(pl.multiple_of(s * sub, sub), sub)
        xs = xs_ref[rows, :]
        h = jax.nn.silu(_dot(xs, wg_ref[0])) * _dot(xs, wu_ref[0])
        acc_ref[rows, :] += _dot((ge_ref[rows, :] * h).astype(BF16), wd_ref[0])
        return carry

    lax.fori_loop(0, nsub, experts, 0)

    @pl.when(last_k)
    def _():
        lane = lax.broadcasted_iota(I32, (tb, LANES), 1)
        pick = lambda a: jnp.sum(jnp.where(lane == e, a, 0.0), axis=-1, keepdims=True)
        m_e = pick(gate_ref[...]) > 0.0
        c_e = pick(ccol_ref[...])
        slot = lax.broadcasted_iota(I32, (tb, sub), 1).astype(F32)

        def unpack(s, carry):
            rows = pl.ds(pl.multiple_of(s * sub, sub), sub)
            hit = jnp.logical_and(m_e, c_e == slot + (s * sub).astype(F32))
            onehot = jnp.where(hit, 1.0, 0.0).astype(BF16)
            y = acc_ref[rows, :]
            hi = y.astype(BF16)
            lo = (y - hi.astype(F32)).astype(BF16)
            yblk_ref[...] += _dot(jnp.concatenate([onehot, onehot], axis=1), jnp.concatenate([hi, lo], axis=0))
            return carry

        lax.fori_loop(0, nsub, unpack, 0)

    @pl.when(jnp.logical_and(e == pl.num_programs(1) - 1, last_k))
    def _():
        y_ref[...] = _rms(x_ref[...] + yblk_ref[...], gf_ref[...])


def _moe_routed(x, xn, gate, gatet, gf, wg, wu, wd, tb, fb):
    n, d = x.shape
    ne, _, dff = wg.shape
    return pl.pallas_call(
        functools.partial(_moe_routed_body, tb=tb, sub=min(MOE_ROWS, tb)),
        grid=(n // tb, ne, dff // fb),
        in_specs=[pl.BlockSpec((tb, d), lambda i, e, k: (i, 0)),
                  pl.BlockSpec((tb, d), lambda i, e, k: (i, 0)),
                  pl.BlockSpec((tb, LANES), lambda i, e, k: (i, 0)),
                  pl.BlockSpec((GATE_ROWS, tb), lambda i, e, k: (0, i)),
                  pl.BlockSpec((1, d), lambda i, e, k: (0, 0)),
                  pl.BlockSpec((1, d, fb), lambda i, e, k: (e, 0, k)),
                  pl.BlockSpec((1, d, fb), lambda i, e, k: (e, 0, k)),
                  pl.BlockSpec((1, fb, d), lambda i, e, k: (e, k, 0))],
        out_specs=pl.BlockSpec((tb, d), lambda i, e, k: (i, 0)),
        out_shape=jax.ShapeDtypeStruct((n, d), F32),
        scratch_shapes=[pltpu.VMEM((tb, tb), BF16), pltpu.VMEM((tb, tb), BF16),
                        pltpu.VMEM((GATE_ROWS, tb), F32), pltpu.VMEM((tb, LANES), F32),
                        pltpu.VMEM((tb, d), BF16), pltpu.VMEM((tb, 1), F32),
                        pltpu.VMEM((tb, d), F32), pltpu.VMEM((tb, d), F32),
                        pltpu.SMEM((1,), I32)],
        compiler_params=_cparams("parallel", "arbitrary", "arbitrary"),
        name="moe_routed",
    )(x, xn, gate, gatet, gf, wg, wu, wd)


def _layer1_tail(x, o, p):
    n = x.shape[0]
    rows = _largest_divisor(n, 512, 16)
    x3, xn, gate, gatet = _out_router(x, o, p["w_out1"], p["ln_ffn1"], p["router"], rows)
    return _moe_routed(x3, xn, gate, gatet, p["ln_final"], p["exp_wg"], p["exp_wu"], p["exp_wd"],
                       _largest_divisor(n, 1024, 128), p["exp_fb"])


def kernel(x_prompt, x_sample, state_b_conv, cache_c_k, cache_c_v, cache_idx_k, cache_d_k, cache_d_v, rel_bias, ln_mix, ln_ffn, ln_final, w_in0, gmlp_ln_g, gmlp_ln_b, gmlp_ws, gmlp_bs, conv_k, w_out0, ffn_wg, ffn_wu, ffn_wd, w_in1, lam_qk, subln_g, w_out1, router, exp_wg, exp_wu, exp_wd):
    bp, seq, d = x_prompt.shape
    bs_, ts, _ = x_sample.shape
    past = cache_c_k.shape[2]
    assert bp == 1 and ln_mix.shape[0] == 2 and seq % PROMPT_LB == 0 and ts % SUBLANES == 0 and ts <= CHUNK
    assert past % KEY_BLOCK == 0
    gmlp_chunk = gmlp_ws.shape[-1]
    lam_init = 0.8 - 0.6 * math.exp(-0.3 * 1)

    def ws_masked(rows):
        r = jnp.arange(rows)
        ok = (r[None, :] // CHUNK) <= (r[:, None] // CHUNK)
        return jnp.where(ok[None], gmlp_ws[0][:, :rows, :rows], 0.0).astype(BF16)

    in1_pad = (-w_in1.shape[2]) % LANES
    lf = lam_qk[0].astype(F32)
    lam = (jnp.exp(jnp.sum(lf[0] * lf[1])) - jnp.exp(jnp.sum(lf[2] * lf[3])) + lam_init).reshape(1, 1)
    p = {
        "ln_mix0": ln_mix[0:1], "ln_ffn0": ln_ffn[0:1], "ln_mix1": ln_mix[1:2], "ln_ffn1": ln_ffn[1:2],
        "ln_final": ln_final.reshape(1, d),
        "w_in0": w_in0[0].astype(BF16),
        "gmlp_ln_g": gmlp_ln_g[0].reshape(1, W_A), "gmlp_ln_b": gmlp_ln_b[0].reshape(1, W_A),
        "ws_prompt": ws_masked(gmlp_chunk), "ws_sample": ws_masked(ts),
        "bs_prompt": gmlp_bs[0][:, :gmlp_chunk].T, "bs_sample": gmlp_bs[0][:, :ts].T,
        "conv_k": conv_k[0], "w_out0": w_out0[0].astype(BF16),
        "ffn_wg": ffn_wg[0].astype(BF16), "ffn_wu": ffn_wu[0].astype(BF16), "ffn_wd": ffn_wd[0].astype(BF16),
        "ffn_fb": _largest_divisor(ffn_wg.shape[2], 1408, LANES),
        "w_in1": jnp.pad(w_in1[0], ((0, 0), (0, in1_pad))).astype(BF16),
        "w_out1": w_out1[0].astype(BF16),
        "router": jnp.pad(router[0], ((0, 0), (0, LANES - N_EXP))).astype(BF16),
        "exp_wg": exp_wg[0].astype(BF16), "exp_wu": exp_wu[0].astype(BF16), "exp_wd": exp_wd[0].astype(BF16),
        "exp_fb": _largest_divisor(exp_wg.shape[3], 896, LANES),
    }
    g_sub = subln_g[0].reshape(1, 2 * DH)
    plan1, out_defs1 = _in1_plan()

    xp = x_prompt.reshape(seq, d)
    xs = x_sample.reshape(bs_ * ts, d)
    xp, p_tail = _layer0(xp, None, gmlp_chunk, False, p)
    hist = jnp.pad(state_b_conv[0], ((0, 0), (SUBLANES - 2, 0), (0, 0)))
    xs, s_tail, s_av = _layer0(xs, hist, ts, True, p)
    p_b_conv = p_tail[-1, SUBLANES - 2:, :].reshape(1, 1, 2, W_B)
    s_b_conv = s_tail[:, SUBLANES - 2:, :].reshape(1, bs_, 2, W_B)
    s_a_v = s_av.reshape(1, bs_, ts, W_A)

    lb = KEY_BLOCK
    w1 = w_in1[0]
    off = np.concatenate([[0], np.cumsum(IN1_SIZES)])
    field = lambda k: w1[:, off[k]:off[k + 1]]
    padc = lambda a: jnp.pad(a, ((0, 0), (0, LANES - a.shape[1])))
    log2e = math.log2(math.e)
    qscale = DH ** -0.5 * log2e
    w_norm = jnp.concatenate([field(1), field(4), field(5), padc(field(7))], axis=1).astype(BF16)
    w_tran = jnp.concatenate([field(0) * qscale, field(3) * qscale, field(6), field(2), field(5),
                              field(1), field(4), padc(field(8))], axis=1).T.astype(BF16)
    (vd32, ki32, kc, kd, ki, qct, qdt, qit, vct, vdt, vc32t, kc32t, kd32t, wit) = _proj1_t(
        xp, p["ln_mix1"], w_norm, w_tran, PROMPT_TQ)
    token_major = lambda a, *dims: jnp.moveaxis(a.reshape(dims + (seq,)), -1, 0).reshape((1, 1, seq) + dims)
    p_c_k, p_c_v = token_major(kc32t, H_C, DH), token_major(vc32t, H_C, DH)
    p_d_k = token_major(kd32t, H_D, 2, DH)
    pos = jnp.arange(seq, dtype=I32)
    ve = ((pos // CHUNK + 1) * CHUNK).reshape(1, seq)
    mask = _select_t(qit, wit, ve, ki, min(TOPK_MAX, seq // 4))
    e_far = -(-(PROMPT_LB - 1 + MAX_DIST) // PROMPT_TQ)
    kpos = np.arange(PROMPT_LB)
    qpos = np.arange(PROMPT_TQ)
    base = e_far * PROMPT_TQ
    rel_near = (rel_bias - rel_bias[N_BUCKETS // 2 - 1:N_BUCKETS // 2]) * log2e
    btiles = jnp.stack([jnp.transpose(_bias_tile(rel_near, base + e * PROMPT_TQ + qpos, base + kpos, base + PROMPT_LB),
                                      (0, 2, 1)) for e in range(e_far)])
    op = _attend_t(qct, qdt, kc, kd, vct, vdt, mask, btiles, lam, subln_g[0].reshape(2 * DH, 1), lam_init)
    y_prompt = _layer1_tail(xp, op, p).reshape(1, seq, d)

    sr = _rms_proj(xs, p["ln_mix1"], p["w_in1"], plan1, out_defs1, _largest_divisor(bs_ * ts, 256, 16))
    sqc, skc32, skc, svc32, svc, sqd, skd32, skd, svd32, svd, sqi, ski32, ski, swi = sr
    nk = past + ts
    lps = -(-nk // lb) * lb

    kis = jnp.pad(jnp.concatenate([cache_idx_k[0].astype(BF16), ski.reshape(bs_, ts, D_I)], axis=1),
                  ((0, 0), (0, lps - nk), (0, 0)))
    ves = jnp.full((bs_ * ts, 1), nk, I32)
    smask = _select(sqi, swi, ves, kis, bs_, ts, ts, min(TOPK_MAX, nk // 4), lambda i: lps // lb)
    sq_pos = past + np.arange(ts)
    sbt = jnp.stack([_bias_tile(rel_bias, sq_pos, j * lb + np.arange(lb), nk) for j in range(lps // lb)])
    sbias = jnp.concatenate([sbt[:, :H_C], jnp.repeat(sbt[:, H_C:], 2, axis=1)], axis=1).reshape(lps // lb, N_MAPS * ts, lb)

    def block_diag(q):
        qb = jnp.transpose(q.reshape(H_C, bs_, ts, DH), (1, 0, 2, 3))
        eye = jnp.eye(H_C, dtype=q.dtype)
        return (qb[:, :, :, None, :] * eye[None, :, None, :, None]).reshape(bs_, H_C * ts, W_C)

    def cache_rows(c, feature_major):
        c = c.reshape(bs_, past, W_C)
        return jnp.transpose(c, (0, 2, 1)) if feature_major else c

    def new_rows(a, feature_major):
        a = jnp.pad(a.reshape(bs_, ts, W_C), ((0, 0), (0, lb - ts), (0, 0)))
        return jnp.transpose(a, (0, 2, 1)) if feature_major else a
    os_ = _attend_s(block_diag(sqc), block_diag(sqd),
                    [cache_rows(c[0], fm) for c, fm in ((cache_c_k, True), (cache_c_v, True), (cache_d_k, True), (cache_d_v, False))],
                    [new_rows(a, fm) for a, fm in ((skc32, True), (svc32, True), (skd32, True), (svd32, False))],
                    smask, sbias, lam, g_sub, lam_init)
    y_sample = _layer1_tail(xs, os_, p).reshape(bs_, ts, d)

    r5 = lambda a, n, t, *tail: a.reshape((1, n, t) + tail)
    return (y_prompt, y_sample, p_b_conv,
            p_c_k, p_c_v, r5(ki32, 1, seq, D_I), p_d_k, r5(vd32, 1, seq, H_D, 2 * DH),
            s_a_v, s_b_conv,
            r5(skc32, bs_, ts, H_C, DH), r5(svc32, bs_, ts, H_C, DH), r5(ski32, bs_, ts, D_I),
            r5(skd32, bs_, ts, H_D, 2, DH), r5(svd32, bs_, ts, H_D, 2 * DH))
```

```python
import functools
import math

import jax
import jax.numpy as jnp
import numpy as np
from jax import lax
from jax.experimental import pallas as pl
from jax.experimental.pallas import tpu as pltpu

F32 = jnp.float32
BF16 = jnp.bfloat16
I32 = jnp.int32

CHUNK = 64
EPS = 1e-6
NEG_INF = -1e30
H_A = 4
C_A = 128
W_A = H_A * C_A
W_B = 512
H_C = 8
DH = 64
W_C = H_C * DH
H_I = 8
D_I = 64
TOPK_MAX = 256
H_D = 4
W_D = H_D * 2 * DH
N_BUCKETS = 32
MAX_DIST = 128
N_EXP = 8
IN1_SIZES = (W_C, W_C, W_C, W_D, W_D, W_D, H_I * D_I, D_I, H_I)
N_HEADS_BIAS = H_C + H_D
N_MAPS = H_C + 2 * H_D

LANES = 128
SUBLANES = 8
VMEM_LIMIT = 56 * 1024 * 1024

INT_MIN = -(2 ** 31)
INT_MAX = 2 ** 31 - 1
ALL_TIES = 2 ** 30

KEY_BLOCK = 512


def _cparams(*sem):
    return pltpu.CompilerParams(dimension_semantics=sem, vmem_limit_bytes=VMEM_LIMIT)


def _largest_divisor(n, target, mult):
    if n <= target:
        return n
    d = (target // mult) * mult
    while d >= mult:
        if n % d == 0:
            return d
        d -= mult
    raise ValueError(f"no block of multiple {mult} divides {n}")


def _rms(x, g):
    return x * lax.rsqrt(jnp.mean(x * x, axis=-1, keepdims=True) + EPS) * g


def _dot(a, b):
    return jnp.dot(a, b, preferred_element_type=F32)


def _dot_nt(a, b):
    return lax.dot_general(a, b, (((1,), (1,)), ((), ())), preferred_element_type=F32)


def _rms_proj_body(x_ref, g_ref, w_ref, *out_refs, plan):
    xn = _rms(x_ref[...], g_ref[...]).astype(BF16)
    for (c0, c1), writes in plan:
        z = _dot(xn, w_ref[:, c0:c1])
        for o_idx, head, z0, z1, scale in writes:
            val = z[:, z0:z1]
            if scale != 1.0:
                val = val * scale
            ref = out_refs[o_idx]
            if head is None:
                ref[...] = val.astype(ref.dtype)
            else:
                ref[head] = val.astype(ref.dtype)


def _rms_proj(x, g, w, plan, out_defs, rows):
    n, d = x.shape
    grid = (n // rows,)
    out_shape, out_specs = [], []
    for width, dtype, heads in out_defs:
        if heads is None:
            out_shape.append(jax.ShapeDtypeStruct((n, width), dtype))
            out_specs.append(pl.BlockSpec((rows, width), lambda i: (i, 0)))
        else:
            out_shape.append(jax.ShapeDtypeStruct((heads, n, width), dtype))
            out_specs.append(pl.BlockSpec((heads, rows, width), lambda i: (0, i, 0)))
    return pl.pallas_call(
        functools.partial(_rms_proj_body, plan=plan),
        grid=grid,
        in_specs=[pl.BlockSpec((rows, d), lambda i: (i, 0)),
                  pl.BlockSpec((1, d), lambda i: (0, 0)),
                  pl.BlockSpec(w.shape, lambda i: (0, 0))],
        out_specs=out_specs,
        out_shape=out_shape,
        compiler_params=_cparams("parallel"),
        name="rms_proj",
    )(x, g, w)


def _mixer_ab_body(*refs, rb, from_prev, emit_v):
    it = iter(refs)
    x_ref, u_ref, v_ref, gb_ref, gc_ref, xin_ref = (next(it) for _ in range(6))
    if from_prev:
        gcp_ref, xinp_ref = next(it), next(it)
    else:
        hist_ref = next(it)
    lng_ref, lnb_ref, ws_ref, bs_ref, ck_ref, wout_ref = (next(it) for _ in range(6))
    x1_ref, tail_ref = next(it), next(it)
    vout_ref = next(it) if emit_v else None
    wext_ref = next(it)

    w = gc_ref[...] * xin_ref[...]
    if from_prev:
        hist = jnp.where(pl.program_id(0) > 0, gcp_ref[...] * xinp_ref[...], 0.0)
    else:
        hist = hist_ref[0]
    wext_ref[0:SUBLANES, :] = hist
    wext_ref[SUBLANES:, :] = w
    ck = ck_ref[...]
    conv = (ck[0:1] * wext_ref[SUBLANES - 2:SUBLANES - 2 + rb, :]
            + ck[1:2] * wext_ref[SUBLANES - 1:SUBLANES - 1 + rb, :]
            + ck[2:3] * w)
    y_b = gb_ref[...] * conv
    tail_ref[0] = w[rb - SUBLANES:, :]

    u = jax.nn.gelu(u_ref[...])
    v = jax.nn.gelu(v_ref[...])
    lng = lng_ref[...]
    lnb = lnb_ref[...]
    bs = bs_ref[...]
    acc = _dot(y_b.astype(BF16), wout_ref[W_A:, :])
    for h in range(H_A):
        sl = slice(h * C_A, (h + 1) * C_A)
        vh = v[:, sl]
        mu = jnp.mean(vh, axis=-1, keepdims=True)
        xc = vh - mu
        var = jnp.mean(xc * xc, axis=-1, keepdims=True)
        vln = xc * lax.rsqrt(var + EPS) * lng[:, sl] + lnb[:, sl]
        if emit_v:
            vout_ref[:, sl] = vln
        s = _dot(ws_ref[h], vln.astype(BF16)) + bs[:, h:h + 1]
        y_a = u[:, sl] * s
        acc = acc + _dot(y_a.astype(BF16), wout_ref[sl, :])
    x1_ref[...] = x_ref[...] + acc


def _mixer_ab(x, z, hist, lng, lnb, ws, bs, ck, wout, rb, emit_v):
    n, d = x.shape
    nb = n // rb
    from_prev = hist is None
    col = lambda c: pl.BlockSpec((rb, 512), lambda i, c=c: (i, c))
    in_specs = [pl.BlockSpec((rb, d), lambda i: (i, 0)), col(0), col(1), col(2), col(3), col(4)]
    args = [x, z, z, z, z, z]
    if from_prev:
        per = rb // SUBLANES
        prev = lambda c: pl.BlockSpec((SUBLANES, 512), lambda i, c=c: (jnp.maximum(i * per - 1, 0), c))
        in_specs += [prev(3), prev(4)]
        args += [z, z]
    else:
        in_specs += [pl.BlockSpec((1, SUBLANES, 512), lambda i: (i, 0, 0))]
        args += [hist]
    const = lambda a: pl.BlockSpec(a.shape, lambda i, nd=a.ndim: (0,) * nd)
    for a in (lng, lnb, ws, bs, ck, wout):
        in_specs.append(const(a))
        args.append(a)
    out_shape = [jax.ShapeDtypeStruct((n, d), F32), jax.ShapeDtypeStruct((nb, SUBLANES, 512), F32)]
    out_specs = [pl.BlockSpec((rb, d), lambda i: (i, 0)), pl.BlockSpec((1, SUBLANES, 512), lambda i: (i, 0, 0))]
    if emit_v:
        out_shape.append(jax.ShapeDtypeStruct((n, W_A), F32))
        out_specs.append(pl.BlockSpec((rb, W_A), lambda i: (i, 0)))
    return pl.pallas_call(
        functools.partial(_mixer_ab_body, rb=rb, from_prev=from_prev, emit_v=emit_v),
        grid=(nb,),
        in_specs=in_specs,
        out_specs=out_specs,
        out_shape=out_shape,
        scratch_shapes=[pltpu.VMEM((rb + SUBLANES, 512), F32)],
        compiler_params=_cparams("arbitrary"),
        name="mixer_ab",
    )(*args)


def _ffn_body(x_ref, g_ref, wg_ref, wu_ref, wd_ref, o_ref, xn_ref, acc_ref):
    k = pl.program_id(1)

    @pl.when(k == 0)
    def _():
        xn_ref[...] = _rms(x_ref[...], g_ref[...]).astype(BF16)
        acc_ref[...] = jnp.zeros_like(acc_ref)

    xn = xn_ref[...]
    h = jax.nn.silu(_dot(xn, wg_ref[...])) * _dot(xn, wu_ref[...])
    acc_ref[...] += _dot(h.astype(BF16), wd_ref[...])

    @pl.when(k == pl.num_programs(1) - 1)
    def _():
        o_ref[...] = x_ref[...] + acc_ref[...]


def _ffn(x, g, wg, wu, wd, rows, fb):
    n, d = x.shape
    dff = wg.shape[1]
    return pl.pallas_call(
        _ffn_body,
        grid=(n // rows, dff // fb),
        in_specs=[pl.BlockSpec((rows, d), lambda i, k: (i, 0)),
                  pl.BlockSpec((1, d), lambda i, k: (0, 0)),
                  pl.BlockSpec((d, fb), lambda i, k: (0, k)),
                  pl.BlockSpec((d, fb), lambda i, k: (0, k)),
                  pl.BlockSpec((fb, d), lambda i, k: (k, 0))],
        out_specs=pl.BlockSpec((rows, d), lambda i, k: (i, 0)),
        out_shape=jax.ShapeDtypeStruct((n, d), F32),
        scratch_shapes=[pltpu.VMEM((rows, d), BF16), pltpu.VMEM((rows, d), F32)],
        compiler_params=_cparams("parallel", "arbitrary"),
        name="ffn",
    )(x, g, wg, wu, wd)


def _select_body(qi_ref, wi_ref, ve_ref, ki_ref, out_ref, keys_ref, *, tq, lb, nkb, topk, nvalid_fn):
    nv = nvalid_fn(pl.program_id(1))
    wi = wi_ref[...]
    ve = ve_ref[...]
    qs = [qi_ref[h] for h in range(H_I)]
    wcols = [wi[:, h:h + 1] for h in range(H_I)]
    lane = lax.broadcasted_iota(I32, (tq, lb), 1)

    def score_block(b, carry):
        kb = ki_ref[0, pl.ds(pl.multiple_of(b * lb, lb), lb), :]
        sc = jnp.zeros((tq, lb), F32)
        for h in range(H_I):
            sc = sc + wcols[h] * jnp.maximum(_dot_nt(qs[h], kb), 0.0)
        bits = lax.bitcast_convert_type(sc, I32)
        key = bits ^ ((bits >> 31) & INT_MAX)
        key = jnp.where(sc == 0.0, 0, key)
        key = jnp.where(lane + b * lb < ve, key, INT_MIN)
        keys_ref[b] = key
        return carry

    lax.fori_loop(0, nv, score_block, 0)

    def count_ge(mid):
        midb = jnp.broadcast_to(mid, (tq, LANES))

        def body(b, acc):
            for c in range(lb // LANES):
                k = keys_ref[b, :, c * LANES:(c + 1) * LANES]
                acc = acc + jnp.where(k >= midb, 1, 0)
            return acc

        acc = lax.fori_loop(0, nv, body, jnp.zeros((tq, LANES), I32))
        return jnp.sum(acc.astype(F32), axis=-1, keepdims=True).astype(I32)

    def bisect(_, st):
        lo, hi, clo, chi = st
        mid = (lo >> 1) + (hi >> 1) + (lo & hi & 1)
        cnt = count_ge(mid)
        active = mid != lo
        up = jnp.logical_and(active, cnt >= topk)
        dn = jnp.logical_and(active, cnt < topk)
        return (jnp.where(up, mid, lo), jnp.where(dn, mid, hi),
                jnp.where(up, cnt, clo), jnp.where(dn, cnt, chi))

    full = lambda v: jnp.full((tq, 1), v, I32)
    lo, hi, clo, chi = lax.fori_loop(0, 32, bisect, (full(INT_MIN), full(INT_MAX), ve, full(0)))

    need = jnp.where(clo > topk, topk - chi, ALL_TIES)
    need = jnp.where(lo == INT_MIN, 0, need).astype(F32)
    lob = jnp.broadcast_to(lo, (tq, LANES))
    needb = jnp.broadcast_to(need, (tq, LANES))
    r = lax.broadcasted_iota(I32, (LANES, LANES), 0)
    c = lax.broadcasted_iota(I32, (LANES, LANES), 1)
    tri = jnp.where(r <= c, 1.0, 0.0).astype(BF16)

    def mask_block(b, seen):
        for cc in range(lb // LANES):
            k = keys_ref[b, :, cc * LANES:(cc + 1) * LANES]
            eq = jnp.where(k == lob, 1.0, 0.0)
            cum = _dot(eq.astype(BF16), tri)
            rank = seen + cum - eq
            take = jnp.where(rank < needb, eq, 0.0)
            sel = jnp.where(k > lob, 1.0, take)
            out_ref[0, b, :, cc * LANES:(cc + 1) * LANES] = jnp.where(sel > 0.5, 0.0, NEG_INF).astype(out_ref.dtype)
            seen = seen + cum[:, LANES - 1:LANES]
        return seen

    lax.fori_loop(0, nv, mask_block, jnp.zeros((tq, 1), F32))

    def fill_block(b, carry):
        out_ref[0, b] = jnp.full((tq, lb), NEG_INF, out_ref.dtype)
        return carry

    lax.fori_loop(nv, nkb, fill_block, 0)


def _select(qi, wi, ve, ki, nbatch, t, tq, topk, nvalid_fn):
    lp = ki.shape[1]
    lb = KEY_BLOCK
    nkb = lp // lb
    nq = t // tq
    row = lambda b, i: b * nq + i
    return pl.pallas_call(
        functools.partial(_select_body, tq=tq, lb=lb, nkb=nkb, topk=topk, nvalid_fn=nvalid_fn),
        grid=(nbatch, nq),
        in_specs=[pl.BlockSpec((H_I, tq, D_I), lambda b, i: (0, row(b, i), 0)),
                  pl.BlockSpec((tq, H_I), lambda b, i: (row(b, i), 0)),
                  pl.BlockSpec((tq, 1), lambda b, i: (row(b, i), 0)),
                  pl.BlockSpec((1, lp, D_I), lambda b, i: (b, 0, 0))],
        out_specs=pl.BlockSpec((1, nkb, tq, lb), lambda b, i: (b, 0, i, 0)),
        out_shape=jax.ShapeDtypeStruct((nbatch, nkb, t, lb), BF16),
        scratch_shapes=[pltpu.VMEM((nkb, tq, lb), I32)],
        compiler_params=_cparams("parallel", "arbitrary"),
        name="index_select",
    )(qi, wi, ve, ki)


def _attn_body(qc_ref, qd_ref, kc_ref, vc_ref, kd_ref, vd_ref, mask_ref, bt_ref, lam_ref, g_ref,
               o_ref, m_ref, l_ref, accc_ref, accd_ref, *, tq, nvalid_fn, lam_init):
    i = pl.program_id(1)
    j = pl.program_id(2)

    @pl.when(j == 0)
    def _():
        m_ref[...] = jnp.full(m_ref.shape, NEG_INF, F32)
        l_ref[...] = jnp.zeros_like(l_ref)
        accc_ref[...] = jnp.zeros_like(accc_ref)
        accd_ref[...] = jnp.zeros_like(accd_ref)

    @pl.when(j < nvalid_fn(i))
    def _():
        sel = mask_ref[0, 0].astype(F32)
        for mp in range(N_MAPS):
            sparse = mp < H_C
            if sparse:
                s = _dot_nt(qc_ref[mp], kc_ref[0, mp]) + bt_ref[0, mp] + sel
                v = vc_ref[0, mp]
                acc_ref, a_idx, dv = accc_ref, mp, DH
            else:
                dm = mp - H_C
                s = _dot_nt(qd_ref[dm], kd_ref[0, dm]) + bt_ref[0, H_C + dm // 2]
                v = vd_ref[0, dm // 2]
                acc_ref, a_idx, dv = accd_ref, dm, 2 * DH
            m_prev = m_ref[mp]
            m_new = jnp.maximum(m_prev, jnp.max(s, axis=-1, keepdims=True))
            alpha = jnp.exp(m_prev - m_new)
            p = jnp.exp(s - m_new[:, 0:1])
            l_ref[mp] = alpha * l_ref[mp] + jnp.sum(p, axis=-1, keepdims=True)
            m_ref[mp] = m_new
            acc_ref[a_idx] = alpha[:, 0:dv] * acc_ref[a_idx] + _dot(p.astype(BF16), v)

    @pl.when(j == pl.num_programs(2) - 1)
    def _():
        for h in range(H_C):
            o_ref[:, h * DH:(h + 1) * DH] = (accc_ref[h] / l_ref[h][:, 0:DH]).astype(o_ref.dtype)
        lam = lam_ref[...]
        g = g_ref[...]
        for h in range(H_D):
            a0 = accd_ref[2 * h] / l_ref[H_C + 2 * h]
            a1 = accd_ref[2 * h + 1] / l_ref[H_C + 2 * h + 1]
            od = _rms(a0 - lam * a1, g) * (1.0 - lam_init)
            o_ref[:, W_C + h * 2 * DH:W_C + (h + 1) * 2 * DH] = od.astype(o_ref.dtype)


def _attend(qc, qd, kc, vc, kd, vd, mask, btiles, lam, g, nbatch, t, tq, nvalid_fn, tile_fn, lam_init):
    lp = kc.shape[2]
    lb = KEY_BLOCK
    nkb = lp // lb
    nq = t // tq
    row = lambda b, i: b * nq + i
    kblk = lambda i, j: jnp.minimum(j, nvalid_fn(i) - 1)
    qspec = pl.BlockSpec((H_C, tq, DH), lambda b, i, j: (0, row(b, i), 0))
    kspec = lambda nh, w: pl.BlockSpec((1, nh, lb, w), lambda b, i, j: (b, 0, kblk(i, j), 0))
    return pl.pallas_call(
        functools.partial(_attn_body, tq=tq, nvalid_fn=nvalid_fn, lam_init=lam_init),
        grid=(nbatch, nq, nkb),
        in_specs=[qspec, qspec, kspec(H_C, DH), kspec(H_C, DH), kspec(2 * H_D, DH), kspec(H_D, 2 * DH),
                  pl.BlockSpec((1, 1, tq, lb), lambda b, i, j: (b, kblk(i, j), i, 0)),
                  pl.BlockSpec((1, N_HEADS_BIAS, tq, lb), lambda b, i, j: (tile_fn(i, kblk(i, j)), 0, 0, 0)),
                  pl.BlockSpec((1, 1), lambda b, i, j: (0, 0)),
                  pl.BlockSpec((1, 2 * DH), lambda b, i, j: (0, 0))],
        out_specs=pl.BlockSpec((tq, W_C + W_D), lambda b, i, j: (row(b, i), 0)),
        out_shape=jax.ShapeDtypeStruct((nbatch * t, W_C + W_D), BF16),
        scratch_shapes=[pltpu.VMEM((N_MAPS, tq, LANES), F32), pltpu.VMEM((N_MAPS, tq, LANES), F32),
                        pltpu.VMEM((H_C, tq, DH), F32), pltpu.VMEM((2 * H_D, tq, 2 * DH), F32)],
        compiler_params=_cparams("parallel", "parallel", "arbitrary"),
        name="attend",
    )(qc, qd, kc, vc, kd, vd, mask, btiles, lam, g)


PROMPT_TQ = 256
PROMPT_LB = 512
SEL_ROWS = 256
TIE_ROWS = 128
MAPS_PER_DOT = 4
PV_ROWS = 256
ONES_ROWS = 16


def _proj1_t_body(x_ref, g_ref, w_ref, wt_ref, vd32_ref, ki32_ref, kc_ref, kd_ref, ki_ref,
                  qct_ref, qdt_ref, qit_ref, vct_ref, vdt_ref, vc32t_ref, kc32t_ref, kd32t_ref, wit_ref):
    xn32 = _rms(x_ref[...], g_ref[...])
    xn = xn32.astype(BF16)
    xnt = xn32.T.astype(BF16)
    kc_ref[...] = _dot(xn, w_ref[:, 0:512]).astype(BF16)
    kd_ref[...] = _dot(xn, w_ref[:, 512:1024]).astype(BF16)
    vd32_ref[...] = _dot(xn, w_ref[:, 1024:1536])
    z = _dot(xn, w_ref[:, 1536:1536 + LANES])
    ki32_ref[...] = z[:, 0:D_I]
    ki_ref[...] = z[:, 0:D_I].astype(BF16)
    for c, (head_ref, full_ref) in enumerate(((qct_ref, None), (qdt_ref, None), (qit_ref, None), (vct_ref, vc32t_ref),
                                              (vdt_ref, None), (None, kc32t_ref), (None, kd32t_ref))):
        zt = _dot(wt_ref[c * 512:(c + 1) * 512, :], xnt)
        if full_ref is not None:
            full_ref[...] = zt
        if head_ref is not None:
            nh = head_ref.shape[0]
            w = 512 // nh
            for h in range(nh):
                head_ref[h, 0:w, :] = zt[h * w:(h + 1) * w, :].astype(BF16)
                if head_ref.shape[1] > w:
                    pad = lax.broadcasted_iota(I32, (head_ref.shape[1] - w, zt.shape[1]), 0)
                    head_ref[h, w:, :] = jnp.where(pad == 0, 1.0, 0.0).astype(BF16)
    zt = _dot(wt_ref[3584:3584 + LANES, :], xnt)
    wit_ref[...] = zt[0:H_I, :]


def _proj1_t(x, g, w, wt, rows):
    n, d = x.shape
    full = lambda width: (jax.ShapeDtypeStruct((n, width), F32), pl.BlockSpec((rows, width), lambda i: (i, 0)))
    packed = lambda width: (jax.ShapeDtypeStruct((n, width), BF16), pl.BlockSpec((rows, width), lambda i: (i, 0)))
    heads_t = lambda nh, width: (jax.ShapeDtypeStruct((nh, width, n), BF16),
                                 pl.BlockSpec((nh, width, rows), lambda i: (0, 0, i)))
    full_t = lambda: (jax.ShapeDtypeStruct((512, n), F32), pl.BlockSpec((512, rows), lambda i: (0, i)))
    outs = [full(512), full(D_I), packed(512), packed(512), packed(D_I),
            heads_t(8, DH), heads_t(8, DH), heads_t(8, DH), heads_t(8, DH + ONES_ROWS), heads_t(H_D, 2 * DH + ONES_ROWS),
            full_t(), full_t(), full_t(),
            (jax.ShapeDtypeStruct((H_I, n), F32), pl.BlockSpec((H_I, rows), lambda i: (0, i)))]
    return pl.pallas_call(
        _proj1_t_body,
        grid=(n // rows,),
        in_specs=[pl.BlockSpec((rows, d), lambda i: (i, 0)),
                  pl.BlockSpec((1, d), lambda i: (0, 0)),
                  pl.BlockSpec(w.shape, lambda i: (0, 0)),
                  pl.BlockSpec(wt.shape, lambda i: (0, 0))],
        out_specs=[o[1] for o in outs],
        out_shape=[o[0] for o in outs],
        compiler_params=_cparams("parallel"),
        name="proj1_t",
    )(x, g, w, wt)


def _sublane_all(x8, op):
    for shift in (4, 2, 1):
        x8 = op(x8, pltpu.roll(x8, shift, 0))
    return x8


def _select_t_body(qit_ref, wit_ref, ve_ref, ki_ref, out_ref, keys_ref, gmax_ref, *, tq, lp, topk):
    rb = SEL_ROWS
    nv = (pl.program_id(0) + 1) * (tq // rb)
    wit = wit_ref[...]
    ve = ve_ref[...]
    rows = lax.broadcasted_iota(I32, (rb, tq), 0)
    gmax_ref[...] = jnp.full((rb, tq), INT_MIN, I32)

    def score_chunk(c, carry):
        r0 = pl.multiple_of(c * rb, rb)
        kb = ki_ref[pl.ds(r0, rb), :]
        sc = jnp.zeros((rb, tq), F32)
        for h in range(H_I):
            sc = sc + wit[h:h + 1, :] * jnp.maximum(_dot(kb, qit_ref[h]), 0.0)
        bits = lax.bitcast_convert_type(sc, I32)
        key = bits ^ ((bits >> 31) & INT_MAX)
        key = jnp.where(sc == 0.0, 0, key)
        key = jnp.where(rows + r0 < ve, key, INT_MIN)
        keys_ref[pl.ds(r0, rb), :] = key
        gmax_ref[...] = jnp.maximum(gmax_ref[...], key)
        return carry

    lax.fori_loop(0, nv, score_chunk, 0)

    def count_ge(mid):
        midb = jnp.broadcast_to(mid, (SUBLANES, tq))

        def body(c, accs):
            kc = keys_ref[pl.ds(pl.multiple_of(c * rb, rb), rb), :]
            accs = list(accs)
            for g in range(rb // SUBLANES):
                a = g % len(accs)
                accs[a] = accs[a] + jnp.where(kc[g * SUBLANES:(g + 1) * SUBLANES, :] >= midb, 1, 0)
            return tuple(accs)

        accs = lax.fori_loop(0, nv, body, (jnp.zeros((SUBLANES, tq), I32),) * 4)
        return _sublane_all(accs[0] + accs[1] + accs[2] + accs[3], jnp.add)[0:1, :]

    floor_avg = lambda a, b: (a >> 1) + (b >> 1) + (a & b & 1)

    def pending(lo, hi, clo):
        open_ = jnp.logical_and(floor_avg(lo, hi) != lo, clo > topk)
        return jnp.max(jnp.where(open_, 1.0, 0.0))

    def key_as_score(k):
        return lax.bitcast_convert_type(k ^ ((k >> 31) & INT_MAX), F32)

    def score_as_key(v):
        bits = lax.bitcast_convert_type(v, I32)
        return bits ^ ((bits >> 31) & INT_MAX)

    def bisect(st):
        lo, hi, clo, chi, it, _ = st
        half = floor_avg(lo, hi)
        lclo = jnp.log2(clo.astype(F32))
        frac = (lclo - math.log2(topk + 0.5)) / (lclo - jnp.log2(jnp.maximum(chi.astype(F32), 0.5)))
        slo, shi = key_as_score(lo), key_as_score(hi)
        guess = jnp.clip(score_as_key(slo + frac * (shi - slo)), lo + 1, hi - 1)
        use_guess = jnp.logical_and(it % 2 == 0, lo != INT_MIN)
        mid = jnp.where(use_guess, guess, half)
        cnt = count_ge(mid)
        active = jnp.logical_and(half != lo, clo > topk)
        up = jnp.logical_and(active, cnt >= topk)
        dn = jnp.logical_and(active, cnt < topk)
        lo, hi = jnp.where(up, mid, lo), jnp.where(dn, mid, hi)
        clo, chi = jnp.where(up, cnt, clo), jnp.where(dn, cnt, chi)
        return lo, hi, clo, chi, it + 1, pending(lo, hi, clo)

    g8 = gmax_ref[0:SUBLANES, :]
    h8 = g8
    for g in range(1, rb // SUBLANES):
        blk = gmax_ref[g * SUBLANES:(g + 1) * SUBLANES, :]
        g8 = jnp.minimum(g8, blk)
        h8 = jnp.maximum(h8, blk)
    lo0 = _sublane_all(g8, jnp.minimum)[0:1, :]
    hi0 = _sublane_all(h8, jnp.maximum)[0:1, :] + 1
    clo0 = jnp.where(lo0 == INT_MIN, ve, count_ge(lo0))
    lo, hi, clo, chi, _, _ = lax.while_loop(
        lambda st: st[5] > 0.5, bisect,
        (lo0, hi0, clo0, jnp.zeros((1, tq), I32), jnp.int32(0), pending(lo0, hi0, clo0)))

    need = jnp.where(clo > topk, topk - chi, ALL_TIES)
    need = jnp.where(lo == INT_MIN, 0, need).astype(F32)
    tr = TIE_ROWS
    lob = jnp.broadcast_to(lo, (tr, tq))
    needb = jnp.broadcast_to(need, (tr, tq))
    r = lax.broadcasted_iota(I32, (tr, tr), 0)
    c = lax.broadcasted_iota(I32, (tr, tr), 1)
    tri = jnp.where(c <= r, 1.0, 0.0).astype(BF16)

    def mask_chunk(cidx, seen):
        r0 = pl.multiple_of(cidx * tr, tr)
        k = keys_ref[pl.ds(r0, tr), :]
        eq = jnp.where(k == lob, 1.0, 0.0)
        cum = _dot(tri, eq.astype(BF16))
        take = jnp.where(seen + cum - eq < needb, eq, 0.0)
        sel = jnp.where(k > lob, 1.0, take)
        out_ref[pl.ds(r0, tr), :] = jnp.where(sel > 0.5, 0.0, NEG_INF).astype(out_ref.dtype)
        return seen + cum[tr - 1:tr, :]

    def plain_chunk(cidx, carry):
        r0 = pl.multiple_of(cidx * rb, rb)
        k = keys_ref[pl.ds(r0, rb), :]
        hit = jnp.where(k == INT_MIN, NEG_INF, 0.0)
        out_ref[pl.ds(r0, rb), :] = jnp.where(k >= lo, hit, NEG_INF).astype(out_ref.dtype)
        return carry

    any_tie = jnp.max(jnp.where(clo > topk, 1.0, 0.0)) > 0.5
    lax.cond(any_tie,
             lambda: lax.fori_loop(0, nv * (rb // tr), mask_chunk, jnp.zeros((1, tq), F32)),
             lambda: lax.fori_loop(0, nv, plain_chunk, jnp.zeros((1, tq), F32)))

    def fill_chunk(cidx, carry):
        out_ref[pl.ds(pl.multiple_of(cidx * rb, rb), rb), :] = jnp.full((rb, tq), NEG_INF, out_ref.dtype)
        return carry

    lax.fori_loop(nv, lp // rb, fill_chunk, 0)


def _select_t(qit, wit, ve, ki, topk):
    t = qit.shape[2]
    tq = PROMPT_TQ
    return pl.pallas_call(
        functools.partial(_select_t_body, tq=tq, lp=t, topk=topk),
        grid=(t // tq,),
        in_specs=[pl.BlockSpec((H_I, D_I, tq), lambda i: (0, 0, i)),
                  pl.BlockSpec((H_I, tq), lambda i: (0, i)),
                  pl.BlockSpec((1, tq), lambda i: (0, i)),
                  pl.BlockSpec((t, D_I), lambda i: (0, 0))],
        out_specs=pl.BlockSpec((t, tq), lambda i: (0, i)),
        out_shape=jax.ShapeDtypeStruct((t, t), BF16),
        scratch_shapes=[pltpu.VMEM((t, tq), I32), pltpu.VMEM((SEL_ROWS, tq), I32)],
        compiler_params=_cparams("parallel"),
        name="index_select_t",
    )(qit, wit, ve, ki)


def _attn_t_body(qi_ref, kj_ref, qct_ref, qdt_ref, kc_ref, kd_ref, vct_ref, vdt_ref, mask_ref, bt_ref, lam_ref,
                 g_ref, o_ref, m_ref, accc_ref, accd_ref, ot_ref, qbd_ref, s_ref, *, tq, per, e_far,
                 lam_init):
    i = qi_ref[pl.program_id(0)]
    j = kj_ref[pl.program_id(0)]
    e = i - j * per
    grp = MAPS_PER_DOT

    @pl.when(j == 0)
    def _():
        m_ref[...] = jnp.full(m_ref.shape, NEG_INF, F32)
        accc_ref[...] = jnp.zeros_like(accc_ref)
        accd_ref[...] = jnp.zeros_like(accd_ref)
        qbd_ref[...] = jnp.zeros_like(qbd_ref)
        for mp in range(N_MAPS):
            q = qct_ref[mp] if mp < H_C else qdt_ref[mp - H_C]
            a = mp % grp
            qbd_ref[mp // grp, a * DH:(a + 1) * DH, a * tq:(a + 1) * tq] = q

    def logits(g):
        k_ref = kc_ref if g < H_C // grp else kd_ref
        half = g % (H_C // grp)
        return _dot(k_ref[:, half * grp * DH:(half + 1) * grp * DH], qbd_ref[g])

    def step(near):
        sel = mask_ref[...].astype(F32)
        sel = jnp.concatenate([sel] * grp, axis=1)
        for g in range(N_MAPS // grp):
            s_ref[g] = logits(g) + sel if g < H_C // grp else logits(g)
        for mp in range(N_MAPS):
            g, a = mp // grp, mp % grp
            cols = slice(a * tq, (a + 1) * tq)
            if mp < H_C:
                bias_idx, vt, acc_ref, a_idx = mp, vct_ref[mp], accc_ref, mp
            else:
                dm = mp - H_C
                bias_idx, vt, acc_ref, a_idx = H_C + dm // 2, vdt_ref[dm // 2], accd_ref, dm

            def biased(rows):
                s = s_ref[g, rows, cols]
                if near:
                    s = s + bt_ref[0, bias_idx, rows, :]
                return s

            m_prev = m_ref[mp]
            m_new = jnp.maximum(m_prev, jnp.max(biased(slice(None)), axis=0, keepdims=True))
            alpha = jnp.exp2(m_prev - m_new)
            m_ref[mp] = m_new
            pv = None
            for r0 in range(0, s_ref.shape[1], PV_ROWS):
                rows = slice(r0, r0 + PV_ROWS)
                part = _dot(vt[:, rows], jnp.exp2(biased(rows) - m_new).astype(BF16))
                pv = part if pv is None else pv + part
            acc_ref[a_idx] = alpha * acc_ref[a_idx] + pv

    pl.when(e < e_far)(functools.partial(step, True))
    pl.when(e >= e_far)(functools.partial(step, False))

    @pl.when(j == i // per)
    def _():
        for h in range(H_C):
            ot_ref[h * DH:(h + 1) * DH, :] = accc_ref[h, 0:DH, :] / accc_ref[h, DH:DH + 1, :]
        lam = lam_ref[...]
        g = g_ref[...]
        dv = 2 * DH
        for h in range(H_D):
            a0 = accd_ref[2 * h, 0:dv, :] / accd_ref[2 * h, dv:dv + 1, :]
            a1 = accd_ref[2 * h + 1, 0:dv, :] / accd_ref[2 * h + 1, dv:dv + 1, :]
            od = a0 - lam * a1
            od = od * lax.rsqrt(jnp.mean(od * od, axis=0, keepdims=True) + EPS) * g * (1.0 - lam_init)
            ot_ref[W_C + h * 2 * DH:W_C + (h + 1) * 2 * DH, :] = od
        o_ref[...] = ot_ref[...].T.astype(o_ref.dtype)


def _attend_t(qct, qdt, kc, kd, vct, vdt, mask, btiles, lam, g, lam_init):
    t = qct.shape[2]
    tq, lb = PROMPT_TQ, PROMPT_LB
    per = lb // tq
    e_far = btiles.shape[0]
    pairs = [(i, j) for i in range(t // tq) for j in range(i // per + 1)]
    qi_tab = jnp.asarray(np.array([p[0] for p in pairs], np.int32))
    kj_tab = jnp.asarray(np.array([p[1] for p in pairs], np.int32))
    qspec = pl.BlockSpec((H_C, DH, tq), lambda s, qi, kj: (0, 0, qi[s]))
    kspec = pl.BlockSpec((lb, H_C * DH), lambda s, qi, kj: (kj[s], 0))
    grid_spec = pltpu.PrefetchScalarGridSpec(
        num_scalar_prefetch=2,
        grid=(len(pairs),),
        in_specs=[qspec, qspec, kspec, kspec,
                  pl.BlockSpec((H_C, DH + ONES_ROWS, lb), lambda s, qi, kj: (0, 0, kj[s])),
                  pl.BlockSpec((H_D, 2 * DH + ONES_ROWS, lb), lambda s, qi, kj: (0, 0, kj[s])),
                  pl.BlockSpec((lb, tq), lambda s, qi, kj: (kj[s], qi[s])),
                  pl.BlockSpec((1, N_HEADS_BIAS, lb, tq),
                               lambda s, qi, kj: (jnp.minimum(qi[s] - kj[s] * per, e_far - 1), 0, 0, 0)),
                  pl.BlockSpec((1, 1), lambda s, qi, kj: (0, 0)),
                  pl.BlockSpec((2 * DH, 1), lambda s, qi, kj: (0, 0))],
        out_specs=pl.BlockSpec((tq, W_C + W_D), lambda s, qi, kj: (qi[s], 0)),
        scratch_shapes=[pltpu.VMEM((N_MAPS, 1, tq), F32),
                        pltpu.VMEM((H_C, DH + ONES_ROWS, tq), F32),
                        pltpu.VMEM((2 * H_D, 2 * DH + ONES_ROWS, tq), F32),
                        pltpu.VMEM((W_C + W_D, tq), F32),
                        pltpu.VMEM((N_MAPS // MAPS_PER_DOT, MAPS_PER_DOT * DH, MAPS_PER_DOT * tq), BF16),
                        pltpu.VMEM((N_MAPS // MAPS_PER_DOT, lb, MAPS_PER_DOT * tq), F32)])
    return pl.pallas_call(
        functools.partial(_attn_t_body, tq=tq, per=per, e_far=e_far, lam_init=lam_init),
        grid_spec=grid_spec,
        out_shape=jax.ShapeDtypeStruct((t, W_C + W_D), BF16),
        compiler_params=_cparams("arbitrary"),
        name="attend_t",
    )(qi_tab, kj_tab, qct, qdt, kc, kd, vct, vdt, mask, btiles, lam, g)


def _attn_s_body(qc_ref, qd_ref, ck_ref, cv_ref, dk_ref, dv_ref, nck_ref, ncv_ref, ndk_ref, ndv_ref,
                 mask_ref, bias_ref, lam_ref, g_ref, o_ref, m_ref, l_ref, acc_ref, *, ts, ncache, lam_init):
    j = pl.program_id(1)
    rows = H_C * ts

    @pl.when(j == 0)
    def _():
        m_ref[...] = jnp.full(m_ref.shape, NEG_INF, F32)
        l_ref[...] = jnp.zeros_like(l_ref)
        acc_ref[...] = jnp.zeros_like(acc_ref)

    def step(kc_t, vc_t, kd_t, vd):
        sel = jnp.tile(mask_ref[0, 0].astype(F32), (H_C, 1))
        bias = bias_ref[0]
        s_c = _dot(qc_ref[0], kc_t.astype(BF16)) + sel + bias[0:rows]
        s_d = _dot(qd_ref[0], kd_t.astype(BF16)) + bias[rows:2 * rows]
        for idx, (s, v) in enumerate(((s_c, vc_t), (s_d, vd))):
            m_prev = m_ref[idx]
            m_new = jnp.maximum(m_prev, jnp.max(s, axis=-1, keepdims=True))
            alpha = jnp.exp(m_prev - m_new)
            p = jnp.exp(s - m_new)
            l_ref[idx] = alpha * l_ref[idx] + jnp.sum(p, axis=-1, keepdims=True)
            m_ref[idx] = m_new
            p = p.astype(BF16)
            pv = _dot_nt(p, v.astype(BF16)) if idx == 0 else _dot(p, v.astype(BF16))
            acc_ref[idx] = alpha * acc_ref[idx] + pv

    @pl.when(j < ncache)
    def _():
        step(ck_ref[0], cv_ref[0], dk_ref[0], dv_ref[0])

    @pl.when(j == ncache)
    def _():
        step(nck_ref[0], ncv_ref[0], ndk_ref[0], ndv_ref[0])
        for h in range(H_C):
            r = slice(h * ts, (h + 1) * ts)
            o_ref[:, h * DH:(h + 1) * DH] = (acc_ref[0, r, h * DH:(h + 1) * DH] / l_ref[0, r, :]).astype(o_ref.dtype)
        lam = lam_ref[...]
        g = g_ref[...]
        for h in range(H_D):
            r0 = slice(2 * h * ts, (2 * h + 1) * ts)
            r1 = slice((2 * h + 1) * ts, (2 * h + 2) * ts)
            c = slice(h * 2 * DH, (h + 1) * 2 * DH)
            od = acc_ref[1, r0, c] / l_ref[1, r0, :] - lam * (acc_ref[1, r1, c] / l_ref[1, r1, :])
            od = _rms(od, g) * (1.0 - lam_init)
            o_ref[:, W_C + h * 2 * DH:W_C + (h + 1) * 2 * DH] = od.astype(o_ref.dtype)


def _attend_s(qbd_c, qbd_d, caches, news, mask, bias, lam, g, lam_init):
    nb, rows, _ = qbd_c.shape
    ts = rows // H_C
    lb = KEY_BLOCK
    ncache = caches[3].shape[1] // lb
    qspec = pl.BlockSpec((1, rows, W_C), lambda b, j: (b, 0, 0))
    ctspec = pl.BlockSpec((1, W_C, lb), lambda b, j: (b, 0, jnp.minimum(j, ncache - 1)))
    cspec = pl.BlockSpec((1, lb, W_C), lambda b, j: (b, jnp.minimum(j, ncache - 1), 0))
    ntspec = pl.BlockSpec((1, W_C, lb), lambda b, j: (b, 0, 0))
    nspec = pl.BlockSpec((1, lb, W_C), lambda b, j: (b, 0, 0))
    return pl.pallas_call(
        functools.partial(_attn_s_body, ts=ts, ncache=ncache, lam_init=lam_init),
        grid=(nb, ncache + 1),
        in_specs=[qspec, qspec, ctspec, ctspec, ctspec, cspec, ntspec, ntspec, ntspec, nspec,
                  pl.BlockSpec((1, 1, ts, lb), lambda b, j: (b, j, 0, 0)),
                  pl.BlockSpec((1, 2 * rows, lb), lambda b, j: (j, 0, 0)),
                  pl.BlockSpec((1, 1), lambda b, j: (0, 0)),
                  pl.BlockSpec((1, 2 * DH), lambda b, j: (0, 0))],
        out_specs=pl.BlockSpec((ts, W_C + W_D), lambda b, j: (b, 0)),
        out_shape=jax.ShapeDtypeStruct((nb * ts, W_C + W_D), BF16),
        scratch_shapes=[pltpu.VMEM((2, rows, 1), F32), pltpu.VMEM((2, rows, 1), F32),
                        pltpu.VMEM((2, rows, W_C), F32)],
        compiler_params=_cparams("parallel", "arbitrary"),
        name="attend_s",
    )(qbd_c, qbd_d, *caches, *news, mask, bias, lam, g)


def _out_router_body(x_ref, o_ref, w_ref, g_ref, r_ref, x3_ref, xn_ref, gate_ref, gatet_ref):
    x3 = x_ref[...] + _dot(o_ref[...], w_ref[...])
    x3_ref[...] = x3
    xn = _rms(x3, g_ref[...]).astype(BF16)
    xn_ref[...] = xn
    logits = _dot(xn, r_ref[...])
    lane = lax.broadcasted_iota(I32, logits.shape, 1)
    logits = jnp.where(lane < N_EXP, logits, -jnp.inf)
    m1 = jnp.max(logits, axis=-1, keepdims=True)
    i1 = jnp.min(jnp.where(logits == m1, lane, LANES), axis=-1, keepdims=True)
    rest = jnp.where(lane == i1, -jnp.inf, logits)
    m2 = jnp.max(rest, axis=-1, keepdims=True)
    i2 = jnp.min(jnp.where(rest == m2, lane, LANES), axis=-1, keepdims=True)
    e = jnp.exp(m2 - m1)
    g1 = 1.0 / (1.0 + e)
    g2 = e / (1.0 + e)
    gate = jnp.where(lane == i1, g1, 0.0) + jnp.where(lane == i2, g2, 0.0)
    gate_ref[...] = gate
    gatet_ref[...] = gate.T[0:GATE_ROWS, :]


def _out_router(x, o, w, g, router, rows):
    n, d = x.shape
    return pl.pallas_call(
        _out_router_body,
        grid=(n // rows,),
        in_specs=[pl.BlockSpec((rows, d), lambda i: (i, 0)),
                  pl.BlockSpec((rows, d), lambda i: (i, 0)),
                  pl.BlockSpec(w.shape, lambda i: (0, 0)),
                  pl.BlockSpec((1, d), lambda i: (0, 0)),
                  pl.BlockSpec(router.shape, lambda i: (0, 0))],
        out_specs=[pl.BlockSpec((rows, d), lambda i: (i, 0)),
                   pl.BlockSpec((rows, d), lambda i: (i, 0)),
                   pl.BlockSpec((rows, LANES), lambda i: (i, 0)),
                   pl.BlockSpec((GATE_ROWS, rows), lambda i: (0, i))],
        out_shape=[jax.ShapeDtypeStruct((n, d), F32), jax.ShapeDtypeStruct((n, d), BF16),
                   jax.ShapeDtypeStruct((n, LANES), F32), jax.ShapeDtypeStruct((GATE_ROWS, n), F32)],
        compiler_params=_cparams("parallel"),
        name="out_router",
    )(x, o, w, g, router)


def _moe_body(x_ref, xn_ref, gate_ref, gf_ref, wg_ref, wu_ref, wd_ref, y_ref, acc_ref):
    e = pl.program_id(1)
    k = pl.program_id(2)

    @pl.when(jnp.logical_and(e == 0, k == 0))
    def _():
        acc_ref[...] = jnp.zeros_like(acc_ref)

    gate = gate_ref[...]
    lane = lax.broadcasted_iota(I32, gate.shape, 1)
    ge = jnp.sum(jnp.where(lane == e, gate, 0.0), axis=-1, keepdims=True)
    xn = xn_ref[...]
    h = jax.nn.silu(_dot(xn, wg_ref[0])) * _dot(xn, wu_ref[0])
    acc_ref[...] += _dot((ge * h).astype(BF16), wd_ref[0])

    @pl.when(jnp.logical_and(e == pl.num_programs(1) - 1, k == pl.num_programs(2) - 1))
    def _():
        y_ref[...] = _rms(x_ref[...] + acc_ref[...], gf_ref[...])


def _moe(x, xn, gate, gf, wg, wu, wd, rows, fb):
    n, d = x.shape
    ne, _, dff = wg.shape
    return pl.pallas_call(
        _moe_body,
        grid=(n // rows, ne, dff // fb),
        in_specs=[pl.BlockSpec((rows, d), lambda i, e, k: (i, 0)),
                  pl.BlockSpec((rows, d), lambda i, e, k: (i, 0)),
                  pl.BlockSpec((rows, LANES), lambda i, e, k: (i, 0)),
                  pl.BlockSpec((1, d), lambda i, e, k: (0, 0)),
                  pl.BlockSpec((1, d, fb), lambda i, e, k: (e, 0, k)),
                  pl.BlockSpec((1, d, fb), lambda i, e, k: (e, 0, k)),
                  pl.BlockSpec((1, fb, d), lambda i, e, k: (e, k, 0))],
        out_specs=pl.BlockSpec((rows, d), lambda i, e, k: (i, 0)),
        out_shape=jax.ShapeDtypeStruct((n, d), F32),
        scratch_shapes=[pltpu.VMEM((rows, d), F32)],
        compiler_params=_cparams("parallel", "arbitrary", "arbitrary"),
        name="moe",
    )(x, xn, gate, gf, wg, wu, wd)


def _t5_bucket(rel):
    half = N_BUCKETS // 2
    max_exact = half // 2
    ret = np.where(rel > 0, half, 0)
    n = np.abs(rel)
    nf = np.maximum(n, 1).astype(np.float32)
    large = max_exact + (np.log(nf / np.float32(max_exact)) / np.float32(math.log(MAX_DIST / max_exact))
                         * np.float32(half - max_exact)).astype(np.int32)
    large = np.minimum(large, half - 1)
    return (ret + np.where(n < max_exact, n, large)).astype(np.int32)


def _bias_tile(rel_bias, q_pos, k_pos, k_real):
    rel = k_pos[None, :] - q_pos[:, None]
    onehot = (jnp.asarray(_t5_bucket(rel).astype(np.int8))[:, :, None]
              == jnp.arange(N_BUCKETS, dtype=jnp.int8)).astype(F32)
    bias = jnp.einsum("qkb,bh->hqk", onehot, rel_bias.astype(F32), precision=lax.Precision.HIGHEST)
    ok = np.logical_and(k_pos[None, :] // CHUNK <= q_pos[:, None] // CHUNK, k_pos[None, :] < k_real)
    return jnp.where(ok[None], bias, NEG_INF)


def _in1_plan():
    scale = DH ** -0.5
    heads64 = lambda o, sc=1.0: [(o, h, h * DH, (h + 1) * DH, sc) for h in range(8)]
    plan = [
        ((0, 512), heads64(0, scale)),
        ((512, 1024), [(1, None, 0, 512, 1.0)] + heads64(2)),
        ((1024, 1536), [(3, None, 0, 512, 1.0)] + heads64(4)),
        ((1536, 2048), heads64(5, scale)),
        ((2048, 2560), [(6, None, 0, 512, 1.0)] + heads64(7)),
        ((2560, 3072), [(8, None, 0, 512, 1.0)] + [(9, h, h * 128, (h + 1) * 128, 1.0) for h in range(H_D)]),
        ((3072, 3584), heads64(10)),
        ((3584, 3712), [(11, None, 0, D_I, 1.0), (12, None, 0, D_I, 1.0), (13, None, D_I, D_I + H_I, 1.0)]),
    ]
    out_defs = [(DH, BF16, 8), (512, F32, None), (DH, BF16, 8), (512, F32, None), (DH, BF16, 8),
                (DH, BF16, 8), (512, F32, None), (DH, BF16, 8), (512, F32, None), (2 * DH, BF16, H_D),
                (D_I, BF16, 8), (D_I, F32, None), (D_I, BF16, None), (H_I, F32, None)]
    return plan, out_defs


def _layer0(x, hist, rb, emit_v, p):
    n = x.shape[0]
    rows = _largest_divisor(n, 512, 16)
    plan = [((0, p["w_in0"].shape[1]), [(0, None, 0, p["w_in0"].shape[1], 1.0)])]
    (z,) = _rms_proj(x, p["ln_mix0"], p["w_in0"], plan, [(p["w_in0"].shape[1], F32, None)], rows)
    ws = p["ws_prompt"] if hist is None else p["ws_sample"]
    bs = p["bs_prompt"] if hist is None else p["bs_sample"]
    outs = _mixer_ab(x, z, hist, p["gmlp_ln_g"], p["gmlp_ln_b"], ws, bs, p["conv_k"], p["w_out0"], rb, emit_v)
    x1 = outs[0]
    x2 = _ffn(x1, p["ln_ffn0"], p["ffn_wg"], p["ffn_wu"], p["ffn_wd"], rows, p["ffn_fb"])
    return (x2,) + tuple(outs[1:])


GATE_ROWS = 16
MOE_ROWS = 128


def _moe_routed_body(x_ref, xn_ref, gate_ref, gatet_ref, gf_ref, wg_ref, wu_ref, wd_ref, y_ref,
                     triu_ref, tril_ref, crow_ref, ccol_ref, xs_ref, ge_ref, acc_ref, yblk_ref, nsub_ref,
                     *, tb, sub):
    e = pl.program_id(1)
    k = pl.program_id(2)
    first_k = k == 0
    last_k = k == pl.num_programs(2) - 1

    @pl.when(jnp.logical_and(e == 0, first_k))
    def _():
        r = lax.broadcasted_iota(I32, (tb, tb), 0)
        c = lax.broadcasted_iota(I32, (tb, tb), 1)
        triu_ref[...] = jnp.where(r < c, 1.0, 0.0).astype(BF16)
        tril_ref[...] = jnp.where(c < r, 1.0, 0.0).astype(BF16)
        crow_ref[...] = _dot(jnp.where(gatet_ref[...] > 0.0, 1.0, 0.0).astype(BF16), triu_ref[...])
        ccol_ref[...] = _dot(tril_ref[...], jnp.where(gate_ref[...] > 0.0, 1.0, 0.0).astype(BF16))
        yblk_ref[...] = jnp.zeros_like(yblk_ref)
        acc_ref[...] = jnp.zeros_like(acc_ref)

    @pl.when(first_k)
    def _():
        g_e = gatet_ref[pl.ds(e, 1), :]
        m_e = g_e > 0.0
        c_e = crow_ref[pl.ds(e, 1), :]
        nsub = (jnp.sum(jnp.where(m_e, 1.0, 0.0)).astype(I32) + sub - 1) // sub
        nsub_ref[0] = nsub
        xn = xn_ref[...]
        slot = lax.broadcasted_iota(I32, (sub, tb), 0).astype(F32)

        def pack(s, carry):
            r0 = pl.multiple_of(s * sub, sub)
            hit = jnp.logical_and(m_e, c_e == slot + (s * sub).astype(F32))
            onehot = jnp.where(hit, 1.0, 0.0)
            xs_ref[pl.ds(r0, sub), :] = _dot(onehot.astype(BF16), xn).astype(BF16)
            ge_ref[pl.ds(r0, sub), :] = jnp.sum(onehot * g_e, axis=-1, keepdims=True)
            acc_ref[pl.ds(r0, sub), :] = jnp.zeros((sub, acc_ref.shape[1]), F32)
            return carry

        lax.fori_loop(0, nsub, pack, 0)

    nsub = nsub_ref[0]

    def experts(s, carry):
        rows = pl.ds(pl.multiple_of(s * sub, sub), sub)
        xs = xs_ref[rows, :]
        h = jax.nn.silu(_dot(xs, wg_ref[0])) * _dot(xs, wu_ref[0])
        acc_ref[rows, :] += _dot((ge_ref[rows, :] * h).astype(BF16), wd_ref[0])
        return carry

    lax.fori_loop(0, nsub, experts, 0)

    @pl.when(last_k)
    def _():
        lane = lax.broadcasted_iota(I32, (tb, LANES), 1)
        pick = lambda a: jnp.sum(jnp.where(lane == e, a, 0.0), axis=-1, keepdims=True)
        m_e = pick(gate_ref[...]) > 0.0
        c_e = pick(ccol_ref[...])
        slot = lax.broadcasted_iota(I32, (tb, sub), 1).astype(F32)

        def unpack(s, carry):
            rows = pl.ds(pl.multiple_of(s * sub, sub), sub)
            hit = jnp.logical_and(m_e, c_e == slot + (s * sub).astype(F32))
            onehot = jnp.where(hit, 1.0, 0.0).astype(BF16)
            y = acc_ref[rows, :]
            hi = y.astype(BF16)
            lo = (y - hi.astype(F32)).astype(BF16)
            yblk_ref[...] += _dot(jnp.concatenate([onehot, onehot], axis=1), jnp.concatenate([hi, lo], axis=0))
            return carry

        lax.fori_loop(0, nsub, unpack, 0)

    @pl.when(jnp.logical_and(e == pl.num_programs(1) - 1, last_k))
    def _():
        y_ref[...] = _rms(x_ref[...] + yblk_ref[...], gf_ref[...])


def _moe_routed(x, xn, gate, gatet, gf, wg, wu, wd, tb, fb):
    n, d = x.shape
    ne, _, dff = wg.shape
    return pl.pallas_call(
        functools.partial(_moe_routed_body, tb=tb, sub=min(MOE_ROWS, tb)),
        grid=(n // tb, ne, dff // fb),
        in_specs=[pl.BlockSpec((tb, d), lambda i, e, k: (i, 0)),
                  pl.BlockSpec((tb, d), lambda i, e, k: (i, 0)),
                  pl.BlockSpec((tb, LANES), lambda i, e, k: (i, 0)),
                  pl.BlockSpec((GATE_ROWS, tb), lambda i, e, k: (0, i)),
                  pl.BlockSpec((1, d), lambda i, e, k: (0, 0)),
                  pl.BlockSpec((1, d, fb), lambda i, e, k: (e, 0, k)),
                  pl.BlockSpec((1, d, fb), lambda i, e, k: (e, 0, k)),
                  pl.BlockSpec((1, fb, d), lambda i, e, k: (e, k, 0))],
        out_specs=pl.BlockSpec((tb, d), lambda i, e, k: (i, 0)),
        out_shape=jax.ShapeDtypeStruct((n, d), F32),
        scratch_shapes=[pltpu.VMEM((tb, tb), BF16), pltpu.VMEM((tb, tb), BF16),
                        pltpu.VMEM((GATE_ROWS, tb), F32), pltpu.VMEM((tb, LANES), F32),
                        pltpu.VMEM((tb, d), BF16), pltpu.VMEM((tb, 1), F32),
                        pltpu.VMEM((tb, d), F32), pltpu.VMEM((tb, d), F32),
                        pltpu.SMEM((1,), I32)],
        compiler_params=_cparams("parallel", "arbitrary", "arbitrary"),
        name="moe_routed",
    )(x, xn, gate, gatet, gf, wg, wu, wd)


def _layer1_tail(x, o, p):
    n = x.shape[0]
    rows = _largest_divisor(n, 512, 16)
    x3, xn, gate, gatet = _out_router(x, o, p["w_out1"], p["ln_ffn1"], p["router"], rows)
    return _moe_routed(x3, xn, gate, gatet, p["ln_final"], p["exp_wg"], p["exp_wu"], p["exp_wd"],
                       _largest_divisor(n, 1024, 128), p["exp_fb"])


def kernel(x_prompt, x_sample, state_b_conv, cache_c_k, cache_c_v, cache_idx_k, cache_d_k, cache_d_v, rel_bias, ln_mix, ln_ffn, ln_final, w_in0, gmlp_ln_g, gmlp_ln_b, gmlp_ws, gmlp_bs, conv_k, w_out0, ffn_wg, ffn_wu, ffn_wd, w_in1, lam_qk, subln_g, w_out1, router, exp_wg, exp_wu, exp_wd):
    bp, seq, d = x_prompt.shape
    bs_, ts, _ = x_sample.shape
    past = cache_c_k.shape[2]
    assert bp == 1 and ln_mix.shape[0] == 2 and seq % PROMPT_LB == 0 and ts % SUBLANES == 0 and ts <= CHUNK
    assert past % KEY_BLOCK == 0
    gmlp_chunk = gmlp_ws.shape[-1]
    lam_init = 0.8 - 0.6 * math.exp(-0.3 * 1)

    def ws_masked(rows):
        r = jnp.arange(rows)
        ok = (r[None, :] // CHUNK) <= (r[:, None] // CHUNK)
        return jnp.where(ok[None], gmlp_ws[0][:, :rows, :rows], 0.0).astype(BF16)

    in1_pad = (-w_in1.shape[2]) % LANES
    lf = lam_qk[0].astype(F32)
    lam = (jnp.exp(jnp.sum(lf[0] * lf[1])) - jnp.exp(jnp.sum(lf[2] * lf[3])) + lam_init).reshape(1, 1)
    p = {
        "ln_mix0": ln_mix[0:1], "ln_ffn0": ln_ffn[0:1], "ln_mix1": ln_mix[1:2], "ln_ffn1": ln_ffn[1:2],
        "ln_final": ln_final.reshape(1, d),
        "w_in0": w_in0[0].astype(BF16),
        "gmlp_ln_g": gmlp_ln_g[0].reshape(1, W_A), "gmlp_ln_b": gmlp_ln_b[0].reshape(1, W_A),
        "ws_prompt": ws_masked(gmlp_chunk), "ws_sample": ws_masked(ts),
        "bs_prompt": gmlp_bs[0][:, :gmlp_chunk].T, "bs_sample": gmlp_bs[0][:, :ts].T,
        "conv_k": conv_k[0], "w_out0": w_out0[0].astype(BF16),
        "ffn_wg": ffn_wg[0].astype(BF16), "ffn_wu": ffn_wu[0].astype(BF16), "ffn_wd": ffn_wd[0].astype(BF16),
        "ffn_fb": _largest_divisor(ffn_wg.shape[2], 1408, LANES),
        "w_in1": jnp.pad(w_in1[0], ((0, 0), (0, in1_pad))).astype(BF16),
        "w_out1": w_out1[0].astype(BF16),
        "router": jnp.pad(router[0], ((0, 0), (0, LANES - N_EXP))).astype(BF16),
        "exp_wg": exp_wg[0].astype(BF16), "exp_wu": exp_wu[0].astype(BF16), "exp_wd": exp_wd[0].astype(BF16),
        "exp_fb": _largest_divisor(exp_wg.shape[3], 896, LANES),
    }
    g_sub = subln_g[0].reshape(1, 2 * DH)
    plan1, out_defs1 = _in1_plan()

    xp = x_prompt.reshape(seq, d)
    xs = x_sample.reshape(bs_ * ts, d)
    xp, p_tail = _layer0(xp, None, gmlp_chunk, False, p)
    hist = jnp.pad(state_b_conv[0], ((0, 0), (SUBLANES - 2, 0), (0, 0)))
    xs, s_tail, s_av = _layer0(xs, hist, ts, True, p)
    p_b_conv = p_tail[-1, SUBLANES - 2:, :].reshape(1, 1, 2, W_B)
    s_b_conv = s_tail[:, SUBLANES - 2:, :].reshape(1, bs_, 2, W_B)
    s_a_v = s_av.reshape(1, bs_, ts, W_A)

    lb = KEY_BLOCK
    w1 = w_in1[0]
    off = np.concatenate([[0], np.cumsum(IN1_SIZES)])
    field = lambda k: w1[:, off[k]:off[k + 1]]
    padc = lambda a: jnp.pad(a, ((0, 0), (0, LANES - a.shape[1])))
    log2e = math.log2(math.e)
    qscale = DH ** -0.5 * log2e
    w_norm = jnp.concatenate([field(1), field(4), field(5), padc(field(7))], axis=1).astype(BF16)
    w_tran = jnp.concatenate([field(0) * qscale, field(3) * qscale, field(6), field(2), field(5),
                              field(1), field(4), padc(field(8))], axis=1).T.astype(BF16)
    (vd32, ki32, kc, kd, ki, qct, qdt, qit, vct, vdt, vc32t, kc32t, kd32t, wit) = _proj1_t(
        xp, p["ln_mix1"], w_norm, w_tran, PROMPT_TQ)
    token_major = lambda a, *dims: jnp.moveaxis(a.reshape(dims + (seq,)), -1, 0).reshape((1, 1, seq) + dims)
    p_c_k, p_c_v = token_major(kc32t, H_C, DH), token_major(vc32t, H_C, DH)
    p_d_k = token_major(kd32t, H_D, 2, DH)
    pos = jnp.arange(seq, dtype=I32)
    ve = ((pos // CHUNK + 1) * CHUNK).reshape(1, seq)
    mask = _select_t(qit, wit, ve, ki, min(TOPK_MAX, seq // 4))
    e_far = -(-(PROMPT_LB - 1 + MAX_DIST) // PROMPT_TQ)
    kpos = np.arange(PROMPT_LB)
    qpos = np.arange(PROMPT_TQ)
    base = e_far * PROMPT_TQ
    rel_near = (rel_bias - rel_bias[N_BUCKETS // 2 - 1:N_BUCKETS // 2]) * log2e
    btiles = jnp.stack([jnp.transpose(_bias_tile(rel_near, base + e * PROMPT_TQ + qpos, base + kpos, base + PROMPT_LB),
                                      (0, 2, 1)) for e in range(e_far)])
    op = _attend_t(qct, qdt, kc, kd, vct, vdt, mask, btiles, lam, subln_g[0].reshape(2 * DH, 1), lam_init)
    y_prompt = _layer1_tail(xp, op, p).reshape(1, seq, d)

    sr = _rms_proj(xs, p["ln_mix1"], p["w_in1"], plan1, out_defs1, _largest_divisor(bs_ * ts, 256, 16))
    sqc, skc32, skc, svc32, svc, sqd, skd32, skd, svd32, svd, sqi, ski32, ski, swi = sr
    nk = past + ts
    lps = -(-nk // lb) * lb

    kis = jnp.pad(jnp.concatenate([cache_idx_k[0].astype(BF16), ski.reshape(bs_, ts, D_I)], axis=1),
                  ((0, 0), (0, lps - nk), (0, 0)))
    ves = jnp.full((bs_ * ts, 1), nk, I32)
    smask = _select(sqi, swi, ves, kis, bs_, ts, ts, min(TOPK_MAX, nk // 4), lambda i: lps // lb)
    sq_pos = past + np.arange(ts)
    sbt = jnp.stack([_bias_tile(rel_bias, sq_pos, j * lb + np.arange(lb), nk) for j in range(lps // lb)])
    sbias = jnp.concatenate([sbt[:, :H_C], jnp.repeat(sbt[:, H_C:], 2, axis=1)], axis=1).reshape(lps // lb, N_MAPS * ts, lb)

    def block_diag(q):
        qb = jnp.transpose(q.reshape(H_C, bs_, ts, DH), (1, 0, 2, 3))
        eye = jnp.eye(H_C, dtype=q.dtype)
        return (qb[:, :, :, None, :] * eye[None, :, None, :, None]).reshape(bs_, H_C * ts, W_C)

    def cache_rows(c, feature_major):
        c = c.reshape(bs_, past, W_C)
        return jnp.transpose(c, (0, 2, 1)) if feature_major else c

    def new_rows(a, feature_major):
        a = jnp.pad(a.reshape(bs_, ts, W_C), ((0, 0), (0, lb - ts), (0, 0)))
        return jnp.transpose(a, (0, 2, 1)) if feature_major else a
    os_ = _attend_s(block_diag(sqc), block_diag(sqd),
                    [cache_rows(c[0], fm) for c, fm in ((cache_c_k, True), (cache_c_v, True), (cache_d_k, True), (cache_d_v, False))],
                    [new_rows(a, fm) for a, fm in ((skc32, True), (svc32, True), (skd32, True), (svd32, False))],
                    smask, sbias, lam, g_sub, lam_init)
    y_sample = _layer1_tail(xs, os_, p).reshape(bs_, ts, d)

    r5 = lambda a, n, t, *tail: a.reshape((1, n, t) + tail)
    return (y_prompt, y_sample, p_b_conv,
            p_c_k, p_c_v, r5(ki32, 1, seq, D_I), p_d_k, r5(vd32, 1, seq, H_D, 2 * DH),
            s_a_v, s_b_conv,
            r5(skc32, bs_, ts, H_C, DH), r5(svc32, bs_, ts, H_C, DH), r5(ski32, bs_, ts, D_I),
            r5(skd32, bs_, ts, H_D, 2, DH), r5(svd32, bs_, ts, H_D, 2 * DH))
```

```python
import functools
import math

import jax
import jax.numpy as jnp
import numpy as np
from jax import lax
from jax.experimental import pallas as pl
from jax.experimental.pallas import tpu as pltpu

F32 = jnp.float32
BF16 = jnp.bfloat16
I32 = jnp.int32

CHUNK = 64
EPS = 1e-6
NEG_INF = -1e30
H_A = 4
C_A = 128
W_A = H_A * C_A
W_B = 512
H_C = 8
DH = 64
W_C = H_C * DH
H_I = 8
D_I = 64
TOPK_MAX = 256
H_D = 4
W_D = H_D * 2 * DH
N_BUCKETS = 32
MAX_DIST = 128
N_EXP = 8
IN1_SIZES = (W_C, W_C, W_C, W_D, W_D, W_D, H_I * D_I, D_I, H_I)
N_HEADS_BIAS = H_C + H_D
N_MAPS = H_C + 2 * H_D

LANES = 128
SUBLANES = 8
VMEM_LIMIT = 56 * 1024 * 1024

INT_MIN = -(2 ** 31)
INT_MAX = 2 ** 31 - 1
ALL_TIES = 2 ** 30

KEY_BLOCK = 512


def _cparams(*sem):
    return pltpu.CompilerParams(dimension_semantics=sem, vmem_limit_bytes=VMEM_LIMIT)


def _largest_divisor(n, target, mult):
    if n <= target:
        return n
    d = (target // mult) * mult
    while d >= mult:
        if n % d == 0:
            return d
        d -= mult
    raise ValueError(f"no block of multiple {mult} divides {n}")


def _rms(x, g):
    return x * lax.rsqrt(jnp.mean(x * x, axis=-1, keepdims=True) + EPS) * g


def _dot(a, b):
    return jnp.dot(a, b, preferred_element_type=F32)


def _dot_nt(a, b):
    return lax.dot_general(a, b, (((1,), (1,)), ((), ())), preferred_element_type=F32)


def _rms_proj_body(x_ref, g_ref, w_ref, *out_refs, plan):
    xn = _rms(x_ref[...], g_ref[...]).astype(BF16)
    for (c0, c1), writes in plan:
        z = _dot(xn, w_ref[:, c0:c1])
        for o_idx, head, z0, z1, scale in writes:
            val = z[:, z0:z1]
            if scale != 1.0:
                val = val * scale
            ref = out_refs[o_idx]
            if head is None:
                ref[...] = val.astype(ref.dtype)
            else:
                ref[head] = val.astype(ref.dtype)


def _rms_proj(x, g, w, plan, out_defs, rows):
    n, d = x.shape
    grid = (n // rows,)
    out_shape, out_specs = [], []
    for width, dtype, heads in out_defs:
        if heads is None:
            out_shape.append(jax.ShapeDtypeStruct((n, width), dtype))
            out_specs.append(pl.BlockSpec((rows, width), lambda i: (i, 0)))
        else:
            out_shape.append(jax.ShapeDtypeStruct((heads, n, width), dtype))
            out_specs.append(pl.BlockSpec((heads, rows, width), lambda i: (0, i, 0)))
    return pl.pallas_call(
        functools.partial(_rms_proj_body, plan=plan),
        grid=grid,
        in_specs=[pl.BlockSpec((rows, d), lambda i: (i, 0)),
                  pl.BlockSpec((1, d), lambda i: (0, 0)),
                  pl.BlockSpec(w.shape, lambda i: (0, 0))],
        out_specs=out_specs,
        out_shape=out_shape,
        compiler_params=_cparams("parallel"),
        name="rms_proj",
    )(x, g, w)


def _mixer_ab_body(*refs, rb, from_prev, emit_v):
    it = iter(refs)
    x_ref, u_ref, v_ref, gb_ref, gc_ref, xin_ref = (next(it) for _ in range(6))
    if from_prev:
        gcp_ref, xinp_ref = next(it), next(it)
    else:
        hist_ref = next(it)
    lng_ref, lnb_ref, ws_ref, bs_ref, ck_ref, wout_ref = (next(it) for _ in range(6))
    x1_ref, tail_ref = next(it), next(it)
    vout_ref = next(it) if emit_v else None
    wext_ref = next(it)

    w = gc_ref[...] * xin_ref[...]
    if from_prev:
        hist = jnp.where(pl.program_id(0) > 0, gcp_ref[...] * xinp_ref[...], 0.0)
    else:
        hist = hist_ref[0]
    wext_ref[0:SUBLANES, :] = hist
    wext_ref[SUBLANES:, :] = w
    ck = ck_ref[...]
    conv = (ck[0:1] * wext_ref[SUBLANES - 2:SUBLANES - 2 + rb, :]
            + ck[1:2] * wext_ref[SUBLANES - 1:SUBLANES - 1 + rb, :]
            + ck[2:3] * w)
    y_b = gb_ref[...] * conv
    tail_ref[0] = w[rb - SUBLANES:, :]

    u = jax.nn.gelu(u_ref[...])
    v = jax.nn.gelu(v_ref[...])
    lng = lng_ref[...]
    lnb = lnb_ref[...]
    bs = bs_ref[...]
    acc = _dot(y_b.astype(BF16), wout_ref[W_A:, :])
    for h in range(H_A):
        sl = slice(h * C_A, (h + 1) * C_A)
        vh = v[:, sl]
        mu = jnp.mean(vh, axis=-1, keepdims=True)
        xc = vh - mu
        var = jnp.mean(xc * xc, axis=-1, keepdims=True)
        vln = xc * lax.rsqrt(var + EPS) * lng[:, sl] + lnb[:, sl]
        if emit_v:
            vout_ref[:, sl] = vln
        s = _dot(ws_ref[h], vln.astype(BF16)) + bs[:, h:h + 1]
        y_a = u[:, sl] * s
        acc = acc + _dot(y_a.astype(BF16), wout_ref[sl, :])
    x1_ref[...] = x_ref[...] + acc


def _mixer_ab(x, z, hist, lng, lnb, ws, bs, ck, wout, rb, emit_v):
    n, d = x.shape
    nb = n // rb
    from_prev = hist is None
    col = lambda c: pl.BlockSpec((rb, 512), lambda i, c=c: (i, c))
    in_specs = [pl.BlockSpec((rb, d), lambda i: (i, 0)), col(0), col(1), col(2), col(3), col(4)]
    args = [x, z, z, z, z, z]
    if from_prev:
        per = rb // SUBLANES
        prev = lambda c: pl.BlockSpec((SUBLANES, 512), lambda i, c=c: (jnp.maximum(i * per - 1, 0), c))
        in_specs += [prev(3), prev(4)]
        args += [z, z]
    else:
        in_specs += [pl.BlockSpec((1, SUBLANES, 512), lambda i: (i, 0, 0))]
        args += [hist]
    const = lambda a: pl.BlockSpec(a.shape, lambda i, nd=a.ndim: (0,) * nd)
    for a in (lng, lnb, ws, bs, ck, wout):
        in_specs.append(const(a))
        args.append(a)
    out_shape = [jax.ShapeDtypeStruct((n, d), F32), jax.ShapeDtypeStruct((nb, SUBLANES, 512), F32)]
    out_specs = [pl.BlockSpec((rb, d), lambda i: (i, 0)), pl.BlockSpec((1, SUBLANES, 512), lambda i: (i, 0, 0))]
    if emit_v:
        out_shape.append(jax.ShapeDtypeStruct((n, W_A), F32))
        out_specs.append(pl.BlockSpec((rb, W_A), lambda i: (i, 0)))
    return pl.pallas_call(
        functools.partial(_mixer_ab_body, rb=rb, from_prev=from_prev, emit_v=emit_v),
        grid=(nb,),
        in_specs=in_specs,
        out_specs=out_specs,
        out_shape=out_shape,
        scratch_shapes=[pltpu.VMEM((rb + SUBLANES, 512), F32)],
        compiler_params=_cparams("arbitrary"),
        name="mixer_ab",
    )(*args)


def _ffn_body(x_ref, g_ref, wg_ref, wu_ref, wd_ref, o_ref, xn_ref, acc_ref):
    k = pl.program_id(1)

    @pl.when(k == 0)
    def _():
        xn_ref[...] = _rms(x_ref[...], g_ref[...]).astype(BF16)
        acc_ref[...] = jnp.zeros_like(acc_ref)

    xn = xn_ref[...]
    h = jax.nn.silu(_dot(xn, wg_ref[...])) * _dot(xn, wu_ref[...])
    acc_ref[...] += _dot(h.astype(BF16), wd_ref[...])

    @pl.when(k == pl.num_programs(1) - 1)
    def _():
        o_ref[...] = x_ref[...] + acc_ref[...]


def _ffn(x, g, wg, wu, wd, rows, fb):
    n, d = x.shape
    dff = wg.shape[1]
    return pl.pallas_call(
        _ffn_body,
        grid=(n // rows, dff // fb),
        in_specs=[pl.BlockSpec((rows, d), lambda i, k: (i, 0)),
                  pl.BlockSpec((1, d), lambda i, k: (0, 0)),
                  pl.BlockSpec((d, fb), lambda i, k: (0, k)),
                  pl.BlockSpec((d, fb), lambda i, k: (0, k)),
                  pl.BlockSpec((fb, d), lambda i, k: (k, 0))],
        out_specs=pl.BlockSpec((rows, d), lambda i, k: (i, 0)),
        out_shape=jax.ShapeDtypeStruct((n, d), F32),
        scratch_shapes=[pltpu.VMEM((rows, d), BF16), pltpu.VMEM((rows, d), F32)],
        compiler_params=_cparams("parallel", "arbitrary"),
        name="ffn",
    )(x, g, wg, wu, wd)


def _select_body(qi_ref, wi_ref, ve_ref, ki_ref, out_ref, keys_ref, *, tq, lb, nkb, topk, nvalid_fn):
    nv = nvalid_fn(pl.program_id(1))
    wi = wi_ref[...]
    ve = ve_ref[...]
    qs = [qi_ref[h] for h in range(H_I)]
    wcols = [wi[:, h:h + 1] for h in range(H_I)]
    lane = lax.broadcasted_iota(I32, (tq, lb), 1)

    def score_block(b, carry):
        kb = ki_ref[0, pl.ds(pl.multiple_of(b * lb, lb), lb), :]
        sc = jnp.zeros((tq, lb), F32)
        for h in range(H_I):
            sc = sc + wcols[h] * jnp.maximum(_dot_nt(qs[h], kb), 0.0)
        bits = lax.bitcast_convert_type(sc, I32)
        key = bits ^ ((bits >> 31) & INT_MAX)
        key = jnp.where(sc == 0.0, 0, key)
        key = jnp.where(lane + b * lb < ve, key, INT_MIN)
        keys_ref[b] = key
        return carry

    lax.fori_loop(0, nv, score_block, 0)

    def count_ge(mid):
        midb = jnp.broadcast_to(mid, (tq, LANES))

        def body(b, acc):
            for c in range(lb // LANES):
                k = keys_ref[b, :, c * LANES:(c + 1) * LANES]
                acc = acc + jnp.where(k >= midb, 1, 0)
            return acc

        acc = lax.fori_loop(0, nv, body, jnp.zeros((tq, LANES), I32))
        return jnp.sum(acc.astype(F32), axis=-1, keepdims=True).astype(I32)

    def bisect(_, st):
        lo, hi, clo, chi = st
        mid = (lo >> 1) + (hi >> 1) + (lo & hi & 1)
        cnt = count_ge(mid)
        active = mid != lo
        up = jnp.logical_and(active, cnt >= topk)
        dn = jnp.logical_and(active, cnt < topk)
        return (jnp.where(up, mid, lo), jnp.where(dn, mid, hi),
                jnp.where(up, cnt, clo), jnp.where(dn, cnt, chi))

    full = lambda v: jnp.full((tq, 1), v, I32)
    lo, hi, clo, chi = lax.fori_loop(0, 32, bisect, (full(INT_MIN), full(INT_MAX), ve, full(0)))

    need = jnp.where(clo > topk, topk - chi, ALL_TIES)
    need = jnp.where(lo == INT_MIN, 0, need).astype(F32)
    lob = jnp.broadcast_to(lo, (tq, LANES))
    needb = jnp.broadcast_to(need, (tq, LANES))
    r = lax.broadcasted_iota(I32, (LANES, LANES), 0)
    c = lax.broadcasted_iota(I32, (LANES, LANES), 1)
    tri = jnp.where(r <= c, 1.0, 0.0).astype(BF16)

    def mask_block(b, seen):
        for cc in range(lb // LANES):
            k = keys_ref[b, :, cc * LANES:(cc + 1) * LANES]
            eq = jnp.where(k == lob, 1.0, 0.0)
            cum = _dot(eq.astype(BF16), tri)
            rank = seen + cum - eq
            take = jnp.where(rank < needb, eq, 0.0)
            sel = jnp.where(k > lob, 1.0, take)
            out_ref[0, b, :, cc * LANES:(cc + 1) * LANES] = jnp.where(sel > 0.5, 0.0, NEG_INF).astype(out_ref.dtype)
            seen = seen + cum[:, LANES - 1:LANES]
        return seen

    lax.fori_loop(0, nv, mask_block, jnp.zeros((tq, 1), F32))

    def fill_block(b, carry):
        out_ref[0, b] = jnp.full((tq, lb), NEG_INF, out_ref.dtype)
        return carry

    lax.fori_loop(nv, nkb, fill_block, 0)


def _select(qi, wi, ve, ki, nbatch, t, tq, topk, nvalid_fn):
    lp = ki.shape[1]
    lb = KEY_BLOCK
    nkb = lp // lb
    nq = t // tq
    row = lambda b, i: b * nq + i
    return pl.pallas_call(
        functools.partial(_select_body, tq=tq, lb=lb, nkb=nkb, topk=topk, nvalid_fn=nvalid_fn),
        grid=(nbatch, nq),
        in_specs=[pl.BlockSpec((H_I, tq, D_I), lambda b, i: (0, row(b, i), 0)),
                  pl.BlockSpec((tq, H_I), lambda b, i: (row(b, i), 0)),
                  pl.BlockSpec((tq, 1), lambda b, i: (row(b, i), 0)),
                  pl.BlockSpec((1, lp, D_I), lambda b, i: (b, 0, 0))],
        out_specs=pl.BlockSpec((1, nkb, tq, lb), lambda b, i: (b, 0, i, 0)),
        out_shape=jax.ShapeDtypeStruct((nbatch, nkb, t, lb), BF16),
        scratch_shapes=[pltpu.VMEM((nkb, tq, lb), I32)],
        compiler_params=_cparams("parallel", "arbitrary"),
        name="index_select",
    )(qi, wi, ve, ki)


def _attn_body(qc_ref, qd_ref, kc_ref, vc_ref, kd_ref, vd_ref, mask_ref, bt_ref, lam_ref, g_ref,
               o_ref, m_ref, l_ref, accc_ref, accd_ref, *, tq, nvalid_fn, lam_init):
    i = pl.program_id(1)
    j = pl.program_id(2)

    @pl.when(j == 0)
    def _():
        m_ref[...] = jnp.full(m_ref.shape, NEG_INF, F32)
        l_ref[...] = jnp.zeros_like(l_ref)
        accc_ref[...] = jnp.zeros_like(accc_ref)
        accd_ref[...] = jnp.zeros_like(accd_ref)

    @pl.when(j < nvalid_fn(i))
    def _():
        sel = mask_ref[0, 0].astype(F32)
        for mp in range(N_MAPS):
            sparse = mp < H_C
            if sparse:
                s = _dot_nt(qc_ref[mp], kc_ref[0, mp]) + bt_ref[0, mp] + sel
                v = vc_ref[0, mp]
                acc_ref, a_idx, dv = accc_ref, mp, DH
            else:
                dm = mp - H_C
                s = _dot_nt(qd_ref[dm], kd_ref[0, dm]) + bt_ref[0, H_C + dm // 2]
                v = vd_ref[0, dm // 2]
                acc_ref, a_idx, dv = accd_ref, dm, 2 * DH
            m_prev = m_ref[mp]
            m_new = jnp.maximum(m_prev, jnp.max(s, axis=-1, keepdims=True))
            alpha = jnp.exp(m_prev - m_new)
            p = jnp.exp(s - m_new[:, 0:1])
            l_ref[mp] = alpha * l_ref[mp] + jnp.sum(p, axis=-1, keepdims=True)
            m_ref[mp] = m_new
            acc_ref[a_idx] = alpha[:, 0:dv] * acc_ref[a_idx] + _dot(p.astype(BF16), v)

    @pl.when(j == pl.num_programs(2) - 1)
    def _():
        for h in range(H_C):
            o_ref[:, h * DH:(h + 1) * DH] = (accc_ref[h] / l_ref[h][:, 0:DH]).astype(o_ref.dtype)
        lam = lam_ref[...]
        g = g_ref[...]
        for h in range(H_D):
            a0 = accd_ref[2 * h] / l_ref[H_C + 2 * h]
            a1 = accd_ref[2 * h + 1] / l_ref[H_C + 2 * h + 1]
            od = _rms(a0 - lam * a1, g) * (1.0 - lam_init)
            o_ref[:, W_C + h * 2 * DH:W_C + (h + 1) * 2 * DH] = od.astype(o_ref.dtype)


def _attend(qc, qd, kc, vc, kd, vd, mask, btiles, lam, g, nbatch, t, tq, nvalid_fn, tile_fn, lam_init):
    lp = kc.shape[2]
    lb = KEY_BLOCK
    nkb = lp // lb
    nq = t // tq
    row = lambda b, i: b * nq + i
    kblk = lambda i, j: jnp.minimum(j, nvalid_fn(i) - 1)
    qspec = pl.BlockSpec((H_C, tq, DH), lambda b, i, j: (0, row(b, i), 0))
    kspec = lambda nh, w: pl.BlockSpec((1, nh, lb, w), lambda b, i, j: (b, 0, kblk(i, j), 0))
    return pl.pallas_call(
        functools.partial(_attn_body, tq=tq, nvalid_fn=nvalid_fn, lam_init=lam_init),
        grid=(nbatch, nq, nkb),
        in_specs=[qspec, qspec, kspec(H_C, DH), kspec(H_C, DH), kspec(2 * H_D, DH), kspec(H_D, 2 * DH),
                  pl.BlockSpec((1, 1, tq, lb), lambda b, i, j: (b, kblk(i, j), i, 0)),
                  pl.BlockSpec((1, N_HEADS_BIAS, tq, lb), lambda b, i, j: (tile_fn(i, kblk(i, j)), 0, 0, 0)),
                  pl.BlockSpec((1, 1), lambda b, i, j: (0, 0)),
                  pl.BlockSpec((1, 2 * DH), lambda b, i, j: (0, 0))],
        out_specs=pl.BlockSpec((tq, W_C + W_D), lambda b, i, j: (row(b, i), 0)),
        out_shape=jax.ShapeDtypeStruct((nbatch * t, W_C + W_D), BF16),
        scratch_shapes=[pltpu.VMEM((N_MAPS, tq, LANES), F32), pltpu.VMEM((N_MAPS, tq, LANES), F32),
                        pltpu.VMEM((H_C, tq, DH), F32), pltpu.VMEM((2 * H_D, tq, 2 * DH), F32)],
        compiler_params=_cparams("parallel", "parallel", "arbitrary"),
        name="attend",
    )(qc, qd, kc, vc, kd, vd, mask, btiles, lam, g)


PROMPT_TQ = 256
ATTN_TQ = 512
PROMPT_LB = 512
SEL_ROWS = 256
TIE_ROWS = 128
MAPS_PER_DOT = 4
PV_ROWS = 256
ONES_ROWS = 16


def _proj1_t_body(x_ref, g_ref, w_ref, wt_ref, vd32_ref, ki32_ref, kc_ref, kd_ref, ki_ref,
                  qct_ref, qdt_ref, qit_ref, vct_ref, vdt_ref, vc32t_ref, kc32t_ref, kd32t_ref, wit_ref):
    xn32 = _rms(x_ref[...], g_ref[...])
    xn = xn32.astype(BF16)
    xnt = xn32.T.astype(BF16)
    kc_ref[...] = _dot(xn, w_ref[:, 0:512]).astype(BF16)
    kd_ref[...] = _dot(xn, w_ref[:, 512:1024]).astype(BF16)
    vd32_ref[...] = _dot(xn, w_ref[:, 1024:1536])
    z = _dot(xn, w_ref[:, 1536:1536 + LANES])
    ki32_ref[...] = z[:, 0:D_I]
    ki_ref[...] = z[:, 0:D_I].astype(BF16)
    for c, (head_ref, full_ref) in enumerate(((qct_ref, None), (qdt_ref, None), (qit_ref, None), (vct_ref, vc32t_ref),
                                              (vdt_ref, None), (None, kc32t_ref), (None, kd32t_ref))):
        zt = _dot(wt_ref[c * 512:(c + 1) * 512, :], xnt)
        if full_ref is not None:
            full_ref[...] = zt
        if head_ref is not None:
            nh = head_ref.shape[0]
            w = 512 // nh
            for h in range(nh):
                head_ref[h, 0:w, :] = zt[h * w:(h + 1) * w, :].astype(BF16)
                if head_ref.shape[1] > w:
                    pad = lax.broadcasted_iota(I32, (head_ref.shape[1] - w, zt.shape[1]), 0)
                    head_ref[h, w:, :] = jnp.where(pad == 0, 1.0, 0.0).astype(BF16)
    zt = _dot(wt_ref[3584:3584 + LANES, :], xnt)
    wit_ref[...] = zt[0:H_I, :]


def _proj1_t(x, g, w, wt, rows):
    n, d = x.shape
    full = lambda width: (jax.ShapeDtypeStruct((n, width), F32), pl.BlockSpec((rows, width), lambda i: (i, 0)))
    packed = lambda width: (jax.ShapeDtypeStruct((n, width), BF16), pl.BlockSpec((rows, width), lambda i: (i, 0)))
    heads_t = lambda nh, width: (jax.ShapeDtypeStruct((nh, width, n), BF16),
                                 pl.BlockSpec((nh, width, rows), lambda i: (0, 0, i)))
    full_t = lambda: (jax.ShapeDtypeStruct((512, n), F32), pl.BlockSpec((512, rows), lambda i: (0, i)))
    outs = [full(512), full(D_I), packed(512), packed(512), packed(D_I),
            heads_t(8, DH), heads_t(8, DH), heads_t(8, DH), heads_t(8, DH + ONES_ROWS), heads_t(H_D, 2 * DH + ONES_ROWS),
            full_t(), full_t(), full_t(),
            (jax.ShapeDtypeStruct((H_I, n), F32), pl.BlockSpec((H_I, rows), lambda i: (0, i)))]
    return pl.pallas_call(
        _proj1_t_body,
        grid=(n // rows,),
        in_specs=[pl.BlockSpec((rows, d), lambda i: (i, 0)),
                  pl.BlockSpec((1, d), lambda i: (0, 0)),
                  pl.BlockSpec(w.shape, lambda i: (0, 0)),
                  pl.BlockSpec(wt.shape, lambda i: (0, 0))],
        out_specs=[o[1] for o in outs],
        out_shape=[o[0] for o in outs],
        compiler_params=_cparams("parallel"),
        name="proj1_t",
    )(x, g, w, wt)


def _sublane_all(x8, op):
    for shift in (4, 2, 1):
        x8 = op(x8, pltpu.roll(x8, shift, 0))
    return x8


def _select_t_body(qit_ref, wit_ref, ve_ref, ki_ref, out_ref, keys_ref, gmax_ref, *, tq, lp, topk):
    rb = SEL_ROWS
    nv = (pl.program_id(0) + 1) * (tq // rb)
    wit = wit_ref[...]
    ve = ve_ref[...]
    rows = lax.broadcasted_iota(I32, (rb, tq), 0)
    gmax_ref[...] = jnp.full((rb, tq), INT_MIN, I32)
    q_all = jnp.concatenate([qit_ref[h] for h in range(H_I)], axis=1)

    def score_chunk(c, carry):
        r0 = pl.multiple_of(c * rb, rb)
        dots = _dot(ki_ref[pl.ds(r0, rb), :], q_all)
        sc = jnp.zeros((rb, tq), F32)
        for h in range(H_I):
            sc = sc + wit[h:h + 1, :] * jnp.maximum(dots[:, h * tq:(h + 1) * tq], 0.0)
        bits = lax.bitcast_convert_type(sc, I32)
        key = bits ^ ((bits >> 31) & INT_MAX)
        key = jnp.where(sc == 0.0, 0, key)
        key = jnp.where(rows + r0 < ve, key, INT_MIN)
        keys_ref[pl.ds(r0, rb), :] = key
        gmax_ref[...] = jnp.maximum(gmax_ref[...], key)
        return carry

    lax.fori_loop(0, nv, score_chunk, 0)

    def count_ge(mid):
        midb = jnp.broadcast_to(mid, (SUBLANES, tq))

        def body(c, accs):
            kc = keys_ref[pl.ds(pl.multiple_of(c * rb, rb), rb), :]
            accs = list(accs)
            for g in range(rb // SUBLANES):
                a = g % len(accs)
                accs[a] = accs[a] + jnp.where(kc[g * SUBLANES:(g + 1) * SUBLANES, :] >= midb, 1, 0)
            return tuple(accs)

        accs = lax.fori_loop(0, nv, body, (jnp.zeros((SUBLANES, tq), I32),) * 4)
        return _sublane_all(accs[0] + accs[1] + accs[2] + accs[3], jnp.add)[0:1, :]

    floor_avg = lambda a, b: (a >> 1) + (b >> 1) + (a & b & 1)

    def pending(lo, hi, clo):
        open_ = jnp.logical_and(floor_avg(lo, hi) != lo, clo > topk)
        return jnp.max(jnp.where(open_, 1.0, 0.0))

    def bisect(st):
        lo, hi, clo, chi, _ = st
        mid = floor_avg(lo, hi)
        cnt = count_ge(mid)
        active = jnp.logical_and(mid != lo, clo > topk)
        up = jnp.logical_and(active, cnt >= topk)
        dn = jnp.logical_and(active, cnt < topk)
        lo, hi = jnp.where(up, mid, lo), jnp.where(dn, mid, hi)
        clo, chi = jnp.where(up, cnt, clo), jnp.where(dn, cnt, chi)
        return lo, hi, clo, chi, pending(lo, hi, clo)

    g8 = gmax_ref[0:SUBLANES, :]
    h8 = g8
    for g in range(1, rb // SUBLANES):
        blk = gmax_ref[g * SUBLANES:(g + 1) * SUBLANES, :]
        g8 = jnp.minimum(g8, blk)
        h8 = jnp.maximum(h8, blk)
    lo0 = _sublane_all(g8, jnp.minimum)[0:1, :]
    hi0 = _sublane_all(h8, jnp.maximum)[0:1, :] + 1
    clo0 = jnp.where(lo0 == INT_MIN, ve, count_ge(lo0))
    chi0 = jnp.zeros((1, tq), I32)

    def probe_zero():
        c_pos, c_nonneg = count_ge(jnp.full((1, tq), 1, I32)), count_ge(jnp.zeros((1, tq), I32))
        above = c_pos >= topk
        below = c_nonneg < topk
        lo1 = jnp.where(above, jnp.maximum(lo0, 1), jnp.where(below, lo0, 0))
        clo1 = jnp.where(above, jnp.where(lo0 >= 1, clo0, c_pos), jnp.where(below, clo0, c_nonneg))
        hi1 = jnp.where(above, hi0, jnp.where(below, jnp.minimum(hi0, 0), 1))
        chi1 = jnp.where(above, chi0, jnp.where(below, jnp.where(hi0 <= 0, chi0, c_nonneg), c_pos))
        return lo1, hi1, clo1, chi1

    lo1, hi1, clo1, chi1 = lax.cond(jnp.max(jnp.where(lo0 < 1, 1.0, 0.0)) > 0.5, probe_zero,
                                    lambda: (lo0, hi0, clo0, chi0))
    lo, hi, clo, chi, _ = lax.while_loop(
        lambda st: st[4] > 0.5, bisect, (lo1, hi1, clo1, chi1, pending(lo1, hi1, clo1)))

    need = jnp.where(clo > topk, topk - chi, ALL_TIES)
    need = jnp.where(lo == INT_MIN, 0, need).astype(F32)
    tr = TIE_ROWS
    lob = jnp.broadcast_to(lo, (tr, tq))
    needb = jnp.broadcast_to(need, (tr, tq))
    r = lax.broadcasted_iota(I32, (tr, tr), 0)
    c = lax.broadcasted_iota(I32, (tr, tr), 1)
    tri = jnp.where(c <= r, 1.0, 0.0).astype(BF16)

    def mask_chunk(cidx, seen):
        r0 = pl.multiple_of(cidx * tr, tr)
        k = keys_ref[pl.ds(r0, tr), :]
        eq = jnp.where(k == lob, 1.0, 0.0)
        cum = _dot(tri, eq.astype(BF16))
        take = jnp.where(seen + cum - eq < needb, eq, 0.0)
        sel = jnp.where(k > lob, 1.0, take)
        out_ref[pl.ds(r0, tr), :] = jnp.where(sel > 0.5, 0.0, NEG_INF).astype(out_ref.dtype)
        return seen + cum[tr - 1:tr, :]

    def plain_chunk(cidx, carry):
        r0 = pl.multiple_of(cidx * rb, rb)
        k = keys_ref[pl.ds(r0, rb), :]
        hit = jnp.where(k == INT_MIN, NEG_INF, 0.0)
        out_ref[pl.ds(r0, rb), :] = jnp.where(k >= lo, hit, NEG_INF).astype(out_ref.dtype)
        return carry

    any_tie = jnp.max(jnp.where(clo > topk, 1.0, 0.0)) > 0.5
    lax.cond(any_tie,
             lambda: lax.fori_loop(0, nv * (rb // tr), mask_chunk, jnp.zeros((1, tq), F32)),
             lambda: lax.fori_loop(0, nv, plain_chunk, jnp.zeros((1, tq), F32)))

    def fill_chunk(cidx, carry):
        out_ref[pl.ds(pl.multiple_of(cidx * rb, rb), rb), :] = jnp.full((rb, tq), NEG_INF, out_ref.dtype)
        return carry

    lax.fori_loop(nv, lp // rb, fill_chunk, 0)


def _select_t(qit, wit, ve, ki, topk):
    t = qit.shape[2]
    tq = PROMPT_TQ
    return pl.pallas_call(
        functools.partial(_select_t_body, tq=tq, lp=t, topk=topk),
        grid=(t // tq,),
        in_specs=[pl.BlockSpec((H_I, D_I, tq), lambda i: (0, 0, i)),
                  pl.BlockSpec((H_I, tq), lambda i: (0, i)),
                  pl.BlockSpec((1, tq), lambda i: (0, i)),
                  pl.BlockSpec((t, D_I), lambda i: (0, 0))],
        out_specs=pl.BlockSpec((t, tq), lambda i: (0, i)),
        out_shape=jax.ShapeDtypeStruct((t, t), BF16),
        scratch_shapes=[pltpu.VMEM((t, tq), I32), pltpu.VMEM((SEL_ROWS, tq), I32)],
        compiler_params=_cparams("parallel"),
        name="index_select_t",
    )(qit, wit, ve, ki)


def _attn_t_body(qi_ref, kj_ref, qct_ref, qdt_ref, kc_ref, kd_ref, vct_ref, vdt_ref, mask_ref, bt_ref, lam_ref,
                 g_ref, o_ref, m_ref, accc_ref, accd_ref, ot_ref, qbd_ref, s_ref, *, tq, per, e_far,
                 lam_init):
    i = qi_ref[pl.program_id(0)]
    j = kj_ref[pl.program_id(0)]
    e = i - j * per
    grp = MAPS_PER_DOT

    @pl.when(j == 0)
    def _():
        m_ref[...] = jnp.full(m_ref.shape, NEG_INF, F32)
        accc_ref[...] = jnp.zeros_like(accc_ref)
        accd_ref[...] = jnp.zeros_like(accd_ref)
        qbd_ref[...] = jnp.zeros_like(qbd_ref)
        for mp in range(N_MAPS):
            q = qct_ref[mp] if mp < H_C else qdt_ref[mp - H_C]
            a = mp % grp
            qbd_ref[mp // grp, a * DH:(a + 1) * DH, a * tq:(a + 1) * tq] = q

    def logits(g):
        k_ref = kc_ref if g < H_C // grp else kd_ref
        half = g % (H_C // grp)
        return _dot(k_ref[:, half * grp * DH:(half + 1) * grp * DH], qbd_ref[g])

    def step(near):
        sel = mask_ref[...].astype(F32)
        sel = jnp.concatenate([sel] * grp, axis=1)
        for g in range(N_MAPS // grp):
            s_ref[g] = logits(g) + sel if g < H_C // grp else logits(g)
        for mp in range(N_MAPS):
            g, a = mp // grp, mp % grp
            cols = slice(a * tq, (a + 1) * tq)
            if mp < H_C:
                bias_idx, vt, acc_ref, a_idx = mp, vct_ref[mp], accc_ref, mp
            else:
                dm = mp - H_C
                bias_idx, vt, acc_ref, a_idx = H_C + dm // 2, vdt_ref[dm // 2], accd_ref, dm

            def biased(rows):
                s = s_ref[g, rows, cols]
                if near:
                    s = s + bt_ref[0, bias_idx, rows, :].astype(F32)
                return s

            m_prev = m_ref[mp]
            m_new = jnp.maximum(m_prev, jnp.max(biased(slice(None)), axis=0, keepdims=True))
            alpha = jnp.exp2(m_prev - m_new)
            m_ref[mp] = m_new
            pv = None
            for r0 in range(0, s_ref.shape[1], PV_ROWS):
                rows = slice(r0, r0 + PV_ROWS)
                part = _dot(vt[:, rows], jnp.exp2(biased(rows) - m_new).astype(BF16))
                pv = part if pv is None else pv + part
            acc_ref[a_idx] = alpha * acc_ref[a_idx] + pv

    pl.when(e < e_far)(functools.partial(step, True))
    pl.when(e >= e_far)(functools.partial(step, False))

    @pl.when(j == i // per)
    def _():
        for h in range(H_C):
            ot_ref[h * DH:(h + 1) * DH, :] = accc_ref[h, 0:DH, :] / accc_ref[h, DH:DH + 1, :]
        lam = lam_ref[...]
        g = g_ref[...]
        dv = 2 * DH
        for h in range(H_D):
            a0 = accd_ref[2 * h, 0:dv, :] / accd_ref[2 * h, dv:dv + 1, :]
            a1 = accd_ref[2 * h + 1, 0:dv, :] / accd_ref[2 * h + 1, dv:dv + 1, :]
            od = a0 - lam * a1
            od = od * lax.rsqrt(jnp.mean(od * od, axis=0, keepdims=True) + EPS) * g * (1.0 - lam_init)
            ot_ref[W_C + h * 2 * DH:W_C + (h + 1) * 2 * DH, :] = od
        o_ref[...] = ot_ref[...].T.astype(o_ref.dtype)


def _attend_t(qct, qdt, kc, kd, vct, vdt, mask, btiles, lam, g, lam_init):
    t = qct.shape[2]
    tq, lb = ATTN_TQ, PROMPT_LB
    per = lb // tq
    e_far = btiles.shape[0]
    pairs = [(i, j) for i in range(t // tq) for j in range(i // per + 1)]
    qi_tab = jnp.asarray(np.array([p[0] for p in pairs], np.int32))
    kj_tab = jnp.asarray(np.array([p[1] for p in pairs], np.int32))
    qspec = pl.BlockSpec((H_C, DH, tq), lambda s, qi, kj: (0, 0, qi[s]))
    kspec = pl.BlockSpec((lb, H_C * DH), lambda s, qi, kj: (kj[s], 0))
    grid_spec = pltpu.PrefetchScalarGridSpec(
        num_scalar_prefetch=2,
        grid=(len(pairs),),
        in_specs=[qspec, qspec, kspec, kspec,
                  pl.BlockSpec((H_C, DH + ONES_ROWS, lb), lambda s, qi, kj: (0, 0, kj[s])),
                  pl.BlockSpec((H_D, 2 * DH + ONES_ROWS, lb), lambda s, qi, kj: (0, 0, kj[s])),
                  pl.BlockSpec((lb, tq), lambda s, qi, kj: (kj[s], qi[s])),
                  pl.BlockSpec((1, N_HEADS_BIAS, lb, tq),
                               lambda s, qi, kj: (jnp.minimum(qi[s] - kj[s] * per, e_far - 1), 0, 0, 0)),
                  pl.BlockSpec((1, 1), lambda s, qi, kj: (0, 0)),
                  pl.BlockSpec((2 * DH, 1), lambda s, qi, kj: (0, 0))],
        out_specs=pl.BlockSpec((tq, W_C + W_D), lambda s, qi, kj: (qi[s], 0)),
        scratch_shapes=[pltpu.VMEM((N_MAPS, 1, tq), F32),
                        pltpu.VMEM((H_C, DH + ONES_ROWS, tq), F32),
                        pltpu.VMEM((2 * H_D, 2 * DH + ONES_ROWS, tq), F32),
                        pltpu.VMEM((W_C + W_D, tq), F32),
                        pltpu.VMEM((N_MAPS // MAPS_PER_DOT, MAPS_PER_DOT * DH, MAPS_PER_DOT * tq), BF16),
                        pltpu.VMEM((N_MAPS // MAPS_PER_DOT, lb, MAPS_PER_DOT * tq), F32)])
    return pl.pallas_call(
        functools.partial(_attn_t_body, tq=tq, per=per, e_far=e_far, lam_init=lam_init),
        grid_spec=grid_spec,
        out_shape=jax.ShapeDtypeStruct((t, W_C + W_D), BF16),
        compiler_params=_cparams("arbitrary"),
        name="attend_t",
    )(qi_tab, kj_tab, qct, qdt, kc, kd, vct, vdt, mask, btiles, lam, g)


def _attn_s_body(qc_ref, qd_ref, ck_ref, cv_ref, dk_ref, dv_ref, nck_ref, ncv_ref, ndk_ref, ndv_ref,
                 mask_ref, bias_ref, lam_ref, g_ref, o_ref, m_ref, l_ref, acc_ref, *, ts, ncache, lam_init):
    j = pl.program_id(1)
    rows = H_C * ts

    @pl.when(j == 0)
    def _():
        m_ref[...] = jnp.full(m_ref.shape, NEG_INF, F32)
        l_ref[...] = jnp.zeros_like(l_ref)
        acc_ref[...] = jnp.zeros_like(acc_ref)

    def step(kc_t, vc_t, kd_t, vd):
        sel = jnp.tile(mask_ref[0, 0].astype(F32), (H_C, 1))
        bias = bias_ref[0]
        s_c = _dot(qc_ref[0], kc_t.astype(BF16)) + sel + bias[0:rows]
        s_d = _dot(qd_ref[0], kd_t.astype(BF16)) + bias[rows:2 * rows]
        for idx, (s, v) in enumerate(((s_c, vc_t), (s_d, vd))):
            m_prev = m_ref[idx]
            m_new = jnp.maximum(m_prev, jnp.max(s, axis=-1, keepdims=True))
            alpha = jnp.exp(m_prev - m_new)
            p = jnp.exp(s - m_new)
            l_ref[idx] = alpha * l_ref[idx] + jnp.sum(p, axis=-1, keepdims=True)
            m_ref[idx] = m_new
            p = p.astype(BF16)
            pv = _dot_nt(p, v.astype(BF16)) if idx == 0 else _dot(p, v.astype(BF16))
            acc_ref[idx] = alpha * acc_ref[idx] + pv

    @pl.when(j < ncache)
    def _():
        step(ck_ref[0], cv_ref[0], dk_ref[0], dv_ref[0])

    @pl.when(j == ncache)
    def _():
        step(nck_ref[0], ncv_ref[0], ndk_ref[0], ndv_ref[0])
        for h in range(H_C):
            r = slice(h * ts, (h + 1) * ts)
            o_ref[:, h * DH:(h + 1) * DH] = (acc_ref[0, r, h * DH:(h + 1) * DH] / l_ref[0, r, :]).astype(o_ref.dtype)
        lam = lam_ref[...]
        g = g_ref[...]
        for h in range(H_D):
            r0 = slice(2 * h * ts, (2 * h + 1) * ts)
            r1 = slice((2 * h + 1) * ts, (2 * h + 2) * ts)
            c = slice(h * 2 * DH, (h + 1) * 2 * DH)
            od = acc_ref[1, r0, c] / l_ref[1, r0, :] - lam * (acc_ref[1, r1, c] / l_ref[1, r1, :])
            od = _rms(od, g) * (1.0 - lam_init)
            o_ref[:, W_C + h * 2 * DH:W_C + (h + 1) * 2 * DH] = od.astype(o_ref.dtype)


def _attend_s(qbd_c, qbd_d, caches, news, mask, bias, lam, g, lam_init):
    nb, rows, _ = qbd_c.shape
    ts = rows // H_C
    lb = KEY_BLOCK
    ncache = caches[3].shape[1] // lb
    qspec = pl.BlockSpec((1, rows, W_C), lambda b, j: (b, 0, 0))
    ctspec = pl.BlockSpec((1, W_C, lb), lambda b, j: (b, 0, jnp.minimum(j, ncache - 1)))
    cspec = pl.BlockSpec((1, lb, W_C), lambda b, j: (b, jnp.minimum(j, ncache - 1), 0))
    ntspec = pl.BlockSpec((1, W_C, lb), lambda b, j: (b, 0, 0))
    nspec = pl.BlockSpec((1, lb, W_C), lambda b, j: (b, 0, 0))
    return pl.pallas_call(
        functools.partial(_attn_s_body, ts=ts, ncache=ncache, lam_init=lam_init),
        grid=(nb, ncache + 1),
        in_specs=[qspec, qspec, ctspec, ctspec, ctspec, cspec, ntspec, ntspec, ntspec, nspec,
                  pl.BlockSpec((1, 1, ts, lb), lambda b, j: (b, j, 0, 0)),
                  pl.BlockSpec((1, 2 * rows, lb), lambda b, j: (j, 0, 0)),
                  pl.BlockSpec((1, 1), lambda b, j: (0, 0)),
                  pl.BlockSpec((1, 2 * DH), lambda b, j: (0, 0))],
        out_specs=pl.BlockSpec((ts, W_C + W_D), lambda b, j: (b, 0)),
        out_shape=jax.ShapeDtypeStruct((nb * ts, W_C + W_D), BF16),
        scratch_shapes=[pltpu.VMEM((2, rows, 1), F32), pltpu.VMEM((2, rows, 1), F32),
                        pltpu.VMEM((2, rows, W_C), F32)],
        compiler_params=_cparams("parallel", "arbitrary"),
        name="attend_s",
    )(qbd_c, qbd_d, *caches, *news, mask, bias, lam, g)


def _out_router_body(x_ref, o_ref, w_ref, g_ref, r_ref, x3_ref, xn_ref, gate_ref, gatet_ref):
    x3 = x_ref[...] + _dot(o_ref[...], w_ref[...])
    x3_ref[...] = x3
    xn = _rms(x3, g_ref[...]).astype(BF16)
    xn_ref[...] = xn
    logits = _dot(xn, r_ref[...])
    lane = lax.broadcasted_iota(I32, logits.shape, 1)
    logits = jnp.where(lane < N_EXP, logits, -jnp.inf)
    m1 = jnp.max(logits, axis=-1, keepdims=True)
    i1 = jnp.min(jnp.where(logits == m1, lane, LANES), axis=-1, keepdims=True)
    rest = jnp.where(lane == i1, -jnp.inf, logits)
    m2 = jnp.max(rest, axis=-1, keepdims=True)
    i2 = jnp.min(jnp.where(rest == m2, lane, LANES), axis=-1, keepdims=True)
    e = jnp.exp(m2 - m1)
    g1 = 1.0 / (1.0 + e)
    g2 = e / (1.0 + e)
    gate = jnp.where(lane == i1, g1, 0.0) + jnp.where(lane == i2, g2, 0.0)
    gate_ref[...] = gate
    gatet_ref[...] = gate.T[0:GATE_ROWS, :]


def _out_router(x, o, w, g, router, rows):
    n, d = x.shape
    return pl.pallas_call(
        _out_router_body,
        grid=(n // rows,),
        in_specs=[pl.BlockSpec((rows, d), lambda i: (i, 0)),
                  pl.BlockSpec((rows, d), lambda i: (i, 0)),
                  pl.BlockSpec(w.shape, lambda i: (0, 0)),
                  pl.BlockSpec((1, d), lambda i: (0, 0)),
                  pl.BlockSpec(router.shape, lambda i: (0, 0))],
        out_specs=[pl.BlockSpec((rows, d), lambda i: (i, 0)),
                   pl.BlockSpec((rows, d), lambda i: (i, 0)),
                   pl.BlockSpec((rows, LANES), lambda i: (i, 0)),
                   pl.BlockSpec((GATE_ROWS, rows), lambda i: (0, i))],
        out_shape=[jax.ShapeDtypeStruct((n, d), F32), jax.ShapeDtypeStruct((n, d), BF16),
                   jax.ShapeDtypeStruct((n, LANES), F32), jax.ShapeDtypeStruct((GATE_ROWS, n), F32)],
        compiler_params=_cparams("parallel"),
        name="out_router",
    )(x, o, w, g, router)


def _moe_body(x_ref, xn_ref, gate_ref, gf_ref, wg_ref, wu_ref, wd_ref, y_ref, acc_ref):
    e = pl.program_id(1)
    k = pl.program_id(2)

    @pl.when(jnp.logical_and(e == 0, k == 0))
    def _():
        acc_ref[...] = jnp.zeros_like(acc_ref)

    gate = gate_ref[...]
    lane = lax.broadcasted_iota(I32, gate.shape, 1)
    ge = jnp.sum(jnp.where(lane == e, gate, 0.0), axis=-1, keepdims=True)
    xn = xn_ref[...]
    h = jax.nn.silu(_dot(xn, wg_ref[0])) * _dot(xn, wu_ref[0])
    acc_ref[...] += _dot((ge * h).astype(BF16), wd_ref[0])

    @pl.when(jnp.logical_and(e == pl.num_programs(1) - 1, k == pl.num_programs(2) - 1))
    def _():
        y_ref[...] = _rms(x_ref[...] + acc_ref[...], gf_ref[...])


def _moe(x, xn, gate, gf, wg, wu, wd, rows, fb):
    n, d = x.shape
    ne, _, dff = wg.shape
    return pl.pallas_call(
        _moe_body,
        grid=(n // rows, ne, dff // fb),
        in_specs=[pl.BlockSpec((rows, d), lambda i, e, k: (i, 0)),
                  pl.BlockSpec((rows, d), lambda i, e, k: (i, 0)),
                  pl.BlockSpec((rows, LANES), lambda i, e, k: (i, 0)),
                  pl.BlockSpec((1, d), lambda i, e, k: (0, 0)),
                  pl.BlockSpec((1, d, fb), lambda i, e, k: (e, 0, k)),
                  pl.BlockSpec((1, d, fb), lambda i, e, k: (e, 0, k)),
                  pl.BlockSpec((1, fb, d), lambda i, e, k: (e, k, 0))],
        out_specs=pl.BlockSpec((rows, d), lambda i, e, k: (i, 0)),
        out_shape=jax.ShapeDtypeStruct((n, d), F32),
        scratch_shapes=[pltpu.VMEM((rows, d), F32)],
        compiler_params=_cparams("parallel", "arbitrary", "arbitrary"),
        name="moe",
    )(x, xn, gate, gf, wg, wu, wd)


def _t5_bucket(rel):
    half = N_BUCKETS // 2
    max_exact = half // 2
    ret = np.where(rel > 0, half, 0)
    n = np.abs(rel)
    nf = np.maximum(n, 1).astype(np.float32)
    large = max_exact + (np.log(nf / np.float32(max_exact)) / np.float32(math.log(MAX_DIST / max_exact))
                         * np.float32(half - max_exact)).astype(np.int32)
    large = np.minimum(large, half - 1)
    return (ret + np.where(n < max_exact, n, large)).astype(np.int32)


def _bias_tile(rel_bias, q_pos, k_pos, k_real):
    rel = k_pos[None, :] - q_pos[:, None]
    onehot = (jnp.asarray(_t5_bucket(rel).astype(np.int8))[:, :, None]
              == jnp.arange(N_BUCKETS, dtype=jnp.int8)).astype(F32)
    bias = jnp.einsum("qkb,bh->hqk", onehot, rel_bias.astype(F32), precision=lax.Precision.HIGHEST)
    ok = np.logical_and(k_pos[None, :] // CHUNK <= q_pos[:, None] // CHUNK, k_pos[None, :] < k_real)
    return jnp.where(ok[None], bias, NEG_INF)


def _in1_plan():
    scale = DH ** -0.5
    heads64 = lambda o, sc=1.0: [(o, h, h * DH, (h + 1) * DH, sc) for h in range(8)]
    plan = [
        ((0, 512), heads64(0, scale)),
        ((512, 1024), [(1, None, 0, 512, 1.0)] + heads64(2)),
        ((1024, 1536), [(3, None, 0, 512, 1.0)] + heads64(4)),
        ((1536, 2048), heads64(5, scale)),
        ((2048, 2560), [(6, None, 0, 512, 1.0)] + heads64(7)),
        ((2560, 3072), [(8, None, 0, 512, 1.0)] + [(9, h, h * 128, (h + 1) * 128, 1.0) for h in range(H_D)]),
        ((3072, 3584), heads64(10)),
        ((3584, 3712), [(11, None, 0, D_I, 1.0), (12, None, 0, D_I, 1.0), (13, None, D_I, D_I + H_I, 1.0)]),
    ]
    out_defs = [(DH, BF16, 8), (512, F32, None), (DH, BF16, 8), (512, F32, None), (DH, BF16, 8),
                (DH, BF16, 8), (512, F32, None), (DH, BF16, 8), (512, F32, None), (2 * DH, BF16, H_D),
                (D_I, BF16, 8), (D_I, F32, None), (D_I, BF16, None), (H_I, F32, None)]
    return plan, out_defs


def _layer0(x, hist, rb, emit_v, p):
    n = x.shape[0]
    rows = _largest_divisor(n, 512, 16)
    plan = [((0, p["w_in0"].shape[1]), [(0, None, 0, p["w_in0"].shape[1], 1.0)])]
    (z,) = _rms_proj(x, p["ln_mix0"], p["w_in0"], plan, [(p["w_in0"].shape[1], F32, None)], rows)
    ws = p["ws_prompt"] if hist is None else p["ws_sample"]
    bs = p["bs_prompt"] if hist is None else p["bs_sample"]
    outs = _mixer_ab(x, z, hist, p["gmlp_ln_g"], p["gmlp_ln_b"], ws, bs, p["conv_k"], p["w_out0"], rb, emit_v)
    x1 = outs[0]
    x2 = _ffn(x1, p["ln_ffn0"], p["ffn_wg"], p["ffn_wu"], p["ffn_wd"], rows, p["ffn_fb"])
    return (x2,) + tuple(outs[1:])


GATE_ROWS = 16
MOE_ROWS = 128


def _moe_routed_body(x_ref, xn_ref, gate_ref, gatet_ref, gf_ref, wg_ref, wu_ref, wd_ref, y_ref,
                     triu_ref, tril_ref, crow_ref, ccol_ref, xs_ref, ge_ref, acc_ref, yblk_ref, nsub_ref,
                     *, tb, sub):
    e = pl.program_id(1)
    k = pl.program_id(2)
    first_k = k == 0
    last_k = k == pl.num_programs(2) - 1

    @pl.when(jnp.logical_and(e == 0, first_k))
    def _():
        r = lax.broadcasted_iota(I32, (tb, tb), 0)
        c = lax.broadcasted_iota(I32, (tb, tb), 1)
        triu_ref[...] = jnp.where(r < c, 1.0, 0.0).astype(BF16)
        tril_ref[...] = jnp.where(c < r, 1.0, 0.0).astype(BF16)
        crow_ref[...] = _dot(jnp.where(gatet_ref[...] > 0.0, 1.0, 0.0).astype(BF16), triu_ref[...])
        ccol_ref[...] = _dot(tril_ref[...], jnp.where(gate_ref[...] > 0.0, 1.0, 0.0).astype(BF16))
        yblk_ref[...] = jnp.zeros_like(yblk_ref)
        acc_ref[...] = jnp.zeros_like(acc_ref)

    @pl.when(first_k)
    def _():
        g_e = gatet_ref[pl.ds(e, 1), :]
        m_e = g_e > 0.0
        c_e = crow_ref[pl.ds(e, 1), :]
        nsub = (jnp.sum(jnp.where(m_e, 1.0, 0.0)).astype(I32) + sub - 1) // sub
        nsub_ref[0] = nsub
        xn = xn_ref[...]
        slot = lax.broadcasted_iota(I32, (sub, tb), 0).astype(F32)

        def pack(s, carry):
            r0 = pl.multiple_of(s * sub, sub)
            hit = jnp.logical_and(m_e, c_e == slot + (s * sub).astype(F32))
            onehot = jnp.where(hit, 1.0, 0.0)
            xs_ref[pl.ds(r0, sub), :] = _dot(onehot.astype(BF16), xn).astype(BF16)
            ge_ref[pl.ds(r0, sub), :] = jnp.sum(onehot * g_e, axis=-1, keepdims=True)
            acc_ref[pl.ds(r0, sub), :] = jnp.zeros((sub, acc_ref.shape[1]), F32)
            return carry

        lax.fori_loop(0, nsub, pack, 0)

    nsub = nsub_ref[0]

    def experts(s, carry):
        rows = pl.ds(pl.multiple_of(s * sub, sub), sub)
        xs = xs_ref[rows, :]
        h = jax.nn.silu(_dot(xs, wg_ref[0])) * _dot(xs, wu_ref[0])
        acc_ref[rows, :] += _dot((ge_ref[rows, :] * h).astype(BF16), wd_ref[0])
        return carry

    lax.fori_loop(0, nsub, experts, 0)

    @pl.when(last_k)
    def _():
        lane = lax.broadcasted_iota(I32, (tb, LANES), 1)
        pick = lambda a: jnp.sum(jnp.where(lane == e, a, 0.0), axis=-1, keepdims=True)
        m_e = pick(gate_ref[...]) > 0.0
        c_e = pick(ccol_ref[...])
        slot = lax.broadcasted_iota(I32, (tb, sub), 1).astype(F32)

        def unpack(s, carry):
            rows = pl.ds(pl.multiple_of(s * sub, sub), sub)
            hit = jnp.logical_and(m_e, c_e == slot + (s * sub).astype(F32))
            onehot = jnp.where(hit, 1.0, 0.0).astype(BF16)
            y = acc_ref[rows, :]
            hi = y.astype(BF16)
            lo = (y - hi.astype(F32)).astype(BF16)
            yblk_ref[...] += _dot(jnp.concatenate([onehot, onehot], axis=1), jnp.concatenate([hi, lo], axis=0))
            return carry

        lax.fori_loop(0, nsub, unpack, 0)

    @pl.when(jnp.logical_and(e == pl.num_programs(1) - 1, last_k))
    def _():
        y_ref[...] = _rms(x_ref[...] + yblk_ref[...], gf_ref[...])


def _moe_routed(x, xn, gate, gatet, gf, wg, wu, wd, tb, fb):
    n, d = x.shape
    ne, _, dff = wg.shape
    return pl.pallas_call(
        functools.partial(_moe_routed_body, tb=tb, sub=min(MOE_ROWS, tb)),
        grid=(n // tb, ne, dff // fb),
        in_specs=[pl.BlockSpec((tb, d), lambda i, e, k: (i, 0)),
                  pl.BlockSpec((tb, d), lambda i, e, k: (i, 0)),
                  pl.BlockSpec((tb, LANES), lambda i, e, k: (i, 0)),
                  pl.BlockSpec((GATE_ROWS, tb), lambda i, e, k: (0, i)),
                  pl.BlockSpec((1, d), lambda i, e, k: (0, 0)),
                  pl.BlockSpec((1, d, fb), lambda i, e, k: (e, 0, k)),
                  pl.BlockSpec((1, d, fb), lambda i, e, k: (e, 0, k)),
                  pl.BlockSpec((1, fb, d), lambda i, e, k: (e, k, 0))],
        out_specs=pl.BlockSpec((tb, d), lambda i, e, k: (i, 0)),
        out_shape=jax.ShapeDtypeStruct((n, d), F32),
        scratch_shapes=[pltpu.VMEM((tb, tb), BF16), pltpu.VMEM((tb, tb), BF16),
                        pltpu.VMEM((GATE_ROWS, tb), F32), pltpu.VMEM((tb, LANES), F32),
                        pltpu.VMEM((tb, d), BF16), pltpu.VMEM((tb, 1), F32),
                        pltpu.VMEM((tb, d), F32), pltpu.VMEM((tb, d), F32),
                        pltpu.SMEM((1,), I32)],
        compiler_params=_cparams("parallel", "arbitrary", "arbitrary"),
        name="moe_routed",
    )(x, xn, gate, gatet, gf, wg, wu, wd)


def _layer1_tail(x, o, p):
    n = x.shape[0]
    rows = _largest_divisor(n, 512, 16)
    x3, xn, gate, gatet = _out_router(x, o, p["w_out1"], p["ln_ffn1"], p["router"], rows)
    return _moe_routed(x3, xn, gate, gatet, p["ln_final"], p["exp_wg"], p["exp_wu"], p["exp_wd"],
                       _largest_divisor(n, 1024, 128), p["exp_fb"])


def kernel(x_prompt, x_sample, state_b_conv, cache_c_k, cache_c_v, cache_idx_k, cache_d_k, cache_d_v, rel_bias, ln_mix, ln_ffn, ln_final, w_in0, gmlp_ln_g, gmlp_ln_b, gmlp_ws, gmlp_bs, conv_k, w_out0, ffn_wg, ffn_wu, ffn_wd, w_in1, lam_qk, subln_g, w_out1, router, exp_wg, exp_wu, exp_wd):
    bp, seq, d = x_prompt.shape
    bs_, ts, _ = x_sample.shape
    past = cache_c_k.shape[2]
    assert bp == 1 and ln_mix.shape[0] == 2 and seq % PROMPT_LB == 0 and ts % SUBLANES == 0 and ts <= CHUNK
    assert past % KEY_BLOCK == 0
    gmlp_chunk = gmlp_ws.shape[-1]
    lam_init = 0.8 - 0.6 * math.exp(-0.3 * 1)

    def ws_masked(rows):
        r = jnp.arange(rows)
        ok = (r[None, :] // CHUNK) <= (r[:, None] // CHUNK)
        return jnp.where(ok[None], gmlp_ws[0][:, :rows, :rows], 0.0).astype(BF16)

    in1_pad = (-w_in1.shape[2]) % LANES
    lf = lam_qk[0].astype(F32)
    lam = (jnp.exp(jnp.sum(lf[0] * lf[1])) - jnp.exp(jnp.sum(lf[2] * lf[3])) + lam_init).reshape(1, 1)
    p = {
        "ln_mix0": ln_mix[0:1], "ln_ffn0": ln_ffn[0:1], "ln_mix1": ln_mix[1:2], "ln_ffn1": ln_ffn[1:2],
        "ln_final": ln_final.reshape(1, d),
        "w_in0": w_in0[0].astype(BF16),
        "gmlp_ln_g": gmlp_ln_g[0].reshape(1, W_A), "gmlp_ln_b": gmlp_ln_b[0].reshape(1, W_A),
        "ws_prompt": ws_masked(gmlp_chunk), "ws_sample": ws_masked(ts),
        "bs_prompt": gmlp_bs[0][:, :gmlp_chunk].T, "bs_sample": gmlp_bs[0][:, :ts].T,
        "conv_k": conv_k[0], "w_out0": w_out0[0].astype(BF16),
        "ffn_wg": ffn_wg[0].astype(BF16), "ffn_wu": ffn_wu[0].astype(BF16), "ffn_wd": ffn_wd[0].astype(BF16),
        "ffn_fb": _largest_divisor(ffn_wg.shape[2], 1408, LANES),
        "w_in1": jnp.pad(w_in1[0], ((0, 0), (0, in1_pad))).astype(BF16),
        "w_out1": w_out1[0].astype(BF16),
        "router": jnp.pad(router[0], ((0, 0), (0, LANES - N_EXP))).astype(BF16),
        "exp_wg": exp_wg[0].astype(BF16), "exp_wu": exp_wu[0].astype(BF16), "exp_wd": exp_wd[0].astype(BF16),
        "exp_fb": _largest_divisor(exp_wg.shape[3], 896, LANES),
    }
    g_sub = subln_g[0].reshape(1, 2 * DH)
    plan1, out_defs1 = _in1_plan()

    xp = x_prompt.reshape(seq, d)
    xs = x_sample.reshape(bs_ * ts, d)
    xp, p_tail = _layer0(xp, None, gmlp_chunk, False, p)
    hist = jnp.pad(state_b_conv[0], ((0, 0), (SUBLANES - 2, 0), (0, 0)))
    xs, s_tail, s_av = _layer0(xs, hist, ts, True, p)
    p_b_conv = p_tail[-1, SUBLANES - 2:, :].reshape(1, 1, 2, W_B)
    s_b_conv = s_tail[:, SUBLANES - 2:, :].reshape(1, bs_, 2, W_B)
    s_a_v = s_av.reshape(1, bs_, ts, W_A)

    lb = KEY_BLOCK
    w1 = w_in1[0]
    off = np.concatenate([[0], np.cumsum(IN1_SIZES)])
    field = lambda k: w1[:, off[k]:off[k + 1]]
    padc = lambda a: jnp.pad(a, ((0, 0), (0, LANES - a.shape[1])))
    log2e = math.log2(math.e)
    qscale = DH ** -0.5 * log2e
    w_norm = jnp.concatenate([field(1), field(4), field(5), padc(field(7))], axis=1).astype(BF16)
    w_tran = jnp.concatenate([field(0) * qscale, field(3) * qscale, field(6), field(2), field(5),
                              field(1), field(4), padc(field(8))], axis=1).T.astype(BF16)
    (vd32, ki32, kc, kd, ki, qct, qdt, qit, vct, vdt, vc32t, kc32t, kd32t, wit) = _proj1_t(
        xp, p["ln_mix1"], w_norm, w_tran, PROMPT_TQ)
    token_major = lambda a, *dims: jnp.moveaxis(a.reshape(dims + (seq,)), -1, 0).reshape((1, 1, seq) + dims)
    p_c_k, p_c_v = token_major(kc32t, H_C, DH), token_major(vc32t, H_C, DH)
    p_d_k = token_major(kd32t, H_D, 2, DH)
    pos = jnp.arange(seq, dtype=I32)
    ve = ((pos // CHUNK + 1) * CHUNK).reshape(1, seq)
    mask = _select_t(qit, wit, ve, ki, min(TOPK_MAX, seq // 4))
    e_far = -(-(PROMPT_LB - 1 + MAX_DIST) // ATTN_TQ)
    kpos = np.arange(PROMPT_LB)
    qpos = np.arange(ATTN_TQ)
    base = e_far * ATTN_TQ
    rel_near = (rel_bias - rel_bias[N_BUCKETS // 2 - 1:N_BUCKETS // 2]) * log2e
    btiles = jnp.stack([jnp.transpose(_bias_tile(rel_near, base + e * ATTN_TQ + qpos, base + kpos, base + PROMPT_LB),
                                      (0, 2, 1)) for e in range(e_far)]).astype(BF16)
    op = _attend_t(qct, qdt, kc, kd, vct, vdt, mask, btiles, lam, subln_g[0].reshape(2 * DH, 1), lam_init)
    y_prompt = _layer1_tail(xp, op, p).reshape(1, seq, d)

    sr = _rms_proj(xs, p["ln_mix1"], p["w_in1"], plan1, out_defs1, _largest_divisor(bs_ * ts, 256, 16))
    sqc, skc32, skc, svc32, svc, sqd, skd32, skd, svd32, svd, sqi, ski32, ski, swi = sr
    nk = past + ts
    lps = -(-nk // lb) * lb

    kis = jnp.pad(jnp.concatenate([cache_idx_k[0].astype(BF16), ski.reshape(bs_, ts, D_I)], axis=1),
                  ((0, 0), (0, lps - nk), (0, 0)))
    ves = jnp.full((bs_ * ts, 1), nk, I32)
    smask = _select(sqi, swi, ves, kis, bs_, ts, ts, min(TOPK_MAX, nk // 4), lambda i: lps // lb)
    sq_pos = past + np.arange(ts)
    sbt = jnp.stack([_bias_tile(rel_bias, sq_pos, j * lb + np.arange(lb), nk) for j in range(lps // lb)])
    sbias = jnp.concatenate([sbt[:, :H_C], jnp.repeat(sbt[:, H_C:], 2, axis=1)], axis=1).reshape(lps // lb, N_MAPS * ts, lb)

    def block_diag(q):
        qb = jnp.transpose(q.reshape(H_C, bs_, ts, DH), (1, 0, 2, 3))
        eye = jnp.eye(H_C, dtype=q.dtype)
        return (qb[:, :, :, None, :] * eye[None, :, None, :, None]).reshape(bs_, H_C * ts, W_C)

    def cache_rows(c, feature_major):
        c = c.reshape(bs_, past, W_C)
        return jnp.transpose(c, (0, 2, 1)) if feature_major else c

    def new_rows(a, feature_major):
        a = jnp.pad(a.reshape(bs_, ts, W_C), ((0, 0), (0, lb - ts), (0, 0)))
        return jnp.transpose(a, (0, 2, 1)) if feature_major else a
    os_ = _attend_s(block_diag(sqc), block_diag(sqd),
                    [cache_rows(c[0], fm) for c, fm in ((cache_c_k, True), (cache_c_v, True), (cache_d_k, True), (cache_d_v, False))],
                    [new_rows(a, fm) for a, fm in ((skc32, True), (svc32, True), (skd32, True), (svd32, False))],
                    smask, sbias, lam, g_sub, lam_init)
    y_sample = _layer1_tail(xs, os_, p).reshape(bs_, ts, d)

    r5 = lambda a, n, t, *tail: a.reshape((1, n, t) + tail)
    return (y_prompt, y_sample, p_b_conv,
            p_c_k, p_c_v, r5(ki32, 1, seq, D_I), p_d_k, r5(vd32, 1, seq, H_D, 2 * DH),
            s_a_v, s_b_conv,
            r5(skc32, bs_, ts, H_C, DH), r5(svc32, bs_, ts, H_C, DH), r5(ski32, bs_, ts, D_I),
            r5(skd32, bs_, ts, H_D, 2, DH), r5(svd32, bs_, ts, H_D, 2 * DH))
```

```python
import functools
import math

import jax
import jax.numpy as jnp
import numpy as np
from jax import lax
from jax.experimental import pallas as pl
from jax.experimental.pallas import tpu as pltpu

F32 = jnp.float32
BF16 = jnp.bfloat16
I32 = jnp.int32

CHUNK = 64
EPS = 1e-6
NEG_INF = -1e30
H_A = 4
C_A = 128
W_A = H_A * C_A
W_B = 512
H_C = 8
DH = 64
W_C = H_C * DH
H_I = 8
D_I = 64
TOPK_MAX = 256
H_D = 4
W_D = H_D * 2 * DH
N_BUCKETS = 32
MAX_DIST = 128
N_EXP = 8
IN1_SIZES = (W_C, W_C, W_C, W_D, W_D, W_D, H_I * D_I, D_I, H_I)
N_HEADS_BIAS = H_C + H_D
N_MAPS = H_C + 2 * H_D

LANES = 128
SUBLANES = 8
VMEM_LIMIT = 56 * 1024 * 1024

INT_MIN = -(2 ** 31)
INT_MAX = 2 ** 31 - 1
ALL_TIES = 2 ** 30

KEY_BLOCK = 512


def _cparams(*sem):
    return pltpu.CompilerParams(dimension_semantics=sem, vmem_limit_bytes=VMEM_LIMIT)


def _largest_divisor(n, target, mult):
    if n <= target:
        return n
    d = (target // mult) * mult
    while d >= mult:
        if n % d == 0:
            return d
        d -= mult
    raise ValueError(f"no block of multiple {mult} divides {n}")


def _rms(x, g):
    return x * lax.rsqrt(jnp.mean(x * x, axis=-1, keepdims=True) + EPS) * g


def _dot(a, b):
    return jnp.dot(a, b, preferred_element_type=F32)


def _dot_nt(a, b):
    return lax.dot_general(a, b, (((1,), (1,)), ((), ())), preferred_element_type=F32)


def _rms_proj_body(x_ref, g_ref, w_ref, *out_refs, plan):
    xn = _rms(x_ref[...], g_ref[...]).astype(BF16)
    for (c0, c1), writes in plan:
        z = _dot(xn, w_ref[:, c0:c1])
        for o_idx, head, z0, z1, scale in writes:
            val = z[:, z0:z1]
            if scale != 1.0:
                val = val * scale
            ref = out_refs[o_idx]
            if head is None:
                ref[...] = val.astype(ref.dtype)
            else:
                ref[head] = val.astype(ref.dtype)


def _rms_proj(x, g, w, plan, out_defs, rows):
    n, d = x.shape
    grid = (n // rows,)
    out_shape, out_specs = [], []
    for width, dtype, heads in out_defs:
        if heads is None:
            out_shape.append(jax.ShapeDtypeStruct((n, width), dtype))
            out_specs.append(pl.BlockSpec((rows, width), lambda i: (i, 0)))
        else:
            out_shape.append(jax.ShapeDtypeStruct((heads, n, width), dtype))
            out_specs.append(pl.BlockSpec((heads, rows, width), lambda i: (0, i, 0)))
    return pl.pallas_call(
        functools.partial(_rms_proj_body, plan=plan),
        grid=grid,
        in_specs=[pl.BlockSpec((rows, d), lambda i: (i, 0)),
                  pl.BlockSpec((1, d), lambda i: (0, 0)),
                  pl.BlockSpec(w.shape, lambda i: (0, 0))],
        out_specs=out_specs,
        out_shape=out_shape,
        compiler_params=_cparams("parallel"),
        name="rms_proj",
    )(x, g, w)


def _mixer_ab_body(*refs, rb, from_prev, emit_v):
    it = iter(refs)
    x_ref, u_ref, v_ref, gb_ref, gc_ref, xin_ref = (next(it) for _ in range(6))
    if from_prev:
        gcp_ref, xinp_ref = next(it), next(it)
    else:
        hist_ref = next(it)
    lng_ref, lnb_ref, ws_ref, bs_ref, ck_ref, wout_ref = (next(it) for _ in range(6))
    x1_ref, tail_ref = next(it), next(it)
    vout_ref = next(it) if emit_v else None
    wext_ref = next(it)

    w = gc_ref[...] * xin_ref[...]
    if from_prev:
        hist = jnp.where(pl.program_id(0) > 0, gcp_ref[...] * xinp_ref[...], 0.0)
    else:
        hist = hist_ref[0]
    wext_ref[0:SUBLANES, :] = hist
    wext_ref[SUBLANES:, :] = w
    ck = ck_ref[...]
    conv = (ck[0:1] * wext_ref[SUBLANES - 2:SUBLANES - 2 + rb, :]
            + ck[1:2] * wext_ref[SUBLANES - 1:SUBLANES - 1 + rb, :]
            + ck[2:3] * w)
    y_b = gb_ref[...] * conv
    tail_ref[0] = w[rb - SUBLANES:, :]

    u = jax.nn.gelu(u_ref[...])
    v = jax.nn.gelu(v_ref[...])
    lng = lng_ref[...]
    lnb = lnb_ref[...]
    bs = bs_ref[...]
    acc = _dot(y_b.astype(BF16), wout_ref[W_A:, :])
    for h in range(H_A):
        sl = slice(h * C_A, (h + 1) * C_A)
        vh = v[:, sl]
        mu = jnp.mean(vh, axis=-1, keepdims=True)
        xc = vh - mu
        var = jnp.mean(xc * xc, axis=-1, keepdims=True)
        vln = xc * lax.rsqrt(var + EPS) * lng[:, sl] + lnb[:, sl]
        if emit_v:
            vout_ref[:, sl] = vln
        s = _dot(ws_ref[h], vln.astype(BF16)) + bs[:, h:h + 1]
        y_a = u[:, sl] * s
        acc = acc + _dot(y_a.astype(BF16), wout_ref[sl, :])
    x1_ref[...] = x_ref[...] + acc


def _mixer_ab(x, z, hist, lng, lnb, ws, bs, ck, wout, rb, emit_v):
    n, d = x.shape
    nb = n // rb
    from_prev = hist is None
    col = lambda c: pl.BlockSpec((rb, 512), lambda i, c=c: (i, c))
    in_specs = [pl.BlockSpec((rb, d), lambda i: (i, 0)), col(0), col(1), col(2), col(3), col(4)]
    args = [x, z, z, z, z, z]
    if from_prev:
        per = rb // SUBLANES
        prev = lambda c: pl.BlockSpec((SUBLANES, 512), lambda i, c=c: (jnp.maximum(i * per - 1, 0), c))
        in_specs += [prev(3), prev(4)]
        args += [z, z]
    else:
        in_specs += [pl.BlockSpec((1, SUBLANES, 512), lambda i: (i, 0, 0))]
        args += [hist]
    const = lambda a: pl.BlockSpec(a.shape, lambda i, nd=a.ndim: (0,) * nd)
    for a in (lng, lnb, ws, bs, ck, wout):
        in_specs.append(const(a))
        args.append(a)
    out_shape = [jax.ShapeDtypeStruct((n, d), F32), jax.ShapeDtypeStruct((nb, SUBLANES, 512), F32)]
    out_specs = [pl.BlockSpec((rb, d), lambda i: (i, 0)), pl.BlockSpec((1, SUBLANES, 512), lambda i: (i, 0, 0))]
    if emit_v:
        out_shape.append(jax.ShapeDtypeStruct((n, W_A), F32))
        out_specs.append(pl.BlockSpec((rb, W_A), lambda i: (i, 0)))
    return pl.pallas_call(
        functools.partial(_mixer_ab_body, rb=rb, from_prev=from_prev, emit_v=emit_v),
        grid=(nb,),
        in_specs=in_specs,
        out_specs=out_specs,
        out_shape=out_shape,
        scratch_shapes=[pltpu.VMEM((rb + SUBLANES, 512), F32)],
        compiler_params=_cparams("arbitrary"),
        name="mixer_ab",
    )(*args)


def _ffn_body(x_ref, g_ref, wg_ref, wu_ref, wd_ref, o_ref, xn_ref, acc_ref):
    k = pl.program_id(1)

    @pl.when(k == 0)
    def _():
        xn_ref[...] = _rms(x_ref[...], g_ref[...]).astype(BF16)
        acc_ref[...] = jnp.zeros_like(acc_ref)

    xn = xn_ref[...]
    h = jax.nn.silu(_dot(xn, wg_ref[...])) * _dot(xn, wu_ref[...])
    acc_ref[...] += _dot(h.astype(BF16), wd_ref[...])

    @pl.when(k == pl.num_programs(1) - 1)
    def _():
        o_ref[...] = x_ref[...] + acc_ref[...]


def _ffn(x, g, wg, wu, wd, rows, fb):
    n, d = x.shape
    dff = wg.shape[1]
    return pl.pallas_call(
        _ffn_body,
        grid=(n // rows, dff // fb),
        in_specs=[pl.BlockSpec((rows, d), lambda i, k: (i, 0)),
                  pl.BlockSpec((1, d), lambda i, k: (0, 0)),
                  pl.BlockSpec((d, fb), lambda i, k: (0, k)),
                  pl.BlockSpec((d, fb), lambda i, k: (0, k)),
                  pl.BlockSpec((fb, d), lambda i, k: (k, 0))],
        out_specs=pl.BlockSpec((rows, d), lambda i, k: (i, 0)),
        out_shape=jax.ShapeDtypeStruct((n, d), F32),
        scratch_shapes=[pltpu.VMEM((rows, d), BF16), pltpu.VMEM((rows, d), F32)],
        compiler_params=_cparams("parallel", "arbitrary"),
        name="ffn",
    )(x, g, wg, wu, wd)


def _select_body(qi_ref, wi_ref, ve_ref, ki_ref, out_ref, keys_ref, *, tq, lb, nkb, topk, nvalid_fn):
    nv = nvalid_fn(pl.program_id(1))
    wi = wi_ref[...]
    ve = ve_ref[...]
    qs = [qi_ref[h] for h in range(H_I)]
    wcols = [wi[:, h:h + 1] for h in range(H_I)]
    lane = lax.broadcasted_iota(I32, (tq, lb), 1)

    def score_block(b, carry):
        kb = ki_ref[0, pl.ds(pl.multiple_of(b * lb, lb), lb), :]
        sc = jnp.zeros((tq, lb), F32)
        for h in range(H_I):
            sc = sc + wcols[h] * jnp.maximum(_dot_nt(qs[h], kb), 0.0)
        bits = lax.bitcast_convert_type(sc, I32)
        key = bits ^ ((bits >> 31) & INT_MAX)
        key = jnp.where(sc == 0.0, 0, key)
        key = jnp.where(lane + b * lb < ve, key, INT_MIN)
        keys_ref[b] = key
        return carry

    lax.fori_loop(0, nv, score_block, 0)

    def count_ge(mid):
        midb = jnp.broadcast_to(mid, (tq, LANES))

        def body(b, acc):
            for c in range(lb // LANES):
                k = keys_ref[b, :, c * LANES:(c + 1) * LANES]
                acc = acc + jnp.where(k >= midb, 1, 0)
            return acc

        acc = lax.fori_loop(0, nv, body, jnp.zeros((tq, LANES), I32))
        return jnp.sum(acc.astype(F32), axis=-1, keepdims=True).astype(I32)

    def bisect(_, st):
        lo, hi, clo, chi = st
        mid = (lo >> 1) + (hi >> 1) + (lo & hi & 1)
        cnt = count_ge(mid)
        active = mid != lo
        up = jnp.logical_and(active, cnt >= topk)
        dn = jnp.logical_and(active, cnt < topk)
        return (jnp.where(up, mid, lo), jnp.where(dn, mid, hi),
                jnp.where(up, cnt, clo), jnp.where(dn, cnt, chi))

    full = lambda v: jnp.full((tq, 1), v, I32)
    lo, hi, clo, chi = lax.fori_loop(0, 32, bisect, (full(INT_MIN), full(INT_MAX), ve, full(0)))

    need = jnp.where(clo > topk, topk - chi, ALL_TIES)
    need = jnp.where(lo == INT_MIN, 0, need).astype(F32)
    lob = jnp.broadcast_to(lo, (tq, LANES))
    needb = jnp.broadcast_to(need, (tq, LANES))
    r = lax.broadcasted_iota(I32, (LANES, LANES), 0)
    c = lax.broadcasted_iota(I32, (LANES, LANES), 1)
    tri = jnp.where(r <= c, 1.0, 0.0).astype(BF16)

    def mask_block(b, seen):
        for cc in range(lb // LANES):
            k = keys_ref[b, :, cc * LANES:(cc + 1) * LANES]
            eq = jnp.where(k == lob, 1.0, 0.0)
            cum = _dot(eq.astype(BF16), tri)
            rank = seen + cum - eq
            take = jnp.where(rank < needb, eq, 0.0)
            sel = jnp.where(k > lob, 1.0, take)
            out_ref[0, b, :, cc * LANES:(cc + 1) * LANES] = jnp.where(sel > 0.5, 0.0, NEG_INF).astype(out_ref.dtype)
            seen = seen + cum[:, LANES - 1:LANES]
        return seen

    lax.fori_loop(0, nv, mask_block, jnp.zeros((tq, 1), F32))

    def fill_block(b, carry):
        out_ref[0, b] = jnp.full((tq, lb), NEG_INF, out_ref.dtype)
        return carry

    lax.fori_loop(nv, nkb, fill_block, 0)


def _select(qi, wi, ve, ki, nbatch, t, tq, topk, nvalid_fn):
    lp = ki.shape[1]
    lb = KEY_BLOCK
    nkb = lp // lb
    nq = t // tq
    row = lambda b, i: b * nq + i
    return pl.pallas_call(
        functools.partial(_select_body, tq=tq, lb=lb, nkb=nkb, topk=topk, nvalid_fn=nvalid_fn),
        grid=(nbatch, nq),
        in_specs=[pl.BlockSpec((H_I, tq, D_I), lambda b, i: (0, row(b, i), 0)),
                  pl.BlockSpec((tq, H_I), lambda b, i: (row(b, i), 0)),
                  pl.BlockSpec((tq, 1), lambda b, i: (row(b, i), 0)),
                  pl.BlockSpec((1, lp, D_I), lambda b, i: (b, 0, 0))],
        out_specs=pl.BlockSpec((1, nkb, tq, lb), lambda b, i: (b, 0, i, 0)),
        out_shape=jax.ShapeDtypeStruct((nbatch, nkb, t, lb), BF16),
        scratch_shapes=[pltpu.VMEM((nkb, tq, lb), I32)],
        compiler_params=_cparams("parallel", "arbitrary"),
        name="index_select",
    )(qi, wi, ve, ki)


PROMPT_TQ = 256
ATTN_TQ = 512
PROMPT_LB = 512
SEL_ROWS = 256
TIE_ROWS = 128
MAPS_PER_DOT = 4
PV_ROWS = 256
ONES_ROWS = 16


def _proj1_t_body(x_ref, g_ref, w_ref, wt_ref, vd32_ref, ki32_ref, kc_ref, kd_ref, ki_ref,
                  qct_ref, qdt_ref, qit_ref, vct_ref, vdt_ref, vc32t_ref, kc32t_ref, kd32t_ref, wit_ref):
    xn32 = _rms(x_ref[...], g_ref[...])
    xn = xn32.astype(BF16)
    xnt = xn32.T.astype(BF16)
    kc_ref[...] = _dot(xn, w_ref[:, 0:512]).astype(BF16)
    kd_ref[...] = _dot(xn, w_ref[:, 512:1024]).astype(BF16)
    vd32_ref[...] = _dot(xn, w_ref[:, 1024:1536])
    z = _dot(xn, w_ref[:, 1536:1536 + LANES])
    ki32_ref[...] = z[:, 0:D_I]
    ki_ref[...] = z[:, 0:D_I].astype(BF16)
    for c, (head_ref, full_ref) in enumerate(((qct_ref, None), (qdt_ref, None), (qit_ref, None), (vct_ref, vc32t_ref),
                                              (vdt_ref, None), (None, kc32t_ref), (None, kd32t_ref))):
        zt = _dot(wt_ref[c * 512:(c + 1) * 512, :], xnt)
        if full_ref is not None:
            full_ref[...] = zt
        if head_ref is not None:
            nh = head_ref.shape[0]
            w = 512 // nh
            for h in range(nh):
                head_ref[h, 0:w, :] = zt[h * w:(h + 1) * w, :].astype(BF16)
                if head_ref.shape[1] > w:
                    pad = lax.broadcasted_iota(I32, (head_ref.shape[1] - w, zt.shape[1]), 0)
                    head_ref[h, w:, :] = jnp.where(pad == 0, 1.0, 0.0).astype(BF16)
    zt = _dot(wt_ref[3584:3584 + LANES, :], xnt)
    wit_ref[...] = zt[0:H_I, :]


def _proj1_t(x, g, w, wt, rows):
    n, d = x.shape
    full = lambda width: (jax.ShapeDtypeStruct((n, width), F32), pl.BlockSpec((rows, width), lambda i: (i, 0)))
    packed = lambda width: (jax.ShapeDtypeStruct((n, width), BF16), pl.BlockSpec((rows, width), lambda i: (i, 0)))
    heads_t = lambda nh, width: (jax.ShapeDtypeStruct((nh, width, n), BF16),
                                 pl.BlockSpec((nh, width, rows), lambda i: (0, 0, i)))
    full_t = lambda: (jax.ShapeDtypeStruct((512, n), F32), pl.BlockSpec((512, rows), lambda i: (0, i)))
    outs = [full(512), full(D_I), packed(512), packed(512), packed(D_I),
            heads_t(8, DH), heads_t(8, DH), heads_t(8, DH), heads_t(8, DH + ONES_ROWS), heads_t(H_D, 2 * DH + ONES_ROWS),
            full_t(), full_t(), full_t(),
            (jax.ShapeDtypeStruct((H_I, n), F32), pl.BlockSpec((H_I, rows), lambda i: (0, i)))]
    return pl.pallas_call(
        _proj1_t_body,
        grid=(n // rows,),
        in_specs=[pl.BlockSpec((rows, d), lambda i: (i, 0)),
                  pl.BlockSpec((1, d), lambda i: (0, 0)),
                  pl.BlockSpec(w.shape, lambda i: (0, 0)),
                  pl.BlockSpec(wt.shape, lambda i: (0, 0))],
        out_specs=[o[1] for o in outs],
        out_shape=[o[0] for o in outs],
        compiler_params=_cparams("parallel"),
        name="proj1_t",
    )(x, g, w, wt)


def _sublane_all(x8, op):
    for shift in (4, 2, 1):
        x8 = op(x8, pltpu.roll(x8, shift, 0))
    return x8


def _select_t_body(qit_ref, wit_ref, ve_ref, ki_ref, out_ref, keys_ref, gmax_ref, *, tq, lp, topk):
    rb = SEL_ROWS
    nv = (pl.program_id(0) + 1) * (tq // rb)
    wit = wit_ref[...]
    ve = ve_ref[...]
    rows = lax.broadcasted_iota(I32, (rb, tq), 0)
    gmax_ref[...] = jnp.full((rb, tq), INT_MIN, I32)
    q_all = jnp.concatenate([qit_ref[h] for h in range(H_I)], axis=1)

    def score_chunk(c, carry):
        r0 = pl.multiple_of(c * rb, rb)
        dots = _dot(ki_ref[pl.ds(r0, rb), :], q_all)
        sc = jnp.zeros((rb, tq), F32)
        for h in range(H_I):
            sc = sc + wit[h:h + 1, :] * jnp.maximum(dots[:, h * tq:(h + 1) * tq], 0.0)
        bits = lax.bitcast_convert_type(sc, I32)
        key = bits ^ ((bits >> 31) & INT_MAX)
        key = jnp.where(sc == 0.0, 0, key)
        key = jnp.where(rows + r0 < ve, key, INT_MIN)
        keys_ref[pl.ds(r0, rb), :] = key
        gmax_ref[...] = jnp.maximum(gmax_ref[...], key)
        return carry

    lax.fori_loop(0, nv, score_chunk, 0)

    def count_ge(mid):
        midb = jnp.broadcast_to(mid, (SUBLANES, tq))

        def body(c, accs):
            kc = keys_ref[pl.ds(pl.multiple_of(c * rb, rb), rb), :]
            accs = list(accs)
            for g in range(rb // SUBLANES):
                a = g % len(accs)
                accs[a] = accs[a] + jnp.where(kc[g * SUBLANES:(g + 1) * SUBLANES, :] >= midb, 1, 0)
            return tuple(accs)

        accs = lax.fori_loop(0, nv, body, (jnp.zeros((SUBLANES, tq), I32),) * 4)
        return _sublane_all(accs[0] + accs[1] + accs[2] + accs[3], jnp.add)[0:1, :]

    floor_avg = lambda a, b: (a >> 1) + (b >> 1) + (a & b & 1)

    def pending(lo, hi, clo):
        open_ = jnp.logical_and(floor_avg(lo, hi) != lo, clo > topk)
        return jnp.max(jnp.where(open_, 1.0, 0.0))

    def bisect(st):
        lo, hi, clo, chi, _ = st
        mid = floor_avg(lo, hi)
        cnt = count_ge(mid)
        active = jnp.logical_and(mid != lo, clo > topk)
        up = jnp.logical_and(active, cnt >= topk)
        dn = jnp.logical_and(active, cnt < topk)
        lo, hi = jnp.where(up, mid, lo), jnp.where(dn, mid, hi)
        clo, chi = jnp.where(up, cnt, clo), jnp.where(dn, cnt, chi)
        return lo, hi, clo, chi, pending(lo, hi, clo)

    g8 = gmax_ref[0:SUBLANES, :]
    h8 = g8
    for g in range(1, rb // SUBLANES):
        blk = gmax_ref[g * SUBLANES:(g + 1) * SUBLANES, :]
        g8 = jnp.minimum(g8, blk)
        h8 = jnp.maximum(h8, blk)
    lo0 = _sublane_all(g8, jnp.minimum)[0:1, :]
    hi0 = _sublane_all(h8, jnp.maximum)[0:1, :] + 1
    clo0 = jnp.where(lo0 == INT_MIN, ve, count_ge(lo0))
    chi0 = jnp.zeros((1, tq), I32)

    def probe_zero():
        c_pos, c_nonneg = count_ge(jnp.full((1, tq), 1, I32)), count_ge(jnp.zeros((1, tq), I32))
        above = c_pos >= topk
        below = c_nonneg < topk
        lo1 = jnp.where(above, jnp.maximum(lo0, 1), jnp.where(below, lo0, 0))
        clo1 = jnp.where(above, jnp.where(lo0 >= 1, clo0, c_pos), jnp.where(below, clo0, c_nonneg))
        hi1 = jnp.where(above, hi0, jnp.where(below, jnp.minimum(hi0, 0), 1))
        chi1 = jnp.where(above, chi0, jnp.where(below, jnp.where(hi0 <= 0, chi0, c_nonneg), c_pos))
        return lo1, hi1, clo1, chi1

    lo1, hi1, clo1, chi1 = lax.cond(jnp.max(jnp.where(lo0 < 1, 1.0, 0.0)) > 0.5, probe_zero,
                                    lambda: (lo0, hi0, clo0, chi0))
    lo, hi, clo, chi, _ = lax.while_loop(
        lambda st: st[4] > 0.5, bisect, (lo1, hi1, clo1, chi1, pending(lo1, hi1, clo1)))

    need = jnp.where(clo > topk, topk - chi, ALL_TIES)
    need = jnp.where(lo == INT_MIN, 0, need).astype(F32)
    tr = TIE_ROWS
    lob = jnp.broadcast_to(lo, (tr, tq))
    needb = jnp.broadcast_to(need, (tr, tq))
    r = lax.broadcasted_iota(I32, (tr, tr), 0)
    c = lax.broadcasted_iota(I32, (tr, tr), 1)
    tri = jnp.where(c <= r, 1.0, 0.0).astype(BF16)

    def mask_chunk(cidx, seen):
        r0 = pl.multiple_of(cidx * tr, tr)
        k = keys_ref[pl.ds(r0, tr), :]
        eq = jnp.where(k == lob, 1.0, 0.0)
        cum = _dot(tri, eq.astype(BF16))
        take = jnp.where(seen + cum - eq < needb, eq, 0.0)
        sel = jnp.where(k > lob, 1.0, take)
        out_ref[pl.ds(r0, tr), :] = jnp.where(sel > 0.5, 0.0, NEG_INF).astype(out_ref.dtype)
        return seen + cum[tr - 1:tr, :]

    def plain_chunk(cidx, carry):
        r0 = pl.multiple_of(cidx * rb, rb)
        k = keys_ref[pl.ds(r0, rb), :]
        hit = jnp.where(k == INT_MIN, NEG_INF, 0.0)
        out_ref[pl.ds(r0, rb), :] = jnp.where(k >= lo, hit, NEG_INF).astype(out_ref.dtype)
        return carry

    any_tie = jnp.max(jnp.where(clo > topk, 1.0, 0.0)) > 0.5
    lax.cond(any_tie,
             lambda: lax.fori_loop(0, nv * (rb // tr), mask_chunk, jnp.zeros((1, tq), F32)),
             lambda: lax.fori_loop(0, nv, plain_chunk, jnp.zeros((1, tq), F32)))

    def fill_chunk(cidx, carry):
        out_ref[pl.ds(pl.multiple_of(cidx * rb, rb), rb), :] = jnp.full((rb, tq), NEG_INF, out_ref.dtype)
        return carry

    lax.fori_loop(nv, lp // rb, fill_chunk, 0)


def _select_t(qit, wit, ve, ki, topk):
    t = qit.shape[2]
    tq = PROMPT_TQ
    return pl.pallas_call(
        functools.partial(_select_t_body, tq=tq, lp=t, topk=topk),
        grid=(t // tq,),
        in_specs=[pl.BlockSpec((H_I, D_I, tq), lambda i: (0, 0, i)),
                  pl.BlockSpec((H_I, tq), lambda i: (0, i)),
                  pl.BlockSpec((1, tq), lambda i: (0, i)),
                  pl.BlockSpec((t, D_I), lambda i: (0, 0))],
        out_specs=pl.BlockSpec((t, tq), lambda i: (0, i)),
        out_shape=jax.ShapeDtypeStruct((t, t), BF16),
        scratch_shapes=[pltpu.VMEM((t, tq), I32), pltpu.VMEM((SEL_ROWS, tq), I32)],
        compiler_params=_cparams("parallel"),
        name="index_select_t",
    )(qit, wit, ve, ki)


def _attn_t_body(qi_ref, kj_ref, qct_ref, qdt_ref, kc_ref, kd_ref, vct_ref, vdt_ref, mask_ref, bt_ref, lam_ref,
                 g_ref, o_ref, m_ref, accc_ref, accd_ref, ot_ref, qbd_ref, s_ref, *, tq, per, e_far,
                 lam_init):
    i = qi_ref[pl.program_id(0)]
    j = kj_ref[pl.program_id(0)]
    e = i - j * per
    grp = MAPS_PER_DOT

    @pl.when(j == 0)
    def _():
        m_ref[...] = jnp.full(m_ref.shape, NEG_INF, F32)
        accc_ref[...] = jnp.zeros_like(accc_ref)
        accd_ref[...] = jnp.zeros_like(accd_ref)
        qbd_ref[...] = jnp.zeros_like(qbd_ref)
        for mp in range(N_MAPS):
            q = qct_ref[mp] if mp < H_C else qdt_ref[mp - H_C]
            a = mp % grp
            qbd_ref[mp // grp, a * DH:(a + 1) * DH, a * tq:(a + 1) * tq] = q

    def logits(g):
        k_ref = kc_ref if g < H_C // grp else kd_ref
        half = g % (H_C // grp)
        return _dot(k_ref[:, half * grp * DH:(half + 1) * grp * DH], qbd_ref[g])

    def step(near):
        sel = mask_ref[...].astype(F32)
        sel = jnp.concatenate([sel] * grp, axis=1)
        for g in range(N_MAPS // grp):
            s_ref[g] = logits(g) + sel if g < H_C // grp else logits(g)
        for mp in range(N_MAPS):
            g, a = mp // grp, mp % grp
            cols = slice(a * tq, (a + 1) * tq)
            if mp < H_C:
                bias_idx, vt, acc_ref, a_idx = mp, vct_ref[mp], accc_ref, mp
            else:
                dm = mp - H_C
                bias_idx, vt, acc_ref, a_idx = H_C + dm // 2, vdt_ref[dm // 2], accd_ref, dm

            def biased(rows):
                s = s_ref[g, rows, cols]
                if near:
                    s = s + bt_ref[0, bias_idx, rows, :].astype(F32)
                return s

            m_prev = m_ref[mp]
            m_new = jnp.maximum(m_prev, jnp.max(biased(slice(None)), axis=0, keepdims=True))
            alpha = jnp.exp2(m_prev - m_new)
            m_ref[mp] = m_new
            pv = None
            for r0 in range(0, s_ref.shape[1], PV_ROWS):
                rows = slice(r0, r0 + PV_ROWS)
                part = _dot(vt[:, rows], jnp.exp2(biased(rows) - m_new).astype(BF16))
                pv = part if pv is None else pv + part
            acc_ref[a_idx] = alpha * acc_ref[a_idx] + pv

    pl.when(e < e_far)(functools.partial(step, True))
    pl.when(e >= e_far)(functools.partial(step, False))

    @pl.when(j == i // per)
    def _():
        for h in range(H_C):
            ot_ref[h * DH:(h + 1) * DH, :] = accc_ref[h, 0:DH, :] / accc_ref[h, DH:DH + 1, :]
        lam = lam_ref[...]
        g = g_ref[...]
        dv = 2 * DH
        for h in range(H_D):
            a0 = accd_ref[2 * h, 0:dv, :] / accd_ref[2 * h, dv:dv + 1, :]
            a1 = accd_ref[2 * h + 1, 0:dv, :] / accd_ref[2 * h + 1, dv:dv + 1, :]
            od = a0 - lam * a1
            od = od * lax.rsqrt(jnp.mean(od * od, axis=0, keepdims=True) + EPS) * g * (1.0 - lam_init)
            ot_ref[W_C + h * 2 * DH:W_C + (h + 1) * 2 * DH, :] = od
        o_ref[...] = ot_ref[...].T.astype(o_ref.dtype)


def _attend_t(qct, qdt, kc, kd, vct, vdt, mask, btiles, lam, g, lam_init):
    t = qct.shape[2]
    tq, lb = ATTN_TQ, PROMPT_LB
    per = lb // tq
    e_far = btiles.shape[0]
    pairs = [(i, j) for i in range(t // tq) for j in range(i // per + 1)]
    qi_tab = jnp.asarray(np.array([p[0] for p in pairs], np.int32))
    kj_tab = jnp.asarray(np.array([p[1] for p in pairs], np.int32))
    qspec = pl.BlockSpec((H_C, DH, tq), lambda s, qi, kj: (0, 0, qi[s]))
    kspec = pl.BlockSpec((lb, H_C * DH), lambda s, qi, kj: (kj[s], 0))
    grid_spec = pltpu.PrefetchScalarGridSpec(
        num_scalar_prefetch=2,
        grid=(len(pairs),),
        in_specs=[qspec, qspec, kspec, kspec,
                  pl.BlockSpec((H_C, DH + ONES_ROWS, lb), lambda s, qi, kj: (0, 0, kj[s])),
                  pl.BlockSpec((H_D, 2 * DH + ONES_ROWS, lb), lambda s, qi, kj: (0, 0, kj[s])),
                  pl.BlockSpec((lb, tq), lambda s, qi, kj: (kj[s], qi[s])),
                  pl.BlockSpec((1, N_HEADS_BIAS, lb, tq),
                               lambda s, qi, kj: (jnp.minimum(qi[s] - kj[s] * per, e_far - 1), 0, 0, 0)),
                  pl.BlockSpec((1, 1), lambda s, qi, kj: (0, 0)),
                  pl.BlockSpec((2 * DH, 1), lambda s, qi, kj: (0, 0))],
        out_specs=pl.BlockSpec((tq, W_C + W_D), lambda s, qi, kj: (qi[s], 0)),
        scratch_shapes=[pltpu.VMEM((N_MAPS, 1, tq), F32),
                        pltpu.VMEM((H_C, DH + ONES_ROWS, tq), F32),
                        pltpu.VMEM((2 * H_D, 2 * DH + ONES_ROWS, tq), F32),
                        pltpu.VMEM((W_C + W_D, tq), F32),
                        pltpu.VMEM((N_MAPS // MAPS_PER_DOT, MAPS_PER_DOT * DH, MAPS_PER_DOT * tq), BF16),
                        pltpu.VMEM((N_MAPS // MAPS_PER_DOT, lb, MAPS_PER_DOT * tq), F32)])
    return pl.pallas_call(
        functools.partial(_attn_t_body, tq=tq, per=per, e_far=e_far, lam_init=lam_init),
        grid_spec=grid_spec,
        out_shape=jax.ShapeDtypeStruct((t, W_C + W_D), BF16),
        compiler_params=_cparams("arbitrary"),
        name="attend_t",
    )(qi_tab, kj_tab, qct, qdt, kc, kd, vct, vdt, mask, btiles, lam, g)


def _attn_s_body(qc_ref, qd_ref, ck_ref, cv_ref, dk_ref, dv_ref, nck_ref, ncv_ref, ndk_ref, ndv_ref,
                 mask_ref, bias_ref, lam_ref, g_ref, o_ref, m_ref, l_ref, acc_ref, *, ts, ncache, lam_init):
    j = pl.program_id(1)
    rows = H_C * ts

    @pl.when(j == 0)
    def _():
        m_ref[...] = jnp.full(m_ref.shape, NEG_INF, F32)
        l_ref[...] = jnp.zeros_like(l_ref)
        acc_ref[...] = jnp.zeros_like(acc_ref)

    def step(kc_t, vc_t, kd_t, vd):
        sel = jnp.tile(mask_ref[0, 0].astype(F32), (H_C, 1))
        bias = bias_ref[0]
        s_c = _dot(qc_ref[0], kc_t.astype(BF16)) + sel + bias[0:rows]
        s_d = _dot(qd_ref[0], kd_t.astype(BF16)) + bias[rows:2 * rows]
        for idx, (s, v) in enumerate(((s_c, vc_t), (s_d, vd))):
            m_prev = m_ref[idx]
            m_new = jnp.maximum(m_prev, jnp.max(s, axis=-1, keepdims=True))
            alpha = jnp.exp(m_prev - m_new)
            p = jnp.exp(s - m_new)
            l_ref[idx] = alpha * l_ref[idx] + jnp.sum(p, axis=-1, keepdims=True)
            m_ref[idx] = m_new
            p = p.astype(BF16)
            pv = _dot_nt(p, v.astype(BF16)) if idx == 0 else _dot(p, v.astype(BF16))
            acc_ref[idx] = alpha * acc_ref[idx] + pv

    @pl.when(j < ncache)
    def _():
        step(ck_ref[0], cv_ref[0], dk_ref[0], dv_ref[0])

    @pl.when(j == ncache)
    def _():
        step(nck_ref[0], ncv_ref[0], ndk_ref[0], ndv_ref[0])
        for h in range(H_C):
            r = slice(h * ts, (h + 1) * ts)
            o_ref[:, h * DH:(h + 1) * DH] = (acc_ref[0, r, h * DH:(h + 1) * DH] / l_ref[0, r, :]).astype(o_ref.dtype)
        lam = lam_ref[...]
        g = g_ref[...]
        for h in range(H_D):
            r0 = slice(2 * h * ts, (2 * h + 1) * ts)
            r1 = slice((2 * h + 1) * ts, (2 * h + 2) * ts)
            c = slice(h * 2 * DH, (h + 1) * 2 * DH)
            od = acc_ref[1, r0, c] / l_ref[1, r0, :] - lam * (acc_ref[1, r1, c] / l_ref[1, r1, :])
            od = _rms(od, g) * (1.0 - lam_init)
            o_ref[:, W_C + h * 2 * DH:W_C + (h + 1) * 2 * DH] = od.astype(o_ref.dtype)


def _attend_s(qbd_c, qbd_d, caches, news, mask, bias, lam, g, lam_init):
    nb, rows, _ = qbd_c.shape
    ts = rows // H_C
    lb = KEY_BLOCK
    ncache = caches[3].shape[1] // lb
    qspec = pl.BlockSpec((1, rows, W_C), lambda b, j: (b, 0, 0))
    ctspec = pl.BlockSpec((1, W_C, lb), lambda b, j: (b, 0, jnp.minimum(j, ncache - 1)))
    cspec = pl.BlockSpec((1, lb, W_C), lambda b, j: (b, jnp.minimum(j, ncache - 1), 0))
    ntspec = pl.BlockSpec((1, W_C, lb), lambda b, j: (b, 0, 0))
    nspec = pl.BlockSpec((1, lb, W_C), lambda b, j: (b, 0, 0))
    return pl.pallas_call(
        functools.partial(_attn_s_body, ts=ts, ncache=ncache, lam_init=lam_init),
        grid=(nb, ncache + 1),
        in_specs=[qspec, qspec, ctspec, ctspec, ctspec, cspec, ntspec, ntspec, ntspec, nspec,
                  pl.BlockSpec((1, 1, ts, lb), lambda b, j: (b, j, 0, 0)),
                  pl.BlockSpec((1, 2 * rows, lb), lambda b, j: (j, 0, 0)),
                  pl.BlockSpec((1, 1), lambda b, j: (0, 0)),
                  pl.BlockSpec((1, 2 * DH), lambda b, j: (0, 0))],
        out_specs=pl.BlockSpec((ts, W_C + W_D), lambda b, j: (b, 0)),
        out_shape=jax.ShapeDtypeStruct((nb * ts, W_C + W_D), BF16),
        scratch_shapes=[pltpu.VMEM((2, rows, 1), F32), pltpu.VMEM((2, rows, 1), F32),
                        pltpu.VMEM((2, rows, W_C), F32)],
        compiler_params=_cparams("parallel", "arbitrary"),
        name="attend_s",
    )(qbd_c, qbd_d, *caches, *news, mask, bias, lam, g)


def _out_router_body(x_ref, o_ref, w_ref, g_ref, r_ref, x3_ref, xn_ref, gate_ref, gatet_ref):
    x3 = x_ref[...] + _dot(o_ref[...], w_ref[...])
    x3_ref[...] = x3
    xn = _rms(x3, g_ref[...]).astype(BF16)
    xn_ref[...] = xn
    logits = _dot(xn, r_ref[...])
    lane = lax.broadcasted_iota(I32, logits.shape, 1)
    logits = jnp.where(lane < N_EXP, logits, -jnp.inf)
    m1 = jnp.max(logits, axis=-1, keepdims=True)
    i1 = jnp.min(jnp.where(logits == m1, lane, LANES), axis=-1, keepdims=True)
    rest = jnp.where(lane == i1, -jnp.inf, logits)
    m2 = jnp.max(rest, axis=-1, keepdims=True)
    i2 = jnp.min(jnp.where(rest == m2, lane, LANES), axis=-1, keepdims=True)
    e = jnp.exp(m2 - m1)
    g1 = 1.0 / (1.0 + e)
    g2 = e / (1.0 + e)
    gate = jnp.where(lane == i1, g1, 0.0) + jnp.where(lane == i2, g2, 0.0)
    gate_ref[...] = gate
    gatet_ref[...] = gate.T[0:GATE_ROWS, :]


def _out_router(x, o, w, g, router, rows):
    n, d = x.shape
    return pl.pallas_call(
        _out_router_body,
        grid=(n // rows,),
        in_specs=[pl.BlockSpec((rows, d), lambda i: (i, 0)),
                  pl.BlockSpec((rows, d), lambda i: (i, 0)),
                  pl.BlockSpec(w.shape, lambda i: (0, 0)),
                  pl.BlockSpec((1, d), lambda i: (0, 0)),
                  pl.BlockSpec(router.shape, lambda i: (0, 0))],
        out_specs=[pl.BlockSpec((rows, d), lambda i: (i, 0)),
                   pl.BlockSpec((rows, d), lambda i: (i, 0)),
                   pl.BlockSpec((rows, LANES), lambda i: (i, 0)),
                   pl.BlockSpec((GATE_ROWS, rows), lambda i: (0, i))],
        out_shape=[jax.ShapeDtypeStruct((n, d), F32), jax.ShapeDtypeStruct((n, d), BF16),
                   jax.ShapeDtypeStruct((n, LANES), F32), jax.ShapeDtypeStruct((GATE_ROWS, n), F32)],
        compiler_params=_cparams("parallel"),
        name="out_router",
    )(x, o, w, g, router)


def _t5_bucket(rel):
    half = N_BUCKETS // 2
    max_exact = half // 2
    ret = np.where(rel > 0, half, 0)
    n = np.abs(rel)
    nf = np.maximum(n, 1).astype(np.float32)
    large = max_exact + (np.log(nf / np.float32(max_exact)) / np.float32(math.log(MAX_DIST / max_exact))
                         * np.float32(half - max_exact)).astype(np.int32)
    large = np.minimum(large, half - 1)
    return (ret + np.where(n < max_exact, n, large)).astype(np.int32)


def _bias_tile(rel_bias, q_pos, k_pos, k_real):
    rel = k_pos[None, :] - q_pos[:, None]
    onehot = (jnp.asarray(_t5_bucket(rel).astype(np.int8))[:, :, None]
              == jnp.arange(N_BUCKETS, dtype=jnp.int8)).astype(F32)
    bias = jnp.einsum("qkb,bh->hqk", onehot, rel_bias.astype(F32), precision=lax.Precision.HIGHEST)
    ok = np.logical_and(k_pos[None, :] // CHUNK <= q_pos[:, None] // CHUNK, k_pos[None, :] < k_real)
    return jnp.where(ok[None], bias, NEG_INF)


def _in1_plan():
    scale = DH ** -0.5
    heads64 = lambda o, sc=1.0: [(o, h, h * DH, (h + 1) * DH, sc) for h in range(8)]
    plan = [
        ((0, 512), heads64(0, scale)),
        ((512, 1024), [(1, None, 0, 512, 1.0)] + heads64(2)),
        ((1024, 1536), [(3, None, 0, 512, 1.0)] + heads64(4)),
        ((1536, 2048), heads64(5, scale)),
        ((2048, 2560), [(6, None, 0, 512, 1.0)] + heads64(7)),
        ((2560, 3072), [(8, None, 0, 512, 1.0)] + [(9, h, h * 128, (h + 1) * 128, 1.0) for h in range(H_D)]),
        ((3072, 3584), heads64(10)),
        ((3584, 3712), [(11, None, 0, D_I, 1.0), (12, None, 0, D_I, 1.0), (13, None, D_I, D_I + H_I, 1.0)]),
    ]
    out_defs = [(DH, BF16, 8), (512, F32, None), (DH, BF16, 8), (512, F32, None), (DH, BF16, 8),
                (DH, BF16, 8), (512, F32, None), (DH, BF16, 8), (512, F32, None), (2 * DH, BF16, H_D),
                (D_I, BF16, 8), (D_I, F32, None), (D_I, BF16, None), (H_I, F32, None)]
    return plan, out_defs


def _layer0(x, hist, rb, emit_v, p):
    n = x.shape[0]
    rows = _largest_divisor(n, 512, 16)
    plan = [((0, p["w_in0"].shape[1]), [(0, None, 0, p["w_in0"].shape[1], 1.0)])]
    (z,) = _rms_proj(x, p["ln_mix0"], p["w_in0"], plan, [(p["w_in0"].shape[1], F32, None)], rows)
    ws = p["ws_prompt"] if hist is None else p["ws_sample"]
    bs = p["bs_prompt"] if hist is None else p["bs_sample"]
    outs = _mixer_ab(x, z, hist, p["gmlp_ln_g"], p["gmlp_ln_b"], ws, bs, p["conv_k"], p["w_out0"], rb, emit_v)
    x1 = outs[0]
    x2 = _ffn(x1, p["ln_ffn0"], p["ffn_wg"], p["ffn_wu"], p["ffn_wd"], rows, p["ffn_fb"])
    return (x2,) + tuple(outs[1:])


GATE_ROWS = 16
MOE_ROWS = 128


def _moe_routed_body(x_ref, xn_ref, gate_ref, gatet_ref, gf_ref, wg_ref, wu_ref, wd_ref, y_ref,
                     triu_ref, tril_ref, crow_ref, ccol_ref, xs_ref, ge_ref, acc_ref, yblk_ref, nsub_ref,
                     *, tb, sub):
    e = pl.program_id(1)
    k = pl.program_id(2)
    first_k = k == 0
    last_k = k == pl.num_programs(2) - 1

    @pl.when(jnp.logical_and(e == 0, first_k))
    def _():
        r = lax.broadcasted_iota(I32, (tb, tb), 0)
        c = lax.broadcasted_iota(I32, (tb, tb), 1)
        triu_ref[...] = jnp.where(r < c, 1.0, 0.0).astype(BF16)
        tril_ref[...] = jnp.where(c < r, 1.0, 0.0).astype(BF16)
        crow_ref[...] = _dot(jnp.where(gatet_ref[...] > 0.0, 1.0, 0.0).astype(BF16), triu_ref[...])
        ccol_ref[...] = _dot(tril_ref[...], jnp.where(gate_ref[...] > 0.0, 1.0, 0.0).astype(BF16))
        yblk_ref[...] = jnp.zeros_like(yblk_ref)
        acc_ref[...] = jnp.zeros_like(acc_ref)

    @pl.when(first_k)
    def _():
        g_e = gatet_ref[pl.ds(e, 1), :]
        m_e = g_e > 0.0
        c_e = crow_ref[pl.ds(e, 1), :]
        nsub = (jnp.sum(jnp.where(m_e, 1.0, 0.0)).astype(I32) + sub - 1) // sub
        nsub_ref[0] = nsub
        xn = xn_ref[...]
        slot = lax.broadcasted_iota(I32, (sub, tb), 0).astype(F32)

        def pack(s, carry):
            r0 = pl.multiple_of(s * sub, sub)
            hit = jnp.logical_and(m_e, c_e == slot + (s * sub).astype(F32))
            onehot = jnp.where(hit, 1.0, 0.0)
            xs_ref[pl.ds(r0, sub), :] = _dot(onehot.astype(BF16), xn).astype(BF16)
            ge_ref[pl.ds(r0, sub), :] = jnp.sum(onehot * g_e, axis=-1, keepdims=True)
            acc_ref[pl.ds(r0, sub), :] = jnp.zeros((sub, acc_ref.shape[1]), F32)
            return carry

        lax.fori_loop(0, nsub, pack, 0)

    nsub = nsub_ref[0]

    def experts(rows):
        xs = xs_ref[rows, :]
        h = jax.nn.silu(_dot(xs, wg_ref[0])) * _dot(xs, wu_ref[0])
        acc_ref[rows, :] += _dot((ge_ref[rows, :] * h).astype(BF16), wd_ref[0])

    def pair(s, carry):
        experts(pl.ds(pl.multiple_of(s * 2 * sub, 2 * sub), 2 * sub))
        return carry

    lax.fori_loop(0, nsub // 2, pair, 0)
    pl.when(nsub % 2 == 1)(lambda: experts(pl.ds(pl.multiple_of((nsub - 1) * sub, sub), sub)))

    @pl.when(last_k)
    def _():
        lane = lax.broadcasted_iota(I32, (tb, LANES), 1)
        pick = lambda a: jnp.sum(jnp.where(lane == e, a, 0.0), axis=-1, keepdims=True)
        m_e = pick(gate_ref[...]) > 0.0
        c_e = pick(ccol_ref[...])
        slot = lax.broadcasted_iota(I32, (tb, sub), 1).astype(F32)

        def unpack(s, carry):
            rows = pl.ds(pl.multiple_of(s * sub, sub), sub)
            hit = jnp.logical_and(m_e, c_e == slot + (s * sub).astype(F32))
            onehot = jnp.where(hit, 1.0, 0.0).astype(BF16)
            y = acc_ref[rows, :]
            hi = y.astype(BF16)
            lo = (y - hi.astype(F32)).astype(BF16)
            yblk_ref[...] += _dot(jnp.concatenate([onehot, onehot], axis=1), jnp.concatenate([hi, lo], axis=0))
            return carry

        lax.fori_loop(0, nsub, unpack, 0)

    @pl.when(jnp.logical_and(e == pl.num_programs(1) - 1, last_k))
    def _():
        y_ref[...] = _rms(x_ref[...] + yblk_ref[...], gf_ref[...])


def _moe_routed(x, xn, gate, gatet, gf, wg, wu, wd, tb, fb):
    n, d = x.shape
    ne, _, dff = wg.shape
    return pl.pallas_call(
        functools.partial(_moe_routed_body, tb=tb, sub=min(MOE_ROWS, tb)),
        grid=(n // tb, ne, dff // fb),
        in_specs=[pl.BlockSpec((tb, d), lambda i, e, k: (i, 0)),
                  pl.BlockSpec((tb, d), lambda i, e, k: (i, 0)),
                  pl.BlockSpec((tb, LANES), lambda i, e, k: (i, 0)),
                  pl.BlockSpec((GATE_ROWS, tb), lambda i, e, k: (0, i)),
                  pl.BlockSpec((1, d), lambda i, e, k: (0, 0)),
                  pl.BlockSpec((1, d, fb), lambda i, e, k: (e, 0, k)),
                  pl.BlockSpec((1, d, fb), lambda i, e, k: (e, 0, k)),
                  pl.BlockSpec((1, fb, d), lambda i, e, k: (e, k, 0))],
        out_specs=pl.BlockSpec((tb, d), lambda i, e, k: (i, 0)),
        out_shape=jax.ShapeDtypeStruct((n, d), F32),
        scratch_shapes=[pltpu.VMEM((tb, tb), BF16), pltpu.VMEM((tb, tb), BF16),
                        pltpu.VMEM((GATE_ROWS, tb), F32), pltpu.VMEM((tb, LANES), F32),
                        pltpu.VMEM((tb, d), BF16), pltpu.VMEM((tb, 1), F32),
                        pltpu.VMEM((tb, d), F32), pltpu.VMEM((tb, d), F32),
                        pltpu.SMEM((1,), I32)],
        compiler_params=_cparams("parallel", "arbitrary", "arbitrary"),
        name="moe_routed",
    )(x, xn, gate, gatet, gf, wg, wu, wd)


def _layer1_tail(x, o, p):
    n = x.shape[0]
    rows = _largest_divisor(n, 512, 16)
    x3, xn, gate, gatet = _out_router(x, o, p["w_out1"], p["ln_ffn1"], p["router"], rows)
    return _moe_routed(x3, xn, gate, gatet, p["ln_final"], p["exp_wg"], p["exp_wu"], p["exp_wd"],
                       _largest_divisor(n, 1024, 128), p["exp_fb"])


def kernel(x_prompt, x_sample, state_b_conv, cache_c_k, cache_c_v, cache_idx_k, cache_d_k, cache_d_v, rel_bias, ln_mix, ln_ffn, ln_final, w_in0, gmlp_ln_g, gmlp_ln_b, gmlp_ws, gmlp_bs, conv_k, w_out0, ffn_wg, ffn_wu, ffn_wd, w_in1, lam_qk, subln_g, w_out1, router, exp_wg, exp_wu, exp_wd):
    bp, seq, d = x_prompt.shape
    bs_, ts, _ = x_sample.shape
    past = cache_c_k.shape[2]
    assert bp == 1 and ln_mix.shape[0] == 2 and seq % PROMPT_LB == 0 and ts % SUBLANES == 0 and ts <= CHUNK
    assert past % KEY_BLOCK == 0
    gmlp_chunk = gmlp_ws.shape[-1]
    lam_init = 0.8 - 0.6 * math.exp(-0.3 * 1)

    def ws_masked(rows):
        r = jnp.arange(rows)
        ok = (r[None, :] // CHUNK) <= (r[:, None] // CHUNK)
        return jnp.where(ok[None], gmlp_ws[0][:, :rows, :rows], 0.0).astype(BF16)

    in1_pad = (-w_in1.shape[2]) % LANES
    lf = lam_qk[0].astype(F32)
    lam = (jnp.exp(jnp.sum(lf[0] * lf[1])) - jnp.exp(jnp.sum(lf[2] * lf[3])) + lam_init).reshape(1, 1)
    p = {
        "ln_mix0": ln_mix[0:1], "ln_ffn0": ln_ffn[0:1], "ln_mix1": ln_mix[1:2], "ln_ffn1": ln_ffn[1:2],
        "ln_final": ln_final.reshape(1, d),
        "w_in0": w_in0[0].astype(BF16),
        "gmlp_ln_g": gmlp_ln_g[0].reshape(1, W_A), "gmlp_ln_b": gmlp_ln_b[0].reshape(1, W_A),
        "ws_prompt": ws_masked(gmlp_chunk), "ws_sample": ws_masked(ts),
        "bs_prompt": gmlp_bs[0][:, :gmlp_chunk].T, "bs_sample": gmlp_bs[0][:, :ts].T,
        "conv_k": conv_k[0], "w_out0": w_out0[0].astype(BF16),
        "ffn_wg": ffn_wg[0].astype(BF16), "ffn_wu": ffn_wu[0].astype(BF16), "ffn_wd": ffn_wd[0].astype(BF16),
        "ffn_fb": _largest_divisor(ffn_wg.shape[2], 1408, LANES),
        "w_in1": jnp.pad(w_in1[0], ((0, 0), (0, in1_pad))).astype(BF16),
        "w_out1": w_out1[0].astype(BF16),
        "router": jnp.pad(router[0], ((0, 0), (0, LANES - N_EXP))).astype(BF16),
        "exp_wg": exp_wg[0].astype(BF16), "exp_wu": exp_wu[0].astype(BF16), "exp_wd": exp_wd[0].astype(BF16),
        "exp_fb": _largest_divisor(exp_wg.shape[3], 896, LANES),
    }
    g_sub = subln_g[0].reshape(1, 2 * DH)
    plan1, out_defs1 = _in1_plan()

    xp = x_prompt.reshape(seq, d)
    xs = x_sample.reshape(bs_ * ts, d)
    xp, p_tail = _layer0(xp, None, gmlp_chunk, False, p)
    hist = jnp.pad(state_b_conv[0], ((0, 0), (SUBLANES - 2, 0), (0, 0)))
    xs, s_tail, s_av = _layer0(xs, hist, ts, True, p)
    p_b_conv = p_tail[-1, SUBLANES - 2:, :].reshape(1, 1, 2, W_B)
    s_b_conv = s_tail[:, SUBLANES - 2:, :].reshape(1, bs_, 2, W_B)
    s_a_v = s_av.reshape(1, bs_, ts, W_A)

    lb = KEY_BLOCK
    w1 = w_in1[0]
    off = np.concatenate([[0], np.cumsum(IN1_SIZES)])
    field = lambda k: w1[:, off[k]:off[k + 1]]
    padc = lambda a: jnp.pad(a, ((0, 0), (0, LANES - a.shape[1])))
    log2e = math.log2(math.e)
    qscale = DH ** -0.5 * log2e
    w_norm = jnp.concatenate([field(1), field(4), field(5), padc(field(7))], axis=1).astype(BF16)
    w_tran = jnp.concatenate([field(0) * qscale, field(3) * qscale, field(6), field(2), field(5),
                              field(1), field(4), padc(field(8))], axis=1).T.astype(BF16)
    (vd32, ki32, kc, kd, ki, qct, qdt, qit, vct, vdt, vc32t, kc32t, kd32t, wit) = _proj1_t(
        xp, p["ln_mix1"], w_norm, w_tran, PROMPT_TQ)
    token_major = lambda a, *dims: jnp.moveaxis(a.reshape(dims + (seq,)), -1, 0).reshape((1, 1, seq) + dims)
    p_c_k, p_c_v = token_major(kc32t, H_C, DH), token_major(vc32t, H_C, DH)
    p_d_k = token_major(kd32t, H_D, 2, DH)
    pos = jnp.arange(seq, dtype=I32)
    ve = ((pos // CHUNK + 1) * CHUNK).reshape(1, seq)
    mask = _select_t(qit, wit, ve, ki, min(TOPK_MAX, seq // 4))
    e_far = -(-(PROMPT_LB - 1 + MAX_DIST) // ATTN_TQ)
    kpos = np.arange(PROMPT_LB)
    qpos = np.arange(ATTN_TQ)
    base = e_far * ATTN_TQ
    rel_near = (rel_bias - rel_bias[N_BUCKETS // 2 - 1:N_BUCKETS // 2]) * log2e
    btiles = jnp.stack([jnp.transpose(_bias_tile(rel_near, base + e * ATTN_TQ + qpos, base + kpos, base + PROMPT_LB),
                                      (0, 2, 1)) for e in range(e_far)]).astype(BF16)
    op = _attend_t(qct, qdt, kc, kd, vct, vdt, mask, btiles, lam, subln_g[0].reshape(2 * DH, 1), lam_init)
    y_prompt = _layer1_tail(xp, op, p).reshape(1, seq, d)

    sr = _rms_proj(xs, p["ln_mix1"], p["w_in1"], plan1, out_defs1, _largest_divisor(bs_ * ts, 256, 16))
    sqc, skc32, skc, svc32, svc, sqd, skd32, skd, svd32, svd, sqi, ski32, ski, swi = sr
    nk = past + ts
    lps = -(-nk // lb) * lb

    kis = jnp.pad(jnp.concatenate([cache_idx_k[0].astype(BF16), ski.reshape(bs_, ts, D_I)], axis=1),
                  ((0, 0), (0, lps - nk), (0, 0)))
    ves = jnp.full((bs_ * ts, 1), nk, I32)
    smask = _select(sqi, swi, ves, kis, bs_, ts, ts, min(TOPK_MAX, nk // 4), lambda i: lps // lb)
    sq_pos = past + np.arange(ts)
    sbt = jnp.stack([_bias_tile(rel_bias, sq_pos, j * lb + np.arange(lb), nk) for j in range(lps // lb)])
    sbias = jnp.concatenate([sbt[:, :H_C], jnp.repeat(sbt[:, H_C:], 2, axis=1)], axis=1).reshape(lps // lb, N_MAPS * ts, lb)

    def block_diag(q):
        qb = jnp.transpose(q.reshape(H_C, bs_, ts, DH), (1, 0, 2, 3))
        eye = jnp.eye(H_C, dtype=q.dtype)
        return (qb[:, :, :, None, :] * eye[None, :, None, :, None]).reshape(bs_, H_C * ts, W_C)

    def cache_rows(c, feature_major):
        c = c.reshape(bs_, past, W_C)
        return jnp.transpose(c, (0, 2, 1)) if feature_major else c

    def new_rows(a, feature_major):
        a = jnp.pad(a.reshape(bs_, ts, W_C), ((0, 0), (0, lb - ts), (0, 0)))
        return jnp.transpose(a, (0, 2, 1)) if feature_major else a
    os_ = _attend_s(block_diag(sqc), block_diag(sqd),
                    [cache_rows(c[0], fm) for c, fm in ((cache_c_k, True), (cache_c_v, True), (cache_d_k, True), (cache_d_v, False))],
                    [new_rows(a, fm) for a, fm in ((skc32, True), (svc32, True), (skd32, True), (svd32, False))],
                    smask, sbias, lam, g_sub, lam_init)
    y_sample = _layer1_tail(xs, os_, p).reshape(bs_, ts, d)

    r5 = lambda a, n, t, *tail: a.reshape((1, n, t) + tail)
    return (y_prompt, y_sample, p_b_conv,
            p_c_k, p_c_v, r5(ki32, 1, seq, D_I), p_d_k, r5(vd32, 1, seq, H_D, 2 * DH),
            s_a_v, s_b_conv,
            r5(skc32, bs_, ts, H_C, DH), r5(svc32, bs_, ts, H_C, DH), r5(ski32, bs_, ts, D_I),
            r5(skd32, bs_, ts, H_D, 2, DH), r5(svd32, bs_, ts, H_D, 2 * DH))
```

```python
import functools
import math

import jax
import jax.numpy as jnp
import numpy as np
from jax import lax
from jax.experimental import pallas as pl
from jax.experimental.pallas import tpu as pltpu

F32 = jnp.float32
BF16 = jnp.bfloat16
I32 = jnp.int32

CHUNK = 64
EPS = 1e-6
NEG_INF = -1e30
H_A = 4
C_A = 128
W_A = H_A * C_A
W_B = 512
H_C = 8
DH = 64
W_C = H_C * DH
H_I = 8
D_I = 64
TOPK_MAX = 256
H_D = 4
W_D = H_D * 2 * DH
N_BUCKETS = 32
MAX_DIST = 128
N_EXP = 8
IN1_SIZES = (W_C, W_C, W_C, W_D, W_D, W_D, H_I * D_I, D_I, H_I)
N_HEADS_BIAS = H_C + H_D
N_MAPS = H_C + 2 * H_D

LANES = 128
SUBLANES = 8
VMEM_LIMIT = 56 * 1024 * 1024

INT_MIN = -(2 ** 31)
INT_MAX = 2 ** 31 - 1
ALL_TIES = 2 ** 30

KEY_BLOCK = 512


def _cparams(*sem):
    return pltpu.CompilerParams(dimension_semantics=sem, vmem_limit_bytes=VMEM_LIMIT)


def _largest_divisor(n, target, mult):
    if n <= target:
        return n
    d = (target // mult) * mult
    while d >= mult:
        if n % d == 0:
            return d
        d -= mult
    raise ValueError(f"no block of multiple {mult} divides {n}")


def _rms(x, g):
    return x * lax.rsqrt(jnp.mean(x * x, axis=-1, keepdims=True) + EPS) * g


def _dot(a, b):
    return jnp.dot(a, b, preferred_element_type=F32)


def _dot_nt(a, b):
    return lax.dot_general(a, b, (((1,), (1,)), ((), ())), preferred_element_type=F32)


def _rms_proj_body(x_ref, g_ref, w_ref, *out_refs, plan):
    xn = _rms(x_ref[...], g_ref[...]).astype(BF16)
    for (c0, c1), writes in plan:
        z = _dot(xn, w_ref[:, c0:c1])
        for o_idx, head, z0, z1, scale in writes:
            val = z[:, z0:z1]
            if scale != 1.0:
                val = val * scale
            ref = out_refs[o_idx]
            if head is None:
                ref[...] = val.astype(ref.dtype)
            else:
                ref[head] = val.astype(ref.dtype)


def _rms_proj(x, g, w, plan, out_defs, rows):
    n, d = x.shape
    grid = (n // rows,)
    out_shape, out_specs = [], []
    for width, dtype, heads in out_defs:
        if heads is None:
            out_shape.append(jax.ShapeDtypeStruct((n, width), dtype))
            out_specs.append(pl.BlockSpec((rows, width), lambda i: (i, 0)))
        else:
            out_shape.append(jax.ShapeDtypeStruct((heads, n, width), dtype))
            out_specs.append(pl.BlockSpec((heads, rows, width), lambda i: (0, i, 0)))
    return pl.pallas_call(
        functools.partial(_rms_proj_body, plan=plan),
        grid=grid,
        in_specs=[pl.BlockSpec((rows, d), lambda i: (i, 0)),
                  pl.BlockSpec((1, d), lambda i: (0, 0)),
                  pl.BlockSpec(w.shape, lambda i: (0, 0))],
        out_specs=out_specs,
        out_shape=out_shape,
        compiler_params=_cparams("parallel"),
        name="rms_proj",
    )(x, g, w)


def _mixer_ab_body(*refs, rb, from_prev, emit_v):
    it = iter(refs)
    x_ref, u_ref, v_ref, gb_ref, gc_ref, xin_ref = (next(it) for _ in range(6))
    if from_prev:
        gcp_ref, xinp_ref = next(it), next(it)
    else:
        hist_ref = next(it)
    lng_ref, lnb_ref, ws_ref, bs_ref, ck_ref, wout_ref = (next(it) for _ in range(6))
    x1_ref, tail_ref = next(it), next(it)
    vout_ref = next(it) if emit_v else None
    wext_ref = next(it)

    w = gc_ref[...] * xin_ref[...]
    if from_prev:
        hist = jnp.where(pl.program_id(0) > 0, gcp_ref[...] * xinp_ref[...], 0.0)
    else:
        hist = hist_ref[0]
    wext_ref[0:SUBLANES, :] = hist
    wext_ref[SUBLANES:, :] = w
    ck = ck_ref[...]
    conv = (ck[0:1] * wext_ref[SUBLANES - 2:SUBLANES - 2 + rb, :]
            + ck[1:2] * wext_ref[SUBLANES - 1:SUBLANES - 1 + rb, :]
            + ck[2:3] * w)
    y_b = gb_ref[...] * conv
    tail_ref[0] = w[rb - SUBLANES:, :]

    u = jax.nn.gelu(u_ref[...])
    v = jax.nn.gelu(v_ref[...])
    lng = lng_ref[...]
    lnb = lnb_ref[...]
    bs = bs_ref[...]
    acc = _dot(y_b.astype(BF16), wout_ref[W_A:, :])
    for h in range(H_A):
        sl = slice(h * C_A, (h + 1) * C_A)
        vh = v[:, sl]
        mu = jnp.mean(vh, axis=-1, keepdims=True)
        xc = vh - mu
        var = jnp.mean(xc * xc, axis=-1, keepdims=True)
        vln = xc * lax.rsqrt(var + EPS) * lng[:, sl] + lnb[:, sl]
        if emit_v:
            vout_ref[:, sl] = vln
        s = _dot(ws_ref[h], vln.astype(BF16)) + bs[:, h:h + 1]
        y_a = u[:, sl] * s
        acc = acc + _dot(y_a.astype(BF16), wout_ref[sl, :])
    x1_ref[...] = x_ref[...] + acc


def _mixer_ab(x, z, hist, lng, lnb, ws, bs, ck, wout, rb, emit_v):
    n, d = x.shape
    nb = n // rb
    from_prev = hist is None
    col = lambda c: pl.BlockSpec((rb, 512), lambda i, c=c: (i, c))
    in_specs = [pl.BlockSpec((rb, d), lambda i: (i, 0)), col(0), col(1), col(2), col(3), col(4)]
    args = [x, z, z, z, z, z]
    if from_prev:
        per = rb // SUBLANES
        prev = lambda c: pl.BlockSpec((SUBLANES, 512), lambda i, c=c: (jnp.maximum(i * per - 1, 0), c))
        in_specs += [prev(3), prev(4)]
        args += [z, z]
    else:
        in_specs += [pl.BlockSpec((1, SUBLANES, 512), lambda i: (i, 0, 0))]
        args += [hist]
    const = lambda a: pl.BlockSpec(a.shape, lambda i, nd=a.ndim: (0,) * nd)
    for a in (lng, lnb, ws, bs, ck, wout):
        in_specs.append(const(a))
        args.append(a)
    out_shape = [jax.ShapeDtypeStruct((n, d), F32), jax.ShapeDtypeStruct((nb, SUBLANES, 512), F32)]
    out_specs = [pl.BlockSpec((rb, d), lambda i: (i, 0)), pl.BlockSpec((1, SUBLANES, 512), lambda i: (i, 0, 0))]
    if emit_v:
        out_shape.append(jax.ShapeDtypeStruct((n, W_A), F32))
        out_specs.append(pl.BlockSpec((rb, W_A), lambda i: (i, 0)))
    return pl.pallas_call(
        functools.partial(_mixer_ab_body, rb=rb, from_prev=from_prev, emit_v=emit_v),
        grid=(nb,),
        in_specs=in_specs,
        out_specs=out_specs,
        out_shape=out_shape,
        scratch_shapes=[pltpu.VMEM((rb + SUBLANES, 512), F32)],
        compiler_params=_cparams("arbitrary"),
        name="mixer_ab",
    )(*args)


def _ffn_body(x_ref, g_ref, wg_ref, wu_ref, wd_ref, o_ref, xn_ref, acc_ref):
    k = pl.program_id(1)

    @pl.when(k == 0)
    def _():
        xn_ref[...] = _rms(x_ref[...], g_ref[...]).astype(BF16)
        acc_ref[...] = jnp.zeros_like(acc_ref)

    xn = xn_ref[...]
    h = jax.nn.silu(_dot(xn, wg_ref[...])) * _dot(xn, wu_ref[...])
    acc_ref[...] += _dot(h.astype(BF16), wd_ref[...])

    @pl.when(k == pl.num_programs(1) - 1)
    def _():
        o_ref[...] = x_ref[...] + acc_ref[...]


def _ffn(x, g, wg, wu, wd, rows, fb):
    n, d = x.shape
    dff = wg.shape[1]
    return pl.pallas_call(
        _ffn_body,
        grid=(n // rows, dff // fb),
        in_specs=[pl.BlockSpec((rows, d), lambda i, k: (i, 0)),
                  pl.BlockSpec((1, d), lambda i, k: (0, 0)),
                  pl.BlockSpec((d, fb), lambda i, k: (0, k)),
                  pl.BlockSpec((d, fb), lambda i, k: (0, k)),
                  pl.BlockSpec((fb, d), lambda i, k: (k, 0))],
        out_specs=pl.BlockSpec((rows, d), lambda i, k: (i, 0)),
        out_shape=jax.ShapeDtypeStruct((n, d), F32),
        scratch_shapes=[pltpu.VMEM((rows, d), BF16), pltpu.VMEM((rows, d), F32)],
        compiler_params=_cparams("parallel", "arbitrary"),
        name="ffn",
    )(x, g, wg, wu, wd)


def _select_body(qi_ref, wi_ref, ve_ref, ki_ref, out_ref, keys_ref, *, tq, lb, nkb, topk, nvalid_fn):
    nv = nvalid_fn(pl.program_id(1))
    wi = wi_ref[...]
    ve = ve_ref[...]
    q_all = jnp.concatenate([qi_ref[h] for h in range(H_I)], axis=0)
    wcols = [wi[:, h:h + 1] for h in range(H_I)]
    lane = lax.broadcasted_iota(I32, (tq, lb), 1)

    def score_block(b, carry):
        kb = ki_ref[0, pl.ds(pl.multiple_of(b * lb, lb), lb), :]
        dots = _dot_nt(q_all, kb)
        sc = jnp.zeros((tq, lb), F32)
        for h in range(H_I):
            sc = sc + wcols[h] * jnp.maximum(dots[h * tq:(h + 1) * tq, :], 0.0)
        bits = lax.bitcast_convert_type(sc, I32)
        key = bits ^ ((bits >> 31) & INT_MAX)
        key = jnp.where(sc == 0.0, 0, key)
        key = jnp.where(lane + b * lb < ve, key, INT_MIN)
        keys_ref[b] = key
        return carry

    lax.fori_loop(0, nv, score_block, 0)

    def count_ge(mid):
        midb = jnp.broadcast_to(mid, (tq, LANES))

        def body(b, acc):
            for c in range(lb // LANES):
                k = keys_ref[b, :, c * LANES:(c + 1) * LANES]
                acc = acc + jnp.where(k >= midb, 1, 0)
            return acc

        acc = lax.fori_loop(0, nv, body, jnp.zeros((tq, LANES), I32))
        return jnp.sum(acc.astype(F32), axis=-1, keepdims=True).astype(I32)

    def bisect(_, st):
        lo, hi, clo, chi = st
        mid = (lo >> 1) + (hi >> 1) + (lo & hi & 1)
        cnt = count_ge(mid)
        active = mid != lo
        up = jnp.logical_and(active, cnt >= topk)
        dn = jnp.logical_and(active, cnt < topk)
        return (jnp.where(up, mid, lo), jnp.where(dn, mid, hi),
                jnp.where(up, cnt, clo), jnp.where(dn, cnt, chi))

    full = lambda v: jnp.full((tq, 1), v, I32)
    lo, hi, clo, chi = lax.fori_loop(0, 32, bisect, (full(INT_MIN), full(INT_MAX), ve, full(0)))

    need = jnp.where(clo > topk, topk - chi, ALL_TIES)
    need = jnp.where(lo == INT_MIN, 0, need).astype(F32)
    lob = jnp.broadcast_to(lo, (tq, LANES))
    needb = jnp.broadcast_to(need, (tq, LANES))
    r = lax.broadcasted_iota(I32, (LANES, LANES), 0)
    c = lax.broadcasted_iota(I32, (LANES, LANES), 1)
    tri = jnp.where(r <= c, 1.0, 0.0).astype(BF16)

    def mask_block(b, seen):
        for cc in range(lb // LANES):
            k = keys_ref[b, :, cc * LANES:(cc + 1) * LANES]
            eq = jnp.where(k == lob, 1.0, 0.0)
            cum = _dot(eq.astype(BF16), tri)
            rank = seen + cum - eq
            take = jnp.where(rank < needb, eq, 0.0)
            sel = jnp.where(k > lob, 1.0, take)
            out_ref[0, b, :, cc * LANES:(cc + 1) * LANES] = jnp.where(sel > 0.5, 0.0, NEG_INF).astype(out_ref.dtype)
            seen = seen + cum[:, LANES - 1:LANES]
        return seen

    lax.fori_loop(0, nv, mask_block, jnp.zeros((tq, 1), F32))

    def fill_block(b, carry):
        out_ref[0, b] = jnp.full((tq, lb), NEG_INF, out_ref.dtype)
        return carry

    lax.fori_loop(nv, nkb, fill_block, 0)


def _select(qi, wi, ve, ki, nbatch, t, tq, topk, nvalid_fn):
    lp = ki.shape[1]
    lb = KEY_BLOCK
    nkb = lp // lb
    nq = t // tq
    row = lambda b, i: b * nq + i
    return pl.pallas_call(
        functools.partial(_select_body, tq=tq, lb=lb, nkb=nkb, topk=topk, nvalid_fn=nvalid_fn),
        grid=(nbatch, nq),
        in_specs=[pl.BlockSpec((H_I, tq, D_I), lambda b, i: (0, row(b, i), 0)),
                  pl.BlockSpec((tq, H_I), lambda b, i: (row(b, i), 0)),
                  pl.BlockSpec((tq, 1), lambda b, i: (row(b, i), 0)),
                  pl.BlockSpec((1, lp, D_I), lambda b, i: (b, 0, 0))],
        out_specs=pl.BlockSpec((1, nkb, tq, lb), lambda b, i: (b, 0, i, 0)),
        out_shape=jax.ShapeDtypeStruct((nbatch, nkb, t, lb), BF16),
        scratch_shapes=[pltpu.VMEM((nkb, tq, lb), I32)],
        compiler_params=_cparams("parallel", "arbitrary"),
        name="index_select",
    )(qi, wi, ve, ki)


PROMPT_TQ = 256
ATTN_TQ = 512
PROMPT_LB = 512
SEL_ROWS = 256
TIE_ROWS = 128
MAPS_PER_DOT = 4
PV_ROWS = 256
ONES_ROWS = 16


def _proj1_t_body(x_ref, g_ref, w_ref, wt_ref, vd32_ref, ki32_ref, kc_ref, kd_ref, ki_ref,
                  qct_ref, qdt_ref, qit_ref, vct_ref, vdt_ref, vc32t_ref, kc32t_ref, kd32t_ref, wit_ref):
    xn32 = _rms(x_ref[...], g_ref[...])
    xn = xn32.astype(BF16)
    xnt = xn32.T.astype(BF16)
    kc_ref[...] = _dot(xn, w_ref[:, 0:512]).astype(BF16)
    kd_ref[...] = _dot(xn, w_ref[:, 512:1024]).astype(BF16)
    vd32_ref[...] = _dot(xn, w_ref[:, 1024:1536])
    z = _dot(xn, w_ref[:, 1536:1536 + LANES])
    ki32_ref[...] = z[:, 0:D_I]
    ki_ref[...] = z[:, 0:D_I].astype(BF16)
    for c, (head_ref, full_ref) in enumerate(((qct_ref, None), (qdt_ref, None), (qit_ref, None), (vct_ref, vc32t_ref),
                                              (vdt_ref, None), (None, kc32t_ref), (None, kd32t_ref))):
        zt = _dot(wt_ref[c * 512:(c + 1) * 512, :], xnt)
        if full_ref is not None:
            full_ref[...] = zt
        if head_ref is not None:
            nh = head_ref.shape[0]
            w = 512 // nh
            for h in range(nh):
                head_ref[h, 0:w, :] = zt[h * w:(h + 1) * w, :].astype(BF16)
                if head_ref.shape[1] > w:
                    pad = lax.broadcasted_iota(I32, (head_ref.shape[1] - w, zt.shape[1]), 0)
                    head_ref[h, w:, :] = jnp.where(pad == 0, 1.0, 0.0).astype(BF16)
    zt = _dot(wt_ref[3584:3584 + LANES, :], xnt)
    wit_ref[...] = zt[0:H_I, :]


def _proj1_t(x, g, w, wt, rows):
    n, d = x.shape
    full = lambda width: (jax.ShapeDtypeStruct((n, width), F32), pl.BlockSpec((rows, width), lambda i: (i, 0)))
    packed = lambda width: (jax.ShapeDtypeStruct((n, width), BF16), pl.BlockSpec((rows, width), lambda i: (i, 0)))
    heads_t = lambda nh, width: (jax.ShapeDtypeStruct((nh, width, n), BF16),
                                 pl.BlockSpec((nh, width, rows), lambda i: (0, 0, i)))
    full_t = lambda: (jax.ShapeDtypeStruct((512, n), F32), pl.BlockSpec((512, rows), lambda i: (0, i)))
    outs = [full(512), full(D_I), packed(512), packed(512), packed(D_I),
            heads_t(8, DH), heads_t(8, DH), heads_t(8, DH), heads_t(8, DH + ONES_ROWS), heads_t(H_D, 2 * DH + ONES_ROWS),
            full_t(), full_t(), full_t(),
            (jax.ShapeDtypeStruct((H_I, n), F32), pl.BlockSpec((H_I, rows), lambda i: (0, i)))]
    return pl.pallas_call(
        _proj1_t_body,
        grid=(n // rows,),
        in_specs=[pl.BlockSpec((rows, d), lambda i: (i, 0)),
                  pl.BlockSpec((1, d), lambda i: (0, 0)),
                  pl.BlockSpec(w.shape, lambda i: (0, 0)),
                  pl.BlockSpec(wt.shape, lambda i: (0, 0))],
        out_specs=[o[1] for o in outs],
        out_shape=[o[0] for o in outs],
        compiler_params=_cparams("parallel"),
        name="proj1_t",
    )(x, g, w, wt)


def _sublane_all(x8, op):
    for shift in (4, 2, 1):
        x8 = op(x8, pltpu.roll(x8, shift, 0))
    return x8


def _select_t_body(qit_ref, wit_ref, ve_ref, ki_ref, out_ref, keys_ref, gmax_ref, *, tq, lp, topk):
    rb = SEL_ROWS
    nv = (pl.program_id(0) + 1) * (tq // rb)
    wit = wit_ref[...]
    ve = ve_ref[...]
    rows = lax.broadcasted_iota(I32, (rb, tq), 0)
    gmax_ref[...] = jnp.full((rb, tq), INT_MIN, I32)
    q_all = jnp.concatenate([qit_ref[h] for h in range(H_I)], axis=1)

    def score_chunk(c, carry):
        r0 = pl.multiple_of(c * rb, rb)
        dots = _dot(ki_ref[pl.ds(r0, rb), :], q_all)
        sc = jnp.zeros((rb, tq), F32)
        for h in range(H_I):
            sc = sc + wit[h:h + 1, :] * jnp.maximum(dots[:, h * tq:(h + 1) * tq], 0.0)
        bits = lax.bitcast_convert_type(sc, I32)
        key = bits ^ ((bits >> 31) & INT_MAX)
        key = jnp.where(sc == 0.0, 0, key)
        key = jnp.where(rows + r0 < ve, key, INT_MIN)
        keys_ref[pl.ds(r0, rb), :] = key
        gmax_ref[...] = jnp.maximum(gmax_ref[...], key)
        return carry

    lax.fori_loop(0, nv, score_chunk, 0)

    def count_ge(mid):
        midb = jnp.broadcast_to(mid, (SUBLANES, tq))

        def body(c, accs):
            kc = keys_ref[pl.ds(pl.multiple_of(c * rb, rb), rb), :]
            accs = list(accs)
            for g in range(rb // SUBLANES):
                a = g % len(accs)
                accs[a] = accs[a] + jnp.where(kc[g * SUBLANES:(g + 1) * SUBLANES, :] >= midb, 1, 0)
            return tuple(accs)

        accs = lax.fori_loop(0, nv, body, (jnp.zeros((SUBLANES, tq), I32),) * 4)
        return _sublane_all(accs[0] + accs[1] + accs[2] + accs[3], jnp.add)[0:1, :]

    floor_avg = lambda a, b: (a >> 1) + (b >> 1) + (a & b & 1)

    def pending(lo, hi, clo):
        open_ = jnp.logical_and(floor_avg(lo, hi) != lo, clo > topk)
        return jnp.max(jnp.where(open_, 1.0, 0.0))

    def bisect(st):
        lo, hi, clo, chi, _ = st
        mid = floor_avg(lo, hi)
        cnt = count_ge(mid)
        active = jnp.logical_and(mid != lo, clo > topk)
        up = jnp.logical_and(active, cnt >= topk)
        dn = jnp.logical_and(active, cnt < topk)
        lo, hi = jnp.where(up, mid, lo), jnp.where(dn, mid, hi)
        clo, chi = jnp.where(up, cnt, clo), jnp.where(dn, cnt, chi)
        return lo, hi, clo, chi, pending(lo, hi, clo)

    g8 = gmax_ref[0:SUBLANES, :]
    h8 = g8
    for g in range(1, rb // SUBLANES):
        blk = gmax_ref[g * SUBLANES:(g + 1) * SUBLANES, :]
        g8 = jnp.minimum(g8, blk)
        h8 = jnp.maximum(h8, blk)
    lo0 = _sublane_all(g8, jnp.minimum)[0:1, :]
    hi0 = _sublane_all(h8, jnp.maximum)[0:1, :] + 1
    clo0 = jnp.where(lo0 == INT_MIN, ve, count_ge(lo0))
    chi0 = jnp.zeros((1, tq), I32)

    def probe_zero():
        c_pos, c_nonneg = count_ge(jnp.full((1, tq), 1, I32)), count_ge(jnp.zeros((1, tq), I32))
        above = c_pos >= topk
        below = c_nonneg < topk
        lo1 = jnp.where(above, jnp.maximum(lo0, 1), jnp.where(below, lo0, 0))
        clo1 = jnp.where(above, jnp.where(lo0 >= 1, clo0, c_pos), jnp.where(below, clo0, c_nonneg))
        hi1 = jnp.where(above, hi0, jnp.where(below, jnp.minimum(hi0, 0), 1))
        chi1 = jnp.where(above, chi0, jnp.where(below, jnp.where(hi0 <= 0, chi0, c_nonneg), c_pos))
        return lo1, hi1, clo1, chi1

    lo1, hi1, clo1, chi1 = lax.cond(jnp.max(jnp.where(lo0 < 1, 1.0, 0.0)) > 0.5, probe_zero,
                                    lambda: (lo0, hi0, clo0, chi0))
    lo, hi, clo, chi, _ = lax.while_loop(
        lambda st: st[4] > 0.5, bisect, (lo1, hi1, clo1, chi1, pending(lo1, hi1, clo1)))

    need = jnp.where(clo > topk, topk - chi, ALL_TIES)
    need = jnp.where(lo == INT_MIN, 0, need).astype(F32)
    tr = TIE_ROWS
    lob = jnp.broadcast_to(lo, (tr, tq))
    needb = jnp.broadcast_to(need, (tr, tq))
    r = lax.broadcasted_iota(I32, (tr, tr), 0)
    c = lax.broadcasted_iota(I32, (tr, tr), 1)
    tri = jnp.where(c <= r, 1.0, 0.0).astype(BF16)

    def mask_chunk(cidx, seen):
        r0 = pl.multiple_of(cidx * tr, tr)
        k = keys_ref[pl.ds(r0, tr), :]
        eq = jnp.where(k == lob, 1.0, 0.0)
        cum = _dot(tri, eq.astype(BF16))
        take = jnp.where(seen + cum - eq < needb, eq, 0.0)
        sel = jnp.where(k > lob, 1.0, take)
        out_ref[pl.ds(r0, tr), :] = jnp.where(sel > 0.5, 0.0, NEG_INF).astype(out_ref.dtype)
        return seen + cum[tr - 1:tr, :]

    def plain_chunk(cidx, carry):
        r0 = pl.multiple_of(cidx * rb, rb)
        k = keys_ref[pl.ds(r0, rb), :]
        hit = jnp.where(k == INT_MIN, NEG_INF, 0.0)
        out_ref[pl.ds(r0, rb), :] = jnp.where(k >= lo, hit, NEG_INF).astype(out_ref.dtype)
        return carry

    any_tie = jnp.max(jnp.where(clo > topk, 1.0, 0.0)) > 0.5
    lax.cond(any_tie,
             lambda: lax.fori_loop(0, nv * (rb // tr), mask_chunk, jnp.zeros((1, tq), F32)),
             lambda: lax.fori_loop(0, nv, plain_chunk, jnp.zeros((1, tq), F32)))

    def fill_chunk(cidx, carry):
        out_ref[pl.ds(pl.multiple_of(cidx * rb, rb), rb), :] = jnp.full((rb, tq), NEG_INF, out_ref.dtype)
        return carry

    lax.fori_loop(nv, lp // rb, fill_chunk, 0)


def _select_t(qit, wit, ve, ki, topk):
    t = qit.shape[2]
    tq = PROMPT_TQ
    return pl.pallas_call(
        functools.partial(_select_t_body, tq=tq, lp=t, topk=topk),
        grid=(t // tq,),
        in_specs=[pl.BlockSpec((H_I, D_I, tq), lambda i: (0, 0, i)),
                  pl.BlockSpec((H_I, tq), lambda i: (0, i)),
                  pl.BlockSpec((1, tq), lambda i: (0, i)),
                  pl.BlockSpec((t, D_I), lambda i: (0, 0))],
        out_specs=pl.BlockSpec((t, tq), lambda i: (0, i)),
        out_shape=jax.ShapeDtypeStruct((t, t), BF16),
        scratch_shapes=[pltpu.VMEM((t, tq), I32), pltpu.VMEM((SEL_ROWS, tq), I32)],
        compiler_params=_cparams("parallel"),
        name="index_select_t",
    )(qit, wit, ve, ki)


def _attn_t_body(qi_ref, kj_ref, qct_ref, qdt_ref, kc_ref, kd_ref, vct_ref, vdt_ref, mask_ref, bt_ref, lam_ref,
                 g_ref, o_ref, m_ref, accc_ref, accd_ref, ot_ref, qbd_ref, s_ref, *, tq, per, e_far,
                 lam_init):
    i = qi_ref[pl.program_id(0)]
    j = kj_ref[pl.program_id(0)]
    e = i - j * per
    grp = MAPS_PER_DOT

    @pl.when(j == 0)
    def _():
        m_ref[...] = jnp.full(m_ref.shape, NEG_INF, F32)
        accc_ref[...] = jnp.zeros_like(accc_ref)
        accd_ref[...] = jnp.zeros_like(accd_ref)
        qbd_ref[...] = jnp.zeros_like(qbd_ref)
        for mp in range(N_MAPS):
            q = qct_ref[mp] if mp < H_C else qdt_ref[mp - H_C]
            a = mp % grp
            qbd_ref[mp // grp, a * DH:(a + 1) * DH, a * tq:(a + 1) * tq] = q

    def logits(g):
        k_ref = kc_ref if g < H_C // grp else kd_ref
        half = g % (H_C // grp)
        return _dot(k_ref[:, half * grp * DH:(half + 1) * grp * DH], qbd_ref[g])

    def step(near):
        sel = mask_ref[...].astype(F32)
        sel = jnp.concatenate([sel] * grp, axis=1)
        for g in range(N_MAPS // grp):
            s_ref[g] = logits(g) + sel if g < H_C // grp else logits(g)
        for mp in range(N_MAPS):
            g, a = mp // grp, mp % grp
            cols = slice(a * tq, (a + 1) * tq)
            if mp < H_C:
                bias_idx, vt, acc_ref, a_idx = mp, vct_ref[mp], accc_ref, mp
            else:
                dm = mp - H_C
                bias_idx, vt, acc_ref, a_idx = H_C + dm // 2, vdt_ref[dm // 2], accd_ref, dm

            def biased(rows):
                s = s_ref[g, rows, cols]
                if near:
                    s = s + bt_ref[0, bias_idx, rows, :].astype(F32)
                return s

            m_prev = m_ref[mp]
            m_new = jnp.maximum(m_prev, jnp.max(biased(slice(None)), axis=0, keepdims=True))
            alpha = jnp.exp2(m_prev - m_new)
            m_ref[mp] = m_new
            pv = None
            for r0 in range(0, s_ref.shape[1], PV_ROWS):
                rows = slice(r0, r0 + PV_ROWS)
                part = _dot(vt[:, rows], jnp.exp2(biased(rows) - m_new).astype(BF16))
                pv = part if pv is None else pv + part
            acc_ref[a_idx] = alpha * acc_ref[a_idx] + pv

    pl.when(e < e_far)(functools.partial(step, True))
    pl.when(e >= e_far)(functools.partial(step, False))

    @pl.when(j == i // per)
    def _():
        for h in range(H_C):
            ot_ref[h * DH:(h + 1) * DH, :] = accc_ref[h, 0:DH, :] / accc_ref[h, DH:DH + 1, :]
        lam = lam_ref[...]
        g = g_ref[...]
        dv = 2 * DH
        for h in range(H_D):
            a0 = accd_ref[2 * h, 0:dv, :] / accd_ref[2 * h, dv:dv + 1, :]
            a1 = accd_ref[2 * h + 1, 0:dv, :] / accd_ref[2 * h + 1, dv:dv + 1, :]
            od = a0 - lam * a1
            od = od * lax.rsqrt(jnp.mean(od * od, axis=0, keepdims=True) + EPS) * g * (1.0 - lam_init)
            ot_ref[W_C + h * 2 * DH:W_C + (h + 1) * 2 * DH, :] = od
        o_ref[...] = ot_ref[...].T.astype(o_ref.dtype)


def _attend_t(qct, qdt, kc, kd, vct, vdt, mask, btiles, lam, g, lam_init):
    t = qct.shape[2]
    tq, lb = ATTN_TQ, PROMPT_LB
    per = lb // tq
    e_far = btiles.shape[0]
    pairs = [(i, j) for i in range(t // tq) for j in range(i // per + 1)]
    qi_tab = jnp.asarray(np.array([p[0] for p in pairs], np.int32))
    kj_tab = jnp.asarray(np.array([p[1] for p in pairs], np.int32))
    qspec = pl.BlockSpec((H_C, DH, tq), lambda s, qi, kj: (0, 0, qi[s]))
    kspec = pl.BlockSpec((lb, H_C * DH), lambda s, qi, kj: (kj[s], 0))
    grid_spec = pltpu.PrefetchScalarGridSpec(
        num_scalar_prefetch=2,
        grid=(len(pairs),),
        in_specs=[qspec, qspec, kspec, kspec,
                  pl.BlockSpec((H_C, DH + ONES_ROWS, lb), lambda s, qi, kj: (0, 0, kj[s])),
                  pl.BlockSpec((H_D, 2 * DH + ONES_ROWS, lb), lambda s, qi, kj: (0, 0, kj[s])),
                  pl.BlockSpec((lb, tq), lambda s, qi, kj: (kj[s], qi[s])),
                  pl.BlockSpec((1, N_HEADS_BIAS, lb, tq),
                               lambda s, qi, kj: (jnp.minimum(qi[s] - kj[s] * per, e_far - 1), 0, 0, 0)),
                  pl.BlockSpec((1, 1), lambda s, qi, kj: (0, 0)),
                  pl.BlockSpec((2 * DH, 1), lambda s, qi, kj: (0, 0))],
        out_specs=pl.BlockSpec((tq, W_C + W_D), lambda s, qi, kj: (qi[s], 0)),
        scratch_shapes=[pltpu.VMEM((N_MAPS, 1, tq), F32),
                        pltpu.VMEM((H_C, DH + ONES_ROWS, tq), F32),
                        pltpu.VMEM((2 * H_D, 2 * DH + ONES_ROWS, tq), F32),
                        pltpu.VMEM((W_C + W_D, tq), F32),
                        pltpu.VMEM((N_MAPS // MAPS_PER_DOT, MAPS_PER_DOT * DH, MAPS_PER_DOT * tq), BF16),
                        pltpu.VMEM((N_MAPS // MAPS_PER_DOT, lb, MAPS_PER_DOT * tq), F32)])
    return pl.pallas_call(
        functools.partial(_attn_t_body, tq=tq, per=per, e_far=e_far, lam_init=lam_init),
        grid_spec=grid_spec,
        out_shape=jax.ShapeDtypeStruct((t, W_C + W_D), BF16),
        compiler_params=_cparams("arbitrary"),
        name="attend_t",
    )(qi_tab, kj_tab, qct, qdt, kc, kd, vct, vdt, mask, btiles, lam, g)


def _attn_s_body(qc_ref, qd_ref, ck_ref, cv_ref, dk_ref, dv_ref, nck_ref, ncv_ref, ndk_ref, ndv_ref,
                 mask_ref, bias_ref, lam_ref, g_ref, o_ref, m_ref, l_ref, acc_ref, *, ts, ncache, lam_init):
    j = pl.program_id(1)
    rows = H_C * ts

    @pl.when(j == 0)
    def _():
        m_ref[...] = jnp.full(m_ref.shape, NEG_INF, F32)
        l_ref[...] = jnp.zeros_like(l_ref)
        acc_ref[...] = jnp.zeros_like(acc_ref)

    def step(kc_t, vc_t, kd_t, vd):
        sel = jnp.tile(mask_ref[0, 0].astype(F32), (H_C, 1))
        bias = bias_ref[0]
        s_c = _dot(qc_ref[0], kc_t.astype(BF16)) + sel + bias[0:rows]
        s_d = _dot(qd_ref[0], kd_t.astype(BF16)) + bias[rows:2 * rows]
        for idx, (s, v) in enumerate(((s_c, vc_t), (s_d, vd))):
            m_prev = m_ref[idx]
            m_new = jnp.maximum(m_prev, jnp.max(s, axis=-1, keepdims=True))
            alpha = jnp.exp(m_prev - m_new)
            p = jnp.exp(s - m_new)
            l_ref[idx] = alpha * l_ref[idx] + jnp.sum(p, axis=-1, keepdims=True)
            m_ref[idx] = m_new
            p = p.astype(BF16)
            pv = _dot_nt(p, v.astype(BF16)) if idx == 0 else _dot(p, v.astype(BF16))
            acc_ref[idx] = alpha * acc_ref[idx] + pv

    heads_packed = lambda ref: jnp.concatenate([ref[0, :, h, :] for h in range(H_D)], axis=1)

    @pl.when(j < ncache)
    def _():
        step(ck_ref[0], cv_ref[0], dk_ref[0], heads_packed(dv_ref))

    @pl.when(j == ncache)
    def _():
        step(nck_ref[0], ncv_ref[0], ndk_ref[0], heads_packed(ndv_ref))
        for h in range(H_C):
            r = slice(h * ts, (h + 1) * ts)
            o_ref[:, h * DH:(h + 1) * DH] = (acc_ref[0, r, h * DH:(h + 1) * DH] / l_ref[0, r, :]).astype(o_ref.dtype)
        lam = lam_ref[...]
        g = g_ref[...]
        for h in range(H_D):
            r0 = slice(2 * h * ts, (2 * h + 1) * ts)
            r1 = slice((2 * h + 1) * ts, (2 * h + 2) * ts)
            c = slice(h * 2 * DH, (h + 1) * 2 * DH)
            od = acc_ref[1, r0, c] / l_ref[1, r0, :] - lam * (acc_ref[1, r1, c] / l_ref[1, r1, :])
            od = _rms(od, g) * (1.0 - lam_init)
            o_ref[:, W_C + h * 2 * DH:W_C + (h + 1) * 2 * DH] = od.astype(o_ref.dtype)


def _attend_s(qbd_c, qbd_d, caches, news, mask, bias, lam, g, lam_init):
    nb, rows, _ = qbd_c.shape
    ts = rows // H_C
    lb = KEY_BLOCK
    ncache = caches[3].shape[1] // lb
    qspec = pl.BlockSpec((1, rows, W_C), lambda b, j: (b, 0, 0))
    ctspec = pl.BlockSpec((1, W_C, lb), lambda b, j: (b, 0, jnp.minimum(j, ncache - 1)))
    cspec = pl.BlockSpec((1, lb, H_D, 2 * DH), lambda b, j: (b, jnp.minimum(j, ncache - 1), 0, 0))
    ntspec = pl.BlockSpec((1, W_C, lb), lambda b, j: (b, 0, 0))
    nspec = pl.BlockSpec((1, lb, H_D, 2 * DH), lambda b, j: (b, 0, 0, 0))
    return pl.pallas_call(
        functools.partial(_attn_s_body, ts=ts, ncache=ncache, lam_init=lam_init),
        grid=(nb, ncache + 1),
        in_specs=[qspec, qspec, ctspec, ctspec, ctspec, cspec, ntspec, ntspec, ntspec, nspec,
                  pl.BlockSpec((1, 1, ts, lb), lambda b, j: (b, j, 0, 0)),
                  pl.BlockSpec((1, 2 * rows, lb), lambda b, j: (j, 0, 0)),
                  pl.BlockSpec((1, 1), lambda b, j: (0, 0)),
                  pl.BlockSpec((1, 2 * DH), lambda b, j: (0, 0))],
        out_specs=pl.BlockSpec((ts, W_C + W_D), lambda b, j: (b, 0)),
        out_shape=jax.ShapeDtypeStruct((nb * ts, W_C + W_D), BF16),
        scratch_shapes=[pltpu.VMEM((2, rows, 1), F32), pltpu.VMEM((2, rows, 1), F32),
                        pltpu.VMEM((2, rows, W_C), F32)],
        compiler_params=_cparams("parallel", "arbitrary"),
        name="attend_s",
    )(qbd_c, qbd_d, *caches, *news, mask, bias, lam, g)


def _out_router_body(x_ref, o_ref, w_ref, g_ref, r_ref, x3_ref, xn_ref, gate_ref, gatet_ref):
    x3 = x_ref[...] + _dot(o_ref[...], w_ref[...])
    x3_ref[...] = x3
    xn = _rms(x3, g_ref[...]).astype(BF16)
    xn_ref[...] = xn
    logits = _dot(xn, r_ref[...])
    lane = lax.broadcasted_iota(I32, logits.shape, 1)
    logits = jnp.where(lane < N_EXP, logits, -jnp.inf)
    m1 = jnp.max(logits, axis=-1, keepdims=True)
    i1 = jnp.min(jnp.where(logits == m1, lane, LANES), axis=-1, keepdims=True)
    rest = jnp.where(lane == i1, -jnp.inf, logits)
    m2 = jnp.max(rest, axis=-1, keepdims=True)
    i2 = jnp.min(jnp.where(rest == m2, lane, LANES), axis=-1, keepdims=True)
    e = jnp.exp(m2 - m1)
    g1 = 1.0 / (1.0 + e)
    g2 = e / (1.0 + e)
    gate = jnp.where(lane == i1, g1, 0.0) + jnp.where(lane == i2, g2, 0.0)
    gate_ref[...] = gate
    gatet_ref[...] = gate.T[0:GATE_ROWS, :]


def _out_router(x, o, w, g, router, rows):
    n, d = x.shape
    return pl.pallas_call(
        _out_router_body,
        grid=(n // rows,),
        in_specs=[pl.BlockSpec((rows, d), lambda i: (i, 0)),
                  pl.BlockSpec((rows, d), lambda i: (i, 0)),
                  pl.BlockSpec(w.shape, lambda i: (0, 0)),
                  pl.BlockSpec((1, d), lambda i: (0, 0)),
                  pl.BlockSpec(router.shape, lambda i: (0, 0))],
        out_specs=[pl.BlockSpec((rows, d), lambda i: (i, 0)),
                   pl.BlockSpec((rows, d), lambda i: (i, 0)),
                   pl.BlockSpec((rows, LANES), lambda i: (i, 0)),
                   pl.BlockSpec((GATE_ROWS, rows), lambda i: (0, i))],
        out_shape=[jax.ShapeDtypeStruct((n, d), F32), jax.ShapeDtypeStruct((n, d), BF16),
                   jax.ShapeDtypeStruct((n, LANES), F32), jax.ShapeDtypeStruct((GATE_ROWS, n), F32)],
        compiler_params=_cparams("parallel"),
        name="out_router",
    )(x, o, w, g, router)


def _t5_bucket(rel):
    half = N_BUCKETS // 2
    max_exact = half // 2
    ret = np.where(rel > 0, half, 0)
    n = np.abs(rel)
    nf = np.maximum(n, 1).astype(np.float32)
    large = max_exact + (np.log(nf / np.float32(max_exact)) / np.float32(math.log(MAX_DIST / max_exact))
                         * np.float32(half - max_exact)).astype(np.int32)
    large = np.minimum(large, half - 1)
    return (ret + np.where(n < max_exact, n, large)).astype(np.int32)


def _bias_tile(rel_bias, q_pos, k_pos, k_real):
    rel = k_pos[None, :] - q_pos[:, None]
    onehot = (jnp.asarray(_t5_bucket(rel).astype(np.int8))[:, :, None]
              == jnp.arange(N_BUCKETS, dtype=jnp.int8)).astype(F32)
    bias = jnp.einsum("qkb,bh->hqk", onehot, rel_bias.astype(F32), precision=lax.Precision.HIGHEST)
    ok = np.logical_and(k_pos[None, :] // CHUNK <= q_pos[:, None] // CHUNK, k_pos[None, :] < k_real)
    return jnp.where(ok[None], bias, NEG_INF)


def _in1_plan():
    scale = DH ** -0.5
    heads64 = lambda o, sc=1.0: [(o, h, h * DH, (h + 1) * DH, sc) for h in range(8)]
    plan = [
        ((0, 512), heads64(0, scale)),
        ((512, 1024), [(1, None, 0, 512, 1.0)] + heads64(2)),
        ((1024, 1536), [(3, None, 0, 512, 1.0)] + heads64(4)),
        ((1536, 2048), heads64(5, scale)),
        ((2048, 2560), [(6, None, 0, 512, 1.0)] + heads64(7)),
        ((2560, 3072), [(8, None, 0, 512, 1.0)] + [(9, h, h * 128, (h + 1) * 128, 1.0) for h in range(H_D)]),
        ((3072, 3584), heads64(10)),
        ((3584, 3712), [(11, None, 0, D_I, 1.0), (12, None, 0, D_I, 1.0), (13, None, D_I, D_I + H_I, 1.0)]),
    ]
    out_defs = [(DH, BF16, 8), (512, F32, None), (DH, BF16, 8), (512, F32, None), (DH, BF16, 8),
                (DH, BF16, 8), (512, F32, None), (DH, BF16, 8), (512, F32, None), (2 * DH, BF16, H_D),
                (D_I, BF16, 8), (D_I, F32, None), (D_I, BF16, None), (H_I, F32, None)]
    return plan, out_defs


def _layer0(x, hist, rb, emit_v, p):
    n = x.shape[0]
    rows = _largest_divisor(n, 512, 16)
    plan = [((0, p["w_in0"].shape[1]), [(0, None, 0, p["w_in0"].shape[1], 1.0)])]
    (z,) = _rms_proj(x, p["ln_mix0"], p["w_in0"], plan, [(p["w_in0"].shape[1], F32, None)], rows)
    ws = p["ws_prompt"] if hist is None else p["ws_sample"]
    bs = p["bs_prompt"] if hist is None else p["bs_sample"]
    outs = _mixer_ab(x, z, hist, p["gmlp_ln_g"], p["gmlp_ln_b"], ws, bs, p["conv_k"], p["w_out0"], rb, emit_v)
    x1 = outs[0]
    x2 = _ffn(x1, p["ln_ffn0"], p["ffn_wg"], p["ffn_wu"], p["ffn_wd"], rows, p["ffn_fb"])
    return (x2,) + tuple(outs[1:])


GATE_ROWS = 16
MOE_ROWS = 128


def _moe_routed_body(x_ref, xn_ref, gate_ref, gatet_ref, gf_ref, wg_ref, wu_ref, wd_ref, y_ref,
                     triu_ref, tril_ref, crow_ref, ccol_ref, xs_ref, ge_ref, acc_ref, yblk_ref, nsub_ref,
                     *, tb, sub):
    e = pl.program_id(1)
    k = pl.program_id(2)
    first_k = k == 0
    last_k = k == pl.num_programs(2) - 1

    @pl.when(jnp.logical_and(e == 0, first_k))
    def _():
        r = lax.broadcasted_iota(I32, (tb, tb), 0)
        c = lax.broadcasted_iota(I32, (tb, tb), 1)
        triu_ref[...] = jnp.where(r < c, 1.0, 0.0).astype(BF16)
        tril_ref[...] = jnp.where(c < r, 1.0, 0.0).astype(BF16)
        crow_ref[...] = _dot(jnp.where(gatet_ref[...] > 0.0, 1.0, 0.0).astype(BF16), triu_ref[...])
        ccol_ref[...] = _dot(tril_ref[...], jnp.where(gate_ref[...] > 0.0, 1.0, 0.0).astype(BF16))
        yblk_ref[...] = jnp.zeros_like(yblk_ref)
        acc_ref[...] = jnp.zeros_like(acc_ref)

    @pl.when(first_k)
    def _():
        g_e = gatet_ref[pl.ds(e, 1), :]
        m_e = g_e > 0.0
        c_e = crow_ref[pl.ds(e, 1), :]
        nsub = (jnp.sum(jnp.where(m_e, 1.0, 0.0)).astype(I32) + sub - 1) // sub
        nsub_ref[0] = nsub
        xn = xn_ref[...]
        slot = lax.broadcasted_iota(I32, (sub, tb), 0).astype(F32)

        def pack(s, carry):
            r0 = pl.multiple_of(s * sub, sub)
            hit = jnp.logical_and(m_e, c_e == slot + (s * sub).astype(F32))
            onehot = jnp.where(hit, 1.0, 0.0)
            xs_ref[pl.ds(r0, sub), :] = _dot(onehot.astype(BF16), xn).astype(BF16)
            ge_ref[pl.ds(r0, sub), :] = jnp.sum(onehot * g_e, axis=-1, keepdims=True)
            acc_ref[pl.ds(r0, sub), :] = jnp.zeros((sub, acc_ref.shape[1]), F32)
            return carry

        lax.fori_loop(0, nsub, pack, 0)

    nsub = nsub_ref[0]

    def experts(rows):
        xs = xs_ref[rows, :]
        h = jax.nn.silu(_dot(xs, wg_ref[0])) * _dot(xs, wu_ref[0])
        acc_ref[rows, :] += _dot((ge_ref[rows, :] * h).astype(BF16), wd_ref[0])

    def pair(s, carry):
        experts(pl.ds(pl.multiple_of(s * 2 * sub, 2 * sub), 2 * sub))
        return carry

    lax.fori_loop(0, nsub // 2, pair, 0)
    pl.when(nsub % 2 == 1)(lambda: experts(pl.ds(pl.multiple_of((nsub - 1) * sub, sub), sub)))

    @pl.when(last_k)
    def _():
        lane = lax.broadcasted_iota(I32, (tb, LANES), 1)
        pick = lambda a: jnp.sum(jnp.where(lane == e, a, 0.0), axis=-1, keepdims=True)
        m_e = pick(gate_ref[...]) > 0.0
        c_e = pick(ccol_ref[...])
        slot = lax.broadcasted_iota(I32, (tb, sub), 1).astype(F32)

        def unpack(s, carry):
            rows = pl.ds(pl.multiple_of(s * sub, sub), sub)
            hit = jnp.logical_and(m_e, c_e == slot + (s * sub).astype(F32))
            onehot = jnp.where(hit, 1.0, 0.0).astype(BF16)
            y = acc_ref[rows, :]
            hi = y.astype(BF16)
            lo = (y - hi.astype(F32)).astype(BF16)
            yblk_ref[...] += _dot(jnp.concatenate([onehot, onehot], axis=1), jnp.concatenate([hi, lo], axis=0))
            return carry

        lax.fori_loop(0, nsub, unpack, 0)

    @pl.when(jnp.logical_and(e == pl.num_programs(1) - 1, last_k))
    def _():
        y_ref[...] = _rms(x_ref[...] + yblk_ref[...], gf_ref[...])


def _moe_routed(x, xn, gate, gatet, gf, wg, wu, wd, tb, fb):
    n, d = x.shape
    ne, _, dff = wg.shape
    return pl.pallas_call(
        functools.partial(_moe_routed_body, tb=tb, sub=min(MOE_ROWS, tb)),
        grid=(n // tb, ne, dff // fb),
        in_specs=[pl.BlockSpec((tb, d), lambda i, e, k: (i, 0)),
                  pl.BlockSpec((tb, d), lambda i, e, k: (i, 0)),
                  pl.BlockSpec((tb, LANES), lambda i, e, k: (i, 0)),
                  pl.BlockSpec((GATE_ROWS, tb), lambda i, e, k: (0, i)),
                  pl.BlockSpec((1, d), lambda i, e, k: (0, 0)),
                  pl.BlockSpec((1, d, fb), lambda i, e, k: (e, 0, k)),
                  pl.BlockSpec((1, d, fb), lambda i, e, k: (e, 0, k)),
                  pl.BlockSpec((1, fb, d), lambda i, e, k: (e, k, 0))],
        out_specs=pl.BlockSpec((tb, d), lambda i, e, k: (i, 0)),
        out_shape=jax.ShapeDtypeStruct((n, d), F32),
        scratch_shapes=[pltpu.VMEM((tb, tb), BF16), pltpu.VMEM((tb, tb), BF16),
                        pltpu.VMEM((GATE_ROWS, tb), F32), pltpu.VMEM((tb, LANES), F32),
                        pltpu.VMEM((tb, d), BF16), pltpu.VMEM((tb, 1), F32),
                        pltpu.VMEM((tb, d), F32), pltpu.VMEM((tb, d), F32),
                        pltpu.SMEM((1,), I32)],
        compiler_params=_cparams("parallel", "arbitrary", "arbitrary"),
        name="moe_routed",
    )(x, xn, gate, gatet, gf, wg, wu, wd)


def _layer1_tail(x, o, p):
    n = x.shape[0]
    rows = _largest_divisor(n, 512, 16)
    x3, xn, gate, gatet = _out_router(x, o, p["w_out1"], p["ln_ffn1"], p["router"], rows)
    return _moe_routed(x3, xn, gate, gatet, p["ln_final"], p["exp_wg"], p["exp_wu"], p["exp_wd"],
                       _largest_divisor(n, 1024, 128), p["exp_fb"])


def kernel(x_prompt, x_sample, state_b_conv, cache_c_k, cache_c_v, cache_idx_k, cache_d_k, cache_d_v, rel_bias, ln_mix, ln_ffn, ln_final, w_in0, gmlp_ln_g, gmlp_ln_b, gmlp_ws, gmlp_bs, conv_k, w_out0, ffn_wg, ffn_wu, ffn_wd, w_in1, lam_qk, subln_g, w_out1, router, exp_wg, exp_wu, exp_wd):
    bp, seq, d = x_prompt.shape
    bs_, ts, _ = x_sample.shape
    past = cache_c_k.shape[2]
    assert bp == 1 and ln_mix.shape[0] == 2 and seq % PROMPT_LB == 0 and ts % SUBLANES == 0 and ts <= CHUNK
    assert past % KEY_BLOCK == 0
    gmlp_chunk = gmlp_ws.shape[-1]
    lam_init = 0.8 - 0.6 * math.exp(-0.3 * 1)

    def ws_masked(rows):
        r = jnp.arange(rows)
        ok = (r[None, :] // CHUNK) <= (r[:, None] // CHUNK)
        return jnp.where(ok[None], gmlp_ws[0][:, :rows, :rows], 0.0).astype(BF16)

    in1_pad = (-w_in1.shape[2]) % LANES
    lf = lam_qk[0].astype(F32)
    lam = (jnp.exp(jnp.sum(lf[0] * lf[1])) - jnp.exp(jnp.sum(lf[2] * lf[3])) + lam_init).reshape(1, 1)
    p = {
        "ln_mix0": ln_mix[0:1], "ln_ffn0": ln_ffn[0:1], "ln_mix1": ln_mix[1:2], "ln_ffn1": ln_ffn[1:2],
        "ln_final": ln_final.reshape(1, d),
        "w_in0": w_in0[0].astype(BF16),
        "gmlp_ln_g": gmlp_ln_g[0].reshape(1, W_A), "gmlp_ln_b": gmlp_ln_b[0].reshape(1, W_A),
        "ws_prompt": ws_masked(gmlp_chunk), "ws_sample": ws_masked(ts),
        "bs_prompt": gmlp_bs[0][:, :gmlp_chunk].T, "bs_sample": gmlp_bs[0][:, :ts].T,
        "conv_k": conv_k[0], "w_out0": w_out0[0].astype(BF16),
        "ffn_wg": ffn_wg[0].astype(BF16), "ffn_wu": ffn_wu[0].astype(BF16), "ffn_wd": ffn_wd[0].astype(BF16),
        "ffn_fb": _largest_divisor(ffn_wg.shape[2], 1408, LANES),
        "w_in1": jnp.pad(w_in1[0], ((0, 0), (0, in1_pad))).astype(BF16),
        "w_out1": w_out1[0].astype(BF16),
        "router": jnp.pad(router[0], ((0, 0), (0, LANES - N_EXP))).astype(BF16),
        "exp_wg": exp_wg[0].astype(BF16), "exp_wu": exp_wu[0].astype(BF16), "exp_wd": exp_wd[0].astype(BF16),
        "exp_fb": _largest_divisor(exp_wg.shape[3], 896, LANES),
    }
    g_sub = subln_g[0].reshape(1, 2 * DH)
    plan1, out_defs1 = _in1_plan()

    xp = x_prompt.reshape(seq, d)
    xs = x_sample.reshape(bs_ * ts, d)
    xp, p_tail = _layer0(xp, None, gmlp_chunk, False, p)
    hist = jnp.pad(state_b_conv[0], ((0, 0), (SUBLANES - 2, 0), (0, 0)))
    xs, s_tail, s_av = _layer0(xs, hist, ts, True, p)
    p_b_conv = p_tail[-1, SUBLANES - 2:, :].reshape(1, 1, 2, W_B)
    s_b_conv = s_tail[:, SUBLANES - 2:, :].reshape(1, bs_, 2, W_B)
    s_a_v = s_av.reshape(1, bs_, ts, W_A)

    lb = KEY_BLOCK
    w1 = w_in1[0]
    off = np.concatenate([[0], np.cumsum(IN1_SIZES)])
    field = lambda k: w1[:, off[k]:off[k + 1]]
    padc = lambda a: jnp.pad(a, ((0, 0), (0, LANES - a.shape[1])))
    log2e = math.log2(math.e)
    qscale = DH ** -0.5 * log2e
    w_norm = jnp.concatenate([field(1), field(4), field(5), padc(field(7))], axis=1).astype(BF16)
    w_tran = jnp.concatenate([field(0) * qscale, field(3) * qscale, field(6), field(2), field(5),
                              field(1), field(4), padc(field(8))], axis=1).T.astype(BF16)
    (vd32, ki32, kc, kd, ki, qct, qdt, qit, vct, vdt, vc32t, kc32t, kd32t, wit) = _proj1_t(
        xp, p["ln_mix1"], w_norm, w_tran, PROMPT_TQ)
    token_major = lambda a, *dims: jnp.moveaxis(a.reshape(dims + (seq,)), -1, 0).reshape((1, 1, seq) + dims)
    p_c_k, p_c_v = token_major(kc32t, H_C, DH), token_major(vc32t, H_C, DH)
    p_d_k = token_major(kd32t, H_D, 2, DH)
    pos = jnp.arange(seq, dtype=I32)
    ve = ((pos // CHUNK + 1) * CHUNK).reshape(1, seq)
    mask = _select_t(qit, wit, ve, ki, min(TOPK_MAX, seq // 4))
    e_far = -(-(PROMPT_LB - 1 + MAX_DIST) // ATTN_TQ)
    kpos = np.arange(PROMPT_LB)
    qpos = np.arange(ATTN_TQ)
    base = e_far * ATTN_TQ
    rel_near = (rel_bias - rel_bias[N_BUCKETS // 2 - 1:N_BUCKETS // 2]) * log2e
    btiles = jnp.stack([jnp.transpose(_bias_tile(rel_near, base + e * ATTN_TQ + qpos, base + kpos, base + PROMPT_LB),
                                      (0, 2, 1)) for e in range(e_far)]).astype(BF16)
    op = _attend_t(qct, qdt, kc, kd, vct, vdt, mask, btiles, lam, subln_g[0].reshape(2 * DH, 1), lam_init)
    y_prompt = _layer1_tail(xp, op, p).reshape(1, seq, d)

    sr = _rms_proj(xs, p["ln_mix1"], p["w_in1"], plan1, out_defs1, _largest_divisor(bs_ * ts, 256, 16))
    sqc, skc32, skc, svc32, svc, sqd, skd32, skd, svd32, svd, sqi, ski32, ski, swi = sr
    nk = past + ts
    lps = -(-nk // lb) * lb

    kis = jnp.pad(jnp.concatenate([cache_idx_k[0].astype(BF16), ski.reshape(bs_, ts, D_I)], axis=1),
                  ((0, 0), (0, lps - nk), (0, 0)))
    ves = jnp.full((bs_ * ts, 1), nk, I32)
    smask = _select(sqi, swi, ves, kis, bs_, ts, ts, min(TOPK_MAX, nk // 4), lambda i: lps // lb)
    sq_pos = past + np.arange(ts)
    sbt = jnp.stack([_bias_tile(rel_bias, sq_pos, j * lb + np.arange(lb), nk) for j in range(lps // lb)])
    sbias = jnp.concatenate([sbt[:, :H_C], jnp.repeat(sbt[:, H_C:], 2, axis=1)], axis=1).reshape(lps // lb, N_MAPS * ts, lb)

    def block_diag(q):
        qb = jnp.transpose(q.reshape(H_C, bs_, ts, DH), (1, 0, 2, 3))
        eye = jnp.eye(H_C, dtype=q.dtype)
        return (qb[:, :, :, None, :] * eye[None, :, None, :, None]).reshape(bs_, H_C * ts, W_C)

    def cache_rows(c, feature_major):
        if not feature_major:
            return c
        return jnp.transpose(c.reshape(bs_, past, W_C), (0, 2, 1))

    def new_rows(a, feature_major):
        a = jnp.pad(a.reshape(bs_, ts, W_C), ((0, 0), (0, lb - ts), (0, 0)))
        return jnp.transpose(a, (0, 2, 1)) if feature_major else a.reshape(bs_, lb, H_D, 2 * DH)
    os_ = _attend_s(block_diag(sqc), block_diag(sqd),
                    [cache_rows(c[0], fm) for c, fm in ((cache_c_k, True), (cache_c_v, True), (cache_d_k, True), (cache_d_v, False))],
                    [new_rows(a, fm) for a, fm in ((skc32, True), (svc32, True), (skd32, True), (svd32, False))],
                    smask, sbias, lam, g_sub, lam_init)
    y_sample = _layer1_tail(xs, os_, p).reshape(bs_, ts, d)

    r5 = lambda a, n, t, *tail: a.reshape((1, n, t) + tail)
    return (y_prompt, y_sample, p_b_conv,
            p_c_k, p_c_v, r5(ki32, 1, seq, D_I), p_d_k, r5(vd32, 1, seq, H_D, 2 * DH),
            s_a_v, s_b_conv,
            r5(skc32, bs_, ts, H_C, DH), r5(svc32, bs_, ts, H_C, DH), r5(ski32, bs_, ts, D_I),
            r5(skd32, bs_, ts, H_D, 2, DH), r5(svd32, bs_, ts, H_D, 2 * DH))
```

```python
import functools
import math

import jax
import jax.numpy as jnp
import numpy as np
from jax import lax
from jax.experimental import pallas as pl
from jax.experimental.pallas import tpu as pltpu

F32 = jnp.float32
BF16 = jnp.bfloat16
I32 = jnp.int32

CHUNK = 64
EPS = 1e-6
NEG_INF = -1e30
H_A = 4
C_A = 128
W_A = H_A * C_A
W_B = 512
H_C = 8
DH = 64
W_C = H_C * DH
H_I = 8
D_I = 64
TOPK_MAX = 256
H_D = 4
W_D = H_D * 2 * DH
N_BUCKETS = 32
MAX_DIST = 128
N_EXP = 8
IN1_SIZES = (W_C, W_C, W_C, W_D, W_D, W_D, H_I * D_I, D_I, H_I)
N_HEADS_BIAS = H_C + H_D
N_MAPS = H_C + 2 * H_D

LANES = 128
SUBLANES = 8
VMEM_LIMIT = 56 * 1024 * 1024

INT_MIN = -(2 ** 31)
INT_MAX = 2 ** 31 - 1
ALL_TIES = 2 ** 30

KEY_BLOCK = 512


def _cparams(*sem):
    return pltpu.CompilerParams(dimension_semantics=sem, vmem_limit_bytes=VMEM_LIMIT)


def _largest_divisor(n, target, mult):
    if n <= target:
        return n
    d = (target // mult) * mult
    while d >= mult:
        if n % d == 0:
            return d
        d -= mult
    raise ValueError(f"no block of multiple {mult} divides {n}")


def _rms(x, g):
    return x * lax.rsqrt(jnp.mean(x * x, axis=-1, keepdims=True) + EPS) * g


def _dot(a, b):
    return jnp.dot(a, b, preferred_element_type=F32)


def _dot_nt(a, b):
    return lax.dot_general(a, b, (((1,), (1,)), ((), ())), preferred_element_type=F32)


def _rms_proj_body(x_ref, g_ref, w_ref, *out_refs, plan):
    xn = _rms(x_ref[...], g_ref[...]).astype(BF16)
    for (c0, c1), writes in plan:
        z = _dot(xn, w_ref[:, c0:c1])
        for o_idx, head, z0, z1, scale in writes:
            val = z[:, z0:z1]
            if scale != 1.0:
                val = val * scale
            ref = out_refs[o_idx]
            if head is None:
                ref[...] = val.astype(ref.dtype)
            else:
                ref[head] = val.astype(ref.dtype)


def _rms_proj(x, g, w, plan, out_defs, rows):
    n, d = x.shape
    grid = (n // rows,)
    out_shape, out_specs = [], []
    for width, dtype, heads in out_defs:
        if heads is None:
            out_shape.append(jax.ShapeDtypeStruct((n, width), dtype))
            out_specs.append(pl.BlockSpec((rows, width), lambda i: (i, 0)))
        else:
            out_shape.append(jax.ShapeDtypeStruct((heads, n, width), dtype))
            out_specs.append(pl.BlockSpec((heads, rows, width), lambda i: (0, i, 0)))
    return pl.pallas_call(
        functools.partial(_rms_proj_body, plan=plan),
        grid=grid,
        in_specs=[pl.BlockSpec((rows, d), lambda i: (i, 0)),
                  pl.BlockSpec((1, d), lambda i: (0, 0)),
                  pl.BlockSpec(w.shape, lambda i: (0, 0))],
        out_specs=out_specs,
        out_shape=out_shape,
        compiler_params=_cparams("parallel"),
        name="rms_proj",
    )(x, g, w)


def _mixer_ab_body(*refs, rb, from_prev, emit_v):
    it = iter(refs)
    x_ref, u_ref, v_ref, gb_ref, gc_ref, xin_ref = (next(it) for _ in range(6))
    if from_prev:
        gcp_ref, xinp_ref = next(it), next(it)
    else:
        hist_ref = next(it)
    lng_ref, lnb_ref, ws_ref, bs_ref, ck_ref, wout_ref = (next(it) for _ in range(6))
    x1_ref, tail_ref = next(it), next(it)
    vout_ref = next(it) if emit_v else None
    wext_ref = next(it)

    w = gc_ref[...] * xin_ref[...]
    if from_prev:
        hist = jnp.where(pl.program_id(0) > 0, gcp_ref[...] * xinp_ref[...], 0.0)
    else:
        hist = hist_ref[0]
    wext_ref[0:SUBLANES, :] = hist
    wext_ref[SUBLANES:, :] = w
    ck = ck_ref[...]
    conv = (ck[0:1] * wext_ref[SUBLANES - 2:SUBLANES - 2 + rb, :]
            + ck[1:2] * wext_ref[SUBLANES - 1:SUBLANES - 1 + rb, :]
            + ck[2:3] * w)
    y_b = gb_ref[...] * conv
    tail_ref[0] = w[rb - SUBLANES:, :]

    u = jax.nn.gelu(u_ref[...])
    v = jax.nn.gelu(v_ref[...])
    lng = lng_ref[...]
    lnb = lnb_ref[...]
    bs = bs_ref[...]
    acc = _dot(y_b.astype(BF16), wout_ref[W_A:, :])
    for h in range(H_A):
        sl = slice(h * C_A, (h + 1) * C_A)
        vh = v[:, sl]
        mu = jnp.mean(vh, axis=-1, keepdims=True)
        xc = vh - mu
        var = jnp.mean(xc * xc, axis=-1, keepdims=True)
        vln = xc * lax.rsqrt(var + EPS) * lng[:, sl] + lnb[:, sl]
        if emit_v:
            vout_ref[:, sl] = vln
        s = _dot(ws_ref[h], vln.astype(BF16)) + bs[:, h:h + 1]
        y_a = u[:, sl] * s
        acc = acc + _dot(y_a.astype(BF16), wout_ref[sl, :])
    x1_ref[...] = x_ref[...] + acc


def _mixer_ab(x, z, hist, lng, lnb, ws, bs, ck, wout, rb, emit_v):
    n, d = x.shape
    nb = n // rb
    from_prev = hist is None
    col = lambda c: pl.BlockSpec((rb, 512), lambda i, c=c: (i, c))
    in_specs = [pl.BlockSpec((rb, d), lambda i: (i, 0)), col(0), col(1), col(2), col(3), col(4)]
    args = [x, z, z, z, z, z]
    if from_prev:
        per = rb // SUBLANES
        prev = lambda c: pl.BlockSpec((SUBLANES, 512), lambda i, c=c: (jnp.maximum(i * per - 1, 0), c))
        in_specs += [prev(3), prev(4)]
        args += [z, z]
    else:
        in_specs += [pl.BlockSpec((1, SUBLANES, 512), lambda i: (i, 0, 0))]
        args += [hist]
    const = lambda a: pl.BlockSpec(a.shape, lambda i, nd=a.ndim: (0,) * nd)
    for a in (lng, lnb, ws, bs, ck, wout):
        in_specs.append(const(a))
        args.append(a)
    out_shape = [jax.ShapeDtypeStruct((n, d), F32), jax.ShapeDtypeStruct((nb, SUBLANES, 512), F32)]
    out_specs = [pl.BlockSpec((rb, d), lambda i: (i, 0)), pl.BlockSpec((1, SUBLANES, 512), lambda i: (i, 0, 0))]
    if emit_v:
        out_shape.append(jax.ShapeDtypeStruct((n, W_A), F32))
        out_specs.append(pl.BlockSpec((rb, W_A), lambda i: (i, 0)))
    return pl.pallas_call(
        functools.partial(_mixer_ab_body, rb=rb, from_prev=from_prev, emit_v=emit_v),
        grid=(nb,),
        in_specs=in_specs,
        out_specs=out_specs,
        out_shape=out_shape,
        scratch_shapes=[pltpu.VMEM((rb + SUBLANES, 512), F32)],
        compiler_params=_cparams("arbitrary"),
        name="mixer_ab",
    )(*args)


def _ffn_body(x_ref, g_ref, wg_ref, wu_ref, wd_ref, o_ref, xn_ref, acc_ref):
    k = pl.program_id(1)

    @pl.when(k == 0)
    def _():
        xn_ref[...] = _rms(x_ref[...], g_ref[...]).astype(BF16)
        acc_ref[...] = jnp.zeros_like(acc_ref)

    xn = xn_ref[...]
    h = jax.nn.silu(_dot(xn, wg_ref[...])) * _dot(xn, wu_ref[...])
    acc_ref[...] += _dot(h.astype(BF16), wd_ref[...])

    @pl.when(k == pl.num_programs(1) - 1)
    def _():
        o_ref[...] = x_ref[...] + acc_ref[...]


def _ffn(x, g, wg, wu, wd, rows, fb):
    n, d = x.shape
    dff = wg.shape[1]
    return pl.pallas_call(
        _ffn_body,
        grid=(n // rows, dff // fb),
        in_specs=[pl.BlockSpec((rows, d), lambda i, k: (i, 0)),
                  pl.BlockSpec((1, d), lambda i, k: (0, 0)),
                  pl.BlockSpec((d, fb), lambda i, k: (0, k)),
                  pl.BlockSpec((d, fb), lambda i, k: (0, k)),
                  pl.BlockSpec((fb, d), lambda i, k: (k, 0))],
        out_specs=pl.BlockSpec((rows, d), lambda i, k: (i, 0)),
        out_shape=jax.ShapeDtypeStruct((n, d), F32),
        scratch_shapes=[pltpu.VMEM((rows, d), BF16), pltpu.VMEM((rows, d), F32)],
        compiler_params=_cparams("parallel", "arbitrary"),
        name="ffn",
    )(x, g, wg, wu, wd)


def _select_body(qi_ref, wi_ref, ve_ref, ki_ref, out_ref, keys_ref, *, tq, lb, nkb, topk, nvalid_fn):
    nv = nvalid_fn(pl.program_id(1))
    wi = wi_ref[...]
    ve = ve_ref[...]
    q_all = jnp.concatenate([qi_ref[h] for h in range(H_I)], axis=0)
    wcols = [wi[:, h:h + 1] for h in range(H_I)]
    lane = lax.broadcasted_iota(I32, (tq, lb), 1)

    def score_block(b, carry):
        kb = ki_ref[0, pl.ds(pl.multiple_of(b * lb, lb), lb), :]
        dots = _dot_nt(q_all, kb)
        sc = jnp.zeros((tq, lb), F32)
        for h in range(H_I):
            sc = sc + wcols[h] * jnp.maximum(dots[h * tq:(h + 1) * tq, :], 0.0)
        bits = lax.bitcast_convert_type(sc, I32)
        key = bits ^ ((bits >> 31) & INT_MAX)
        key = jnp.where(sc == 0.0, 0, key)
        key = jnp.where(lane + b * lb < ve, key, INT_MIN)
        keys_ref[b] = key
        return carry

    lax.fori_loop(0, nv, score_block, 0)

    def count_ge(mid):
        midb = jnp.broadcast_to(mid, (tq, LANES))

        def body(b, acc):
            for c in range(lb // LANES):
                k = keys_ref[b, :, c * LANES:(c + 1) * LANES]
                acc = acc + jnp.where(k >= midb, 1, 0)
            return acc

        acc = lax.fori_loop(0, nv, body, jnp.zeros((tq, LANES), I32))
        return jnp.sum(acc.astype(F32), axis=-1, keepdims=True).astype(I32)

    def bisect(_, st):
        lo, hi, clo, chi = st
        mid = (lo >> 1) + (hi >> 1) + (lo & hi & 1)
        cnt = count_ge(mid)
        active = mid != lo
        up = jnp.logical_and(active, cnt >= topk)
        dn = jnp.logical_and(active, cnt < topk)
        return (jnp.where(up, mid, lo), jnp.where(dn, mid, hi),
                jnp.where(up, cnt, clo), jnp.where(dn, cnt, chi))

    full = lambda v: jnp.full((tq, 1), v, I32)
    lo, hi, clo, chi = lax.fori_loop(0, 32, bisect, (full(INT_MIN), full(INT_MAX), ve, full(0)))

    need = jnp.where(clo > topk, topk - chi, ALL_TIES)
    need = jnp.where(lo == INT_MIN, 0, need).astype(F32)
    lob = jnp.broadcast_to(lo, (tq, LANES))
    needb = jnp.broadcast_to(need, (tq, LANES))
    r = lax.broadcasted_iota(I32, (LANES, LANES), 0)
    c = lax.broadcasted_iota(I32, (LANES, LANES), 1)
    tri = jnp.where(r <= c, 1.0, 0.0).astype(BF16)

    def mask_block(b, seen):
        for cc in range(lb // LANES):
            k = keys_ref[b, :, cc * LANES:(cc + 1) * LANES]
            eq = jnp.where(k == lob, 1.0, 0.0)
            cum = _dot(eq.astype(BF16), tri)
            rank = seen + cum - eq
            take = jnp.where(rank < needb, eq, 0.0)
            sel = jnp.where(k > lob, 1.0, take)
            out_ref[0, b, :, cc * LANES:(cc + 1) * LANES] = jnp.where(sel > 0.5, 0.0, NEG_INF).astype(out_ref.dtype)
            seen = seen + cum[:, LANES - 1:LANES]
        return seen

    lax.fori_loop(0, nv, mask_block, jnp.zeros((tq, 1), F32))

    def fill_block(b, carry):
        out_ref[0, b] = jnp.full((tq, lb), NEG_INF, out_ref.dtype)
        return carry

    lax.fori_loop(nv, nkb, fill_block, 0)


def _select(qi, wi, ve, ki, nbatch, t, tq, topk, nvalid_fn):
    lp = ki.shape[1]
    lb = KEY_BLOCK
    nkb = lp // lb
    nq = t // tq
    row = lambda b, i: b * nq + i
    return pl.pallas_call(
        functools.partial(_select_body, tq=tq, lb=lb, nkb=nkb, topk=topk, nvalid_fn=nvalid_fn),
        grid=(nbatch, nq),
        in_specs=[pl.BlockSpec((H_I, tq, D_I), lambda b, i: (0, row(b, i), 0)),
                  pl.BlockSpec((tq, H_I), lambda b, i: (row(b, i), 0)),
                  pl.BlockSpec((tq, 1), lambda b, i: (row(b, i), 0)),
                  pl.BlockSpec((1, lp, D_I), lambda b, i: (b, 0, 0))],
        out_specs=pl.BlockSpec((1, nkb, tq, lb), lambda b, i: (b, 0, i, 0)),
        out_shape=jax.ShapeDtypeStruct((nbatch, nkb, t, lb), BF16),
        scratch_shapes=[pltpu.VMEM((nkb, tq, lb), I32)],
        compiler_params=_cparams("parallel", "arbitrary"),
        name="index_select",
    )(qi, wi, ve, ki)


PROMPT_TQ = 256
ATTN_TQ = 512
PROMPT_LB = 512
SEL_ROWS = 256
TIE_ROWS = 128
MAPS_PER_DOT = 4
PV_ROWS = 256
ONES_ROWS = 16


def _proj1_t_body(x_ref, g_ref, w_ref, wt_ref, vd32_ref, ki32_ref, kc_ref, kd_ref, ki_ref,
                  qct_ref, qdt_ref, qit_ref, vct_ref, vdt_ref, vc32t_ref, kc32t_ref, kd32t_ref, wit_ref):
    xn32 = _rms(x_ref[...], g_ref[...])
    xn = xn32.astype(BF16)
    xnt = xn32.T.astype(BF16)
    kc_ref[...] = _dot(xn, w_ref[:, 0:512]).astype(BF16)
    kd_ref[...] = _dot(xn, w_ref[:, 512:1024]).astype(BF16)
    vd32_ref[...] = _dot(xn, w_ref[:, 1024:1536])
    z = _dot(xn, w_ref[:, 1536:1536 + LANES])
    ki32_ref[...] = z[:, 0:D_I]
    ki_ref[...] = z[:, 0:D_I].astype(BF16)
    for c, (head_ref, full_ref) in enumerate(((qct_ref, None), (qdt_ref, None), (qit_ref, None), (vct_ref, vc32t_ref),
                                              (vdt_ref, None), (None, kc32t_ref), (None, kd32t_ref))):
        zt = _dot(wt_ref[c * 512:(c + 1) * 512, :], xnt)
        if full_ref is not None:
            full_ref[...] = zt
        if head_ref is not None:
            nh = head_ref.shape[0]
            w = 512 // nh
            for h in range(nh):
                head_ref[h, 0:w, :] = zt[h * w:(h + 1) * w, :].astype(BF16)
                if head_ref.shape[1] > w:
                    pad = lax.broadcasted_iota(I32, (head_ref.shape[1] - w, zt.shape[1]), 0)
                    head_ref[h, w:, :] = jnp.where(pad == 0, 1.0, 0.0).astype(BF16)
    zt = _dot(wt_ref[3584:3584 + LANES, :], xnt)
    wit_ref[...] = zt[0:H_I, :]


def _proj1_t(x, g, w, wt, rows):
    n, d = x.shape
    full = lambda width: (jax.ShapeDtypeStruct((n, width), F32), pl.BlockSpec((rows, width), lambda i: (i, 0)))
    packed = lambda width: (jax.ShapeDtypeStruct((n, width), BF16), pl.BlockSpec((rows, width), lambda i: (i, 0)))
    heads_t = lambda nh, width: (jax.ShapeDtypeStruct((nh, width, n), BF16),
                                 pl.BlockSpec((nh, width, rows), lambda i: (0, 0, i)))
    full_t = lambda: (jax.ShapeDtypeStruct((512, n), F32), pl.BlockSpec((512, rows), lambda i: (0, i)))
    outs = [full(512), full(D_I), packed(512), packed(512), packed(D_I),
            heads_t(8, DH), heads_t(8, DH), heads_t(8, DH), heads_t(8, DH + ONES_ROWS), heads_t(H_D, 2 * DH + ONES_ROWS),
            full_t(), full_t(), full_t(),
            (jax.ShapeDtypeStruct((H_I, n), F32), pl.BlockSpec((H_I, rows), lambda i: (0, i)))]
    return pl.pallas_call(
        _proj1_t_body,
        grid=(n // rows,),
        in_specs=[pl.BlockSpec((rows, d), lambda i: (i, 0)),
                  pl.BlockSpec((1, d), lambda i: (0, 0)),
                  pl.BlockSpec(w.shape, lambda i: (0, 0)),
                  pl.BlockSpec(wt.shape, lambda i: (0, 0))],
        out_specs=[o[1] for o in outs],
        out_shape=[o[0] for o in outs],
        compiler_params=_cparams("parallel"),
        name="proj1_t",
    )(x, g, w, wt)


def _sublane_all(x8, op):
    for shift in (4, 2, 1):
        x8 = op(x8, pltpu.roll(x8, shift, 0))
    return x8


def _select_t_body(qit_ref, wit_ref, ve_ref, ki_ref, out_ref, keys_ref, gmax_ref, *, tq, lp, topk):
    rb = SEL_ROWS
    nv = (pl.program_id(0) + 1) * (tq // rb)
    wit = wit_ref[...]
    ve = ve_ref[...]
    rows = lax.broadcasted_iota(I32, (rb, tq), 0)
    gmax_ref[...] = jnp.full((rb, tq), INT_MIN, I32)
    q_all = jnp.concatenate([qit_ref[h] for h in range(H_I)], axis=1)

    def score_chunk(c, carry):
        r0 = pl.multiple_of(c * rb, rb)
        dots = _dot(ki_ref[pl.ds(r0, rb), :], q_all)
        sc = jnp.zeros((rb, tq), F32)
        for h in range(H_I):
            sc = sc + wit[h:h + 1, :] * jnp.maximum(dots[:, h * tq:(h + 1) * tq], 0.0)
        bits = lax.bitcast_convert_type(sc, I32)
        key = bits ^ ((bits >> 31) & INT_MAX)
        key = jnp.where(sc == 0.0, 0, key)
        key = jnp.where(rows + r0 < ve, key, INT_MIN)
        keys_ref[pl.ds(r0, rb), :] = key
        gmax_ref[...] = jnp.maximum(gmax_ref[...], key)
        return carry

    lax.fori_loop(0, nv, score_chunk, 0)

    def count_ge(mid):
        midb = jnp.broadcast_to(mid, (SUBLANES, tq))

        def body(c, accs):
            kc = keys_ref[pl.ds(pl.multiple_of(c * rb, rb), rb), :]
            accs = list(accs)
            for g in range(rb // SUBLANES):
                a = g % len(accs)
                accs[a] = accs[a] + jnp.where(kc[g * SUBLANES:(g + 1) * SUBLANES, :] >= midb, 1, 0)
            return tuple(accs)

        accs = lax.fori_loop(0, nv, body, (jnp.zeros((SUBLANES, tq), I32),) * 4)
        return _sublane_all(accs[0] + accs[1] + accs[2] + accs[3], jnp.add)[0:1, :]

    floor_avg = lambda a, b: (a >> 1) + (b >> 1) + (a & b & 1)

    def pending(lo, hi, clo):
        open_ = jnp.logical_and(floor_avg(lo, hi) != lo, clo > topk)
        return jnp.max(jnp.where(open_, 1.0, 0.0))

    def bisect(st):
        lo, hi, clo, chi, _ = st
        mid = floor_avg(lo, hi)
        cnt = count_ge(mid)
        active = jnp.logical_and(mid != lo, clo > topk)
        up = jnp.logical_and(active, cnt >= topk)
        dn = jnp.logical_and(active, cnt < topk)
        lo, hi = jnp.where(up, mid, lo), jnp.where(dn, mid, hi)
        clo, chi = jnp.where(up, cnt, clo), jnp.where(dn, cnt, chi)
        return lo, hi, clo, chi, pending(lo, hi, clo)

    g8 = gmax_ref[0:SUBLANES, :]
    h8 = g8
    for g in range(1, rb // SUBLANES):
        blk = gmax_ref[g * SUBLANES:(g + 1) * SUBLANES, :]
        g8 = jnp.minimum(g8, blk)
        h8 = jnp.maximum(h8, blk)
    lo0 = _sublane_all(g8, jnp.minimum)[0:1, :]
    hi0 = _sublane_all(h8, jnp.maximum)[0:1, :] + 1
    clo0 = jnp.where(lo0 == INT_MIN, ve, count_ge(lo0))
    chi0 = jnp.zeros((1, tq), I32)

    def probe_zero():
        c_pos, c_nonneg = count_ge(jnp.full((1, tq), 1, I32)), count_ge(jnp.zeros((1, tq), I32))
        above = c_pos >= topk
        below = c_nonneg < topk
        lo1 = jnp.where(above, jnp.maximum(lo0, 1), jnp.where(below, lo0, 0))
        clo1 = jnp.where(above, jnp.where(lo0 >= 1, clo0, c_pos), jnp.where(below, clo0, c_nonneg))
        hi1 = jnp.where(above, hi0, jnp.where(below, jnp.minimum(hi0, 0), 1))
        chi1 = jnp.where(above, chi0, jnp.where(below, jnp.where(hi0 <= 0, chi0, c_nonneg), c_pos))
        return lo1, hi1, clo1, chi1

    lo1, hi1, clo1, chi1 = lax.cond(jnp.max(jnp.where(lo0 < 1, 1.0, 0.0)) > 0.5, probe_zero,
                                    lambda: (lo0, hi0, clo0, chi0))
    lo, hi, clo, chi, _ = lax.while_loop(
        lambda st: st[4] > 0.5, bisect, (lo1, hi1, clo1, chi1, pending(lo1, hi1, clo1)))

    need = jnp.where(clo > topk, topk - chi, ALL_TIES)
    need = jnp.where(lo == INT_MIN, 0, need).astype(F32)
    tr = TIE_ROWS
    lob = jnp.broadcast_to(lo, (tr, tq))
    needb = jnp.broadcast_to(need, (tr, tq))
    r = lax.broadcasted_iota(I32, (tr, tr), 0)
    c = lax.broadcasted_iota(I32, (tr, tr), 1)
    tri = jnp.where(c <= r, 1.0, 0.0).astype(BF16)

    def mask_chunk(cidx, seen):
        r0 = pl.multiple_of(cidx * tr, tr)
        k = keys_ref[pl.ds(r0, tr), :]
        eq = jnp.where(k == lob, 1.0, 0.0)
        cum = _dot(tri, eq.astype(BF16))
        take = jnp.where(seen + cum - eq < needb, eq, 0.0)
        sel = jnp.where(k > lob, 1.0, take)
        out_ref[pl.ds(r0, tr), :] = jnp.where(sel > 0.5, 0.0, NEG_INF).astype(out_ref.dtype)
        return seen + cum[tr - 1:tr, :]

    def plain_chunk(cidx, carry):
        r0 = pl.multiple_of(cidx * rb, rb)
        k = keys_ref[pl.ds(r0, rb), :]
        hit = jnp.where(k == INT_MIN, NEG_INF, 0.0)
        out_ref[pl.ds(r0, rb), :] = jnp.where(k >= lo, hit, NEG_INF).astype(out_ref.dtype)
        return carry

    any_tie = jnp.max(jnp.where(clo > topk, 1.0, 0.0)) > 0.5
    lax.cond(any_tie,
             lambda: lax.fori_loop(0, nv * (rb // tr), mask_chunk, jnp.zeros((1, tq), F32)),
             lambda: lax.fori_loop(0, nv, plain_chunk, jnp.zeros((1, tq), F32)))

    def fill_chunk(cidx, carry):
        out_ref[pl.ds(pl.multiple_of(cidx * rb, rb), rb), :] = jnp.full((rb, tq), NEG_INF, out_ref.dtype)
        return carry

    lax.fori_loop(nv, lp // rb, fill_chunk, 0)


def _select_t(qit, wit, ve, ki, topk):
    t = qit.shape[2]
    tq = PROMPT_TQ
    return pl.pallas_call(
        functools.partial(_select_t_body, tq=tq, lp=t, topk=topk),
        grid=(t // tq,),
        in_specs=[pl.BlockSpec((H_I, D_I, tq), lambda i: (0, 0, i)),
                  pl.BlockSpec((H_I, tq), lambda i: (0, i)),
                  pl.BlockSpec((1, tq), lambda i: (0, i)),
                  pl.BlockSpec((t, D_I), lambda i: (0, 0))],
        out_specs=pl.BlockSpec((t, tq), lambda i: (0, i)),
        out_shape=jax.ShapeDtypeStruct((t, t), BF16),
        scratch_shapes=[pltpu.VMEM((t, tq), I32), pltpu.VMEM((SEL_ROWS, tq), I32)],
        compiler_params=_cparams("parallel"),
        name="index_select_t",
    )(qit, wit, ve, ki)


def _attn_t_body(qi_ref, kj_ref, qct_ref, qdt_ref, kc_ref, kd_ref, vct_ref, vdt_ref, mask_ref, bt_ref, lam_ref,
                 g_ref, o_ref, m_ref, accc_ref, accd_ref, ot_ref, qbd_ref, s_ref, *, tq, per, e_far,
                 lam_init):
    i = qi_ref[pl.program_id(0)]
    j = kj_ref[pl.program_id(0)]
    e = i - j * per
    grp = MAPS_PER_DOT

    @pl.when(j == 0)
    def _():
        m_ref[...] = jnp.full(m_ref.shape, NEG_INF, F32)
        accc_ref[...] = jnp.zeros_like(accc_ref)
        accd_ref[...] = jnp.zeros_like(accd_ref)
        qbd_ref[...] = jnp.zeros_like(qbd_ref)
        for mp in range(N_MAPS):
            q = qct_ref[mp] if mp < H_C else qdt_ref[mp - H_C]
            a = mp % grp
            qbd_ref[mp // grp, a * DH:(a + 1) * DH, a * tq:(a + 1) * tq] = q

    def logits(g):
        k_ref = kc_ref if g < H_C // grp else kd_ref
        half = g % (H_C // grp)
        return _dot(k_ref[:, half * grp * DH:(half + 1) * grp * DH], qbd_ref[g])

    def step(near):
        sel = mask_ref[...].astype(F32)
        sel = jnp.concatenate([sel] * grp, axis=1)
        for g in range(N_MAPS // grp):
            s = logits(g) + sel if g < H_C // grp else logits(g)
            if near:
                heads = [g * grp + a if g < H_C // grp else H_C + (g * grp + a - H_C) // 2 for a in range(grp)]
                s = s + jnp.concatenate([bt_ref[0, h].astype(F32) for h in heads], axis=1)
            s_ref[g] = s
        for mp in range(N_MAPS):
            g, a = mp // grp, mp % grp
            cols = slice(a * tq, (a + 1) * tq)
            if mp < H_C:
                vt, acc_ref, a_idx = vct_ref[mp], accc_ref, mp
            else:
                vt, acc_ref, a_idx = vdt_ref[(mp - H_C) // 2], accd_ref, mp - H_C

            biased = lambda rows: s_ref[g, rows, cols]
            m_prev = m_ref[mp]
            m_new = jnp.maximum(m_prev, jnp.max(biased(slice(None)), axis=0, keepdims=True))
            alpha = jnp.exp2(m_prev - m_new)
            m_ref[mp] = m_new
            pv = None
            for r0 in range(0, s_ref.shape[1], PV_ROWS):
                rows = slice(r0, r0 + PV_ROWS)
                part = _dot(vt[:, rows], jnp.exp2(biased(rows) - m_new).astype(BF16))
                pv = part if pv is None else pv + part
            acc_ref[a_idx] = alpha * acc_ref[a_idx] + pv

    pl.when(e < e_far)(functools.partial(step, True))
    pl.when(e >= e_far)(functools.partial(step, False))

    @pl.when(j == i // per)
    def _():
        for h in range(H_C):
            ot_ref[h * DH:(h + 1) * DH, :] = accc_ref[h, 0:DH, :] / accc_ref[h, DH:DH + 1, :]
        lam = lam_ref[...]
        g = g_ref[...]
        dv = 2 * DH
        for h in range(H_D):
            a0 = accd_ref[2 * h, 0:dv, :] / accd_ref[2 * h, dv:dv + 1, :]
            a1 = accd_ref[2 * h + 1, 0:dv, :] / accd_ref[2 * h + 1, dv:dv + 1, :]
            od = a0 - lam * a1
            od = od * lax.rsqrt(jnp.mean(od * od, axis=0, keepdims=True) + EPS) * g * (1.0 - lam_init)
            ot_ref[W_C + h * 2 * DH:W_C + (h + 1) * 2 * DH, :] = od
        o_ref[...] = ot_ref[...].T.astype(o_ref.dtype)


def _attend_t(qct, qdt, kc, kd, vct, vdt, mask, btiles, lam, g, lam_init):
    t = qct.shape[2]
    tq, lb = ATTN_TQ, PROMPT_LB
    per = lb // tq
    e_far = btiles.shape[0]
    pairs = [(i, j) for i in range(t // tq) for j in range(i // per + 1)]
    qi_tab = jnp.asarray(np.array([p[0] for p in pairs], np.int32))
    kj_tab = jnp.asarray(np.array([p[1] for p in pairs], np.int32))
    qspec = pl.BlockSpec((H_C, DH, tq), lambda s, qi, kj: (0, 0, qi[s]))
    kspec = pl.BlockSpec((lb, H_C * DH), lambda s, qi, kj: (kj[s], 0))
    grid_spec = pltpu.PrefetchScalarGridSpec(
        num_scalar_prefetch=2,
        grid=(len(pairs),),
        in_specs=[qspec, qspec, kspec, kspec,
                  pl.BlockSpec((H_C, DH + ONES_ROWS, lb), lambda s, qi, kj: (0, 0, kj[s])),
                  pl.BlockSpec((H_D, 2 * DH + ONES_ROWS, lb), lambda s, qi, kj: (0, 0, kj[s])),
                  pl.BlockSpec((lb, tq), lambda s, qi, kj: (kj[s], qi[s])),
                  pl.BlockSpec((1, N_HEADS_BIAS, lb, tq),
                               lambda s, qi, kj: (jnp.minimum(qi[s] - kj[s] * per, e_far - 1), 0, 0, 0)),
                  pl.BlockSpec((1, 1), lambda s, qi, kj: (0, 0)),
                  pl.BlockSpec((2 * DH, 1), lambda s, qi, kj: (0, 0))],
        out_specs=pl.BlockSpec((tq, W_C + W_D), lambda s, qi, kj: (qi[s], 0)),
        scratch_shapes=[pltpu.VMEM((N_MAPS, 1, tq), F32),
                        pltpu.VMEM((H_C, DH + ONES_ROWS, tq), F32),
                        pltpu.VMEM((2 * H_D, 2 * DH + ONES_ROWS, tq), F32),
                        pltpu.VMEM((W_C + W_D, tq), F32),
                        pltpu.VMEM((N_MAPS // MAPS_PER_DOT, MAPS_PER_DOT * DH, MAPS_PER_DOT * tq), BF16),
                        pltpu.VMEM((N_MAPS // MAPS_PER_DOT, lb, MAPS_PER_DOT * tq), F32)])
    return pl.pallas_call(
        functools.partial(_attn_t_body, tq=tq, per=per, e_far=e_far, lam_init=lam_init),
        grid_spec=grid_spec,
        out_shape=jax.ShapeDtypeStruct((t, W_C + W_D), BF16),
        compiler_params=_cparams("arbitrary"),
        name="attend_t",
    )(qi_tab, kj_tab, qct, qdt, kc, kd, vct, vdt, mask, btiles, lam, g)


def _attn_s_body(qc_ref, qd_ref, ck_ref, cv_ref, dk_ref, dv_ref, nck_ref, ncv_ref, ndk_ref, ndv_ref,
                 mask_ref, bias_ref, lam_ref, g_ref, o_ref, m_ref, l_ref, acc_ref, *, ts, ncache, lam_init):
    j = pl.program_id(1)
    rows = H_C * ts

    @pl.when(j == 0)
    def _():
        m_ref[...] = jnp.full(m_ref.shape, NEG_INF, F32)
        l_ref[...] = jnp.zeros_like(l_ref)
        acc_ref[...] = jnp.zeros_like(acc_ref)

    def step(kc_t, vc_t, kd_t, vd):
        sel = jnp.tile(mask_ref[0, 0].astype(F32), (H_C, 1))
        bias = bias_ref[0]
        s_c = _dot(qc_ref[0], kc_t.astype(BF16)) + sel + bias[0:rows]
        s_d = _dot(qd_ref[0], kd_t.astype(BF16)) + bias[rows:2 * rows]
        for idx, (s, v) in enumerate(((s_c, vc_t), (s_d, vd))):
            m_prev = m_ref[idx]
            m_new = jnp.maximum(m_prev, jnp.max(s, axis=-1, keepdims=True))
            alpha = jnp.exp(m_prev - m_new)
            p = jnp.exp(s - m_new)
            l_ref[idx] = alpha * l_ref[idx] + jnp.sum(p, axis=-1, keepdims=True)
            m_ref[idx] = m_new
            p = p.astype(BF16)
            pv = _dot_nt(p, v.astype(BF16)) if idx == 0 else _dot(p, v.astype(BF16))
            acc_ref[idx] = alpha * acc_ref[idx] + pv

    heads_packed = lambda ref: jnp.concatenate([ref[0, :, h, :] for h in range(H_D)], axis=1)

    @pl.when(j < ncache)
    def _():
        step(ck_ref[0], cv_ref[0], dk_ref[0], heads_packed(dv_ref))

    @pl.when(j == ncache)
    def _():
        step(nck_ref[0], ncv_ref[0], ndk_ref[0], heads_packed(ndv_ref))
        for h in range(H_C):
            r = slice(h * ts, (h + 1) * ts)
            o_ref[:, h * DH:(h + 1) * DH] = (acc_ref[0, r, h * DH:(h + 1) * DH] / l_ref[0, r, :]).astype(o_ref.dtype)
        lam = lam_ref[...]
        g = g_ref[...]
        for h in range(H_D):
            r0 = slice(2 * h * ts, (2 * h + 1) * ts)
            r1 = slice((2 * h + 1) * ts, (2 * h + 2) * ts)
            c = slice(h * 2 * DH, (h + 1) * 2 * DH)
            od = acc_ref[1, r0, c] / l_ref[1, r0, :] - lam * (acc_ref[1, r1, c] / l_ref[1, r1, :])
            od = _rms(od, g) * (1.0 - lam_init)
            o_ref[:, W_C + h * 2 * DH:W_C + (h + 1) * 2 * DH] = od.astype(o_ref.dtype)


def _attend_s(qbd_c, qbd_d, caches, news, mask, bias, lam, g, lam_init):
    nb, rows, _ = qbd_c.shape
    ts = rows // H_C
    lb = KEY_BLOCK
    ncache = caches[3].shape[1] // lb
    qspec = pl.BlockSpec((1, rows, W_C), lambda b, j: (b, 0, 0))
    ctspec = pl.BlockSpec((1, W_C, lb), lambda b, j: (b, 0, jnp.minimum(j, ncache - 1)))
    cspec = pl.BlockSpec((1, lb, H_D, 2 * DH), lambda b, j: (b, jnp.minimum(j, ncache - 1), 0, 0))
    ntspec = pl.BlockSpec((1, W_C, lb), lambda b, j: (b, 0, 0))
    nspec = pl.BlockSpec((1, lb, H_D, 2 * DH), lambda b, j: (b, 0, 0, 0))
    return pl.pallas_call(
        functools.partial(_attn_s_body, ts=ts, ncache=ncache, lam_init=lam_init),
        grid=(nb, ncache + 1),
        in_specs=[qspec, qspec, ctspec, ctspec, ctspec, cspec, ntspec, ntspec, ntspec, nspec,
                  pl.BlockSpec((1, 1, ts, lb), lambda b, j: (b, j, 0, 0)),
                  pl.BlockSpec((1, 2 * rows, lb), lambda b, j: (j, 0, 0)),
                  pl.BlockSpec((1, 1), lambda b, j: (0, 0)),
                  pl.BlockSpec((1, 2 * DH), lambda b, j: (0, 0))],
        out_specs=pl.BlockSpec((ts, W_C + W_D), lambda b, j: (b, 0)),
        out_shape=jax.ShapeDtypeStruct((nb * ts, W_C + W_D), BF16),
        scratch_shapes=[pltpu.VMEM((2, rows, 1), F32), pltpu.VMEM((2, rows, 1), F32),
                        pltpu.VMEM((2, rows, W_C), F32)],
        compiler_params=_cparams("parallel", "arbitrary"),
        name="attend_s",
    )(qbd_c, qbd_d, *caches, *news, mask, bias, lam, g)


def _out_router_body(x_ref, o_ref, w_ref, g_ref, r_ref, x3_ref, xn_ref, gate_ref, gatet_ref):
    x3 = x_ref[...] + _dot(o_ref[...], w_ref[...])
    x3_ref[...] = x3
    xn = _rms(x3, g_ref[...]).astype(BF16)
    xn_ref[...] = xn
    logits = _dot(xn, r_ref[...])
    lane = lax.broadcasted_iota(I32, logits.shape, 1)
    logits = jnp.where(lane < N_EXP, logits, -jnp.inf)
    m1 = jnp.max(logits, axis=-1, keepdims=True)
    i1 = jnp.min(jnp.where(logits == m1, lane, LANES), axis=-1, keepdims=True)
    rest = jnp.where(lane == i1, -jnp.inf, logits)
    m2 = jnp.max(rest, axis=-1, keepdims=True)
    i2 = jnp.min(jnp.where(rest == m2, lane, LANES), axis=-1, keepdims=True)
    e = jnp.exp(m2 - m1)
    g1 = 1.0 / (1.0 + e)
    g2 = e / (1.0 + e)
    gate = jnp.where(lane == i1, g1, 0.0) + jnp.where(lane == i2, g2, 0.0)
    gate_ref[...] = gate
    gatet_ref[...] = gate.T[0:GATE_ROWS, :]


def _out_router(x, o, w, g, router, rows):
    n, d = x.shape
    return pl.pallas_call(
        _out_router_body,
        grid=(n // rows,),
        in_specs=[pl.BlockSpec((rows, d), lambda i: (i, 0)),
                  pl.BlockSpec((rows, d), lambda i: (i, 0)),
                  pl.BlockSpec(w.shape, lambda i: (0, 0)),
                  pl.BlockSpec((1, d), lambda i: (0, 0)),
                  pl.BlockSpec(router.shape, lambda i: (0, 0))],
        out_specs=[pl.BlockSpec((rows, d), lambda i: (i, 0)),
                   pl.BlockSpec((rows, d), lambda i: (i, 0)),
                   pl.BlockSpec((rows, LANES), lambda i: (i, 0)),
                   pl.BlockSpec((GATE_ROWS, rows), lambda i: (0, i))],
        out_shape=[jax.ShapeDtypeStruct((n, d), F32), jax.ShapeDtypeStruct((n, d), BF16),
                   jax.ShapeDtypeStruct((n, LANES), F32), jax.ShapeDtypeStruct((GATE_ROWS, n), F32)],
        compiler_params=_cparams("parallel"),
        name="out_router",
    )(x, o, w, g, router)


def _t5_bucket(rel):
    half = N_BUCKETS // 2
    max_exact = half // 2
    ret = np.where(rel > 0, half, 0)
    n = np.abs(rel)
    nf = np.maximum(n, 1).astype(np.float32)
    large = max_exact + (np.log(nf / np.float32(max_exact)) / np.float32(math.log(MAX_DIST / max_exact))
                         * np.float32(half - max_exact)).astype(np.int32)
    large = np.minimum(large, half - 1)
    return (ret + np.where(n < max_exact, n, large)).astype(np.int32)


def _bias_tile(rel_bias, q_pos, k_pos, k_real):
    rel = k_pos[None, :] - q_pos[:, None]
    onehot = (jnp.asarray(_t5_bucket(rel).astype(np.int8))[:, :, None]
              == jnp.arange(N_BUCKETS, dtype=jnp.int8)).astype(F32)
    bias = jnp.einsum("qkb,bh->hqk", onehot, rel_bias.astype(F32), precision=lax.Precision.HIGHEST)
    ok = np.logical_and(k_pos[None, :] // CHUNK <= q_pos[:, None] // CHUNK, k_pos[None, :] < k_real)
    return jnp.where(ok[None], bias, NEG_INF)


def _in1_plan():
    scale = DH ** -0.5
    heads64 = lambda o, sc=1.0: [(o, h, h * DH, (h + 1) * DH, sc) for h in range(8)]
    plan = [
        ((0, 512), heads64(0, scale)),
        ((512, 1024), [(1, None, 0, 512, 1.0)] + heads64(2)),
        ((1024, 1536), [(3, None, 0, 512, 1.0)] + heads64(4)),
        ((1536, 2048), heads64(5, scale)),
        ((2048, 2560), [(6, None, 0, 512, 1.0)] + heads64(7)),
        ((2560, 3072), [(8, None, 0, 512, 1.0)] + [(9, h, h * 128, (h + 1) * 128, 1.0) for h in range(H_D)]),
        ((3072, 3584), heads64(10)),
        ((3584, 3712), [(11, None, 0, D_I, 1.0), (12, None, 0, D_I, 1.0), (13, None, D_I, D_I + H_I, 1.0)]),
    ]
    out_defs = [(DH, BF16, 8), (512, F32, None), (DH, BF16, 8), (512, F32, None), (DH, BF16, 8),
                (DH, BF16, 8), (512, F32, None), (DH, BF16, 8), (512, F32, None), (2 * DH, BF16, H_D),
                (D_I, BF16, 8), (D_I, F32, None), (D_I, BF16, None), (H_I, F32, None)]
    return plan, out_defs


def _layer0(x, hist, rb, emit_v, p):
    n = x.shape[0]
    rows = _largest_divisor(n, 512, 16)
    plan = [((0, p["w_in0"].shape[1]), [(0, None, 0, p["w_in0"].shape[1], 1.0)])]
    (z,) = _rms_proj(x, p["ln_mix0"], p["w_in0"], plan, [(p["w_in0"].shape[1], F32, None)], rows)
    ws = p["ws_prompt"] if hist is None else p["ws_sample"]
    bs = p["bs_prompt"] if hist is None else p["bs_sample"]
    outs = _mixer_ab(x, z, hist, p["gmlp_ln_g"], p["gmlp_ln_b"], ws, bs, p["conv_k"], p["w_out0"], rb, emit_v)
    x1 = outs[0]
    x2 = _ffn(x1, p["ln_ffn0"], p["ffn_wg"], p["ffn_wu"], p["ffn_wd"], rows, p["ffn_fb"])
    return (x2,) + tuple(outs[1:])


GATE_ROWS = 16
MOE_ROWS = 128


def _moe_routed_body(x_ref, xn_ref, gate_ref, gatet_ref, gf_ref, wg_ref, wu_ref, wd_ref, y_ref,
                     triu_ref, tril_ref, crow_ref, ccol_ref, xs_ref, ge_ref, acc_ref, yblk_ref, nsub_ref,
                     *, tb, sub):
    e = pl.program_id(1)
    k = pl.program_id(2)
    first_k = k == 0
    last_k = k == pl.num_programs(2) - 1

    @pl.when(jnp.logical_and(e == 0, first_k))
    def _():
        r = lax.broadcasted_iota(I32, (tb, tb), 0)
        c = lax.broadcasted_iota(I32, (tb, tb), 1)
        triu_ref[...] = jnp.where(r < c, 1.0, 0.0).astype(BF16)
        tril_ref[...] = jnp.where(c < r, 1.0, 0.0).astype(BF16)
        crow_ref[...] = _dot(jnp.where(gatet_ref[...] > 0.0, 1.0, 0.0).astype(BF16), triu_ref[...])
        ccol_ref[...] = _dot(tril_ref[...], jnp.where(gate_ref[...] > 0.0, 1.0, 0.0).astype(BF16))
        yblk_ref[...] = jnp.zeros_like(yblk_ref)
        acc_ref[...] = jnp.zeros_like(acc_ref)

    @pl.when(first_k)
    def _():
        g_e = gatet_ref[pl.ds(e, 1), :]
        m_e = g_e > 0.0
        c_e = crow_ref[pl.ds(e, 1), :]
        nsub = (jnp.sum(jnp.where(m_e, 1.0, 0.0)).astype(I32) + sub - 1) // sub
        nsub_ref[0] = nsub
        xn = xn_ref[...]
        slot = lax.broadcasted_iota(I32, (sub, tb), 0).astype(F32)

        def pack(s, carry):
            r0 = pl.multiple_of(s * sub, sub)
            hit = jnp.logical_and(m_e, c_e == slot + (s * sub).astype(F32))
            onehot = jnp.where(hit, 1.0, 0.0)
            xs_ref[pl.ds(r0, sub), :] = _dot(onehot.astype(BF16), xn).astype(BF16)
            ge_ref[pl.ds(r0, sub), :] = jnp.sum(onehot * g_e, axis=-1, keepdims=True)
            acc_ref[pl.ds(r0, sub), :] = jnp.zeros((sub, acc_ref.shape[1]), F32)
            return carry

        lax.fori_loop(0, nsub, pack, 0)

    nsub = nsub_ref[0]

    def experts(rows):
        xs = xs_ref[rows, :]
        h = jax.nn.silu(_dot(xs, wg_ref[0])) * _dot(xs, wu_ref[0])
        acc_ref[rows, :] += _dot((ge_ref[rows, :] * h).astype(BF16), wd_ref[0])

    def pair(s, carry):
        experts(pl.ds(pl.multiple_of(s * 2 * sub, 2 * sub), 2 * sub))
        return carry

    lax.fori_loop(0, nsub // 2, pair, 0)
    pl.when(nsub % 2 == 1)(lambda: experts(pl.ds(pl.multiple_of((nsub - 1) * sub, sub), sub)))

    @pl.when(last_k)
    def _():
        lane = lax.broadcasted_iota(I32, (tb, LANES), 1)
        pick = lambda a: jnp.sum(jnp.where(lane == e, a, 0.0), axis=-1, keepdims=True)
        m_e = pick(gate_ref[...]) > 0.0
        c_e = pick(ccol_ref[...])
        slot = lax.broadcasted_iota(I32, (tb, sub), 1).astype(F32)

        def unpack(s, carry):
            rows = pl.ds(pl.multiple_of(s * sub, sub), sub)
            hit = jnp.logical_and(m_e, c_e == slot + (s * sub).astype(F32))
            onehot = jnp.where(hit, 1.0, 0.0).astype(BF16)
            y = acc_ref[rows, :]
            hi = y.astype(BF16)
            lo = (y - hi.astype(F32)).astype(BF16)
            yblk_ref[...] += _dot(jnp.concatenate([onehot, onehot], axis=1), jnp.concatenate([hi, lo], axis=0))
            return carry

        lax.fori_loop(0, nsub, unpack, 0)

    @pl.when(jnp.logical_and(e == pl.num_programs(1) - 1, last_k))
    def _():
        y_ref[...] = _rms(x_ref[...] + yblk_ref[...], gf_ref[...])


def _moe_routed(x, xn, gate, gatet, gf, wg, wu, wd, tb, fb):
    n, d = x.shape
    ne, _, dff = wg.shape
    return pl.pallas_call(
        functools.partial(_moe_routed_body, tb=tb, sub=min(MOE_ROWS, tb)),
        grid=(n // tb, ne, dff // fb),
        in_specs=[pl.BlockSpec((tb, d), lambda i, e, k: (i, 0)),
                  pl.BlockSpec((tb, d), lambda i, e, k: (i, 0)),
                  pl.BlockSpec((tb, LANES), lambda i, e, k: (i, 0)),
                  pl.BlockSpec((GATE_ROWS, tb), lambda i, e, k: (0, i)),
                  pl.BlockSpec((1, d), lambda i, e, k: (0, 0)),
                  pl.BlockSpec((1, d, fb), lambda i, e, k: (e, 0, k)),
                  pl.BlockSpec((1, d, fb), lambda i, e, k: (e, 0, k)),
                  pl.BlockSpec((1, fb, d), lambda i, e, k: (e, k, 0))],
        out_specs=pl.BlockSpec((tb, d), lambda i, e, k: (i, 0)),
        out_shape=jax.ShapeDtypeStruct((n, d), F32),
        scratch_shapes=[pltpu.VMEM((tb, tb), BF16), pltpu.VMEM((tb, tb), BF16),
                        pltpu.VMEM((GATE_ROWS, tb), F32), pltpu.VMEM((tb, LANES), F32),
                        pltpu.VMEM((tb, d), BF16), pltpu.VMEM((tb, 1), F32),
                        pltpu.VMEM((tb, d), F32), pltpu.VMEM((tb, d), F32),
                        pltpu.SMEM((1,), I32)],
        compiler_params=_cparams("parallel", "arbitrary", "arbitrary"),
        name="moe_routed",
    )(x, xn, gate, gatet, gf, wg, wu, wd)


def _layer1_tail(x, o, p):
    n = x.shape[0]
    rows = _largest_divisor(n, 512, 16)
    x3, xn, gate, gatet = _out_router(x, o, p["w_out1"], p["ln_ffn1"], p["router"], rows)
    return _moe_routed(x3, xn, gate, gatet, p["ln_final"], p["exp_wg"], p["exp_wu"], p["exp_wd"],
                       _largest_divisor(n, 1024, 128), p["exp_fb"])


def kernel(x_prompt, x_sample, state_b_conv, cache_c_k, cache_c_v, cache_idx_k, cache_d_k, cache_d_v, rel_bias, ln_mix, ln_ffn, ln_final, w_in0, gmlp_ln_g, gmlp_ln_b, gmlp_ws, gmlp_bs, conv_k, w_out0, ffn_wg, ffn_wu, ffn_wd, w_in1, lam_qk, subln_g, w_out1, router, exp_wg, exp_wu, exp_wd):
    bp, seq, d = x_prompt.shape
    bs_, ts, _ = x_sample.shape
    past = cache_c_k.shape[2]
    assert bp == 1 and ln_mix.shape[0] == 2 and seq % PROMPT_LB == 0 and ts % SUBLANES == 0 and ts <= CHUNK
    assert past % KEY_BLOCK == 0
    gmlp_chunk = gmlp_ws.shape[-1]
    lam_init = 0.8 - 0.6 * math.exp(-0.3 * 1)

    def ws_masked(rows):
        r = jnp.arange(rows)
        ok = (r[None, :] // CHUNK) <= (r[:, None] // CHUNK)
        return jnp.where(ok[None], gmlp_ws[0][:, :rows, :rows], 0.0).astype(BF16)

    in1_pad = (-w_in1.shape[2]) % LANES
    lf = lam_qk[0].astype(F32)
    lam = (jnp.exp(jnp.sum(lf[0] * lf[1])) - jnp.exp(jnp.sum(lf[2] * lf[3])) + lam_init).reshape(1, 1)
    p = {
        "ln_mix0": ln_mix[0:1], "ln_ffn0": ln_ffn[0:1], "ln_mix1": ln_mix[1:2], "ln_ffn1": ln_ffn[1:2],
        "ln_final": ln_final.reshape(1, d),
        "w_in0": w_in0[0].astype(BF16),
        "gmlp_ln_g": gmlp_ln_g[0].reshape(1, W_A), "gmlp_ln_b": gmlp_ln_b[0].reshape(1, W_A),
        "ws_prompt": ws_masked(gmlp_chunk), "ws_sample": ws_masked(ts),
        "bs_prompt": gmlp_bs[0][:, :gmlp_chunk].T, "bs_sample": gmlp_bs[0][:, :ts].T,
        "conv_k": conv_k[0], "w_out0": w_out0[0].astype(BF16),
        "ffn_wg": ffn_wg[0].astype(BF16), "ffn_wu": ffn_wu[0].astype(BF16), "ffn_wd": ffn_wd[0].astype(BF16),
        "ffn_fb": _largest_divisor(ffn_wg.shape[2], 1408, LANES),
        "w_in1": jnp.pad(w_in1[0], ((0, 0), (0, in1_pad))).astype(BF16),
        "w_out1": w_out1[0].astype(BF16),
        "router": jnp.pad(router[0], ((0, 0), (0, LANES - N_EXP))).astype(BF16),
        "exp_wg": exp_wg[0].astype(BF16), "exp_wu": exp_wu[0].astype(BF16), "exp_wd": exp_wd[0].astype(BF16),
        "exp_fb": _largest_divisor(exp_wg.shape[3], 896, LANES),
    }
    g_sub = subln_g[0].reshape(1, 2 * DH)
    plan1, out_defs1 = _in1_plan()

    xp = x_prompt.reshape(seq, d)
    xs = x_sample.reshape(bs_ * ts, d)
    xp, p_tail = _layer0(xp, None, gmlp_chunk, False, p)
    hist = jnp.pad(state_b_conv[0], ((0, 0), (SUBLANES - 2, 0), (0, 0)))
    xs, s_tail, s_av = _layer0(xs, hist, ts, True, p)
    p_b_conv = p_tail[-1, SUBLANES - 2:, :].reshape(1, 1, 2, W_B)
    s_b_conv = s_tail[:, SUBLANES - 2:, :].reshape(1, bs_, 2, W_B)
    s_a_v = s_av.reshape(1, bs_, ts, W_A)

    lb = KEY_BLOCK
    w1 = w_in1[0]
    off = np.concatenate([[0], np.cumsum(IN1_SIZES)])
    field = lambda k: w1[:, off[k]:off[k + 1]]
    padc = lambda a: jnp.pad(a, ((0, 0), (0, LANES - a.shape[1])))
    log2e = math.log2(math.e)
    qscale = DH ** -0.5 * log2e
    w_norm = jnp.concatenate([field(1), field(4), field(5), padc(field(7))], axis=1).astype(BF16)
    w_tran = jnp.concatenate([field(0) * qscale, field(3) * qscale, field(6), field(2), field(5),
                              field(1), field(4), padc(field(8))], axis=1).T.astype(BF16)
    (vd32, ki32, kc, kd, ki, qct, qdt, qit, vct, vdt, vc32t, kc32t, kd32t, wit) = _proj1_t(
        xp, p["ln_mix1"], w_norm, w_tran, PROMPT_TQ)
    token_major = lambda a, *dims: jnp.moveaxis(a.reshape(dims + (seq,)), -1, 0).reshape((1, 1, seq) + dims)
    p_c_k, p_c_v = token_major(kc32t, H_C, DH), token_major(vc32t, H_C, DH)
    p_d_k = token_major(kd32t, H_D, 2, DH)
    pos = jnp.arange(seq, dtype=I32)
    ve = ((pos // CHUNK + 1) * CHUNK).reshape(1, seq)
    mask = _select_t(qit, wit, ve, ki, min(TOPK_MAX, seq // 4))
    e_far = -(-(PROMPT_LB - 1 + MAX_DIST) // ATTN_TQ)
    kpos = np.arange(PROMPT_LB)
    qpos = np.arange(ATTN_TQ)
    base = e_far * ATTN_TQ
    rel_near = (rel_bias - rel_bias[N_BUCKETS // 2 - 1:N_BUCKETS // 2]) * log2e
    btiles = jnp.stack([jnp.transpose(_bias_tile(rel_near, base + e * ATTN_TQ + qpos, base + kpos, base + PROMPT_LB),
                                      (0, 2, 1)) for e in range(e_far)]).astype(BF16)
    op = _attend_t(qct, qdt, kc, kd, vct, vdt, mask, btiles, lam, subln_g[0].reshape(2 * DH, 1), lam_init)
    y_prompt = _layer1_tail(xp, op, p).reshape(1, seq, d)

    sr = _rms_proj(xs, p["ln_mix1"], p["w_in1"], plan1, out_defs1, _largest_divisor(bs_ * ts, 256, 16))
    sqc, skc32, skc, svc32, svc, sqd, skd32, skd, svd32, svd, sqi, ski32, ski, swi = sr
    nk = past + ts
    lps = -(-nk // lb) * lb

    kis = jnp.pad(jnp.concatenate([cache_idx_k[0].astype(BF16), ski.reshape(bs_, ts, D_I)], axis=1),
                  ((0, 0), (0, lps - nk), (0, 0)))
    ves = jnp.full((bs_ * ts, 1), nk, I32)
    smask = _select(sqi, swi, ves, kis, bs_, ts, ts, min(TOPK_MAX, nk // 4), lambda i: lps // lb)
    sq_pos = past + np.arange(ts)
    sbt = jnp.stack([_bias_tile(rel_bias, sq_pos, j * lb + np.arange(lb), nk) for j in range(lps // lb)])
    sbias = jnp.concatenate([sbt[:, :H_C], jnp.repeat(sbt[:, H_C:], 2, axis=1)], axis=1).reshape(lps // lb, N_MAPS * ts, lb)

    def block_diag(q):
        qb = jnp.transpose(q.reshape(H_C, bs_, ts, DH), (1, 0, 2, 3))
        eye = jnp.eye(H_C, dtype=q.dtype)
        return (qb[:, :, :, None, :] * eye[None, :, None, :, None]).reshape(bs_, H_C * ts, W_C)

    def cache_rows(c, feature_major):
        if not feature_major:
            return c
        return jnp.transpose(c.reshape(bs_, past, W_C), (0, 2, 1))

    def new_rows(a, feature_major):
        a = jnp.pad(a.reshape(bs_, ts, W_C), ((0, 0), (0, lb - ts), (0, 0)))
        return jnp.transpose(a, (0, 2, 1)) if feature_major else a.reshape(bs_, lb, H_D, 2 * DH)
    os_ = _attend_s(block_diag(sqc), block_diag(sqd),
                    [cache_rows(c[0], fm) for c, fm in ((cache_c_k, True), (cache_c_v, True), (cache_d_k, True), (cache_d_v, False))],
                    [new_rows(a, fm) for a, fm in ((skc32, True), (svc32, True), (skd32, True), (svd32, False))],
                    smask, sbias, lam, g_sub, lam_init)
    y_sample = _layer1_tail(xs, os_, p).reshape(bs_, ts, d)

    r5 = lambda a, n, t, *tail: a.reshape((1, n, t) + tail)
    return (y_prompt, y_sample, p_b_conv,
            p_c_k, p_c_v, r5(ki32, 1, seq, D_I), p_d_k, r5(vd32, 1, seq, H_D, 2 * DH),
            s_a_v, s_b_conv,
            r5(skc32, bs_, ts, H_C, DH), r5(svc32, bs_, ts, H_C, DH), r5(ski32, bs_, ts, D_I),
            r5(skd32, bs_, ts, H_D, 2, DH), r5(svd32, bs_, ts, H_D, 2 * DH))
```

```python
import functools
import math

import jax
import jax.numpy as jnp
import numpy as np
from jax import lax
from jax.experimental import pallas as pl
from jax.experimental.pallas import tpu as pltpu

F32 = jnp.float32
BF16 = jnp.bfloat16
I32 = jnp.int32

CHUNK = 64
EPS = 1e-6
NEG_INF = -1e30
H_A = 4
C_A = 128
W_A = H_A * C_A
W_B = 512
H_C = 8
DH = 64
W_C = H_C * DH
H_I = 8
D_I = 64
TOPK_MAX = 256
H_D = 4
W_D = H_D * 2 * DH
N_BUCKETS = 32
MAX_DIST = 128
N_EXP = 8
IN1_SIZES = (W_C, W_C, W_C, W_D, W_D, W_D, H_I * D_I, D_I, H_I)
N_HEADS_BIAS = H_C + H_D
N_MAPS = H_C + 2 * H_D

LANES = 128
SUBLANES = 8
VMEM_LIMIT = 56 * 1024 * 1024

INT_MIN = -(2 ** 31)
INT_MAX = 2 ** 31 - 1
ALL_TIES = 2 ** 30

KEY_BLOCK = 512


def _cparams(*sem):
    return pltpu.CompilerParams(dimension_semantics=sem, vmem_limit_bytes=VMEM_LIMIT)


def _largest_divisor(n, target, mult):
    if n <= target:
        return n
    d = (target // mult) * mult
    while d >= mult:
        if n % d == 0:
            return d
        d -= mult
    raise ValueError(f"no block of multiple {mult} divides {n}")


def _rms(x, g):
    return x * lax.rsqrt(jnp.mean(x * x, axis=-1, keepdims=True) + EPS) * g


def _dot(a, b):
    return jnp.dot(a, b, preferred_element_type=F32)


def _dot_nt(a, b):
    return lax.dot_general(a, b, (((1,), (1,)), ((), ())), preferred_element_type=F32)


def _rms_proj_body(x_ref, g_ref, w_ref, *out_refs, plan):
    xn = _rms(x_ref[...], g_ref[...]).astype(BF16)
    for (c0, c1), writes in plan:
        z = _dot(xn, w_ref[:, c0:c1])
        for o_idx, head, z0, z1, scale in writes:
            val = z[:, z0:z1]
            if scale != 1.0:
                val = val * scale
            ref = out_refs[o_idx]
            if head is None:
                ref[...] = val.astype(ref.dtype)
            else:
                ref[head] = val.astype(ref.dtype)


def _rms_proj(x, g, w, plan, out_defs, rows):
    n, d = x.shape
    grid = (n // rows,)
    out_shape, out_specs = [], []
    for width, dtype, heads in out_defs:
        if heads is None:
            out_shape.append(jax.ShapeDtypeStruct((n, width), dtype))
            out_specs.append(pl.BlockSpec((rows, width), lambda i: (i, 0)))
        else:
            out_shape.append(jax.ShapeDtypeStruct((heads, n, width), dtype))
            out_specs.append(pl.BlockSpec((heads, rows, width), lambda i: (0, i, 0)))
    return pl.pallas_call(
        functools.partial(_rms_proj_body, plan=plan),
        grid=grid,
        in_specs=[pl.BlockSpec((rows, d), lambda i: (i, 0)),
                  pl.BlockSpec((1, d), lambda i: (0, 0)),
                  pl.BlockSpec(w.shape, lambda i: (0, 0))],
        out_specs=out_specs,
        out_shape=out_shape,
        compiler_params=_cparams("parallel"),
        name="rms_proj",
    )(x, g, w)


def _mixer_ab_body(*refs, rb, from_prev, emit_v):
    it = iter(refs)
    x_ref, u_ref, v_ref, gb_ref, gc_ref, xin_ref = (next(it) for _ in range(6))
    if from_prev:
        gcp_ref, xinp_ref = next(it), next(it)
    else:
        hist_ref = next(it)
    lng_ref, lnb_ref, ws_ref, bs_ref, ck_ref, wout_ref = (next(it) for _ in range(6))
    x1_ref, tail_ref = next(it), next(it)
    vout_ref = next(it) if emit_v else None
    wext_ref = next(it)

    w = gc_ref[...] * xin_ref[...]
    if from_prev:
        hist = jnp.where(pl.program_id(0) > 0, gcp_ref[...] * xinp_ref[...], 0.0)
    else:
        hist = hist_ref[0]
    wext_ref[0:SUBLANES, :] = hist
    wext_ref[SUBLANES:, :] = w
    ck = ck_ref[...]
    conv = (ck[0:1] * wext_ref[SUBLANES - 2:SUBLANES - 2 + rb, :]
            + ck[1:2] * wext_ref[SUBLANES - 1:SUBLANES - 1 + rb, :]
            + ck[2:3] * w)
    y_b = gb_ref[...] * conv
    tail_ref[0] = w[rb - SUBLANES:, :]

    u = jax.nn.gelu(u_ref[...])
    v = jax.nn.gelu(v_ref[...])
    lng = lng_ref[...]
    lnb = lnb_ref[...]
    bs = bs_ref[...]
    acc = _dot(y_b.astype(BF16), wout_ref[W_A:, :])
    for h in range(H_A):
        sl = slice(h * C_A, (h + 1) * C_A)
        vh = v[:, sl]
        mu = jnp.mean(vh, axis=-1, keepdims=True)
        xc = vh - mu
        var = jnp.mean(xc * xc, axis=-1, keepdims=True)
        vln = xc * lax.rsqrt(var + EPS) * lng[:, sl] + lnb[:, sl]
        if emit_v:
            vout_ref[:, sl] = vln
        s = _dot(ws_ref[h], vln.astype(BF16)) + bs[:, h:h + 1]
        y_a = u[:, sl] * s
        acc = acc + _dot(y_a.astype(BF16), wout_ref[sl, :])
    x1_ref[...] = x_ref[...] + acc


def _mixer_ab(x, z, hist, lng, lnb, ws, bs, ck, wout, rb, emit_v):
    n, d = x.shape
    nb = n // rb
    from_prev = hist is None
    col = lambda c: pl.BlockSpec((rb, 512), lambda i, c=c: (i, c))
    in_specs = [pl.BlockSpec((rb, d), lambda i: (i, 0)), col(0), col(1), col(2), col(3), col(4)]
    args = [x, z, z, z, z, z]
    if from_prev:
        per = rb // SUBLANES
        prev = lambda c: pl.BlockSpec((SUBLANES, 512), lambda i, c=c: (jnp.maximum(i * per - 1, 0), c))
        in_specs += [prev(3), prev(4)]
        args += [z, z]
    else:
        in_specs += [pl.BlockSpec((1, SUBLANES, 512), lambda i: (i, 0, 0))]
        args += [hist]
    const = lambda a: pl.BlockSpec(a.shape, lambda i, nd=a.ndim: (0,) * nd)
    for a in (lng, lnb, ws, bs, ck, wout):
        in_specs.append(const(a))
        args.append(a)
    out_shape = [jax.ShapeDtypeStruct((n, d), F32), jax.ShapeDtypeStruct((nb, SUBLANES, 512), F32)]
    out_specs = [pl.BlockSpec((rb, d), lambda i: (i, 0)), pl.BlockSpec((1, SUBLANES, 512), lambda i: (i, 0, 0))]
    if emit_v:
        out_shape.append(jax.ShapeDtypeStruct((n, W_A), F32))
        out_specs.append(pl.BlockSpec((rb, W_A), lambda i: (i, 0)))
    return pl.pallas_call(
        functools.partial(_mixer_ab_body, rb=rb, from_prev=from_prev, emit_v=emit_v),
        grid=(nb,),
        in_specs=in_specs,
        out_specs=out_specs,
        out_shape=out_shape,
        scratch_shapes=[pltpu.VMEM((rb + SUBLANES, 512), F32)],
        compiler_params=_cparams("arbitrary"),
        name="mixer_ab",
    )(*args)


def _ffn_body(x_ref, g_ref, wg_ref, wu_ref, wd_ref, o_ref, xn_ref, acc_ref):
    k = pl.program_id(1)

    @pl.when(k == 0)
    def _():
        xn_ref[...] = _rms(x_ref[...], g_ref[...]).astype(BF16)
        acc_ref[...] = jnp.zeros_like(acc_ref)

    xn = xn_ref[...]
    h = jax.nn.silu(_dot(xn, wg_ref[...])) * _dot(xn, wu_ref[...])
    acc_ref[...] += _dot(h.astype(BF16), wd_ref[...])

    @pl.when(k == pl.num_programs(1) - 1)
    def _():
        o_ref[...] = x_ref[...] + acc_ref[...]


def _ffn(x, g, wg, wu, wd, rows, fb):
    n, d = x.shape
    dff = wg.shape[1]
    return pl.pallas_call(
        _ffn_body,
        grid=(n // rows, dff // fb),
        in_specs=[pl.BlockSpec((rows, d), lambda i, k: (i, 0)),
                  pl.BlockSpec((1, d), lambda i, k: (0, 0)),
                  pl.BlockSpec((d, fb), lambda i, k: (0, k)),
                  pl.BlockSpec((d, fb), lambda i, k: (0, k)),
                  pl.BlockSpec((fb, d), lambda i, k: (k, 0))],
        out_specs=pl.BlockSpec((rows, d), lambda i, k: (i, 0)),
        out_shape=jax.ShapeDtypeStruct((n, d), F32),
        scratch_shapes=[pltpu.VMEM((rows, d), BF16), pltpu.VMEM((rows, d), F32)],
        compiler_params=_cparams("parallel", "arbitrary"),
        name="ffn",
    )(x, g, wg, wu, wd)


def _select_body(qi_ref, wi_ref, ve_ref, ki_ref, out_ref, keys_ref, *, tq, lb, nkb, topk, nvalid_fn):
    nv = nvalid_fn(pl.program_id(1))
    wi = wi_ref[...]
    ve = ve_ref[...]
    q_all = jnp.concatenate([qi_ref[h] for h in range(H_I)], axis=0)
    wcols = [wi[:, h:h + 1] for h in range(H_I)]
    lane = lax.broadcasted_iota(I32, (tq, lb), 1)

    def score_block(b, carry):
        kb = ki_ref[0, pl.ds(pl.multiple_of(b * lb, lb), lb), :]
        dots = _dot_nt(q_all, kb)
        sc = jnp.zeros((tq, lb), F32)
        for h in range(H_I):
            sc = sc + wcols[h] * jnp.maximum(dots[h * tq:(h + 1) * tq, :], 0.0)
        bits = lax.bitcast_convert_type(sc, I32)
        key = bits ^ ((bits >> 31) & INT_MAX)
        key = jnp.where(sc == 0.0, 0, key)
        key = jnp.where(lane + b * lb < ve, key, INT_MIN)
        keys_ref[b] = key
        return carry

    lax.fori_loop(0, nv, score_block, 0)

    def count_ge(mid):
        midb = jnp.broadcast_to(mid, (tq, LANES))

        def body(b, acc):
            for c in range(lb // LANES):
                k = keys_ref[b, :, c * LANES:(c + 1) * LANES]
                acc = acc + jnp.where(k >= midb, 1, 0)
            return acc

        acc = lax.fori_loop(0, nv, body, jnp.zeros((tq, LANES), I32))
        return jnp.sum(acc.astype(F32), axis=-1, keepdims=True).astype(I32)

    def bisect(_, st):
        lo, hi, clo, chi = st
        mid = (lo >> 1) + (hi >> 1) + (lo & hi & 1)
        cnt = count_ge(mid)
        active = mid != lo
        up = jnp.logical_and(active, cnt >= topk)
        dn = jnp.logical_and(active, cnt < topk)
        return (jnp.where(up, mid, lo), jnp.where(dn, mid, hi),
                jnp.where(up, cnt, clo), jnp.where(dn, cnt, chi))

    full = lambda v: jnp.full((tq, 1), v, I32)
    lo, hi, clo, chi = lax.fori_loop(0, 32, bisect, (full(INT_MIN), full(INT_MAX), ve, full(0)))

    need = jnp.where(clo > topk, topk - chi, ALL_TIES)
    need = jnp.where(lo == INT_MIN, 0, need).astype(F32)
    lob = jnp.broadcast_to(lo, (tq, LANES))
    needb = jnp.broadcast_to(need, (tq, LANES))
    r = lax.broadcasted_iota(I32, (LANES, LANES), 0)
    c = lax.broadcasted_iota(I32, (LANES, LANES), 1)
    tri = jnp.where(r <= c, 1.0, 0.0).astype(BF16)

    def mask_block(b, seen):
        for cc in range(lb // LANES):
            k = keys_ref[b, :, cc * LANES:(cc + 1) * LANES]
            eq = jnp.where(k == lob, 1.0, 0.0)
            cum = _dot(eq.astype(BF16), tri)
            rank = seen + cum - eq
            take = jnp.where(rank < needb, eq, 0.0)
            sel = jnp.where(k > lob, 1.0, take)
            out_ref[0, b, :, cc * LANES:(cc + 1) * LANES] = jnp.where(sel > 0.5, 0.0, NEG_INF).astype(out_ref.dtype)
            seen = seen + cum[:, LANES - 1:LANES]
        return seen

    lax.fori_loop(0, nv, mask_block, jnp.zeros((tq, 1), F32))

    def fill_block(b, carry):
        out_ref[0, b] = jnp.full((tq, lb), NEG_INF, out_ref.dtype)
        return carry

    lax.fori_loop(nv, nkb, fill_block, 0)


def _select(qi, wi, ve, ki, nbatch, t, tq, topk, nvalid_fn):
    lp = ki.shape[1]
    lb = KEY_BLOCK
    nkb = lp // lb
    nq = t // tq
    row = lambda b, i: b * nq + i
    return pl.pallas_call(
        functools.partial(_select_body, tq=tq, lb=lb, nkb=nkb, topk=topk, nvalid_fn=nvalid_fn),
        grid=(nbatch, nq),
        in_specs=[pl.BlockSpec((H_I, tq, D_I), lambda b, i: (0, row(b, i), 0)),
                  pl.BlockSpec((tq, H_I), lambda b, i: (row(b, i), 0)),
                  pl.BlockSpec((tq, 1), lambda b, i: (row(b, i), 0)),
                  pl.BlockSpec((1, lp, D_I), lambda b, i: (b, 0, 0))],
        out_specs=pl.BlockSpec((1, nkb, tq, lb), lambda b, i: (b, 0, i, 0)),
        out_shape=jax.ShapeDtypeStruct((nbatch, nkb, t, lb), BF16),
        scratch_shapes=[pltpu.VMEM((nkb, tq, lb), I32)],
        compiler_params=_cparams("parallel", "arbitrary"),
        name="index_select",
    )(qi, wi, ve, ki)


PROMPT_TQ = 256
ATTN_TQ = 512
PROMPT_LB = 512
SEL_ROWS = 256
TIE_ROWS = 128
MAPS_PER_DOT = 4
PV_ROWS = 256
ONES_ROWS = 16


def _proj1_t_body(x_ref, g_ref, w_ref, wt_ref, vd32_ref, ki32_ref, kc_ref, kd_ref, ki_ref,
                  qct_ref, qdt_ref, qit_ref, vct_ref, vdt_ref, vc32t_ref, kc32t_ref, kd32t_ref, wit_ref):
    xn32 = _rms(x_ref[...], g_ref[...])
    xn = xn32.astype(BF16)
    xnt = xn32.T.astype(BF16)
    kc_ref[...] = _dot(xn, w_ref[:, 0:512]).astype(BF16)
    kd_ref[...] = _dot(xn, w_ref[:, 512:1024]).astype(BF16)
    vd32_ref[...] = _dot(xn, w_ref[:, 1024:1536])
    z = _dot(xn, w_ref[:, 1536:1536 + LANES])
    ki32_ref[...] = z[:, 0:D_I]
    ki_ref[...] = z[:, 0:D_I].astype(BF16)
    for c, (head_ref, full_ref) in enumerate(((qct_ref, None), (qdt_ref, None), (qit_ref, None), (vct_ref, vc32t_ref),
                                              (vdt_ref, None), (None, kc32t_ref), (None, kd32t_ref))):
        zt = _dot(wt_ref[c * 512:(c + 1) * 512, :], xnt)
        if full_ref is not None:
            full_ref[...] = zt
        if head_ref is not None:
            nh = head_ref.shape[0]
            w = 512 // nh
            for h in range(nh):
                head_ref[h, 0:w, :] = zt[h * w:(h + 1) * w, :].astype(BF16)
                if head_ref.shape[1] > w:
                    pad = lax.broadcasted_iota(I32, (head_ref.shape[1] - w, zt.shape[1]), 0)
                    head_ref[h, w:, :] = jnp.where(pad == 0, 1.0, 0.0).astype(BF16)
    zt = _dot(wt_ref[3584:3584 + LANES, :], xnt)
    wit_ref[...] = zt[0:H_I, :]


def _proj1_t(x, g, w, wt, rows):
    n, d = x.shape
    full = lambda width: (jax.ShapeDtypeStruct((n, width), F32), pl.BlockSpec((rows, width), lambda i: (i, 0)))
    packed = lambda width: (jax.ShapeDtypeStruct((n, width), BF16), pl.BlockSpec((rows, width), lambda i: (i, 0)))
    heads_t = lambda nh, width: (jax.ShapeDtypeStruct((nh, width, n), BF16),
                                 pl.BlockSpec((nh, width, rows), lambda i: (0, 0, i)))
    full_t = lambda: (jax.ShapeDtypeStruct((512, n), F32), pl.BlockSpec((512, rows), lambda i: (0, i)))
    outs = [full(512), full(D_I), packed(512), packed(512), packed(D_I),
            heads_t(8, DH), heads_t(8, DH), heads_t(8, DH), heads_t(8, DH + ONES_ROWS), heads_t(H_D, 2 * DH + ONES_ROWS),
            full_t(), full_t(), full_t(),
            (jax.ShapeDtypeStruct((H_I, n), F32), pl.BlockSpec((H_I, rows), lambda i: (0, i)))]
    return pl.pallas_call(
        _proj1_t_body,
        grid=(n // rows,),
        in_specs=[pl.BlockSpec((rows, d), lambda i: (i, 0)),
                  pl.BlockSpec((1, d), lambda i: (0, 0)),
                  pl.BlockSpec(w.shape, lambda i: (0, 0)),
                  pl.BlockSpec(wt.shape, lambda i: (0, 0))],
        out_specs=[o[1] for o in outs],
        out_shape=[o[0] for o in outs],
        compiler_params=_cparams("parallel"),
        name="proj1_t",
    )(x, g, w, wt)


def _sublane_all(x8, op):
    for shift in (4, 2, 1):
        x8 = op(x8, pltpu.roll(x8, shift, 0))
    return x8


def _select_t_body(qit_ref, wit_ref, ve_ref, ki_ref, out_ref, keys_ref, gmax_ref, *, tq, lp, topk):
    rb = SEL_ROWS
    nv = (pl.program_id(0) + 1) * (tq // rb)
    wit = wit_ref[...]
    ve = ve_ref[...]
    rows = lax.broadcasted_iota(I32, (rb, tq), 0)
    gmax_ref[...] = jnp.full((rb, tq), INT_MIN, I32)
    q_all = jnp.concatenate([qit_ref[h] for h in range(H_I)], axis=1)

    def score_chunks(r0, nch):
        dots = _dot(ki_ref[pl.ds(r0, nch * rb), :], q_all)
        for ch in range(nch):
            part = dots[ch * rb:(ch + 1) * rb, :]
            sc = jnp.zeros((rb, tq), F32)
            for h in range(H_I):
                sc = sc + wit[h:h + 1, :] * jnp.maximum(part[:, h * tq:(h + 1) * tq], 0.0)
            bits = lax.bitcast_convert_type(sc, I32)
            key = bits ^ ((bits >> 31) & INT_MAX)
            key = jnp.where(sc == 0.0, 0, key)
            key = jnp.where(rows + (r0 + ch * rb) < ve, key, INT_MIN)
            keys_ref[pl.ds(r0 + ch * rb, rb), :] = key
            gmax_ref[...] = jnp.maximum(gmax_ref[...], key)

    def score_pair(c, carry):
        score_chunks(pl.multiple_of(c * 2 * rb, 2 * rb), 2)
        return carry

    lax.fori_loop(0, nv // 2, score_pair, 0)
    pl.when(nv % 2 == 1)(lambda: score_chunks(pl.multiple_of((nv - 1) * rb, rb), 1))

    def count_ge(mid):
        midb = jnp.broadcast_to(mid, (SUBLANES, tq))

        def body(c, accs):
            kc = keys_ref[pl.ds(pl.multiple_of(c * rb, rb), rb), :]
            accs = list(accs)
            for g in range(rb // SUBLANES):
                a = g % len(accs)
                accs[a] = accs[a] + jnp.where(kc[g * SUBLANES:(g + 1) * SUBLANES, :] >= midb, 1, 0)
            return tuple(accs)

        accs = lax.fori_loop(0, nv, body, (jnp.zeros((SUBLANES, tq), I32),) * 4)
        return _sublane_all(accs[0] + accs[1] + accs[2] + accs[3], jnp.add)[0:1, :]

    floor_avg = lambda a, b: (a >> 1) + (b >> 1) + (a & b & 1)

    def pending(lo, hi, clo):
        open_ = jnp.logical_and(floor_avg(lo, hi) != lo, clo > topk)
        return jnp.max(jnp.where(open_, 1.0, 0.0))

    def bisect(st):
        lo, hi, clo, chi, _ = st
        mid = floor_avg(lo, hi)
        cnt = count_ge(mid)
        active = jnp.logical_and(mid != lo, clo > topk)
        up = jnp.logical_and(active, cnt >= topk)
        dn = jnp.logical_and(active, cnt < topk)
        lo, hi = jnp.where(up, mid, lo), jnp.where(dn, mid, hi)
        clo, chi = jnp.where(up, cnt, clo), jnp.where(dn, cnt, chi)
        return lo, hi, clo, chi, pending(lo, hi, clo)

    g8 = gmax_ref[0:SUBLANES, :]
    h8 = g8
    for g in range(1, rb // SUBLANES):
        blk = gmax_ref[g * SUBLANES:(g + 1) * SUBLANES, :]
        g8 = jnp.minimum(g8, blk)
        h8 = jnp.maximum(h8, blk)
    lo0 = _sublane_all(g8, jnp.minimum)[0:1, :]
    hi0 = _sublane_all(h8, jnp.maximum)[0:1, :] + 1
    clo0 = jnp.where(lo0 == INT_MIN, ve, count_ge(lo0))
    chi0 = jnp.zeros((1, tq), I32)

    def probe_zero():
        c_pos, c_nonneg = count_ge(jnp.full((1, tq), 1, I32)), count_ge(jnp.zeros((1, tq), I32))
        above = c_pos >= topk
        below = c_nonneg < topk
        lo1 = jnp.where(above, jnp.maximum(lo0, 1), jnp.where(below, lo0, 0))
        clo1 = jnp.where(above, jnp.where(lo0 >= 1, clo0, c_pos), jnp.where(below, clo0, c_nonneg))
        hi1 = jnp.where(above, hi0, jnp.where(below, jnp.minimum(hi0, 0), 1))
        chi1 = jnp.where(above, chi0, jnp.where(below, jnp.where(hi0 <= 0, chi0, c_nonneg), c_pos))
        return lo1, hi1, clo1, chi1

    lo1, hi1, clo1, chi1 = lax.cond(jnp.max(jnp.where(lo0 < 1, 1.0, 0.0)) > 0.5, probe_zero,
                                    lambda: (lo0, hi0, clo0, chi0))
    lo, hi, clo, chi, _ = lax.while_loop(
        lambda st: st[4] > 0.5, bisect, (lo1, hi1, clo1, chi1, pending(lo1, hi1, clo1)))

    need = jnp.where(clo > topk, topk - chi, ALL_TIES)
    need = jnp.where(lo == INT_MIN, 0, need).astype(F32)
    tr = TIE_ROWS
    lob = jnp.broadcast_to(lo, (tr, tq))
    needb = jnp.broadcast_to(need, (tr, tq))
    r = lax.broadcasted_iota(I32, (tr, tr), 0)
    c = lax.broadcasted_iota(I32, (tr, tr), 1)
    tri = jnp.where(c <= r, 1.0, 0.0).astype(BF16)

    def mask_chunk(cidx, seen):
        r0 = pl.multiple_of(cidx * tr, tr)
        k = keys_ref[pl.ds(r0, tr), :]
        eq = jnp.where(k == lob, 1.0, 0.0)
        cum = _dot(tri, eq.astype(BF16))
        take = jnp.where(seen + cum - eq < needb, eq, 0.0)
        sel = jnp.where(k > lob, 1.0, take)
        out_ref[pl.ds(r0, tr), :] = jnp.where(sel > 0.5, 0.0, NEG_INF).astype(out_ref.dtype)
        return seen + cum[tr - 1:tr, :]

    def plain_chunk(cidx, carry):
        r0 = pl.multiple_of(cidx * rb, rb)
        k = keys_ref[pl.ds(r0, rb), :]
        hit = jnp.where(k == INT_MIN, NEG_INF, 0.0)
        out_ref[pl.ds(r0, rb), :] = jnp.where(k >= lo, hit, NEG_INF).astype(out_ref.dtype)
        return carry

    any_tie = jnp.max(jnp.where(clo > topk, 1.0, 0.0)) > 0.5
    lax.cond(any_tie,
             lambda: lax.fori_loop(0, nv * (rb // tr), mask_chunk, jnp.zeros((1, tq), F32)),
             lambda: lax.fori_loop(0, nv, plain_chunk, jnp.zeros((1, tq), F32)))

    def fill_chunk(cidx, carry):
        out_ref[pl.ds(pl.multiple_of(cidx * rb, rb), rb), :] = jnp.full((rb, tq), NEG_INF, out_ref.dtype)
        return carry

    lax.fori_loop(nv, lp // rb, fill_chunk, 0)


def _select_t(qit, wit, ve, ki, topk):
    t = qit.shape[2]
    tq = PROMPT_TQ
    return pl.pallas_call(
        functools.partial(_select_t_body, tq=tq, lp=t, topk=topk),
        grid=(t // tq,),
        in_specs=[pl.BlockSpec((H_I, D_I, tq), lambda i: (0, 0, i)),
                  pl.BlockSpec((H_I, tq), lambda i: (0, i)),
                  pl.BlockSpec((1, tq), lambda i: (0, i)),
                  pl.BlockSpec((t, D_I), lambda i: (0, 0))],
        out_specs=pl.BlockSpec((t, tq), lambda i: (0, i)),
        out_shape=jax.ShapeDtypeStruct((t, t), BF16),
        scratch_shapes=[pltpu.VMEM((t, tq), I32), pltpu.VMEM((SEL_ROWS, tq), I32)],
        compiler_params=_cparams("parallel"),
        name="index_select_t",
    )(qit, wit, ve, ki)


def _attn_t_body(qi_ref, kj_ref, qct_ref, qdt_ref, kc_ref, kd_ref, vct_ref, vdt_ref, mask_ref, bt_ref, lam_ref,
                 g_ref, o_ref, m_ref, accc_ref, accd_ref, ot_ref, qbd_ref, s_ref, *, tq, per, e_far,
                 lam_init):
    i = qi_ref[pl.program_id(0)]
    j = kj_ref[pl.program_id(0)]
    e = i - j * per
    grp = MAPS_PER_DOT

    @pl.when(j == 0)
    def _():
        m_ref[...] = jnp.full(m_ref.shape, NEG_INF, F32)
        accc_ref[...] = jnp.zeros_like(accc_ref)
        accd_ref[...] = jnp.zeros_like(accd_ref)
        qbd_ref[...] = jnp.zeros_like(qbd_ref)
        for mp in range(N_MAPS):
            q = qct_ref[mp] if mp < H_C else qdt_ref[mp - H_C]
            a = mp % grp
            qbd_ref[mp // grp, a * DH:(a + 1) * DH, a * tq:(a + 1) * tq] = q

    def logits(g):
        k_ref = kc_ref if g < H_C // grp else kd_ref
        half = g % (H_C // grp)
        return _dot(k_ref[:, half * grp * DH:(half + 1) * grp * DH], qbd_ref[g])

    def step(near):
        sel = mask_ref[...].astype(F32)
        sel = jnp.concatenate([sel] * grp, axis=1)
        for g in range(N_MAPS // grp):
            s = logits(g) + sel if g < H_C // grp else logits(g)
            if near:
                heads = [g * grp + a if g < H_C // grp else H_C + (g * grp + a - H_C) // 2 for a in range(grp)]
                s = s + jnp.concatenate([bt_ref[0, h].astype(F32) for h in heads], axis=1)
            s_ref[g] = s
        for mp in range(N_MAPS):
            g, a = mp // grp, mp % grp
            cols = slice(a * tq, (a + 1) * tq)
            if mp < H_C:
                vt, acc_ref, a_idx = vct_ref[mp], accc_ref, mp
            else:
                vt, acc_ref, a_idx = vdt_ref[(mp - H_C) // 2], accd_ref, mp - H_C

            biased = lambda rows: s_ref[g, rows, cols]
            m_prev = m_ref[mp]
            m_new = jnp.maximum(m_prev, jnp.max(biased(slice(None)), axis=0, keepdims=True))
            alpha = jnp.exp2(m_prev - m_new)
            m_ref[mp] = m_new
            pv = None
            for r0 in range(0, s_ref.shape[1], PV_ROWS):
                rows = slice(r0, r0 + PV_ROWS)
                part = _dot(vt[:, rows], jnp.exp2(biased(rows) - m_new).astype(BF16))
                pv = part if pv is None else pv + part
            acc_ref[a_idx] = alpha * acc_ref[a_idx] + pv

    pl.when(e < e_far)(functools.partial(step, True))
    pl.when(e >= e_far)(functools.partial(step, False))

    @pl.when(j == i // per)
    def _():
        for h in range(H_C):
            ot_ref[h * DH:(h + 1) * DH, :] = accc_ref[h, 0:DH, :] / accc_ref[h, DH:DH + 1, :]
        lam = lam_ref[...]
        g = g_ref[...]
        dv = 2 * DH
        for h in range(H_D):
            a0 = accd_ref[2 * h, 0:dv, :] / accd_ref[2 * h, dv:dv + 1, :]
            a1 = accd_ref[2 * h + 1, 0:dv, :] / accd_ref[2 * h + 1, dv:dv + 1, :]
            od = a0 - lam * a1
            od = od * lax.rsqrt(jnp.mean(od * od, axis=0, keepdims=True) + EPS) * g * (1.0 - lam_init)
            ot_ref[W_C + h * 2 * DH:W_C + (h + 1) * 2 * DH, :] = od
        o_ref[...] = ot_ref[...].T.astype(o_ref.dtype)


def _attend_t(qct, qdt, kc, kd, vct, vdt, mask, btiles, lam, g, lam_init):
    t = qct.shape[2]
    tq, lb = ATTN_TQ, PROMPT_LB
    per = lb // tq
    e_far = btiles.shape[0]
    pairs = [(i, j) for i in range(t // tq) for j in range(i // per + 1)]
    qi_tab = jnp.asarray(np.array([p[0] for p in pairs], np.int32))
    kj_tab = jnp.asarray(np.array([p[1] for p in pairs], np.int32))
    qspec = pl.BlockSpec((H_C, DH, tq), lambda s, qi, kj: (0, 0, qi[s]))
    kspec = pl.BlockSpec((lb, H_C * DH), lambda s, qi, kj: (kj[s], 0))
    grid_spec = pltpu.PrefetchScalarGridSpec(
        num_scalar_prefetch=2,
        grid=(len(pairs),),
        in_specs=[qspec, qspec, kspec, kspec,
                  pl.BlockSpec((H_C, DH + ONES_ROWS, lb), lambda s, qi, kj: (0, 0, kj[s])),
                  pl.BlockSpec((H_D, 2 * DH + ONES_ROWS, lb), lambda s, qi, kj: (0, 0, kj[s])),
                  pl.BlockSpec((lb, tq), lambda s, qi, kj: (kj[s], qi[s])),
                  pl.BlockSpec((1, N_HEADS_BIAS, lb, tq),
                               lambda s, qi, kj: (jnp.minimum(qi[s] - kj[s] * per, e_far - 1), 0, 0, 0)),
                  pl.BlockSpec((1, 1), lambda s, qi, kj: (0, 0)),
                  pl.BlockSpec((2 * DH, 1), lambda s, qi, kj: (0, 0))],
        out_specs=pl.BlockSpec((tq, W_C + W_D), lambda s, qi, kj: (qi[s], 0)),
        scratch_shapes=[pltpu.VMEM((N_MAPS, 1, tq), F32),
                        pltpu.VMEM((H_C, DH + ONES_ROWS, tq), F32),
                        pltpu.VMEM((2 * H_D, 2 * DH + ONES_ROWS, tq), F32),
                        pltpu.VMEM((W_C + W_D, tq), F32),
                        pltpu.VMEM((N_MAPS // MAPS_PER_DOT, MAPS_PER_DOT * DH, MAPS_PER_DOT * tq), BF16),
                        pltpu.VMEM((N_MAPS // MAPS_PER_DOT, lb, MAPS_PER_DOT * tq), F32)])
    return pl.pallas_call(
        functools.partial(_attn_t_body, tq=tq, per=per, e_far=e_far, lam_init=lam_init),
        grid_spec=grid_spec,
        out_shape=jax.ShapeDtypeStruct((t, W_C + W_D), BF16),
        compiler_params=_cparams("arbitrary"),
        name="attend_t",
    )(qi_tab, kj_tab, qct, qdt, kc, kd, vct, vdt, mask, btiles, lam, g)


def _attn_s_body(qc_ref, qd_ref, ck_ref, cv_ref, dk_ref, dv_ref, nck_ref, ncv_ref, ndk_ref, ndv_ref,
                 mask_ref, bias_ref, lam_ref, g_ref, o_ref, m_ref, l_ref, acc_ref, *, ts, ncache, lam_init):
    j = pl.program_id(1)
    rows = H_C * ts

    @pl.when(j == 0)
    def _():
        m_ref[...] = jnp.full(m_ref.shape, NEG_INF, F32)
        l_ref[...] = jnp.zeros_like(l_ref)
        acc_ref[...] = jnp.zeros_like(acc_ref)

    def step(kc_t, vc_t, kd_t, vd):
        sel = jnp.tile(mask_ref[0, 0].astype(F32), (H_C, 1))
        bias = bias_ref[0]
        s_c = _dot(qc_ref[0], kc_t.astype(BF16)) + sel + bias[0:rows]
        s_d = _dot(qd_ref[0], kd_t.astype(BF16)) + bias[rows:2 * rows]
        for idx, (s, v) in enumerate(((s_c, vc_t), (s_d, vd))):
            m_prev = m_ref[idx]
            m_new = jnp.maximum(m_prev, jnp.max(s, axis=-1, keepdims=True))
            alpha = jnp.exp(m_prev - m_new)
            p = jnp.exp(s - m_new)
            l_ref[idx] = alpha * l_ref[idx] + jnp.sum(p, axis=-1, keepdims=True)
            m_ref[idx] = m_new
            p = p.astype(BF16)
            pv = _dot_nt(p, v.astype(BF16)) if idx == 0 else _dot(p, v.astype(BF16))
            acc_ref[idx] = alpha * acc_ref[idx] + pv

    heads_packed = lambda ref: jnp.concatenate([ref[0, :, h, :] for h in range(H_D)], axis=1)

    @pl.when(j < ncache)
    def _():
        step(ck_ref[0], cv_ref[0], dk_ref[0], heads_packed(dv_ref))

    @pl.when(j == ncache)
    def _():
        step(nck_ref[0], ncv_ref[0], ndk_ref[0], heads_packed(ndv_ref))
        for h in range(H_C):
            r = slice(h * ts, (h + 1) * ts)
            o_ref[:, h * DH:(h + 1) * DH] = (acc_ref[0, r, h * DH:(h + 1) * DH] / l_ref[0, r, :]).astype(o_ref.dtype)
        lam = lam_ref[...]
        g = g_ref[...]
        for h in range(H_D):
            r0 = slice(2 * h * ts, (2 * h + 1) * ts)
            r1 = slice((2 * h + 1) * ts, (2 * h + 2) * ts)
            c = slice(h * 2 * DH, (h + 1) * 2 * DH)
            od = acc_ref[1, r0, c] / l_ref[1, r0, :] - lam * (acc_ref[1, r1, c] / l_ref[1, r1, :])
            od = _rms(od, g) * (1.0 - lam_init)
            o_ref[:, W_C + h * 2 * DH:W_C + (h + 1) * 2 * DH] = od.astype(o_ref.dtype)


def _attend_s(qbd_c, qbd_d, caches, news, mask, bias, lam, g, lam_init):
    nb, rows, _ = qbd_c.shape
    ts = rows // H_C
    lb = KEY_BLOCK
    ncache = caches[3].shape[1] // lb
    qspec = pl.BlockSpec((1, rows, W_C), lambda b, j: (b, 0, 0))
    ctspec = pl.BlockSpec((1, W_C, lb), lambda b, j: (b, 0, jnp.minimum(j, ncache - 1)))
    cspec = pl.BlockSpec((1, lb, H_D, 2 * DH), lambda b, j: (b, jnp.minimum(j, ncache - 1), 0, 0))
    ntspec = pl.BlockSpec((1, W_C, lb), lambda b, j: (b, 0, 0))
    nspec = pl.BlockSpec((1, lb, H_D, 2 * DH), lambda b, j: (b, 0, 0, 0))
    return pl.pallas_call(
        functools.partial(_attn_s_body, ts=ts, ncache=ncache, lam_init=lam_init),
        grid=(nb, ncache + 1),
        in_specs=[qspec, qspec, ctspec, ctspec, ctspec, cspec, ntspec, ntspec, ntspec, nspec,
                  pl.BlockSpec((1, 1, ts, lb), lambda b, j: (b, j, 0, 0)),
                  pl.BlockSpec((1, 2 * rows, lb), lambda b, j: (j, 0, 0)),
                  pl.BlockSpec((1, 1), lambda b, j: (0, 0)),
                  pl.BlockSpec((1, 2 * DH), lambda b, j: (0, 0))],
        out_specs=pl.BlockSpec((ts, W_C + W_D), lambda b, j: (b, 0)),
        out_shape=jax.ShapeDtypeStruct((nb * ts, W_C + W_D), BF16),
        scratch_shapes=[pltpu.VMEM((2, rows, 1), F32), pltpu.VMEM((2, rows, 1), F32),
                        pltpu.VMEM((2, rows, W_C), F32)],
        compiler_params=_cparams("parallel", "arbitrary"),
        name="attend_s",
    )(qbd_c, qbd_d, *caches, *news, mask, bias, lam, g)


def _out_router_body(x_ref, o_ref, w_ref, g_ref, r_ref, x3_ref, xn_ref, gate_ref, gatet_ref):
    x3 = x_ref[...] + _dot(o_ref[...], w_ref[...])
    x3_ref[...] = x3
    xn = _rms(x3, g_ref[...]).astype(BF16)
    xn_ref[...] = xn
    logits = _dot(xn, r_ref[...])
    lane = lax.broadcasted_iota(I32, logits.shape, 1)
    logits = jnp.where(lane < N_EXP, logits, -jnp.inf)
    m1 = jnp.max(logits, axis=-1, keepdims=True)
    i1 = jnp.min(jnp.where(logits == m1, lane, LANES), axis=-1, keepdims=True)
    rest = jnp.where(lane == i1, -jnp.inf, logits)
    m2 = jnp.max(rest, axis=-1, keepdims=True)
    i2 = jnp.min(jnp.where(rest == m2, lane, LANES), axis=-1, keepdims=True)
    e = jnp.exp(m2 - m1)
    g1 = 1.0 / (1.0 + e)
    g2 = e / (1.0 + e)
    gate = jnp.where(lane == i1, g1, 0.0) + jnp.where(lane == i2, g2, 0.0)
    gate_ref[...] = gate
    gatet_ref[...] = gate.T[0:GATE_ROWS, :]


def _out_router(x, o, w, g, router, rows):
    n, d = x.shape
    return pl.pallas_call(
        _out_router_body,
        grid=(n // rows,),
        in_specs=[pl.BlockSpec((rows, d), lambda i: (i, 0)),
                  pl.BlockSpec((rows, d), lambda i: (i, 0)),
                  pl.BlockSpec(w.shape, lambda i: (0, 0)),
                  pl.BlockSpec((1, d), lambda i: (0, 0)),
                  pl.BlockSpec(router.shape, lambda i: (0, 0))],
        out_specs=[pl.BlockSpec((rows, d), lambda i: (i, 0)),
                   pl.BlockSpec((rows, d), lambda i: (i, 0)),
                   pl.BlockSpec((rows, LANES), lambda i: (i, 0)),
                   pl.BlockSpec((GATE_ROWS, rows), lambda i: (0, i))],
        out_shape=[jax.ShapeDtypeStruct((n, d), F32), jax.ShapeDtypeStruct((n, d), BF16),
                   jax.ShapeDtypeStruct((n, LANES), F32), jax.ShapeDtypeStruct((GATE_ROWS, n), F32)],
        compiler_params=_cparams("parallel"),
        name="out_router",
    )(x, o, w, g, router)


def _t5_bucket(rel):
    half = N_BUCKETS // 2
    max_exact = half // 2
    ret = np.where(rel > 0, half, 0)
    n = np.abs(rel)
    nf = np.maximum(n, 1).astype(np.float32)
    large = max_exact + (np.log(nf / np.float32(max_exact)) / np.float32(math.log(MAX_DIST / max_exact))
                         * np.float32(half - max_exact)).astype(np.int32)
    large = np.minimum(large, half - 1)
    return (ret + np.where(n < max_exact, n, large)).astype(np.int32)


def _bias_tile(rel_bias, q_pos, k_pos, k_real):
    rel = k_pos[None, :] - q_pos[:, None]
    onehot = (jnp.asarray(_t5_bucket(rel).astype(np.int8))[:, :, None]
              == jnp.arange(N_BUCKETS, dtype=jnp.int8)).astype(F32)
    bias = jnp.einsum("qkb,bh->hqk", onehot, rel_bias.astype(F32), precision=lax.Precision.HIGHEST)
    ok = np.logical_and(k_pos[None, :] // CHUNK <= q_pos[:, None] // CHUNK, k_pos[None, :] < k_real)
    return jnp.where(ok[None], bias, NEG_INF)


def _in1_plan():
    scale = DH ** -0.5
    heads64 = lambda o, sc=1.0: [(o, h, h * DH, (h + 1) * DH, sc) for h in range(8)]
    plan = [
        ((0, 512), heads64(0, scale)),
        ((512, 1024), [(1, None, 0, 512, 1.0)] + heads64(2)),
        ((1024, 1536), [(3, None, 0, 512, 1.0)] + heads64(4)),
        ((1536, 2048), heads64(5, scale)),
        ((2048, 2560), [(6, None, 0, 512, 1.0)] + heads64(7)),
        ((2560, 3072), [(8, None, 0, 512, 1.0)] + [(9, h, h * 128, (h + 1) * 128, 1.0) for h in range(H_D)]),
        ((3072, 3584), heads64(10)),
        ((3584, 3712), [(11, None, 0, D_I, 1.0), (12, None, 0, D_I, 1.0), (13, None, D_I, D_I + H_I, 1.0)]),
    ]
    out_defs = [(DH, BF16, 8), (512, F32, None), (DH, BF16, 8), (512, F32, None), (DH, BF16, 8),
                (DH, BF16, 8), (512, F32, None), (DH, BF16, 8), (512, F32, None), (2 * DH, BF16, H_D),
                (D_I, BF16, 8), (D_I, F32, None), (D_I, BF16, None), (H_I, F32, None)]
    return plan, out_defs


def _layer0(x, hist, rb, emit_v, p):
    n = x.shape[0]
    rows = _largest_divisor(n, 512, 16)
    plan = [((0, p["w_in0"].shape[1]), [(0, None, 0, p["w_in0"].shape[1], 1.0)])]
    (z,) = _rms_proj(x, p["ln_mix0"], p["w_in0"], plan, [(p["w_in0"].shape[1], F32, None)], rows)
    ws = p["ws_prompt"] if hist is None else p["ws_sample"]
    bs = p["bs_prompt"] if hist is None else p["bs_sample"]
    outs = _mixer_ab(x, z, hist, p["gmlp_ln_g"], p["gmlp_ln_b"], ws, bs, p["conv_k"], p["w_out0"], rb, emit_v)
    x1 = outs[0]
    x2 = _ffn(x1, p["ln_ffn0"], p["ffn_wg"], p["ffn_wu"], p["ffn_wd"], rows, p["ffn_fb"])
    return (x2,) + tuple(outs[1:])


GATE_ROWS = 16
MOE_ROWS = 128


def _moe_routed_body(x_ref, xn_ref, gate_ref, gatet_ref, gf_ref, wg_ref, wu_ref, wd_ref, y_ref,
                     triu_ref, tril_ref, crow_ref, ccol_ref, xs_ref, ge_ref, acc_ref, yblk_ref, nsub_ref,
                     *, tb, sub):
    e = pl.program_id(1)
    k = pl.program_id(2)
    first_k = k == 0
    last_k = k == pl.num_programs(2) - 1

    @pl.when(jnp.logical_and(e == 0, first_k))
    def _():
        r = lax.broadcasted_iota(I32, (tb, tb), 0)
        c = lax.broadcasted_iota(I32, (tb, tb), 1)
        triu_ref[...] = jnp.where(r < c, 1.0, 0.0).astype(BF16)
        tril_ref[...] = jnp.where(c < r, 1.0, 0.0).astype(BF16)
        crow_ref[...] = _dot(jnp.where(gatet_ref[...] > 0.0, 1.0, 0.0).astype(BF16), triu_ref[...])
        ccol_ref[...] = _dot(tril_ref[...], jnp.where(gate_ref[...] > 0.0, 1.0, 0.0).astype(BF16))
        yblk_ref[...] = jnp.zeros_like(yblk_ref)
        acc_ref[...] = jnp.zeros_like(acc_ref)

    @pl.when(first_k)
    def _():
        g_e = gatet_ref[pl.ds(e, 1), :]
        m_e = g_e > 0.0
        c_e = crow_ref[pl.ds(e, 1), :]
        nsub = (jnp.sum(jnp.where(m_e, 1.0, 0.0)).astype(I32) + sub - 1) // sub
        nsub_ref[0] = nsub
        xn = xn_ref[...]
        slot = lax.broadcasted_iota(I32, (sub, tb), 0).astype(F32)

        def pack(s, carry):
            r0 = pl.multiple_of(s * sub, sub)
            hit = jnp.logical_and(m_e, c_e == slot + (s * sub).astype(F32))
            onehot = jnp.where(hit, 1.0, 0.0)
            xs_ref[pl.ds(r0, sub), :] = _dot(onehot.astype(BF16), xn).astype(BF16)
            ge_ref[pl.ds(r0, sub), :] = jnp.sum(onehot * g_e, axis=-1, keepdims=True)
            acc_ref[pl.ds(r0, sub), :] = jnp.zeros((sub, acc_ref.shape[1]), F32)
            return carry

        lax.fori_loop(0, nsub, pack, 0)

    nsub = nsub_ref[0]

    def experts(rows):
        xs = xs_ref[rows, :]
        h = jax.nn.silu(_dot(xs, wg_ref[0])) * _dot(xs, wu_ref[0])
        acc_ref[rows, :] += _dot((ge_ref[rows, :] * h).astype(BF16), wd_ref[0])

    def pair(s, carry):
        experts(pl.ds(pl.multiple_of(s * 2 * sub, 2 * sub), 2 * sub))
        return carry

    lax.fori_loop(0, nsub // 2, pair, 0)
    pl.when(nsub % 2 == 1)(lambda: experts(pl.ds(pl.multiple_of((nsub - 1) * sub, sub), sub)))

    @pl.when(last_k)
    def _():
        lane = lax.broadcasted_iota(I32, (tb, LANES), 1)
        pick = lambda a: jnp.sum(jnp.where(lane == e, a, 0.0), axis=-1, keepdims=True)
        m_e = pick(gate_ref[...]) > 0.0
        c_e = pick(ccol_ref[...])
        slot = lax.broadcasted_iota(I32, (tb, sub), 1).astype(F32)

        def unpack(s, carry):
            rows = pl.ds(pl.multiple_of(s * sub, sub), sub)
            hit = jnp.logical_and(m_e, c_e == slot + (s * sub).astype(F32))
            onehot = jnp.where(hit, 1.0, 0.0).astype(BF16)
            y = acc_ref[rows, :]
            hi = y.astype(BF16)
            lo = (y - hi.astype(F32)).astype(BF16)
            yblk_ref[...] += _dot(jnp.concatenate([onehot, onehot], axis=1), jnp.concatenate([hi, lo], axis=0))
            return carry

        lax.fori_loop(0, nsub, unpack, 0)

    @pl.when(jnp.logical_and(e == pl.num_programs(1) - 1, last_k))
    def _():
        y_ref[...] = _rms(x_ref[...] + yblk_ref[...], gf_ref[...])


def _moe_routed(x, xn, gate, gatet, gf, wg, wu, wd, tb, fb):
    n, d = x.shape
    ne, _, dff = wg.shape
    return pl.pallas_call(
        functools.partial(_moe_routed_body, tb=tb, sub=min(MOE_ROWS, tb)),
        grid=(n // tb, ne, dff // fb),
        in_specs=[pl.BlockSpec((tb, d), lambda i, e, k: (i, 0)),
                  pl.BlockSpec((tb, d), lambda i, e, k: (i, 0)),
                  pl.BlockSpec((tb, LANES), lambda i, e, k: (i, 0)),
                  pl.BlockSpec((GATE_ROWS, tb), lambda i, e, k: (0, i)),
                  pl.BlockSpec((1, d), lambda i, e, k: (0, 0)),
                  pl.BlockSpec((1, d, fb), lambda i, e, k: (e, 0, k)),
                  pl.BlockSpec((1, d, fb), lambda i, e, k: (e, 0, k)),
                  pl.BlockSpec((1, fb, d), lambda i, e, k: (e, k, 0))],
        out_specs=pl.BlockSpec((tb, d), lambda i, e, k: (i, 0)),
        out_shape=jax.ShapeDtypeStruct((n, d), F32),
        scratch_shapes=[pltpu.VMEM((tb, tb), BF16), pltpu.VMEM((tb, tb), BF16),
                        pltpu.VMEM((GATE_ROWS, tb), F32), pltpu.VMEM((tb, LANES), F32),
                        pltpu.VMEM((tb, d), BF16), pltpu.VMEM((tb, 1), F32),
                        pltpu.VMEM((tb, d), F32), pltpu.VMEM((tb, d), F32),
                        pltpu.SMEM((1,), I32)],
        compiler_params=_cparams("parallel", "arbitrary", "arbitrary"),
        name="moe_routed",
    )(x, xn, gate, gatet, gf, wg, wu, wd)


def _layer1_tail(x, o, p):
    n = x.shape[0]
    rows = _largest_divisor(n, 512, 16)
    x3, xn, gate, gatet = _out_router(x, o, p["w_out1"], p["ln_ffn1"], p["router"], rows)
    return _moe_routed(x3, xn, gate, gatet, p["ln_final"], p["exp_wg"], p["exp_wu"], p["exp_wd"],
                       _largest_divisor(n, 1024, 128), p["exp_fb"])


def kernel(x_prompt, x_sample, state_b_conv, cache_c_k, cache_c_v, cache_idx_k, cache_d_k, cache_d_v, rel_bias, ln_mix, ln_ffn, ln_final, w_in0, gmlp_ln_g, gmlp_ln_b, gmlp_ws, gmlp_bs, conv_k, w_out0, ffn_wg, ffn_wu, ffn_wd, w_in1, lam_qk, subln_g, w_out1, router, exp_wg, exp_wu, exp_wd):
    bp, seq, d = x_prompt.shape
    bs_, ts, _ = x_sample.shape
    past = cache_c_k.shape[2]
    assert bp == 1 and ln_mix.shape[0] == 2 and seq % PROMPT_LB == 0 and ts % SUBLANES == 0 and ts <= CHUNK
    assert past % KEY_BLOCK == 0
    gmlp_chunk = gmlp_ws.shape[-1]
    lam_init = 0.8 - 0.6 * math.exp(-0.3 * 1)

    def ws_masked(rows):
        r = jnp.arange(rows)
        ok = (r[None, :] // CHUNK) <= (r[:, None] // CHUNK)
        return jnp.where(ok[None], gmlp_ws[0][:, :rows, :rows], 0.0).astype(BF16)

    in1_pad = (-w_in1.shape[2]) % LANES
    lf = lam_qk[0].astype(F32)
    lam = (jnp.exp(jnp.sum(lf[0] * lf[1])) - jnp.exp(jnp.sum(lf[2] * lf[3])) + lam_init).reshape(1, 1)
    p = {
        "ln_mix0": ln_mix[0:1], "ln_ffn0": ln_ffn[0:1], "ln_mix1": ln_mix[1:2], "ln_ffn1": ln_ffn[1:2],
        "ln_final": ln_final.reshape(1, d),
        "w_in0": w_in0[0].astype(BF16),
        "gmlp_ln_g": gmlp_ln_g[0].reshape(1, W_A), "gmlp_ln_b": gmlp_ln_b[0].reshape(1, W_A),
        "ws_prompt": ws_masked(gmlp_chunk), "ws_sample": ws_masked(ts),
        "bs_prompt": gmlp_bs[0][:, :gmlp_chunk].T, "bs_sample": gmlp_bs[0][:, :ts].T,
        "conv_k": conv_k[0], "w_out0": w_out0[0].astype(BF16),
        "ffn_wg": ffn_wg[0].astype(BF16), "ffn_wu": ffn_wu[0].astype(BF16), "ffn_wd": ffn_wd[0].astype(BF16),
        "ffn_fb": _largest_divisor(ffn_wg.shape[2], 1408, LANES),
        "w_in1": jnp.pad(w_in1[0], ((0, 0), (0, in1_pad))).astype(BF16),
        "w_out1": w_out1[0].astype(BF16),
        "router": jnp.pad(router[0], ((0, 0), (0, LANES - N_EXP))).astype(BF16),
        "exp_wg": exp_wg[0].astype(BF16), "exp_wu": exp_wu[0].astype(BF16), "exp_wd": exp_wd[0].astype(BF16),
        "exp_fb": _largest_divisor(exp_wg.shape[3], 896, LANES),
    }
    g_sub = subln_g[0].reshape(1, 2 * DH)
    plan1, out_defs1 = _in1_plan()

    xp = x_prompt.reshape(seq, d)
    xs = x_sample.reshape(bs_ * ts, d)
    xp, p_tail = _layer0(xp, None, gmlp_chunk, False, p)
    hist = jnp.pad(state_b_conv[0], ((0, 0), (SUBLANES - 2, 0), (0, 0)))
    xs, s_tail, s_av = _layer0(xs, hist, ts, True, p)
    p_b_conv = p_tail[-1, SUBLANES - 2:, :].reshape(1, 1, 2, W_B)
    s_b_conv = s_tail[:, SUBLANES - 2:, :].reshape(1, bs_, 2, W_B)
    s_a_v = s_av.reshape(1, bs_, ts, W_A)

    lb = KEY_BLOCK
    w1 = w_in1[0]
    off = np.concatenate([[0], np.cumsum(IN1_SIZES)])
    field = lambda k: w1[:, off[k]:off[k + 1]]
    padc = lambda a: jnp.pad(a, ((0, 0), (0, LANES - a.shape[1])))
    log2e = math.log2(math.e)
    qscale = DH ** -0.5 * log2e
    w_norm = jnp.concatenate([field(1), field(4), field(5), padc(field(7))], axis=1).astype(BF16)
    w_tran = jnp.concatenate([field(0) * qscale, field(3) * qscale, field(6), field(2), field(5),
                              field(1), field(4), padc(field(8))], axis=1).T.astype(BF16)
    (vd32, ki32, kc, kd, ki, qct, qdt, qit, vct, vdt, vc32t, kc32t, kd32t, wit) = _proj1_t(
        xp, p["ln_mix1"], w_norm, w_tran, PROMPT_TQ)
    token_major = lambda a, *dims: jnp.moveaxis(a.reshape(dims + (seq,)), -1, 0).reshape((1, 1, seq) + dims)
    p_c_k, p_c_v = token_major(kc32t, H_C, DH), token_major(vc32t, H_C, DH)
    p_d_k = token_major(kd32t, H_D, 2, DH)
    pos = jnp.arange(seq, dtype=I32)
    ve = ((pos // CHUNK + 1) * CHUNK).reshape(1, seq)
    mask = _select_t(qit, wit, ve, ki, min(TOPK_MAX, seq // 4))
    e_far = -(-(PROMPT_LB - 1 + MAX_DIST) // ATTN_TQ)
    kpos = np.arange(PROMPT_LB)
    qpos = np.arange(ATTN_TQ)
    base = e_far * ATTN_TQ
    rel_near = (rel_bias - rel_bias[N_BUCKETS // 2 - 1:N_BUCKETS // 2]) * log2e
    btiles = jnp.stack([jnp.transpose(_bias_tile(rel_near, base + e * ATTN_TQ + qpos, base + kpos, base + PROMPT_LB),
                                      (0, 2, 1)) for e in range(e_far)]).astype(BF16)
    op = _attend_t(qct, qdt, kc, kd, vct, vdt, mask, btiles, lam, subln_g[0].reshape(2 * DH, 1), lam_init)
    y_prompt = _layer1_tail(xp, op, p).reshape(1, seq, d)

    sr = _rms_proj(xs, p["ln_mix1"], p["w_in1"], plan1, out_defs1, _largest_divisor(bs_ * ts, 256, 16))
    sqc, skc32, skc, svc32, svc, sqd, skd32, skd, svd32, svd, sqi, ski32, ski, swi = sr
    nk = past + ts
    lps = -(-nk // lb) * lb

    kis = jnp.pad(jnp.concatenate([cache_idx_k[0].astype(BF16), ski.reshape(bs_, ts, D_I)], axis=1),
                  ((0, 0), (0, lps - nk), (0, 0)))
    ves = jnp.full((bs_ * ts, 1), nk, I32)
    smask = _select(sqi, swi, ves, kis, bs_, ts, ts, min(TOPK_MAX, nk // 4), lambda i: lps // lb)
    sq_pos = past + np.arange(ts)
    sbt = jnp.stack([_bias_tile(rel_bias, sq_pos, j * lb + np.arange(lb), nk) for j in range(lps // lb)])
    sbias = jnp.concatenate([sbt[:, :H_C], jnp.repeat(sbt[:, H_C:], 2, axis=1)], axis=1).reshape(lps // lb, N_MAPS * ts, lb)

    def block_diag(q):
        qb = jnp.transpose(q.reshape(H_C, bs_, ts, DH), (1, 0, 2, 3))
        eye = jnp.eye(H_C, dtype=q.dtype)
        return (qb[:, :, :, None, :] * eye[None, :, None, :, None]).reshape(bs_, H_C * ts, W_C)

    def cache_rows(c, feature_major):
        if not feature_major:
            return c
        return jnp.transpose(c.reshape(bs_, past, W_C), (0, 2, 1))

    def new_rows(a, feature_major):
        a = jnp.pad(a.reshape(bs_, ts, W_C), ((0, 0), (0, lb - ts), (0, 0)))
        return jnp.transpose(a, (0, 2, 1)) if feature_major else a.reshape(bs_, lb, H_D, 2 * DH)
    os_ = _attend_s(block_diag(sqc), block_diag(sqd),
                    [cache_rows(c[0], fm) for c, fm in ((cache_c_k, True), (cache_c_v, True), (cache_d_k, True), (cache_d_v, False))],
                    [new_rows(a, fm) for a, fm in ((skc32, True), (svc32, True), (skd32, True), (svd32, False))],
                    smask, sbias, lam, g_sub, lam_init)
    y_sample = _layer1_tail(xs, os_, p).reshape(bs_, ts, d)

    r5 = lambda a, n, t, *tail: a.reshape((1, n, t) + tail)
    return (y_prompt, y_sample, p_b_conv,
            p_c_k, p_c_v, r5(ki32, 1, seq, D_I), p_d_k, r5(vd32, 1, seq, H_D, 2 * DH),
            s_a_v, s_b_conv,
            r5(skc32, bs_, ts, H_C, DH), r5(svc32, bs_, ts, H_C, DH), r5(ski32, bs_, ts, D_I),
            r5(skd32, bs_, ts, H_D, 2, DH), r5(svd32, bs_, ts, H_D, 2 * DH))
```

```python
import functools
import math

import jax
import jax.numpy as jnp
import numpy as np
from jax import lax
from jax.experimental import pallas as pl
from jax.experimental.pallas import tpu as pltpu

F32 = jnp.float32
BF16 = jnp.bfloat16
I32 = jnp.int32

CHUNK = 64
EPS = 1e-6
NEG_INF = -1e30
H_A = 4
C_A = 128
W_A = H_A * C_A
W_B = 512
H_C = 8
DH = 64
W_C = H_C * DH
H_I = 8
D_I = 64
TOPK_MAX = 256
H_D = 4
W_D = H_D * 2 * DH
N_BUCKETS = 32
MAX_DIST = 128
N_EXP = 8
IN1_SIZES = (W_C, W_C, W_C, W_D, W_D, W_D, H_I * D_I, D_I, H_I)
N_HEADS_BIAS = H_C + H_D
N_MAPS = H_C + 2 * H_D

LANES = 128
SUBLANES = 8
VMEM_LIMIT = 56 * 1024 * 1024

INT_MIN = -(2 ** 31)
INT_MAX = 2 ** 31 - 1
ALL_TIES = 2 ** 30

KEY_BLOCK = 512


def _cparams(*sem):
    return pltpu.CompilerParams(dimension_semantics=sem, vmem_limit_bytes=VMEM_LIMIT)


def _largest_divisor(n, target, mult):
    if n <= target:
        return n
    d = (target // mult) * mult
    while d >= mult:
        if n % d == 0:
            return d
        d -= mult
    raise ValueError(f"no block of multiple {mult} divides {n}")


def _rms(x, g):
    return x * lax.rsqrt(jnp.mean(x * x, axis=-1, keepdims=True) + EPS) * g


def _dot(a, b):
    return jnp.dot(a, b, preferred_element_type=F32)


def _dot_nt(a, b):
    return lax.dot_general(a, b, (((1,), (1,)), ((), ())), preferred_element_type=F32)


def _rms_proj_body(x_ref, g_ref, w_ref, *out_refs, plan):
    xn = _rms(x_ref[...], g_ref[...]).astype(BF16)
    for (c0, c1), writes in plan:
        z = _dot(xn, w_ref[:, c0:c1])
        for o_idx, head, z0, z1, scale in writes:
            val = z[:, z0:z1]
            if scale != 1.0:
                val = val * scale
            ref = out_refs[o_idx]
            if head is None:
                ref[...] = val.astype(ref.dtype)
            else:
                ref[head] = val.astype(ref.dtype)


def _rms_proj(x, g, w, plan, out_defs, rows):
    n, d = x.shape
    grid = (n // rows,)
    out_shape, out_specs = [], []
    for width, dtype, heads in out_defs:
        if heads is None:
            out_shape.append(jax.ShapeDtypeStruct((n, width), dtype))
            out_specs.append(pl.BlockSpec((rows, width), lambda i: (i, 0)))
        else:
            out_shape.append(jax.ShapeDtypeStruct((heads, n, width), dtype))
            out_specs.append(pl.BlockSpec((heads, rows, width), lambda i: (0, i, 0)))
    return pl.pallas_call(
        functools.partial(_rms_proj_body, plan=plan),
        grid=grid,
        in_specs=[pl.BlockSpec((rows, d), lambda i: (i, 0)),
                  pl.BlockSpec((1, d), lambda i: (0, 0)),
                  pl.BlockSpec(w.shape, lambda i: (0, 0))],
        out_specs=out_specs,
        out_shape=out_shape,
        compiler_params=_cparams("parallel"),
        name="rms_proj",
    )(x, g, w)


def _mixer_ab_body(*refs, rb, from_prev, emit_v):
    it = iter(refs)
    x_ref, u_ref, v_ref, gb_ref, gc_ref, xin_ref = (next(it) for _ in range(6))
    if from_prev:
        gcp_ref, xinp_ref = next(it), next(it)
    else:
        hist_ref = next(it)
    lng_ref, lnb_ref, ws_ref, bs_ref, ck_ref, wout_ref = (next(it) for _ in range(6))
    x1_ref, tail_ref = next(it), next(it)
    vout_ref = next(it) if emit_v else None
    wext_ref = next(it)

    w = gc_ref[...] * xin_ref[...]
    if from_prev:
        hist = jnp.where(pl.program_id(0) > 0, gcp_ref[...] * xinp_ref[...], 0.0)
    else:
        hist = hist_ref[0]
    wext_ref[0:SUBLANES, :] = hist
    wext_ref[SUBLANES:, :] = w
    ck = ck_ref[...]
    conv = (ck[0:1] * wext_ref[SUBLANES - 2:SUBLANES - 2 + rb, :]
            + ck[1:2] * wext_ref[SUBLANES - 1:SUBLANES - 1 + rb, :]
            + ck[2:3] * w)
    y_b = gb_ref[...] * conv
    tail_ref[0] = w[rb - SUBLANES:, :]

    u = jax.nn.gelu(u_ref[...])
    v = jax.nn.gelu(v_ref[...])
    lng = lng_ref[...]
    lnb = lnb_ref[...]
    bs = bs_ref[...]
    acc = _dot(y_b.astype(BF16), wout_ref[W_A:, :])
    for h in range(H_A):
        sl = slice(h * C_A, (h + 1) * C_A)
        vh = v[:, sl]
        mu = jnp.mean(vh, axis=-1, keepdims=True)
        xc = vh - mu
        var = jnp.mean(xc * xc, axis=-1, keepdims=True)
        vln = xc * lax.rsqrt(var + EPS) * lng[:, sl] + lnb[:, sl]
        if emit_v:
            vout_ref[:, sl] = vln
        s = _dot(ws_ref[h], vln.astype(BF16)) + bs[:, h:h + 1]
        y_a = u[:, sl] * s
        acc = acc + _dot(y_a.astype(BF16), wout_ref[sl, :])
    x1_ref[...] = x_ref[...] + acc


def _mixer_ab(x, z, hist, lng, lnb, ws, bs, ck, wout, rb, emit_v):
    n, d = x.shape
    nb = n // rb
    from_prev = hist is None
    col = lambda c: pl.BlockSpec((rb, 512), lambda i, c=c: (i, c))
    in_specs = [pl.BlockSpec((rb, d), lambda i: (i, 0)), col(0), col(1), col(2), col(3), col(4)]
    args = [x, z, z, z, z, z]
    if from_prev:
        per = rb // SUBLANES
        prev = lambda c: pl.BlockSpec((SUBLANES, 512), lambda i, c=c: (jnp.maximum(i * per - 1, 0), c))
        in_specs += [prev(3), prev(4)]
        args += [z, z]
    else:
        in_specs += [pl.BlockSpec((1, SUBLANES, 512), lambda i: (i, 0, 0))]
        args += [hist]
    const = lambda a: pl.BlockSpec(a.shape, lambda i, nd=a.ndim: (0,) * nd)
    for a in (lng, lnb, ws, bs, ck, wout):
        in_specs.append(const(a))
        args.append(a)
    out_shape = [jax.ShapeDtypeStruct((n, d), F32), jax.ShapeDtypeStruct((nb, SUBLANES, 512), F32)]
    out_specs = [pl.BlockSpec((rb, d), lambda i: (i, 0)), pl.BlockSpec((1, SUBLANES, 512), lambda i: (i, 0, 0))]
    if emit_v:
        out_shape.append(jax.ShapeDtypeStruct((n, W_A), F32))
        out_specs.append(pl.BlockSpec((rb, W_A), lambda i: (i, 0)))
    return pl.pallas_call(
        functools.partial(_mixer_ab_body, rb=rb, from_prev=from_prev, emit_v=emit_v),
        grid=(nb,),
        in_specs=in_specs,
        out_specs=out_specs,
        out_shape=out_shape,
        scratch_shapes=[pltpu.VMEM((rb + SUBLANES, 512), F32)],
        compiler_params=_cparams("arbitrary"),
        name="mixer_ab",
    )(*args)


def _ffn_body(x_ref, g_ref, wg_ref, wu_ref, wd_ref, o_ref, xn_ref, acc_ref):
    k = pl.program_id(1)

    @pl.when(k == 0)
    def _():
        xn_ref[...] = _rms(x_ref[...], g_ref[...]).astype(BF16)
        acc_ref[...] = jnp.zeros_like(acc_ref)

    xn = xn_ref[...]
    h = jax.nn.silu(_dot(xn, wg_ref[...])) * _dot(xn, wu_ref[...])
    acc_ref[...] += _dot(h.astype(BF16), wd_ref[...])

    @pl.when(k == pl.num_programs(1) - 1)
    def _():
        o_ref[...] = x_ref[...] + acc_ref[...]


def _ffn(x, g, wg, wu, wd, rows, fb):
    n, d = x.shape
    dff = wg.shape[1]
    return pl.pallas_call(
        _ffn_body,
        grid=(n // rows, dff // fb),
        in_specs=[pl.BlockSpec((rows, d), lambda i, k: (i, 0)),
                  pl.BlockSpec((1, d), lambda i, k: (0, 0)),
                  pl.BlockSpec((d, fb), lambda i, k: (0, k)),
                  pl.BlockSpec((d, fb), lambda i, k: (0, k)),
                  pl.BlockSpec((fb, d), lambda i, k: (k, 0))],
        out_specs=pl.BlockSpec((rows, d), lambda i, k: (i, 0)),
        out_shape=jax.ShapeDtypeStruct((n, d), F32),
        scratch_shapes=[pltpu.VMEM((rows, d), BF16), pltpu.VMEM((rows, d), F32)],
        compiler_params=_cparams("parallel", "arbitrary"),
        name="ffn",
    )(x, g, wg, wu, wd)


def _select_body(qi_ref, wi_ref, ve_ref, ki_ref, out_ref, keys_ref, *, tq, lb, nkb, topk, nvalid_fn):
    nv = nvalid_fn(pl.program_id(1))
    wi = wi_ref[...]
    ve = ve_ref[...]
    q_all = jnp.concatenate([qi_ref[h] for h in range(H_I)], axis=0)
    wcols = [wi[:, h:h + 1] for h in range(H_I)]
    lane = lax.broadcasted_iota(I32, (tq, lb), 1)

    def score_block(b, carry):
        kb = ki_ref[0, pl.ds(pl.multiple_of(b * lb, lb), lb), :]
        dots = _dot_nt(q_all, kb)
        sc = jnp.zeros((tq, lb), F32)
        for h in range(H_I):
            sc = sc + wcols[h] * jnp.maximum(dots[h * tq:(h + 1) * tq, :], 0.0)
        bits = lax.bitcast_convert_type(sc, I32)
        key = bits ^ ((bits >> 31) & INT_MAX)
        key = jnp.where(sc == 0.0, 0, key)
        key = jnp.where(lane + b * lb < ve, key, INT_MIN)
        keys_ref[b] = key
        return carry

    lax.fori_loop(0, nv, score_block, 0)

    def count_ge(mid):
        midb = jnp.broadcast_to(mid, (tq, LANES))

        def body(b, acc):
            for c in range(lb // LANES):
                k = keys_ref[b, :, c * LANES:(c + 1) * LANES]
                acc = acc + jnp.where(k >= midb, 1, 0)
            return acc

        acc = lax.fori_loop(0, nv, body, jnp.zeros((tq, LANES), I32))
        return jnp.sum(acc.astype(F32), axis=-1, keepdims=True).astype(I32)

    def bisect(_, st):
        lo, hi, clo, chi = st
        mid = (lo >> 1) + (hi >> 1) + (lo & hi & 1)
        cnt = count_ge(mid)
        active = mid != lo
        up = jnp.logical_and(active, cnt >= topk)
        dn = jnp.logical_and(active, cnt < topk)
        return (jnp.where(up, mid, lo), jnp.where(dn, mid, hi),
                jnp.where(up, cnt, clo), jnp.where(dn, cnt, chi))

    full = lambda v: jnp.full((tq, 1), v, I32)
    lo, hi, clo, chi = lax.fori_loop(0, 32, bisect, (full(INT_MIN), full(INT_MAX), ve, full(0)))

    need = jnp.where(clo > topk, topk - chi, ALL_TIES)
    need = jnp.where(lo == INT_MIN, 0, need).astype(F32)
    lob = jnp.broadcast_to(lo, (tq, LANES))
    needb = jnp.broadcast_to(need, (tq, LANES))
    r = lax.broadcasted_iota(I32, (LANES, LANES), 0)
    c = lax.broadcasted_iota(I32, (LANES, LANES), 1)
    tri = jnp.where(r <= c, 1.0, 0.0).astype(BF16)

    def mask_block(b, seen):
        for cc in range(lb // LANES):
            k = keys_ref[b, :, cc * LANES:(cc + 1) * LANES]
            eq = jnp.where(k == lob, 1.0, 0.0)
            cum = _dot(eq.astype(BF16), tri)
            rank = seen + cum - eq
            take = jnp.where(rank < needb, eq, 0.0)
            sel = jnp.where(k > lob, 1.0, take)
            out_ref[0, b, :, cc * LANES:(cc + 1) * LANES] = jnp.where(sel > 0.5, 0.0, NEG_INF).astype(out_ref.dtype)
            seen = seen + cum[:, LANES - 1:LANES]
        return seen

    lax.fori_loop(0, nv, mask_block, jnp.zeros((tq, 1), F32))

    def fill_block(b, carry):
        out_ref[0, b] = jnp.full((tq, lb), NEG_INF, out_ref.dtype)
        return carry

    lax.fori_loop(nv, nkb, fill_block, 0)


def _select(qi, wi, ve, ki, nbatch, t, tq, topk, nvalid_fn):
    lp = ki.shape[1]
    lb = KEY_BLOCK
    nkb = lp // lb
    nq = t // tq
    row = lambda b, i: b * nq + i
    return pl.pallas_call(
        functools.partial(_select_body, tq=tq, lb=lb, nkb=nkb, topk=topk, nvalid_fn=nvalid_fn),
        grid=(nbatch, nq),
        in_specs=[pl.BlockSpec((H_I, tq, D_I), lambda b, i: (0, row(b, i), 0)),
                  pl.BlockSpec((tq, H_I), lambda b, i: (row(b, i), 0)),
                  pl.BlockSpec((tq, 1), lambda b, i: (row(b, i), 0)),
                  pl.BlockSpec((1, lp, D_I), lambda b, i: (b, 0, 0))],
        out_specs=pl.BlockSpec((1, nkb, tq, lb), lambda b, i: (b, 0, i, 0)),
        out_shape=jax.ShapeDtypeStruct((nbatch, nkb, t, lb), BF16),
        scratch_shapes=[pltpu.VMEM((nkb, tq, lb), I32)],
        compiler_params=_cparams("parallel", "arbitrary"),
        name="index_select",
    )(qi, wi, ve, ki)


PROMPT_TQ = 256
ATTN_TQ = 512
PROMPT_LB = 512
SEL_ROWS = 256
TIE_ROWS = 128
MAPS_PER_DOT = 4
PV_ROWS = 256
ONES_ROWS = 16


def _proj1_t_body(x_ref, g_ref, w_ref, wt_ref, vd32_ref, ki32_ref, kc_ref, kd_ref, ki_ref,
                  qct_ref, qdt_ref, qit_ref, vct_ref, vdt_ref, vc32t_ref, kc32t_ref, kd32t_ref, wit_ref):
    xn32 = _rms(x_ref[...], g_ref[...])
    xn = xn32.astype(BF16)
    xnt = xn32.T.astype(BF16)
    kc_ref[...] = _dot(xn, w_ref[:, 0:512]).astype(BF16)
    kd_ref[...] = _dot(xn, w_ref[:, 512:1024]).astype(BF16)
    vd32_ref[...] = _dot(xn, w_ref[:, 1024:1536])
    z = _dot(xn, w_ref[:, 1536:1536 + LANES])
    ki32_ref[...] = z[:, 0:D_I]
    ki_ref[...] = z[:, 0:D_I].astype(BF16)
    for c, (head_ref, full_ref) in enumerate(((qct_ref, None), (qdt_ref, None), (qit_ref, None), (vct_ref, vc32t_ref),
                                              (vdt_ref, None), (None, kc32t_ref), (None, kd32t_ref))):
        zt = _dot(wt_ref[c * 512:(c + 1) * 512, :], xnt)
        if full_ref is not None:
            full_ref[...] = zt
        if head_ref is not None:
            nh = head_ref.shape[0]
            w = 512 // nh
            for h in range(nh):
                head_ref[h, 0:w, :] = zt[h * w:(h + 1) * w, :].astype(BF16)
                if head_ref.shape[1] > w:
                    pad = lax.broadcasted_iota(I32, (head_ref.shape[1] - w, zt.shape[1]), 0)
                    head_ref[h, w:, :] = jnp.where(pad == 0, 1.0, 0.0).astype(BF16)
    zt = _dot(wt_ref[3584:3584 + LANES, :], xnt)
    wit_ref[...] = zt[0:H_I, :]


def _proj1_t(x, g, w, wt, rows):
    n, d = x.shape
    full = lambda width: (jax.ShapeDtypeStruct((n, width), F32), pl.BlockSpec((rows, width), lambda i: (i, 0)))
    packed = lambda width: (jax.ShapeDtypeStruct((n, width), BF16), pl.BlockSpec((rows, width), lambda i: (i, 0)))
    heads_t = lambda nh, width: (jax.ShapeDtypeStruct((nh, width, n), BF16),
                                 pl.BlockSpec((nh, width, rows), lambda i: (0, 0, i)))
    full_t = lambda: (jax.ShapeDtypeStruct((512, n), F32), pl.BlockSpec((512, rows), lambda i: (0, i)))
    outs = [full(512), full(D_I), packed(512), packed(512), packed(D_I),
            heads_t(8, DH), heads_t(8, DH), heads_t(8, DH), heads_t(8, DH + ONES_ROWS), heads_t(H_D, 2 * DH + ONES_ROWS),
            full_t(), full_t(), full_t(),
            (jax.ShapeDtypeStruct((H_I, n), F32), pl.BlockSpec((H_I, rows), lambda i: (0, i)))]
    return pl.pallas_call(
        _proj1_t_body,
        grid=(n // rows,),
        in_specs=[pl.BlockSpec((rows, d), lambda i: (i, 0)),
                  pl.BlockSpec((1, d), lambda i: (0, 0)),
                  pl.BlockSpec(w.shape, lambda i: (0, 0)),
                  pl.BlockSpec(wt.shape, lambda i: (0, 0))],
        out_specs=[o[1] for o in outs],
        out_shape=[o[0] for o in outs],
        compiler_params=_cparams("parallel"),
        name="proj1_t",
    )(x, g, w, wt)


def _sublane_all(x8, op):
    for shift in (4, 2, 1):
        x8 = op(x8, pltpu.roll(x8, shift, 0))
    return x8


def _select_t_body(qit_ref, wit_ref, ve_ref, ki_ref, out_ref, keys_ref, gmax_ref, *, tq, lp, topk):
    rb = SEL_ROWS
    nv = (pl.program_id(0) + 1) * (tq // rb)
    wit = wit_ref[...]
    ve = ve_ref[...]
    rows = lax.broadcasted_iota(I32, (rb, tq), 0)
    gmax_ref[...] = jnp.full((rb, tq), INT_MIN, I32)
    q_all = jnp.concatenate([qit_ref[h] for h in range(H_I)], axis=1)

    def score_chunks(r0, nch):
        dots = _dot(ki_ref[pl.ds(r0, nch * rb), :], q_all)
        for ch in range(nch):
            part = dots[ch * rb:(ch + 1) * rb, :]
            sc = jnp.zeros((rb, tq), F32)
            for h in range(H_I):
                sc = sc + wit[h:h + 1, :] * jnp.maximum(part[:, h * tq:(h + 1) * tq], 0.0)
            bits = lax.bitcast_convert_type(sc, I32)
            key = bits ^ ((bits >> 31) & INT_MAX)
            key = jnp.where(sc == 0.0, 0, key)
            key = jnp.where(rows + (r0 + ch * rb) < ve, key, INT_MIN)
            keys_ref[pl.ds(r0 + ch * rb, rb), :] = key
            gmax_ref[...] = jnp.maximum(gmax_ref[...], key)

    def score_pair(c, carry):
        score_chunks(pl.multiple_of(c * 2 * rb, 2 * rb), 2)
        return carry

    lax.fori_loop(0, nv // 2, score_pair, 0)
    pl.when(nv % 2 == 1)(lambda: score_chunks(pl.multiple_of((nv - 1) * rb, rb), 1))

    def count_ge(mid):
        midb = jnp.broadcast_to(mid, (SUBLANES, tq))

        def add_rows(r0, nrows, accs):
            kc = keys_ref[pl.ds(r0, nrows), :]
            accs = list(accs)
            for g in range(nrows // SUBLANES):
                a = g % len(accs)
                accs[a] = accs[a] + jnp.where(kc[g * SUBLANES:(g + 1) * SUBLANES, :] >= midb, 1, 0)
            return tuple(accs)

        accs = lax.fori_loop(0, nv // 2, lambda c, a: add_rows(pl.multiple_of(c * 2 * rb, 2 * rb), 2 * rb, a),
                             (jnp.zeros((SUBLANES, tq), I32),) * 4)
        accs = lax.cond(nv % 2 == 1, lambda a: add_rows(pl.multiple_of((nv - 1) * rb, rb), rb, a), lambda a: a, accs)
        return _sublane_all(accs[0] + accs[1] + accs[2] + accs[3], jnp.add)[0:1, :]

    floor_avg = lambda a, b: (a >> 1) + (b >> 1) + (a & b & 1)

    def pending(lo, hi, clo):
        open_ = jnp.logical_and(floor_avg(lo, hi) != lo, clo > topk)
        return jnp.max(jnp.where(open_, 1.0, 0.0))

    def bisect(st):
        lo, hi, clo, chi, _ = st
        mid = floor_avg(lo, hi)
        cnt = count_ge(mid)
        active = jnp.logical_and(mid != lo, clo > topk)
        up = jnp.logical_and(active, cnt >= topk)
        dn = jnp.logical_and(active, cnt < topk)
        lo, hi = jnp.where(up, mid, lo), jnp.where(dn, mid, hi)
        clo, chi = jnp.where(up, cnt, clo), jnp.where(dn, cnt, chi)
        return lo, hi, clo, chi, pending(lo, hi, clo)

    g8 = gmax_ref[0:SUBLANES, :]
    h8 = g8
    for g in range(1, rb // SUBLANES):
        blk = gmax_ref[g * SUBLANES:(g + 1) * SUBLANES, :]
        g8 = jnp.minimum(g8, blk)
        h8 = jnp.maximum(h8, blk)
    lo0 = _sublane_all(g8, jnp.minimum)[0:1, :]
    hi0 = _sublane_all(h8, jnp.maximum)[0:1, :] + 1
    clo0 = jnp.where(lo0 == INT_MIN, ve, count_ge(lo0))
    chi0 = jnp.zeros((1, tq), I32)

    def probe_zero():
        c_pos, c_nonneg = count_ge(jnp.full((1, tq), 1, I32)), count_ge(jnp.zeros((1, tq), I32))
        above = c_pos >= topk
        below = c_nonneg < topk
        lo1 = jnp.where(above, jnp.maximum(lo0, 1), jnp.where(below, lo0, 0))
        clo1 = jnp.where(above, jnp.where(lo0 >= 1, clo0, c_pos), jnp.where(below, clo0, c_nonneg))
        hi1 = jnp.where(above, hi0, jnp.where(below, jnp.minimum(hi0, 0), 1))
        chi1 = jnp.where(above, chi0, jnp.where(below, jnp.where(hi0 <= 0, chi0, c_nonneg), c_pos))
        return lo1, hi1, clo1, chi1

    lo1, hi1, clo1, chi1 = lax.cond(jnp.max(jnp.where(lo0 < 1, 1.0, 0.0)) > 0.5, probe_zero,
                                    lambda: (lo0, hi0, clo0, chi0))
    lo, hi, clo, chi, _ = lax.while_loop(
        lambda st: st[4] > 0.5, bisect, (lo1, hi1, clo1, chi1, pending(lo1, hi1, clo1)))

    need = jnp.where(clo > topk, topk - chi, ALL_TIES)
    need = jnp.where(lo == INT_MIN, 0, need).astype(F32)
    tr = TIE_ROWS
    lob = jnp.broadcast_to(lo, (tr, tq))
    needb = jnp.broadcast_to(need, (tr, tq))
    r = lax.broadcasted_iota(I32, (tr, tr), 0)
    c = lax.broadcasted_iota(I32, (tr, tr), 1)
    tri = jnp.where(c <= r, 1.0, 0.0).astype(BF16)

    def mask_chunk(cidx, seen):
        r0 = pl.multiple_of(cidx * tr, tr)
        k = keys_ref[pl.ds(r0, tr), :]
        eq = jnp.where(k == lob, 1.0, 0.0)
        cum = _dot(tri, eq.astype(BF16))
        take = jnp.where(seen + cum - eq < needb, eq, 0.0)
        sel = jnp.where(k > lob, 1.0, take)
        out_ref[pl.ds(r0, tr), :] = jnp.where(sel > 0.5, 0.0, NEG_INF).astype(out_ref.dtype)
        return seen + cum[tr - 1:tr, :]

    def plain_chunk(cidx, carry):
        r0 = pl.multiple_of(cidx * rb, rb)
        k = keys_ref[pl.ds(r0, rb), :]
        hit = jnp.where(k == INT_MIN, NEG_INF, 0.0)
        out_ref[pl.ds(r0, rb), :] = jnp.where(k >= lo, hit, NEG_INF).astype(out_ref.dtype)
        return carry

    any_tie = jnp.max(jnp.where(clo > topk, 1.0, 0.0)) > 0.5
    lax.cond(any_tie,
             lambda: lax.fori_loop(0, nv * (rb // tr), mask_chunk, jnp.zeros((1, tq), F32)),
             lambda: lax.fori_loop(0, nv, plain_chunk, jnp.zeros((1, tq), F32)))

    def fill_chunk(cidx, carry):
        out_ref[pl.ds(pl.multiple_of(cidx * rb, rb), rb), :] = jnp.full((rb, tq), NEG_INF, out_ref.dtype)
        return carry

    lax.fori_loop(nv, lp // rb, fill_chunk, 0)


def _select_t(qit, wit, ve, ki, topk):
    t = qit.shape[2]
    tq = PROMPT_TQ
    return pl.pallas_call(
        functools.partial(_select_t_body, tq=tq, lp=t, topk=topk),
        grid=(t // tq,),
        in_specs=[pl.BlockSpec((H_I, D_I, tq), lambda i: (0, 0, i)),
                  pl.BlockSpec((H_I, tq), lambda i: (0, i)),
                  pl.BlockSpec((1, tq), lambda i: (0, i)),
                  pl.BlockSpec((t, D_I), lambda i: (0, 0))],
        out_specs=pl.BlockSpec((t, tq), lambda i: (0, i)),
        out_shape=jax.ShapeDtypeStruct((t, t), BF16),
        scratch_shapes=[pltpu.VMEM((t, tq), I32), pltpu.VMEM((SEL_ROWS, tq), I32)],
        compiler_params=_cparams("parallel"),
        name="index_select_t",
    )(qit, wit, ve, ki)


def _attn_t_body(qi_ref, kj_ref, qct_ref, qdt_ref, kc_ref, kd_ref, vct_ref, vdt_ref, mask_ref, bt_ref, lam_ref,
                 g_ref, o_ref, m_ref, accc_ref, accd_ref, ot_ref, qbd_ref, s_ref, *, tq, per, e_far,
                 lam_init):
    i = qi_ref[pl.program_id(0)]
    j = kj_ref[pl.program_id(0)]
    e = i - j * per
    grp = MAPS_PER_DOT

    @pl.when(j == 0)
    def _():
        m_ref[...] = jnp.full(m_ref.shape, NEG_INF, F32)
        accc_ref[...] = jnp.zeros_like(accc_ref)
        accd_ref[...] = jnp.zeros_like(accd_ref)
        qbd_ref[...] = jnp.zeros_like(qbd_ref)
        for mp in range(N_MAPS):
            q = qct_ref[mp] if mp < H_C else qdt_ref[mp - H_C]
            a = mp % grp
            qbd_ref[mp // grp, a * DH:(a + 1) * DH, a * tq:(a + 1) * tq] = q

    def logits(g):
        k_ref = kc_ref if g < H_C // grp else kd_ref
        half = g % (H_C // grp)
        return _dot(k_ref[:, half * grp * DH:(half + 1) * grp * DH], qbd_ref[g])

    def step(near):
        sel = mask_ref[...].astype(F32)
        sel = jnp.concatenate([sel] * grp, axis=1)
        for g in range(N_MAPS // grp):
            s = logits(g) + sel if g < H_C // grp else logits(g)
            if near:
                heads = [g * grp + a if g < H_C // grp else H_C + (g * grp + a - H_C) // 2 for a in range(grp)]
                s = s + jnp.concatenate([bt_ref[0, h].astype(F32) for h in heads], axis=1)
            s_ref[g] = s
        for mp in range(N_MAPS):
            g, a = mp // grp, mp % grp
            cols = slice(a * tq, (a + 1) * tq)
            if mp < H_C:
                vt, acc_ref, a_idx = vct_ref[mp], accc_ref, mp
            else:
                vt, acc_ref, a_idx = vdt_ref[(mp - H_C) // 2], accd_ref, mp - H_C

            biased = lambda rows: s_ref[g, rows, cols]
            m_prev = m_ref[mp]
            m_new = jnp.maximum(m_prev, jnp.max(biased(slice(None)), axis=0, keepdims=True))
            alpha = jnp.exp2(m_prev - m_new)
            m_ref[mp] = m_new
            pv = None
            for r0 in range(0, s_ref.shape[1], PV_ROWS):
                rows = slice(r0, r0 + PV_ROWS)
                part = _dot(vt[:, rows], jnp.exp2(biased(rows) - m_new).astype(BF16))
                pv = part if pv is None else pv + part
            acc_ref[a_idx] = alpha * acc_ref[a_idx] + pv

    pl.when(e < e_far)(functools.partial(step, True))
    pl.when(e >= e_far)(functools.partial(step, False))

    @pl.when(j == i // per)
    def _():
        for h in range(H_C):
            ot_ref[h * DH:(h + 1) * DH, :] = accc_ref[h, 0:DH, :] / accc_ref[h, DH:DH + 1, :]
        lam = lam_ref[...]
        g = g_ref[...]
        dv = 2 * DH
        for h in range(H_D):
            a0 = accd_ref[2 * h, 0:dv, :] / accd_ref[2 * h, dv:dv + 1, :]
            a1 = accd_ref[2 * h + 1, 0:dv, :] / accd_ref[2 * h + 1, dv:dv + 1, :]
            od = a0 - lam * a1
            od = od * lax.rsqrt(jnp.mean(od * od, axis=0, keepdims=True) + EPS) * g * (1.0 - lam_init)
            ot_ref[W_C + h * 2 * DH:W_C + (h + 1) * 2 * DH, :] = od
        o_ref[...] = ot_ref[...].T.astype(o_ref.dtype)


def _attend_t(qct, qdt, kc, kd, vct, vdt, mask, btiles, lam, g, lam_init):
    t = qct.shape[2]
    tq, lb = ATTN_TQ, PROMPT_LB
    per = lb // tq
    e_far = btiles.shape[0]
    pairs = [(i, j) for i in range(t // tq) for j in range(i // per + 1)]
    qi_tab = jnp.asarray(np.array([p[0] for p in pairs], np.int32))
    kj_tab = jnp.asarray(np.array([p[1] for p in pairs], np.int32))
    qspec = pl.BlockSpec((H_C, DH, tq), lambda s, qi, kj: (0, 0, qi[s]))
    kspec = pl.BlockSpec((lb, H_C * DH), lambda s, qi, kj: (kj[s], 0))
    grid_spec = pltpu.PrefetchScalarGridSpec(
        num_scalar_prefetch=2,
        grid=(len(pairs),),
        in_specs=[qspec, qspec, kspec, kspec,
                  pl.BlockSpec((H_C, DH + ONES_ROWS, lb), lambda s, qi, kj: (0, 0, kj[s])),
                  pl.BlockSpec((H_D, 2 * DH + ONES_ROWS, lb), lambda s, qi, kj: (0, 0, kj[s])),
                  pl.BlockSpec((lb, tq), lambda s, qi, kj: (kj[s], qi[s])),
                  pl.BlockSpec((1, N_HEADS_BIAS, lb, tq),
                               lambda s, qi, kj: (jnp.minimum(qi[s] - kj[s] * per, e_far - 1), 0, 0, 0)),
                  pl.BlockSpec((1, 1), lambda s, qi, kj: (0, 0)),
                  pl.BlockSpec((2 * DH, 1), lambda s, qi, kj: (0, 0))],
        out_specs=pl.BlockSpec((tq, W_C + W_D), lambda s, qi, kj: (qi[s], 0)),
        scratch_shapes=[pltpu.VMEM((N_MAPS, 1, tq), F32),
                        pltpu.VMEM((H_C, DH + ONES_ROWS, tq), F32),
                        pltpu.VMEM((2 * H_D, 2 * DH + ONES_ROWS, tq), F32),
                        pltpu.VMEM((W_C + W_D, tq), F32),
                        pltpu.VMEM((N_MAPS // MAPS_PER_DOT, MAPS_PER_DOT * DH, MAPS_PER_DOT * tq), BF16),
                        pltpu.VMEM((N_MAPS // MAPS_PER_DOT, lb, MAPS_PER_DOT * tq), F32)])
    return pl.pallas_call(
        functools.partial(_attn_t_body, tq=tq, per=per, e_far=e_far, lam_init=lam_init),
        grid_spec=grid_spec,
        out_shape=jax.ShapeDtypeStruct((t, W_C + W_D), BF16),
        compiler_params=_cparams("arbitrary"),
        name="attend_t",
    )(qi_tab, kj_tab, qct, qdt, kc, kd, vct, vdt, mask, btiles, lam, g)


def _attn_s_body(qc_ref, qd_ref, ck_ref, cv_ref, dk_ref, dv_ref, nck_ref, ncv_ref, ndk_ref, ndv_ref,
                 mask_ref, bias_ref, lam_ref, g_ref, o_ref, m_ref, l_ref, acc_ref, *, ts, ncache, lam_init):
    j = pl.program_id(1)
    rows = H_C * ts

    @pl.when(j == 0)
    def _():
        m_ref[...] = jnp.full(m_ref.shape, NEG_INF, F32)
        l_ref[...] = jnp.zeros_like(l_ref)
        acc_ref[...] = jnp.zeros_like(acc_ref)

    def step(kc_t, vc_t, kd_t, vd):
        sel = jnp.tile(mask_ref[0, 0].astype(F32), (H_C, 1))
        bias = bias_ref[0]
        s_c = _dot(qc_ref[0], kc_t.astype(BF16)) + sel + bias[0:rows]
        s_d = _dot(qd_ref[0], kd_t.astype(BF16)) + bias[rows:2 * rows]
        for idx, (s, v) in enumerate(((s_c, vc_t), (s_d, vd))):
            m_prev = m_ref[idx]
            m_new = jnp.maximum(m_prev, jnp.max(s, axis=-1, keepdims=True))
            alpha = jnp.exp(m_prev - m_new)
            p = jnp.exp(s - m_new)
            l_ref[idx] = alpha * l_ref[idx] + jnp.sum(p, axis=-1, keepdims=True)
            m_ref[idx] = m_new
            p = p.astype(BF16)
            pv = _dot_nt(p, v.astype(BF16)) if idx == 0 else _dot(p, v.astype(BF16))
            acc_ref[idx] = alpha * acc_ref[idx] + pv

    heads_packed = lambda ref: jnp.concatenate([ref[0, :, h, :] for h in range(H_D)], axis=1)

    @pl.when(j < ncache)
    def _():
        step(ck_ref[0], cv_ref[0], dk_ref[0], heads_packed(dv_ref))

    @pl.when(j == ncache)
    def _():
        step(nck_ref[0], ncv_ref[0], ndk_ref[0], heads_packed(ndv_ref))
        for h in range(H_C):
            r = slice(h * ts, (h + 1) * ts)
            o_ref[:, h * DH:(h + 1) * DH] = (acc_ref[0, r, h * DH:(h + 1) * DH] / l_ref[0, r, :]).astype(o_ref.dtype)
        lam = lam_ref[...]
        g = g_ref[...]
        for h in range(H_D):
            r0 = slice(2 * h * ts, (2 * h + 1) * ts)
            r1 = slice((2 * h + 1) * ts, (2 * h + 2) * ts)
            c = slice(h * 2 * DH, (h + 1) * 2 * DH)
            od = acc_ref[1, r0, c] / l_ref[1, r0, :] - lam * (acc_ref[1, r1, c] / l_ref[1, r1, :])
            od = _rms(od, g) * (1.0 - lam_init)
            o_ref[:, W_C + h * 2 * DH:W_C + (h + 1) * 2 * DH] = od.astype(o_ref.dtype)


def _attend_s(qbd_c, qbd_d, caches, news, mask, bias, lam, g, lam_init):
    nb, rows, _ = qbd_c.shape
    ts = rows // H_C
    lb = KEY_BLOCK
    ncache = caches[3].shape[1] // lb
    qspec = pl.BlockSpec((1, rows, W_C), lambda b, j: (b, 0, 0))
    ctspec = pl.BlockSpec((1, W_C, lb), lambda b, j: (b, 0, jnp.minimum(j, ncache - 1)))
    cspec = pl.BlockSpec((1, lb, H_D, 2 * DH), lambda b, j: (b, jnp.minimum(j, ncache - 1), 0, 0))
    ntspec = pl.BlockSpec((1, W_C, lb), lambda b, j: (b, 0, 0))
    nspec = pl.BlockSpec((1, lb, H_D, 2 * DH), lambda b, j: (b, 0, 0, 0))
    return pl.pallas_call(
        functools.partial(_attn_s_body, ts=ts, ncache=ncache, lam_init=lam_init),
        grid=(nb, ncache + 1),
        in_specs=[qspec, qspec, ctspec, ctspec, ctspec, cspec, ntspec, ntspec, ntspec, nspec,
                  pl.BlockSpec((1, 1, ts, lb), lambda b, j: (b, j, 0, 0)),
                  pl.BlockSpec((1, 2 * rows, lb), lambda b, j: (j, 0, 0)),
                  pl.BlockSpec((1, 1), lambda b, j: (0, 0)),
                  pl.BlockSpec((1, 2 * DH), lambda b, j: (0, 0))],
        out_specs=pl.BlockSpec((ts, W_C + W_D), lambda b, j: (b, 0)),
        out_shape=jax.ShapeDtypeStruct((nb * ts, W_C + W_D), BF16),
        scratch_shapes=[pltpu.VMEM((2, rows, 1), F32), pltpu.VMEM((2, rows, 1), F32),
                        pltpu.VMEM((2, rows, W_C), F32)],
        compiler_params=_cparams("parallel", "arbitrary"),
        name="attend_s",
    )(qbd_c, qbd_d, *caches, *news, mask, bias, lam, g)


def _out_router_body(x_ref, o_ref, w_ref, g_ref, r_ref, x3_ref, xn_ref, gate_ref, gatet_ref):
    x3 = x_ref[...] + _dot(o_ref[...], w_ref[...])
    x3_ref[...] = x3
    xn = _rms(x3, g_ref[...]).astype(BF16)
    xn_ref[...] = xn
    logits = _dot(xn, r_ref[...])
    lane = lax.broadcasted_iota(I32, logits.shape, 1)
    logits = jnp.where(lane < N_EXP, logits, -jnp.inf)
    m1 = jnp.max(logits, axis=-1, keepdims=True)
    i1 = jnp.min(jnp.where(logits == m1, lane, LANES), axis=-1, keepdims=True)
    rest = jnp.where(lane == i1, -jnp.inf, logits)
    m2 = jnp.max(rest, axis=-1, keepdims=True)
    i2 = jnp.min(jnp.where(rest == m2, lane, LANES), axis=-1, keepdims=True)
    e = jnp.exp(m2 - m1)
    g1 = 1.0 / (1.0 + e)
    g2 = e / (1.0 + e)
    gate = jnp.where(lane == i1, g1, 0.0) + jnp.where(lane == i2, g2, 0.0)
    gate_ref[...] = gate
    gatet_ref[...] = gate.T[0:GATE_ROWS, :]


def _out_router(x, o, w, g, router, rows):
    n, d = x.shape
    return pl.pallas_call(
        _out_router_body,
        grid=(n // rows,),
        in_specs=[pl.BlockSpec((rows, d), lambda i: (i, 0)),
                  pl.BlockSpec((rows, d), lambda i: (i, 0)),
                  pl.BlockSpec(w.shape, lambda i: (0, 0)),
                  pl.BlockSpec((1, d), lambda i: (0, 0)),
                  pl.BlockSpec(router.shape, lambda i: (0, 0))],
        out_specs=[pl.BlockSpec((rows, d), lambda i: (i, 0)),
                   pl.BlockSpec((rows, d), lambda i: (i, 0)),
                   pl.BlockSpec((rows, LANES), lambda i: (i, 0)),
                   pl.BlockSpec((GATE_ROWS, rows), lambda i: (0, i))],
        out_shape=[jax.ShapeDtypeStruct((n, d), F32), jax.ShapeDtypeStruct((n, d), BF16),
                   jax.ShapeDtypeStruct((n, LANES), F32), jax.ShapeDtypeStruct((GATE_ROWS, n), F32)],
        compiler_params=_cparams("parallel"),
        name="out_router",
    )(x, o, w, g, router)


def _t5_bucket(rel):
    half = N_BUCKETS // 2
    max_exact = half // 2
    ret = np.where(rel > 0, half, 0)
    n = np.abs(rel)
    nf = np.maximum(n, 1).astype(np.float32)
    large = max_exact + (np.log(nf / np.float32(max_exact)) / np.float32(math.log(MAX_DIST / max_exact))
                         * np.float32(half - max_exact)).astype(np.int32)
    large = np.minimum(large, half - 1)
    return (ret + np.where(n < max_exact, n, large)).astype(np.int32)


def _bias_tile(rel_bias, q_pos, k_pos, k_real):
    rel = k_pos[None, :] - q_pos[:, None]
    onehot = (jnp.asarray(_t5_bucket(rel).astype(np.int8))[:, :, None]
              == jnp.arange(N_BUCKETS, dtype=jnp.int8)).astype(F32)
    bias = jnp.einsum("qkb,bh->hqk", onehot, rel_bias.astype(F32), precision=lax.Precision.HIGHEST)
    ok = np.logical_and(k_pos[None, :] // CHUNK <= q_pos[:, None] // CHUNK, k_pos[None, :] < k_real)
    return jnp.where(ok[None], bias, NEG_INF)


def _in1_plan():
    scale = DH ** -0.5
    heads64 = lambda o, sc=1.0: [(o, h, h * DH, (h + 1) * DH, sc) for h in range(8)]
    plan = [
        ((0, 512), heads64(0, scale)),
        ((512, 1024), [(1, None, 0, 512, 1.0)] + heads64(2)),
        ((1024, 1536), [(3, None, 0, 512, 1.0)] + heads64(4)),
        ((1536, 2048), heads64(5, scale)),
        ((2048, 2560), [(6, None, 0, 512, 1.0)] + heads64(7)),
        ((2560, 3072), [(8, None, 0, 512, 1.0)] + [(9, h, h * 128, (h + 1) * 128, 1.0) for h in range(H_D)]),
        ((3072, 3584), heads64(10)),
        ((3584, 3712), [(11, None, 0, D_I, 1.0), (12, None, 0, D_I, 1.0), (13, None, D_I, D_I + H_I, 1.0)]),
    ]
    out_defs = [(DH, BF16, 8), (512, F32, None), (DH, BF16, 8), (512, F32, None), (DH, BF16, 8),
                (DH, BF16, 8), (512, F32, None), (DH, BF16, 8), (512, F32, None), (2 * DH, BF16, H_D),
                (D_I, BF16, 8), (D_I, F32, None), (D_I, BF16, None), (H_I, F32, None)]
    return plan, out_defs


def _layer0(x, hist, rb, emit_v, p):
    n = x.shape[0]
    rows = _largest_divisor(n, 512, 16)
    plan = [((0, p["w_in0"].shape[1]), [(0, None, 0, p["w_in0"].shape[1], 1.0)])]
    (z,) = _rms_proj(x, p["ln_mix0"], p["w_in0"], plan, [(p["w_in0"].shape[1], F32, None)], rows)
    ws = p["ws_prompt"] if hist is None else p["ws_sample"]
    bs = p["bs_prompt"] if hist is None else p["bs_sample"]
    outs = _mixer_ab(x, z, hist, p["gmlp_ln_g"], p["gmlp_ln_b"], ws, bs, p["conv_k"], p["w_out0"], rb, emit_v)
    x1 = outs[0]
    x2 = _ffn(x1, p["ln_ffn0"], p["ffn_wg"], p["ffn_wu"], p["ffn_wd"], rows, p["ffn_fb"])
    return (x2,) + tuple(outs[1:])


GATE_ROWS = 16
MOE_ROWS = 128


def _moe_routed_body(x_ref, xn_ref, gate_ref, gatet_ref, gf_ref, wg_ref, wu_ref, wd_ref, y_ref,
                     triu_ref, tril_ref, crow_ref, ccol_ref, xs_ref, ge_ref, acc_ref, yblk_ref, nsub_ref,
                     *, tb, sub):
    e = pl.program_id(1)
    k = pl.program_id(2)
    first_k = k == 0
    last_k = k == pl.num_programs(2) - 1

    @pl.when(jnp.logical_and(e == 0, first_k))
    def _():
        r = lax.broadcasted_iota(I32, (tb, tb), 0)
        c = lax.broadcasted_iota(I32, (tb, tb), 1)
        triu_ref[...] = jnp.where(r < c, 1.0, 0.0).astype(BF16)
        tril_ref[...] = jnp.where(c < r, 1.0, 0.0).astype(BF16)
        crow_ref[...] = _dot(jnp.where(gatet_ref[...] > 0.0, 1.0, 0.0).astype(BF16), triu_ref[...])
        ccol_ref[...] = _dot(tril_ref[...], jnp.where(gate_ref[...] > 0.0, 1.0, 0.0).astype(BF16))
        yblk_ref[...] = jnp.zeros_like(yblk_ref)
        acc_ref[...] = jnp.zeros_like(acc_ref)

    @pl.when(first_k)
    def _():
        g_e = gatet_ref[pl.ds(e, 1), :]
        m_e = g_e > 0.0
        c_e = crow_ref[pl.ds(e, 1), :]
        nsub = (jnp.sum(jnp.where(m_e, 1.0, 0.0)).astype(I32) + sub - 1) // sub
        nsub_ref[0] = nsub
        xn = xn_ref[...]
        slot = lax.broadcasted_iota(I32, (sub, tb), 0).astype(F32)

        def pack(s, carry):
            r0 = pl.multiple_of(s * sub, sub)
            hit = jnp.logical_and(m_e, c_e == slot + (s * sub).astype(F32))
            onehot = jnp.where(hit, 1.0, 0.0)
            xs_ref[pl.ds(r0, sub), :] = _dot(onehot.astype(BF16), xn).astype(BF16)
            ge_ref[pl.ds(r0, sub), :] = jnp.sum(onehot * g_e, axis=-1, keepdims=True)
            acc_ref[pl.ds(r0, sub), :] = jnp.zeros((sub, acc_ref.shape[1]), F32)
            return carry

        lax.fori_loop(0, nsub, pack, 0)

    nsub = nsub_ref[0]

    def experts(rows):
        xs = xs_ref[rows, :]
        h = jax.nn.silu(_dot(xs, wg_ref[0])) * _dot(xs, wu_ref[0])
        acc_ref[rows, :] += _dot((ge_ref[rows, :] * h).astype(BF16), wd_ref[0])

    def pair(s, carry):
        experts(pl.ds(pl.multiple_of(s * 2 * sub, 2 * sub), 2 * sub))
        return carry

    lax.fori_loop(0, nsub // 2, pair, 0)
    pl.when(nsub % 2 == 1)(lambda: experts(pl.ds(pl.multiple_of((nsub - 1) * sub, sub), sub)))

    @pl.when(last_k)
    def _():
        lane = lax.broadcasted_iota(I32, (tb, LANES), 1)
        pick = lambda a: jnp.sum(jnp.where(lane == e, a, 0.0), axis=-1, keepdims=True)
        m_e = pick(gate_ref[...]) > 0.0
        c_e = pick(ccol_ref[...])
        slot = lax.broadcasted_iota(I32, (tb, sub), 1).astype(F32)

        def unpack(s, carry):
            rows = pl.ds(pl.multiple_of(s * sub, sub), sub)
            hit = jnp.logical_and(m_e, c_e == slot + (s * sub).astype(F32))
            onehot = jnp.where(hit, 1.0, 0.0).astype(BF16)
            y = acc_ref[rows, :]
            hi = y.astype(BF16)
            lo = (y - hi.astype(F32)).astype(BF16)
            yblk_ref[...] += _dot(jnp.concatenate([onehot, onehot], axis=1), jnp.concatenate([hi, lo], axis=0))
            return carry

        lax.fori_loop(0, nsub, unpack, 0)

    @pl.when(jnp.logical_and(e == pl.num_programs(1) - 1, last_k))
    def _():
        y_ref[...] = _rms(x_ref[...] + yblk_ref[...], gf_ref[...])


def _moe_routed(x, xn, gate, gatet, gf, wg, wu, wd, tb, fb):
    n, d = x.shape
    ne, _, dff = wg.shape
    return pl.pallas_call(
        functools.partial(_moe_routed_body, tb=tb, sub=min(MOE_ROWS, tb)),
        grid=(n // tb, ne, dff // fb),
        in_specs=[pl.BlockSpec((tb, d), lambda i, e, k: (i, 0)),
                  pl.BlockSpec((tb, d), lambda i, e, k: (i, 0)),
                  pl.BlockSpec((tb, LANES), lambda i, e, k: (i, 0)),
                  pl.BlockSpec((GATE_ROWS, tb), lambda i, e, k: (0, i)),
                  pl.BlockSpec((1, d), lambda i, e, k: (0, 0)),
                  pl.BlockSpec((1, d, fb), lambda i, e, k: (e, 0, k)),
                  pl.BlockSpec((1, d, fb), lambda i, e, k: (e, 0, k)),
                  pl.BlockSpec((1, fb, d), lambda i, e, k: (e, k, 0))],
        out_specs=pl.BlockSpec((tb, d), lambda i, e, k: (i, 0)),
        out_shape=jax.ShapeDtypeStruct((n, d), F32),
        scratch_shapes=[pltpu.VMEM((tb, tb), BF16), pltpu.VMEM((tb, tb), BF16),
                        pltpu.VMEM((GATE_ROWS, tb), F32), pltpu.VMEM((tb, LANES), F32),
                        pltpu.VMEM((tb, d), BF16), pltpu.VMEM((tb, 1), F32),
                        pltpu.VMEM((tb, d), F32), pltpu.VMEM((tb, d), F32),
                        pltpu.SMEM((1,), I32)],
        compiler_params=_cparams("parallel", "arbitrary", "arbitrary"),
        name="moe_routed",
    )(x, xn, gate, gatet, gf, wg, wu, wd)


def _layer1_tail(x, o, p):
    n = x.shape[0]
    rows = _largest_divisor(n, 512, 16)
    x3, xn, gate, gatet = _out_router(x, o, p["w_out1"], p["ln_ffn1"], p["router"], rows)
    return _moe_routed(x3, xn, gate, gatet, p["ln_final"], p["exp_wg"], p["exp_wu"], p["exp_wd"],
                       _largest_divisor(n, 1024, 128), p["exp_fb"])


def kernel(x_prompt, x_sample, state_b_conv, cache_c_k, cache_c_v, cache_idx_k, cache_d_k, cache_d_v, rel_bias, ln_mix, ln_ffn, ln_final, w_in0, gmlp_ln_g, gmlp_ln_b, gmlp_ws, gmlp_bs, conv_k, w_out0, ffn_wg, ffn_wu, ffn_wd, w_in1, lam_qk, subln_g, w_out1, router, exp_wg, exp_wu, exp_wd):
    bp, seq, d = x_prompt.shape
    bs_, ts, _ = x_sample.shape
    past = cache_c_k.shape[2]
    assert bp == 1 and ln_mix.shape[0] == 2 and seq % PROMPT_LB == 0 and ts % SUBLANES == 0 and ts <= CHUNK
    assert past % KEY_BLOCK == 0
    gmlp_chunk = gmlp_ws.shape[-1]
    lam_init = 0.8 - 0.6 * math.exp(-0.3 * 1)

    def ws_masked(rows):
        r = jnp.arange(rows)
        ok = (r[None, :] // CHUNK) <= (r[:, None] // CHUNK)
        return jnp.where(ok[None], gmlp_ws[0][:, :rows, :rows], 0.0).astype(BF16)

    in1_pad = (-w_in1.shape[2]) % LANES
    lf = lam_qk[0].astype(F32)
    lam = (jnp.exp(jnp.sum(lf[0] * lf[1])) - jnp.exp(jnp.sum(lf[2] * lf[3])) + lam_init).reshape(1, 1)
    p = {
        "ln_mix0": ln_mix[0:1], "ln_ffn0": ln_ffn[0:1], "ln_mix1": ln_mix[1:2], "ln_ffn1": ln_ffn[1:2],
        "ln_final": ln_final.reshape(1, d),
        "w_in0": w_in0[0].astype(BF16),
        "gmlp_ln_g": gmlp_ln_g[0].reshape(1, W_A), "gmlp_ln_b": gmlp_ln_b[0].reshape(1, W_A),
        "ws_prompt": ws_masked(gmlp_chunk), "ws_sample": ws_masked(ts),
        "bs_prompt": gmlp_bs[0][:, :gmlp_chunk].T, "bs_sample": gmlp_bs[0][:, :ts].T,
        "conv_k": conv_k[0], "w_out0": w_out0[0].astype(BF16),
        "ffn_wg": ffn_wg[0].astype(BF16), "ffn_wu": ffn_wu[0].astype(BF16), "ffn_wd": ffn_wd[0].astype(BF16),
        "ffn_fb": _largest_divisor(ffn_wg.shape[2], 1408, LANES),
        "w_in1": jnp.pad(w_in1[0], ((0, 0), (0, in1_pad))).astype(BF16),
        "w_out1": w_out1[0].astype(BF16),
        "router": jnp.pad(router[0], ((0, 0), (0, LANES - N_EXP))).astype(BF16),
        "exp_wg": exp_wg[0].astype(BF16), "exp_wu": exp_wu[0].astype(BF16), "exp_wd": exp_wd[0].astype(BF16),
        "exp_fb": _largest_divisor(exp_wg.shape[3], 896, LANES),
    }
    g_sub = subln_g[0].reshape(1, 2 * DH)
    plan1, out_defs1 = _in1_plan()

    xp = x_prompt.reshape(seq, d)
    xs = x_sample.reshape(bs_ * ts, d)
    xp, p_tail = _layer0(xp, None, gmlp_chunk, False, p)
    hist = jnp.pad(state_b_conv[0], ((0, 0), (SUBLANES - 2, 0), (0, 0)))
    xs, s_tail, s_av = _layer0(xs, hist, ts, True, p)
    p_b_conv = p_tail[-1, SUBLANES - 2:, :].reshape(1, 1, 2, W_B)
    s_b_conv = s_tail[:, SUBLANES - 2:, :].reshape(1, bs_, 2, W_B)
    s_a_v = s_av.reshape(1, bs_, ts, W_A)

    lb = KEY_BLOCK
    w1 = w_in1[0]
    off = np.concatenate([[0], np.cumsum(IN1_SIZES)])
    field = lambda k: w1[:, off[k]:off[k + 1]]
    padc = lambda a: jnp.pad(a, ((0, 0), (0, LANES - a.shape[1])))
    log2e = math.log2(math.e)
    qscale = DH ** -0.5 * log2e
    w_norm = jnp.concatenate([field(1), field(4), field(5), padc(field(7))], axis=1).astype(BF16)
    w_tran = jnp.concatenate([field(0) * qscale, field(3) * qscale, field(6), field(2), field(5),
                              field(1), field(4), padc(field(8))], axis=1).T.astype(BF16)
    (vd32, ki32, kc, kd, ki, qct, qdt, qit, vct, vdt, vc32t, kc32t, kd32t, wit) = _proj1_t(
        xp, p["ln_mix1"], w_norm, w_tran, PROMPT_TQ)
    token_major = lambda a, *dims: jnp.moveaxis(a.reshape(dims + (seq,)), -1, 0).reshape((1, 1, seq) + dims)
    p_c_k, p_c_v = token_major(kc32t, H_C, DH), token_major(vc32t, H_C, DH)
    p_d_k = token_major(kd32t, H_D, 2, DH)
    pos = jnp.arange(seq, dtype=I32)
    ve = ((pos // CHUNK + 1) * CHUNK).reshape(1, seq)
    mask = _select_t(qit, wit, ve, ki, min(TOPK_MAX, seq // 4))
    e_far = -(-(PROMPT_LB - 1 + MAX_DIST) // ATTN_TQ)
    kpos = np.arange(PROMPT_LB)
    qpos = np.arange(ATTN_TQ)
    base = e_far * ATTN_TQ
    rel_near = (rel_bias - rel_bias[N_BUCKETS // 2 - 1:N_BUCKETS // 2]) * log2e
    btiles = jnp.stack([jnp.transpose(_bias_tile(rel_near, base + e * ATTN_TQ + qpos, base + kpos, base + PROMPT_LB),
                                      (0, 2, 1)) for e in range(e_far)]).astype(BF16)
    op = _attend_t(qct, qdt, kc, kd, vct, vdt, mask, btiles, lam, subln_g[0].reshape(2 * DH, 1), lam_init)
    y_prompt = _layer1_tail(xp, op, p).reshape(1, seq, d)

    sr = _rms_proj(xs, p["ln_mix1"], p["w_in1"], plan1, out_defs1, _largest_divisor(bs_ * ts, 256, 16))
    sqc, skc32, skc, svc32, svc, sqd, skd32, skd, svd32, svd, sqi, ski32, ski, swi = sr
    nk = past + ts
    lps = -(-nk // lb) * lb

    kis = jnp.pad(jnp.concatenate([cache_idx_k[0].astype(BF16), ski.reshape(bs_, ts, D_I)], axis=1),
                  ((0, 0), (0, lps - nk), (0, 0)))
    ves = jnp.full((bs_ * ts, 1), nk, I32)
    smask = _select(sqi, swi, ves, kis, bs_, ts, ts, min(TOPK_MAX, nk // 4), lambda i: lps // lb)
    sq_pos = past + np.arange(ts)
    sbt = jnp.stack([_bias_tile(rel_bias, sq_pos, j * lb + np.arange(lb), nk) for j in range(lps // lb)])
    sbias = jnp.concatenate([sbt[:, :H_C], jnp.repeat(sbt[:, H_C:], 2, axis=1)], axis=1).reshape(lps // lb, N_MAPS * ts, lb)

    def block_diag(q):
        qb = jnp.transpose(q.reshape(H_C, bs_, ts, DH), (1, 0, 2, 3))
        eye = jnp.eye(H_C, dtype=q.dtype)
        return (qb[:, :, :, None, :] * eye[None, :, None, :, None]).reshape(bs_, H_C * ts, W_C)

    def cache_rows(c, feature_major):
        if not feature_major:
            return c
        return jnp.transpose(c.reshape(bs_, past, W_C), (0, 2, 1))

    def new_rows(a, feature_major):
        a = jnp.pad(a.reshape(bs_, ts, W_C), ((0, 0), (0, lb - ts), (0, 0)))
        return jnp.transpose(a, (0, 2, 1)) if feature_major else a.reshape(bs_, lb, H_D, 2 * DH)
    os_ = _attend_s(block_diag(sqc), block_diag(sqd),
                    [cache_rows(c[0], fm) for c, fm in ((cache_c_k, True), (cache_c_v, True), (cache_d_k, True), (cache_d_v, False))],
                    [new_rows(a, fm) for a, fm in ((skc32, True), (svc32, True), (skd32, True), (svd32, False))],
                    smask, sbias, lam, g_sub, lam_init)
    y_sample = _layer1_tail(xs, os_, p).reshape(bs_, ts, d)

    r5 = lambda a, n, t, *tail: a.reshape((1, n, t) + tail)
    return (y_prompt, y_sample, p_b_conv,
            p_c_k, p_c_v, r5(ki32, 1, seq, D_I), p_d_k, r5(vd32, 1, seq, H_D, 2 * DH),
            s_a_v, s_b_conv,
            r5(skc32, bs_, ts, H_C, DH), r5(svc32, bs_, ts, H_C, DH), r5(ski32, bs_, ts, D_I),
            r5(skd32, bs_, ts, H_D, 2, DH), r5(svd32, bs_, ts, H_D, 2 * DH))
```

```python
import functools
import math

import jax
import jax.numpy as jnp
import numpy as np
from jax import lax
from jax.experimental import pallas as pl
from jax.experimental.pallas import tpu as pltpu

F32 = jnp.float32
BF16 = jnp.bfloat16
I32 = jnp.int32

CHUNK = 64
EPS = 1e-6
NEG_INF = -1e30
H_A = 4
C_A = 128
W_A = H_A * C_A
W_B = 512
H_C = 8
DH = 64
W_C = H_C * DH
H_I = 8
D_I = 64
TOPK_MAX = 256
H_D = 4
W_D = H_D * 2 * DH
N_BUCKETS = 32
MAX_DIST = 128
N_EXP = 8
IN1_SIZES = (W_C, W_C, W_C, W_D, W_D, W_D, H_I * D_I, D_I, H_I)
N_HEADS_BIAS = H_C + H_D
N_MAPS = H_C + 2 * H_D

LANES = 128
SUBLANES = 8
VMEM_LIMIT = 56 * 1024 * 1024

INT_MIN = -(2 ** 31)
INT_MAX = 2 ** 31 - 1
ALL_TIES = 2 ** 30

KEY_BLOCK = 512


def _cparams(*sem):
    return pltpu.CompilerParams(dimension_semantics=sem, vmem_limit_bytes=VMEM_LIMIT)


def _largest_divisor(n, target, mult):
    if n <= target:
        return n
    d = (target // mult) * mult
    while d >= mult:
        if n % d == 0:
            return d
        d -= mult
    raise ValueError(f"no block of multiple {mult} divides {n}")


def _rms(x, g):
    return x * lax.rsqrt(jnp.mean(x * x, axis=-1, keepdims=True) + EPS) * g


def _dot(a, b):
    return jnp.dot(a, b, preferred_element_type=F32)


def _dot_nt(a, b):
    return lax.dot_general(a, b, (((1,), (1,)), ((), ())), preferred_element_type=F32)


def _rms_proj_body(x_ref, g_ref, w_ref, *out_refs, plan):
    xn = _rms(x_ref[...], g_ref[...]).astype(BF16)
    for (c0, c1), writes in plan:
        z = _dot(xn, w_ref[:, c0:c1])
        for o_idx, head, z0, z1, scale in writes:
            val = z[:, z0:z1]
            if scale != 1.0:
                val = val * scale
            ref = out_refs[o_idx]
            if head is None:
                ref[...] = val.astype(ref.dtype)
            else:
                ref[head] = val.astype(ref.dtype)


def _rms_proj(x, g, w, plan, out_defs, rows):
    n, d = x.shape
    grid = (n // rows,)
    out_shape, out_specs = [], []
    for width, dtype, heads in out_defs:
        if heads is None:
            out_shape.append(jax.ShapeDtypeStruct((n, width), dtype))
            out_specs.append(pl.BlockSpec((rows, width), lambda i: (i, 0)))
        else:
            out_shape.append(jax.ShapeDtypeStruct((heads, n, width), dtype))
            out_specs.append(pl.BlockSpec((heads, rows, width), lambda i: (0, i, 0)))
    return pl.pallas_call(
        functools.partial(_rms_proj_body, plan=plan),
        grid=grid,
        in_specs=[pl.BlockSpec((rows, d), lambda i: (i, 0)),
                  pl.BlockSpec((1, d), lambda i: (0, 0)),
                  pl.BlockSpec(w.shape, lambda i: (0, 0))],
        out_specs=out_specs,
        out_shape=out_shape,
        compiler_params=_cparams("parallel"),
        name="rms_proj",
    )(x, g, w)


def _mixer_ab_body(*refs, rb, from_prev, emit_v):
    it = iter(refs)
    x_ref, u_ref, v_ref, gb_ref, gc_ref, xin_ref = (next(it) for _ in range(6))
    if from_prev:
        gcp_ref, xinp_ref = next(it), next(it)
    else:
        hist_ref = next(it)
    lng_ref, lnb_ref, ws_ref, bs_ref, ck_ref, wout_ref = (next(it) for _ in range(6))
    x1_ref, tail_ref = next(it), next(it)
    vout_ref = next(it) if emit_v else None
    wext_ref = next(it)

    w = gc_ref[...] * xin_ref[...]
    if from_prev:
        hist = jnp.where(pl.program_id(0) > 0, gcp_ref[...] * xinp_ref[...], 0.0)
    else:
        hist = hist_ref[0]
    wext_ref[0:SUBLANES, :] = hist
    wext_ref[SUBLANES:, :] = w
    ck = ck_ref[...]
    conv = (ck[0:1] * wext_ref[SUBLANES - 2:SUBLANES - 2 + rb, :]
            + ck[1:2] * wext_ref[SUBLANES - 1:SUBLANES - 1 + rb, :]
            + ck[2:3] * w)
    y_b = gb_ref[...] * conv
    tail_ref[0] = w[rb - SUBLANES:, :]

    u = jax.nn.gelu(u_ref[...])
    v = jax.nn.gelu(v_ref[...])
    lng = lng_ref[...]
    lnb = lnb_ref[...]
    bs = bs_ref[...]
    acc = _dot(y_b.astype(BF16), wout_ref[W_A:, :])
    for h in range(H_A):
        sl = slice(h * C_A, (h + 1) * C_A)
        vh = v[:, sl]
        mu = jnp.mean(vh, axis=-1, keepdims=True)
        xc = vh - mu
        var = jnp.mean(xc * xc, axis=-1, keepdims=True)
        vln = xc * lax.rsqrt(var + EPS) * lng[:, sl] + lnb[:, sl]
        if emit_v:
            vout_ref[:, sl] = vln
        s = _dot(ws_ref[h], vln.astype(BF16)) + bs[:, h:h + 1]
        y_a = u[:, sl] * s
        acc = acc + _dot(y_a.astype(BF16), wout_ref[sl, :])
    x1_ref[...] = x_ref[...] + acc


def _mixer_ab(x, z, hist, lng, lnb, ws, bs, ck, wout, rb, emit_v):
    n, d = x.shape
    nb = n // rb
    from_prev = hist is None
    col = lambda c: pl.BlockSpec((rb, 512), lambda i, c=c: (i, c))
    in_specs = [pl.BlockSpec((rb, d), lambda i: (i, 0)), col(0), col(1), col(2), col(3), col(4)]
    args = [x, z, z, z, z, z]
    if from_prev:
        per = rb // SUBLANES
        prev = lambda c: pl.BlockSpec((SUBLANES, 512), lambda i, c=c: (jnp.maximum(i * per - 1, 0), c))
        in_specs += [prev(3), prev(4)]
        args += [z, z]
    else:
        in_specs += [pl.BlockSpec((1, SUBLANES, 512), lambda i: (i, 0, 0))]
        args += [hist]
    const = lambda a: pl.BlockSpec(a.shape, lambda i, nd=a.ndim: (0,) * nd)
    for a in (lng, lnb, ws, bs, ck, wout):
        in_specs.append(const(a))
        args.append(a)
    out_shape = [jax.ShapeDtypeStruct((n, d), F32), jax.ShapeDtypeStruct((nb, SUBLANES, 512), F32)]
    out_specs = [pl.BlockSpec((rb, d), lambda i: (i, 0)), pl.BlockSpec((1, SUBLANES, 512), lambda i: (i, 0, 0))]
    if emit_v:
        out_shape.append(jax.ShapeDtypeStruct((n, W_A), F32))
        out_specs.append(pl.BlockSpec((rb, W_A), lambda i: (i, 0)))
    return pl.pallas_call(
        functools.partial(_mixer_ab_body, rb=rb, from_prev=from_prev, emit_v=emit_v),
        grid=(nb,),
        in_specs=in_specs,
        out_specs=out_specs,
        out_shape=out_shape,
        scratch_shapes=[pltpu.VMEM((rb + SUBLANES, 512), F32)],
        compiler_params=_cparams("arbitrary"),
        name="mixer_ab",
    )(*args)


def _ffn_body(x_ref, g_ref, wg_ref, wu_ref, wd_ref, o_ref, xn_ref, acc_ref):
    k = pl.program_id(1)

    @pl.when(k == 0)
    def _():
        xn_ref[...] = _rms(x_ref[...], g_ref[...]).astype(BF16)
        acc_ref[...] = jnp.zeros_like(acc_ref)

    xn = xn_ref[...]
    h = jax.nn.silu(_dot(xn, wg_ref[...])) * _dot(xn, wu_ref[...])
    acc_ref[...] += _dot(h.astype(BF16), wd_ref[...])

    @pl.when(k == pl.num_programs(1) - 1)
    def _():
        o_ref[...] = x_ref[...] + acc_ref[...]


def _ffn(x, g, wg, wu, wd, rows, fb):
    n, d = x.shape
    dff = wg.shape[1]
    return pl.pallas_call(
        _ffn_body,
        grid=(n // rows, dff // fb),
        in_specs=[pl.BlockSpec((rows, d), lambda i, k: (i, 0)),
                  pl.BlockSpec((1, d), lambda i, k: (0, 0)),
                  pl.BlockSpec((d, fb), lambda i, k: (0, k)),
                  pl.BlockSpec((d, fb), lambda i, k: (0, k)),
                  pl.BlockSpec((fb, d), lambda i, k: (k, 0))],
        out_specs=pl.BlockSpec((rows, d), lambda i, k: (i, 0)),
        out_shape=jax.ShapeDtypeStruct((n, d), F32),
        scratch_shapes=[pltpu.VMEM((rows, d), BF16), pltpu.VMEM((rows, d), F32)],
        compiler_params=_cparams("parallel", "arbitrary"),
        name="ffn",
    )(x, g, wg, wu, wd)


def _select_body(qi_ref, wi_ref, ve_ref, ki_ref, out_ref, keys_ref, *, tq, lb, nkb, topk, nvalid_fn):
    nv = nvalid_fn(pl.program_id(1))
    wi = wi_ref[...]
    ve = ve_ref[...]
    q_all = jnp.concatenate([qi_ref[h] for h in range(H_I)], axis=0)
    wcols = [wi[:, h:h + 1] for h in range(H_I)]
    lane = lax.broadcasted_iota(I32, (tq, lb), 1)

    def score_block(b, carry):
        kb = ki_ref[0, pl.ds(pl.multiple_of(b * lb, lb), lb), :]
        dots = _dot_nt(q_all, kb)
        sc = jnp.zeros((tq, lb), F32)
        for h in range(H_I):
            sc = sc + wcols[h] * jnp.maximum(dots[h * tq:(h + 1) * tq, :], 0.0)
        bits = lax.bitcast_convert_type(sc, I32)
        key = bits ^ ((bits >> 31) & INT_MAX)
        key = jnp.where(sc == 0.0, 0, key)
        key = jnp.where(lane + b * lb < ve, key, INT_MIN)
        keys_ref[b] = key
        return carry

    lax.fori_loop(0, nv, score_block, 0)

    def count_ge(mid):
        midb = jnp.broadcast_to(mid, (tq, LANES))

        def body(b, acc):
            for c in range(lb // LANES):
                k = keys_ref[b, :, c * LANES:(c + 1) * LANES]
                acc = acc + jnp.where(k >= midb, 1, 0)
            return acc

        acc = lax.fori_loop(0, nv, body, jnp.zeros((tq, LANES), I32))
        return jnp.sum(acc.astype(F32), axis=-1, keepdims=True).astype(I32)

    def bisect(_, st):
        lo, hi, clo, chi = st
        mid = (lo >> 1) + (hi >> 1) + (lo & hi & 1)
        cnt = count_ge(mid)
        active = mid != lo
        up = jnp.logical_and(active, cnt >= topk)
        dn = jnp.logical_and(active, cnt < topk)
        return (jnp.where(up, mid, lo), jnp.where(dn, mid, hi),
                jnp.where(up, cnt, clo), jnp.where(dn, cnt, chi))

    full = lambda v: jnp.full((tq, 1), v, I32)
    lo, hi, clo, chi = lax.fori_loop(0, 32, bisect, (full(INT_MIN), full(INT_MAX), ve, full(0)))

    need = jnp.where(clo > topk, topk - chi, ALL_TIES)
    need = jnp.where(lo == INT_MIN, 0, need).astype(F32)
    lob = jnp.broadcast_to(lo, (tq, LANES))
    needb = jnp.broadcast_to(need, (tq, LANES))
    r = lax.broadcasted_iota(I32, (LANES, LANES), 0)
    c = lax.broadcasted_iota(I32, (LANES, LANES), 1)
    tri = jnp.where(r <= c, 1.0, 0.0).astype(BF16)

    def mask_block(b, seen):
        for cc in range(lb // LANES):
            k = keys_ref[b, :, cc * LANES:(cc + 1) * LANES]
            eq = jnp.where(k == lob, 1.0, 0.0)
            cum = _dot(eq.astype(BF16), tri)
            rank = seen + cum - eq
            take = jnp.where(rank < needb, eq, 0.0)
            sel = jnp.where(k > lob, 1.0, take)
            out_ref[0, b, :, cc * LANES:(cc + 1) * LANES] = jnp.where(sel > 0.5, 0.0, NEG_INF).astype(out_ref.dtype)
            seen = seen + cum[:, LANES - 1:LANES]
        return seen

    lax.fori_loop(0, nv, mask_block, jnp.zeros((tq, 1), F32))

    def fill_block(b, carry):
        out_ref[0, b] = jnp.full((tq, lb), NEG_INF, out_ref.dtype)
        return carry

    lax.fori_loop(nv, nkb, fill_block, 0)


def _select(qi, wi, ve, ki, nbatch, t, tq, topk, nvalid_fn):
    lp = ki.shape[1]
    lb = KEY_BLOCK
    nkb = lp // lb
    nq = t // tq
    row = lambda b, i: b * nq + i
    return pl.pallas_call(
        functools.partial(_select_body, tq=tq, lb=lb, nkb=nkb, topk=topk, nvalid_fn=nvalid_fn),
        grid=(nbatch, nq),
        in_specs=[pl.BlockSpec((H_I, tq, D_I), lambda b, i: (0, row(b, i), 0)),
                  pl.BlockSpec((tq, H_I), lambda b, i: (row(b, i), 0)),
                  pl.BlockSpec((tq, 1), lambda b, i: (row(b, i), 0)),
                  pl.BlockSpec((1, lp, D_I), lambda b, i: (b, 0, 0))],
        out_specs=pl.BlockSpec((1, nkb, tq, lb), lambda b, i: (b, 0, i, 0)),
        out_shape=jax.ShapeDtypeStruct((nbatch, nkb, t, lb), BF16),
        scratch_shapes=[pltpu.VMEM((nkb, tq, lb), I32)],
        compiler_params=_cparams("parallel", "arbitrary"),
        name="index_select",
    )(qi, wi, ve, ki)


PROMPT_TQ = 256
ATTN_TQ = 512
PROMPT_LB = 512
SEL_ROWS = 256
TIE_ROWS = 128
MAPS_PER_DOT = 4
PV_ROWS = 256
ONES_ROWS = 16


def _proj1_t_body(x_ref, g_ref, w_ref, wt_ref, vd32_ref, ki32_ref, kc_ref, kd_ref, ki_ref,
                  qct_ref, qdt_ref, qit_ref, vct_ref, vdt_ref, vc32t_ref, kc32t_ref, kd32t_ref, wit_ref):
    xn32 = _rms(x_ref[...], g_ref[...])
    xn = xn32.astype(BF16)
    xnt = xn32.T.astype(BF16)
    kc_ref[...] = _dot(xn, w_ref[:, 0:512]).astype(BF16)
    kd_ref[...] = _dot(xn, w_ref[:, 512:1024]).astype(BF16)
    vd32_ref[...] = _dot(xn, w_ref[:, 1024:1536])
    z = _dot(xn, w_ref[:, 1536:1536 + LANES])
    ki32_ref[...] = z[:, 0:D_I]
    ki_ref[...] = z[:, 0:D_I].astype(BF16)
    for c, (head_ref, full_ref) in enumerate(((qct_ref, None), (qdt_ref, None), (qit_ref, None), (vct_ref, vc32t_ref),
                                              (vdt_ref, None), (None, kc32t_ref), (None, kd32t_ref))):
        zt = _dot(wt_ref[c * 512:(c + 1) * 512, :], xnt)
        if full_ref is not None:
            full_ref[...] = zt
        if head_ref is not None:
            nh = head_ref.shape[0]
            w = 512 // nh
            for h in range(nh):
                head_ref[h, 0:w, :] = zt[h * w:(h + 1) * w, :].astype(BF16)
                if head_ref.shape[1] > w:
                    pad = lax.broadcasted_iota(I32, (head_ref.shape[1] - w, zt.shape[1]), 0)
                    head_ref[h, w:, :] = jnp.where(pad == 0, 1.0, 0.0).astype(BF16)
    zt = _dot(wt_ref[3584:3584 + LANES, :], xnt)
    wit_ref[...] = zt[0:H_I, :]


def _proj1_t(x, g, w, wt, rows):
    n, d = x.shape
    full = lambda width: (jax.ShapeDtypeStruct((n, width), F32), pl.BlockSpec((rows, width), lambda i: (i, 0)))
    packed = lambda width: (jax.ShapeDtypeStruct((n, width), BF16), pl.BlockSpec((rows, width), lambda i: (i, 0)))
    heads_t = lambda nh, width: (jax.ShapeDtypeStruct((nh, width, n), BF16),
                                 pl.BlockSpec((nh, width, rows), lambda i: (0, 0, i)))
    full_t = lambda: (jax.ShapeDtypeStruct((512, n), F32), pl.BlockSpec((512, rows), lambda i: (0, i)))
    outs = [full(512), full(D_I), packed(512), packed(512), packed(D_I),
            heads_t(8, DH), heads_t(8, DH), heads_t(8, DH), heads_t(8, DH + ONES_ROWS), heads_t(H_D, 2 * DH + ONES_ROWS),
            full_t(), full_t(), full_t(),
            (jax.ShapeDtypeStruct((H_I, n), F32), pl.BlockSpec((H_I, rows), lambda i: (0, i)))]
    return pl.pallas_call(
        _proj1_t_body,
        grid=(n // rows,),
        in_specs=[pl.BlockSpec((rows, d), lambda i: (i, 0)),
                  pl.BlockSpec((1, d), lambda i: (0, 0)),
                  pl.BlockSpec(w.shape, lambda i: (0, 0)),
                  pl.BlockSpec(wt.shape, lambda i: (0, 0))],
        out_specs=[o[1] for o in outs],
        out_shape=[o[0] for o in outs],
        compiler_params=_cparams("parallel"),
        name="proj1_t",
    )(x, g, w, wt)


def _sublane_all(x8, op):
    for shift in (4, 2, 1):
        x8 = op(x8, pltpu.roll(x8, shift, 0))
    return x8


def _select_t_body(qit_ref, wit_ref, ve_ref, ki_ref, out_ref, keys_ref, gmax_ref, *, tq, lp, topk):
    rb = SEL_ROWS
    nv = (pl.program_id(0) + 1) * (tq // rb)
    wit = wit_ref[...]
    ve = ve_ref[...]
    rows = lax.broadcasted_iota(I32, (rb, tq), 0)
    gmax_ref[...] = jnp.full((rb, tq), INT_MIN, I32)
    q_all = jnp.concatenate([qit_ref[h] for h in range(H_I)], axis=1)

    def score_chunks(r0, nch):
        dots = _dot(ki_ref[pl.ds(r0, nch * rb), :], q_all)
        for ch in range(nch):
            part = dots[ch * rb:(ch + 1) * rb, :]
            sc = jnp.zeros((rb, tq), F32)
            for h in range(H_I):
                sc = sc + wit[h:h + 1, :] * jnp.maximum(part[:, h * tq:(h + 1) * tq], 0.0)
            bits = lax.bitcast_convert_type(sc, I32)
            key = bits ^ ((bits >> 31) & INT_MAX)
            key = jnp.where(sc == 0.0, 0, key)
            key = jnp.where(rows + (r0 + ch * rb) < ve, key, INT_MIN)
            keys_ref[pl.ds(r0 + ch * rb, rb), :] = key
            gmax_ref[...] = jnp.maximum(gmax_ref[...], key)

    def score_pair(c, carry):
        score_chunks(pl.multiple_of(c * 2 * rb, 2 * rb), 2)
        return carry

    lax.fori_loop(0, nv // 2, score_pair, 0)
    pl.when(nv % 2 == 1)(lambda: score_chunks(pl.multiple_of((nv - 1) * rb, rb), 1))

    def count_ge(mid):
        midb = jnp.broadcast_to(mid, (SUBLANES, tq))

        def add_rows(r0, nrows, accs):
            kc = keys_ref[pl.ds(r0, nrows), :]
            accs = list(accs)
            for g in range(nrows // SUBLANES):
                a = g % len(accs)
                accs[a] = accs[a] + jnp.where(kc[g * SUBLANES:(g + 1) * SUBLANES, :] >= midb, 1, 0)
            return tuple(accs)

        accs = lax.fori_loop(0, nv // 2, lambda c, a: add_rows(pl.multiple_of(c * 2 * rb, 2 * rb), 2 * rb, a),
                             (jnp.zeros((SUBLANES, tq), I32),) * 4)
        accs = lax.cond(nv % 2 == 1, lambda a: add_rows(pl.multiple_of((nv - 1) * rb, rb), rb, a), lambda a: a, accs)
        return _sublane_all(accs[0] + accs[1] + accs[2] + accs[3], jnp.add)[0:1, :]

    floor_avg = lambda a, b: (a >> 1) + (b >> 1) + (a & b & 1)

    def pending(lo, hi, clo):
        open_ = jnp.logical_and(floor_avg(lo, hi) != lo, clo > topk)
        return jnp.max(jnp.where(open_, 1.0, 0.0))

    def bisect(st):
        lo, hi, clo, chi, _ = st
        mid = floor_avg(lo, hi)
        cnt = count_ge(mid)
        active = jnp.logical_and(mid != lo, clo > topk)
        up = jnp.logical_and(active, cnt >= topk)
        dn = jnp.logical_and(active, cnt < topk)
        lo, hi = jnp.where(up, mid, lo), jnp.where(dn, mid, hi)
        clo, chi = jnp.where(up, cnt, clo), jnp.where(dn, cnt, chi)
        return lo, hi, clo, chi, pending(lo, hi, clo)

    g8 = gmax_ref[0:SUBLANES, :]
    h8 = g8
    for g in range(1, rb // SUBLANES):
        blk = gmax_ref[g * SUBLANES:(g + 1) * SUBLANES, :]
        g8 = jnp.minimum(g8, blk)
        h8 = jnp.maximum(h8, blk)
    lo0 = _sublane_all(g8, jnp.minimum)[0:1, :]
    hi0 = _sublane_all(h8, jnp.maximum)[0:1, :] + 1
    clo0 = jnp.where(lo0 == INT_MIN, ve, count_ge(lo0))
    chi0 = jnp.zeros((1, tq), I32)

    def probe_zero():
        c_pos, c_nonneg = count_ge(jnp.full((1, tq), 1, I32)), count_ge(jnp.zeros((1, tq), I32))
        above = c_pos >= topk
        below = c_nonneg < topk
        lo1 = jnp.where(above, jnp.maximum(lo0, 1), jnp.where(below, lo0, 0))
        clo1 = jnp.where(above, jnp.where(lo0 >= 1, clo0, c_pos), jnp.where(below, clo0, c_nonneg))
        hi1 = jnp.where(above, hi0, jnp.where(below, jnp.minimum(hi0, 0), 1))
        chi1 = jnp.where(above, chi0, jnp.where(below, jnp.where(hi0 <= 0, chi0, c_nonneg), c_pos))
        return lo1, hi1, clo1, chi1

    lo1, hi1, clo1, chi1 = lax.cond(jnp.max(jnp.where(lo0 < 1, 1.0, 0.0)) > 0.5, probe_zero,
                                    lambda: (lo0, hi0, clo0, chi0))
    lo, hi, clo, chi, _ = lax.while_loop(
        lambda st: st[4] > 0.5, bisect, (lo1, hi1, clo1, chi1, pending(lo1, hi1, clo1)))

    need = jnp.where(clo > topk, topk - chi, ALL_TIES)
    need = jnp.where(lo == INT_MIN, 0, need).astype(F32)
    tr = TIE_ROWS
    lob = jnp.broadcast_to(lo, (tr, tq))
    needb = jnp.broadcast_to(need, (tr, tq))
    r = lax.broadcasted_iota(I32, (tr, tr), 0)
    c = lax.broadcasted_iota(I32, (tr, tr), 1)
    tri = jnp.where(c <= r, 1.0, 0.0).astype(BF16)

    def mask_chunk(cidx, seen):
        r0 = pl.multiple_of(cidx * tr, tr)
        k = keys_ref[pl.ds(r0, tr), :]
        eq = jnp.where(k == lob, 1.0, 0.0)
        cum = _dot(tri, eq.astype(BF16))
        take = jnp.where(seen + cum - eq < needb, eq, 0.0)
        sel = jnp.where(k > lob, 1.0, take)
        out_ref[pl.ds(r0, tr), :] = jnp.where(sel > 0.5, 0.0, NEG_INF).astype(out_ref.dtype)
        return seen + cum[tr - 1:tr, :]

    def plain_chunk(cidx, carry):
        r0 = pl.multiple_of(cidx * rb, rb)
        k = keys_ref[pl.ds(r0, rb), :]
        hit = jnp.where(k == INT_MIN, NEG_INF, 0.0)
        out_ref[pl.ds(r0, rb), :] = jnp.where(k >= lo, hit, NEG_INF).astype(out_ref.dtype)
        return carry

    any_tie = jnp.max(jnp.where(clo > topk, 1.0, 0.0)) > 0.5
    lax.cond(any_tie,
             lambda: lax.fori_loop(0, nv * (rb // tr), mask_chunk, jnp.zeros((1, tq), F32)),
             lambda: lax.fori_loop(0, nv, plain_chunk, jnp.zeros((1, tq), F32)))

    def fill_chunk(cidx, carry):
        out_ref[pl.ds(pl.multiple_of(cidx * rb, rb), rb), :] = jnp.full((rb, tq), NEG_INF, out_ref.dtype)
        return carry

    lax.fori_loop(nv, lp // rb, fill_chunk, 0)


def _select_t(qit, wit, ve, ki, topk):
    t = qit.shape[2]
    tq = PROMPT_TQ
    return pl.pallas_call(
        functools.partial(_select_t_body, tq=tq, lp=t, topk=topk),
        grid=(t // tq,),
        in_specs=[pl.BlockSpec((H_I, D_I, tq), lambda i: (0, 0, i)),
                  pl.BlockSpec((H_I, tq), lambda i: (0, i)),
                  pl.BlockSpec((1, tq), lambda i: (0, i)),
                  pl.BlockSpec((t, D_I), lambda i: (0, 0))],
        out_specs=pl.BlockSpec((t, tq), lambda i: (0, i)),
        out_shape=jax.ShapeDtypeStruct((t, t), BF16),
        scratch_shapes=[pltpu.VMEM((t, tq), I32), pltpu.VMEM((SEL_ROWS, tq), I32)],
        compiler_params=_cparams("parallel"),
        name="index_select_t",
    )(qit, wit, ve, ki)


def _attn_t_body(qi_ref, kj_ref, qct_ref, qdt_ref, kc_ref, kd_ref, vct_ref, vdt_ref, mask_ref, bt_ref, lam_ref,
                 g_ref, o_ref, m_ref, accc_ref, accd_ref, ot_ref, qbd_ref, s_ref, *, tq, per, e_far,
                 lam_init):
    i = qi_ref[pl.program_id(0)]
    j = kj_ref[pl.program_id(0)]
    e = i - j * per
    grp = MAPS_PER_DOT

    @pl.when(j == 0)
    def _():
        m_ref[...] = jnp.full(m_ref.shape, NEG_INF, F32)
        accc_ref[...] = jnp.zeros_like(accc_ref)
        accd_ref[...] = jnp.zeros_like(accd_ref)
        qbd_ref[...] = jnp.zeros_like(qbd_ref)
        for mp in range(N_MAPS):
            q = qct_ref[mp] if mp < H_C else qdt_ref[mp - H_C]
            a = mp % grp
            qbd_ref[mp // grp, a * DH:(a + 1) * DH, a * tq:(a + 1) * tq] = q

    def logits(g):
        k_ref = kc_ref if g < H_C // grp else kd_ref
        half = g % (H_C // grp)
        return _dot(k_ref[:, half * grp * DH:(half + 1) * grp * DH], qbd_ref[g])

    def step(near):
        sel = mask_ref[...].astype(F32)
        sel = jnp.concatenate([sel] * grp, axis=1)
        for g in range(N_MAPS // grp):
            s = logits(g) + sel if g < H_C // grp else logits(g)
            if near:
                heads = [g * grp + a if g < H_C // grp else H_C + (g * grp + a - H_C) // 2 for a in range(grp)]
                s = s + jnp.concatenate([bt_ref[0, h].astype(F32) for h in heads], axis=1)
            s_ref[g] = s
        for mp in range(N_MAPS):
            g, a = mp // grp, mp % grp
            cols = slice(a * tq, (a + 1) * tq)
            if mp < H_C:
                vt, acc_ref, a_idx = vct_ref[mp], accc_ref, mp
            else:
                vt, acc_ref, a_idx = vdt_ref[(mp - H_C) // 2], accd_ref, mp - H_C

            biased = lambda rows: s_ref[g, rows, cols]
            m_prev = m_ref[mp]
            m_new = jnp.maximum(m_prev, jnp.max(biased(slice(None)), axis=0, keepdims=True))
            alpha = jnp.exp2(m_prev - m_new)
            m_ref[mp] = m_new
            pv = None
            for r0 in range(0, s_ref.shape[1], PV_ROWS):
                rows = slice(r0, r0 + PV_ROWS)
                part = _dot(vt[:, rows], jnp.exp2(biased(rows) - m_new).astype(BF16))
                pv = part if pv is None else pv + part
            acc_ref[a_idx] = alpha * acc_ref[a_idx] + pv

    pl.when(e < e_far)(functools.partial(step, True))
    pl.when(e >= e_far)(functools.partial(step, False))

    @pl.when(j == i // per)
    def _():
        for h in range(H_C):
            ot_ref[h * DH:(h + 1) * DH, :] = accc_ref[h, 0:DH, :] / accc_ref[h, DH:DH + 1, :]
        lam = lam_ref[...]
        g = g_ref[...]
        dv = 2 * DH
        for h in range(H_D):
            a0 = accd_ref[2 * h, 0:dv, :] / accd_ref[2 * h, dv:dv + 1, :]
            a1 = accd_ref[2 * h + 1, 0:dv, :] / accd_ref[2 * h + 1, dv:dv + 1, :]
            od = a0 - lam * a1
            od = od * lax.rsqrt(jnp.mean(od * od, axis=0, keepdims=True) + EPS) * g * (1.0 - lam_init)
            ot_ref[W_C + h * 2 * DH:W_C + (h + 1) * 2 * DH, :] = od
        o_ref[...] = ot_ref[...].T.astype(o_ref.dtype)


def _attend_t(qct, qdt, kc, kd, vct, vdt, mask, btiles, lam, g, lam_init):
    t = qct.shape[2]
    tq, lb = ATTN_TQ, PROMPT_LB
    per = lb // tq
    e_far = btiles.shape[0]
    pairs = [(i, j) for i in range(t // tq) for j in range(i // per + 1)]
    qi_tab = jnp.asarray(np.array([p[0] for p in pairs], np.int32))
    kj_tab = jnp.asarray(np.array([p[1] for p in pairs], np.int32))
    qspec = pl.BlockSpec((H_C, DH, tq), lambda s, qi, kj: (0, 0, qi[s]))
    kspec = pl.BlockSpec((lb, H_C * DH), lambda s, qi, kj: (kj[s], 0))
    grid_spec = pltpu.PrefetchScalarGridSpec(
        num_scalar_prefetch=2,
        grid=(len(pairs),),
        in_specs=[qspec, qspec, kspec, kspec,
                  pl.BlockSpec((H_C, DH + ONES_ROWS, lb), lambda s, qi, kj: (0, 0, kj[s])),
                  pl.BlockSpec((H_D, 2 * DH + ONES_ROWS, lb), lambda s, qi, kj: (0, 0, kj[s])),
                  pl.BlockSpec((lb, tq), lambda s, qi, kj: (kj[s], qi[s])),
                  pl.BlockSpec((1, N_HEADS_BIAS, lb, tq),
                               lambda s, qi, kj: (jnp.minimum(qi[s] - kj[s] * per, e_far - 1), 0, 0, 0)),
                  pl.BlockSpec((1, 1), lambda s, qi, kj: (0, 0)),
                  pl.BlockSpec((2 * DH, 1), lambda s, qi, kj: (0, 0))],
        out_specs=pl.BlockSpec((tq, W_C + W_D), lambda s, qi, kj: (qi[s], 0)),
        scratch_shapes=[pltpu.VMEM((N_MAPS, 1, tq), F32),
                        pltpu.VMEM((H_C, DH + ONES_ROWS, tq), F32),
                        pltpu.VMEM((2 * H_D, 2 * DH + ONES_ROWS, tq), F32),
                        pltpu.VMEM((W_C + W_D, tq), F32),
                        pltpu.VMEM((N_MAPS // MAPS_PER_DOT, MAPS_PER_DOT * DH, MAPS_PER_DOT * tq), BF16),
                        pltpu.VMEM((N_MAPS // MAPS_PER_DOT, lb, MAPS_PER_DOT * tq), F32)])
    return pl.pallas_call(
        functools.partial(_attn_t_body, tq=tq, per=per, e_far=e_far, lam_init=lam_init),
        grid_spec=grid_spec,
        out_shape=jax.ShapeDtypeStruct((t, W_C + W_D), BF16),
        compiler_params=_cparams("arbitrary"),
        name="attend_t",
    )(qi_tab, kj_tab, qct, qdt, kc, kd, vct, vdt, mask, btiles, lam, g)


def _attn_s_body(qc_ref, qd_ref, ck_ref, cv_ref, dk_ref, dv_ref, nck_ref, ncv_ref, ndk_ref, ndv_ref,
                 mask_ref, bias_ref, lam_ref, g_ref, o_ref, m_ref, l_ref, acc_ref, *, ts, ncache, lam_init):
    j = pl.program_id(1)
    rows = H_C * ts

    @pl.when(j == 0)
    def _():
        m_ref[...] = jnp.full(m_ref.shape, NEG_INF, F32)
        l_ref[...] = jnp.zeros_like(l_ref)
        acc_ref[...] = jnp.zeros_like(acc_ref)

    def step(kc_t, vc_t, kd_t, vd):
        sel = jnp.tile(mask_ref[0, 0].astype(F32), (H_C, 1))
        bias = bias_ref[0]
        s_c = _dot(qc_ref[0], kc_t.astype(BF16)) + sel + bias[0:rows]
        s_d = _dot(qd_ref[0], kd_t.astype(BF16)) + bias[rows:2 * rows]
        for idx, (s, v) in enumerate(((s_c, vc_t), (s_d, vd))):
            m_prev = m_ref[idx]
            m_new = jnp.maximum(m_prev, jnp.max(s, axis=-1, keepdims=True))
            alpha = jnp.exp(m_prev - m_new)
            p = jnp.exp(s - m_new)
            l_ref[idx] = alpha * l_ref[idx] + jnp.sum(p, axis=-1, keepdims=True)
            m_ref[idx] = m_new
            p = p.astype(BF16)
            pv = _dot_nt(p, v.astype(BF16)) if idx == 0 else _dot(p, v.astype(BF16))
            acc_ref[idx] = alpha * acc_ref[idx] + pv

    heads_packed = lambda ref: jnp.concatenate([ref[0, :, h, :] for h in range(H_D)], axis=1)

    @pl.when(j < ncache)
    def _():
        step(ck_ref[0], cv_ref[0], dk_ref[0], heads_packed(dv_ref))

    @pl.when(j == ncache)
    def _():
        step(nck_ref[0], ncv_ref[0], ndk_ref[0], heads_packed(ndv_ref))
        for h in range(H_C):
            r = slice(h * ts, (h + 1) * ts)
            o_ref[:, h * DH:(h + 1) * DH] = (acc_ref[0, r, h * DH:(h + 1) * DH] / l_ref[0, r, :]).astype(o_ref.dtype)
        lam = lam_ref[...]
        g = g_ref[...]
        for h in range(H_D):
            r0 = slice(2 * h * ts, (2 * h + 1) * ts)
            r1 = slice((2 * h + 1) * ts, (2 * h + 2) * ts)
            c = slice(h * 2 * DH, (h + 1) * 2 * DH)
            od = acc_ref[1, r0, c] / l_ref[1, r0, :] - lam * (acc_ref[1, r1, c] / l_ref[1, r1, :])
            od = _rms(od, g) * (1.0 - lam_init)
            o_ref[:, W_C + h * 2 * DH:W_C + (h + 1) * 2 * DH] = od.astype(o_ref.dtype)


def _attend_s(qbd_c, qbd_d, caches, news, mask, bias, lam, g, lam_init):
    nb, rows, _ = qbd_c.shape
    ts = rows // H_C
    lb = KEY_BLOCK
    ncache = caches[3].shape[1] // lb
    qspec = pl.BlockSpec((1, rows, W_C), lambda b, j: (b, 0, 0))
    ctspec = pl.BlockSpec((1, W_C, lb), lambda b, j: (b, 0, jnp.minimum(j, ncache - 1)))
    cspec = pl.BlockSpec((1, lb, H_D, 2 * DH), lambda b, j: (b, jnp.minimum(j, ncache - 1), 0, 0))
    ntspec = pl.BlockSpec((1, W_C, lb), lambda b, j: (b, 0, 0))
    nspec = pl.BlockSpec((1, lb, H_D, 2 * DH), lambda b, j: (b, 0, 0, 0))
    return pl.pallas_call(
        functools.partial(_attn_s_body, ts=ts, ncache=ncache, lam_init=lam_init),
        grid=(nb, ncache + 1),
        in_specs=[qspec, qspec, ctspec, ctspec, ctspec, cspec, ntspec, ntspec, ntspec, nspec,
                  pl.BlockSpec((1, 1, ts, lb), lambda b, j: (b, j, 0, 0)),
                  pl.BlockSpec((1, 2 * rows, lb), lambda b, j: (j, 0, 0)),
                  pl.BlockSpec((1, 1), lambda b, j: (0, 0)),
                  pl.BlockSpec((1, 2 * DH), lambda b, j: (0, 0))],
        out_specs=pl.BlockSpec((ts, W_C + W_D), lambda b, j: (b, 0)),
        out_shape=jax.ShapeDtypeStruct((nb * ts, W_C + W_D), BF16),
        scratch_shapes=[pltpu.VMEM((2, rows, 1), F32), pltpu.VMEM((2, rows, 1), F32),
                        pltpu.VMEM((2, rows, W_C), F32)],
        compiler_params=_cparams("parallel", "arbitrary"),
        name="attend_s",
    )(qbd_c, qbd_d, *caches, *news, mask, bias, lam, g)


def _out_router_body(x_ref, o_ref, w_ref, g_ref, r_ref, x3_ref, xn_ref, gate_ref, gatet_ref):
    x3 = x_ref[...] + _dot(o_ref[...], w_ref[...])
    x3_ref[...] = x3
    xn = _rms(x3, g_ref[...]).astype(BF16)
    xn_ref[...] = xn
    logits = _dot(xn, r_ref[...])
    lane = lax.broadcasted_iota(I32, logits.shape, 1)
    logits = jnp.where(lane < N_EXP, logits, -jnp.inf)
    m1 = jnp.max(logits, axis=-1, keepdims=True)
    i1 = jnp.min(jnp.where(logits == m1, lane, LANES), axis=-1, keepdims=True)
    rest = jnp.where(lane == i1, -jnp.inf, logits)
    m2 = jnp.max(rest, axis=-1, keepdims=True)
    i2 = jnp.min(jnp.where(rest == m2, lane, LANES), axis=-1, keepdims=True)
    e = jnp.exp(m2 - m1)
    g1 = 1.0 / (1.0 + e)
    g2 = e / (1.0 + e)
    gate = jnp.where(lane == i1, g1, 0.0) + jnp.where(lane == i2, g2, 0.0)
    gate_ref[...] = gate
    gatet_ref[...] = gate.T[0:GATE_ROWS, :]


def _out_router(x, o, w, g, router, rows):
    n, d = x.shape
    return pl.pallas_call(
        _out_router_body,
        grid=(n // rows,),
        in_specs=[pl.BlockSpec((rows, d), lambda i: (i, 0)),
                  pl.BlockSpec((rows, d), lambda i: (i, 0)),
                  pl.BlockSpec(w.shape, lambda i: (0, 0)),
                  pl.BlockSpec((1, d), lambda i: (0, 0)),
                  pl.BlockSpec(router.shape, lambda i: (0, 0))],
        out_specs=[pl.BlockSpec((rows, d), lambda i: (i, 0)),
                   pl.BlockSpec((rows, d), lambda i: (i, 0)),
                   pl.BlockSpec((rows, LANES), lambda i: (i, 0)),
                   pl.BlockSpec((GATE_ROWS, rows), lambda i: (0, i))],
        out_shape=[jax.ShapeDtypeStruct((n, d), F32), jax.ShapeDtypeStruct((n, d), BF16),
                   jax.ShapeDtypeStruct((n, LANES), F32), jax.ShapeDtypeStruct((GATE_ROWS, n), F32)],
        compiler_params=_cparams("parallel"),
        name="out_router",
    )(x, o, w, g, router)


def _t5_bucket(rel):
    half = N_BUCKETS // 2
    max_exact = half // 2
    ret = np.where(rel > 0, half, 0)
    n = np.abs(rel)
    nf = np.maximum(n, 1).astype(np.float32)
    large = max_exact + (np.log(nf / np.float32(max_exact)) / np.float32(math.log(MAX_DIST / max_exact))
                         * np.float32(half - max_exact)).astype(np.int32)
    large = np.minimum(large, half - 1)
    return (ret + np.where(n < max_exact, n, large)).astype(np.int32)


def _bias_tile(rel_bias, q_pos, k_pos, k_real):
    rel = k_pos[None, :] - q_pos[:, None]
    onehot = (jnp.asarray(_t5_bucket(rel).astype(np.int8))[:, :, None]
              == jnp.arange(N_BUCKETS, dtype=jnp.int8)).astype(F32)
    bias = jnp.einsum("qkb,bh->hqk", onehot, rel_bias.astype(F32), precision=lax.Precision.HIGHEST)
    ok = np.logical_and(k_pos[None, :] // CHUNK <= q_pos[:, None] // CHUNK, k_pos[None, :] < k_real)
    return jnp.where(ok[None], bias, NEG_INF)


def _in1_plan():
    scale = DH ** -0.5
    heads64 = lambda o, sc=1.0: [(o, h, h * DH, (h + 1) * DH, sc) for h in range(8)]
    plan = [
        ((0, 512), heads64(0, scale)),
        ((512, 1024), [(1, None, 0, 512, 1.0)] + heads64(2)),
        ((1024, 1536), [(3, None, 0, 512, 1.0)] + heads64(4)),
        ((1536, 2048), heads64(5, scale)),
        ((2048, 2560), [(6, None, 0, 512, 1.0)] + heads64(7)),
        ((2560, 3072), [(8, None, 0, 512, 1.0)] + [(9, h, h * 128, (h + 1) * 128, 1.0) for h in range(H_D)]),
        ((3072, 3584), heads64(10)),
        ((3584, 3712), [(11, None, 0, D_I, 1.0), (12, None, 0, D_I, 1.0), (13, None, D_I, D_I + H_I, 1.0)]),
    ]
    out_defs = [(DH, BF16, 8), (512, F32, None), (DH, BF16, 8), (512, F32, None), (DH, BF16, 8),
                (DH, BF16, 8), (512, F32, None), (DH, BF16, 8), (512, F32, None), (2 * DH, BF16, H_D),
                (D_I, BF16, 8), (D_I, F32, None), (D_I, BF16, None), (H_I, F32, None)]
    return plan, out_defs


def _layer0(x, hist, rb, emit_v, p):
    n = x.shape[0]
    rows = _largest_divisor(n, 512, 16)
    plan = [((0, p["w_in0"].shape[1]), [(0, None, 0, p["w_in0"].shape[1], 1.0)])]
    (z,) = _rms_proj(x, p["ln_mix0"], p["w_in0"], plan, [(p["w_in0"].shape[1], F32, None)], rows)
    ws = p["ws_prompt"] if hist is None else p["ws_sample"]
    bs = p["bs_prompt"] if hist is None else p["bs_sample"]
    outs = _mixer_ab(x, z, hist, p["gmlp_ln_g"], p["gmlp_ln_b"], ws, bs, p["conv_k"], p["w_out0"], rb, emit_v)
    x1 = outs[0]
    x2 = _ffn(x1, p["ln_ffn0"], p["ffn_wg"], p["ffn_wu"], p["ffn_wd"], rows, p["ffn_fb"])
    return (x2,) + tuple(outs[1:])


GATE_ROWS = 16
MOE_ROWS = 128


def _moe_routed_body(x_ref, xn_ref, gate_ref, gatet_ref, gf_ref, wg_ref, wu_ref, wd_ref, y_ref,
                     triu_ref, tril_ref, crow_ref, ccol_ref, xs_ref, ge_ref, acc_ref, yblk_ref, nsub_ref,
                     *, tb, sub):
    e = pl.program_id(1)
    k = pl.program_id(2)
    first_k = k == 0
    last_k = k == pl.num_programs(2) - 1

    @pl.when(jnp.logical_and(e == 0, first_k))
    def _():
        r = lax.broadcasted_iota(I32, (tb, tb), 0)
        c = lax.broadcasted_iota(I32, (tb, tb), 1)
        triu_ref[...] = jnp.where(r < c, 1.0, 0.0).astype(BF16)
        tril_ref[...] = jnp.where(c < r, 1.0, 0.0).astype(BF16)
        crow_ref[...] = _dot(jnp.where(gatet_ref[...] > 0.0, 1.0, 0.0).astype(BF16), triu_ref[...])
        ccol_ref[...] = _dot(tril_ref[...], jnp.where(gate_ref[...] > 0.0, 1.0, 0.0).astype(BF16))
        yblk_ref[...] = jnp.zeros_like(yblk_ref)
        acc_ref[...] = jnp.zeros_like(acc_ref)

    @pl.when(first_k)
    def _():
        g_e = gatet_ref[pl.ds(e, 1), :]
        m_e = g_e > 0.0
        c_e = crow_ref[pl.ds(e, 1), :]
        nsub = (jnp.sum(jnp.where(m_e, 1.0, 0.0)).astype(I32) + sub - 1) // sub
        nsub_ref[0] = nsub
        xn = xn_ref[...]

        def pack(r0, n):
            slot = lax.broadcasted_iota(I32, (n, tb), 0).astype(F32)
            hit = jnp.logical_and(m_e, c_e == slot + r0.astype(F32))
            onehot = jnp.where(hit, 1.0, 0.0)
            xs_ref[pl.ds(r0, n), :] = _dot(onehot.astype(BF16), xn).astype(BF16)
            ge_ref[pl.ds(r0, n), :] = jnp.sum(onehot * g_e, axis=-1, keepdims=True)
            acc_ref[pl.ds(r0, n), :] = jnp.zeros((n, acc_ref.shape[1]), F32)

        def pack_pair(s, carry):
            pack(pl.multiple_of(s * 2 * sub, 2 * sub), 2 * sub)
            return carry

        lax.fori_loop(0, nsub // 2, pack_pair, 0)
        pl.when(nsub % 2 == 1)(lambda: pack(pl.multiple_of((nsub - 1) * sub, sub), sub))

    nsub = nsub_ref[0]

    def experts(rows):
        xs = xs_ref[rows, :]
        h = jax.nn.silu(_dot(xs, wg_ref[0])) * _dot(xs, wu_ref[0])
        acc_ref[rows, :] += _dot((ge_ref[rows, :] * h).astype(BF16), wd_ref[0])

    def pair(s, carry):
        experts(pl.ds(pl.multiple_of(s * 2 * sub, 2 * sub), 2 * sub))
        return carry

    lax.fori_loop(0, nsub // 2, pair, 0)
    pl.when(nsub % 2 == 1)(lambda: experts(pl.ds(pl.multiple_of((nsub - 1) * sub, sub), sub)))

    @pl.when(last_k)
    def _():
        lane = lax.broadcasted_iota(I32, (tb, LANES), 1)
        pick = lambda a: jnp.sum(jnp.where(lane == e, a, 0.0), axis=-1, keepdims=True)
        m_e = pick(gate_ref[...]) > 0.0
        c_e = pick(ccol_ref[...])

        def unpack(r0, n):
            slot = lax.broadcasted_iota(I32, (tb, n), 1).astype(F32)
            hit = jnp.logical_and(m_e, c_e == slot + r0.astype(F32))
            onehot = jnp.where(hit, 1.0, 0.0).astype(BF16)
            y = acc_ref[pl.ds(r0, n), :]
            hi = y.astype(BF16)
            lo = (y - hi.astype(F32)).astype(BF16)
            yblk_ref[...] += _dot(jnp.concatenate([onehot, onehot], axis=1), jnp.concatenate([hi, lo], axis=0))

        def unpack_pair(s, carry):
            unpack(pl.multiple_of(s * 2 * sub, 2 * sub), 2 * sub)
            return carry

        lax.fori_loop(0, nsub // 2, unpack_pair, 0)
        pl.when(nsub % 2 == 1)(lambda: unpack(pl.multiple_of((nsub - 1) * sub, sub), sub))

    @pl.when(jnp.logical_and(e == pl.num_programs(1) - 1, last_k))
    def _():
        y_ref[...] = _rms(x_ref[...] + yblk_ref[...], gf_ref[...])


def _moe_routed(x, xn, gate, gatet, gf, wg, wu, wd, tb, fb):
    n, d = x.shape
    ne, _, dff = wg.shape
    return pl.pallas_call(
        functools.partial(_moe_routed_body, tb=tb, sub=min(MOE_ROWS, tb)),
        grid=(n // tb, ne, dff // fb),
        in_specs=[pl.BlockSpec((tb, d), lambda i, e, k: (i, 0)),
                  pl.BlockSpec((tb, d), lambda i, e, k: (i, 0)),
                  pl.BlockSpec((tb, LANES), lambda i, e, k: (i, 0)),
                  pl.BlockSpec((GATE_ROWS, tb), lambda i, e, k: (0, i)),
                  pl.BlockSpec((1, d), lambda i, e, k: (0, 0)),
                  pl.BlockSpec((1, d, fb), lambda i, e, k: (e, 0, k)),
                  pl.BlockSpec((1, d, fb), lambda i, e, k: (e, 0, k)),
                  pl.BlockSpec((1, fb, d), lambda i, e, k: (e, k, 0))],
        out_specs=pl.BlockSpec((tb, d), lambda i, e, k: (i, 0)),
        out_shape=jax.ShapeDtypeStruct((n, d), F32),
        scratch_shapes=[pltpu.VMEM((tb, tb), BF16), pltpu.VMEM((tb, tb), BF16),
                        pltpu.VMEM((GATE_ROWS, tb), F32), pltpu.VMEM((tb, LANES), F32),
                        pltpu.VMEM((tb, d), BF16), pltpu.VMEM((tb, 1), F32),
                        pltpu.VMEM((tb, d), F32), pltpu.VMEM((tb, d), F32),
                        pltpu.SMEM((1,), I32)],
        compiler_params=_cparams("parallel", "arbitrary", "arbitrary"),
        name="moe_routed",
    )(x, xn, gate, gatet, gf, wg, wu, wd)


def _layer1_tail(x, o, p):
    n = x.shape[0]
    rows = _largest_divisor(n, 512, 16)
    x3, xn, gate, gatet = _out_router(x, o, p["w_out1"], p["ln_ffn1"], p["router"], rows)
    return _moe_routed(x3, xn, gate, gatet, p["ln_final"], p["exp_wg"], p["exp_wu"], p["exp_wd"],
                       _largest_divisor(n, 1024, 128), p["exp_fb"])


def kernel(x_prompt, x_sample, state_b_conv, cache_c_k, cache_c_v, cache_idx_k, cache_d_k, cache_d_v, rel_bias, ln_mix, ln_ffn, ln_final, w_in0, gmlp_ln_g, gmlp_ln_b, gmlp_ws, gmlp_bs, conv_k, w_out0, ffn_wg, ffn_wu, ffn_wd, w_in1, lam_qk, subln_g, w_out1, router, exp_wg, exp_wu, exp_wd):
    bp, seq, d = x_prompt.shape
    bs_, ts, _ = x_sample.shape
    past = cache_c_k.shape[2]
    assert bp == 1 and ln_mix.shape[0] == 2 and seq % PROMPT_LB == 0 and ts % SUBLANES == 0 and ts <= CHUNK
    assert past % KEY_BLOCK == 0
    gmlp_chunk = gmlp_ws.shape[-1]
    lam_init = 0.8 - 0.6 * math.exp(-0.3 * 1)

    def ws_masked(rows):
        r = jnp.arange(rows)
        ok = (r[None, :] // CHUNK) <= (r[:, None] // CHUNK)
        return jnp.where(ok[None], gmlp_ws[0][:, :rows, :rows], 0.0).astype(BF16)

    in1_pad = (-w_in1.shape[2]) % LANES
    lf = lam_qk[0].astype(F32)
    lam = (jnp.exp(jnp.sum(lf[0] * lf[1])) - jnp.exp(jnp.sum(lf[2] * lf[3])) + lam_init).reshape(1, 1)
    p = {
        "ln_mix0": ln_mix[0:1], "ln_ffn0": ln_ffn[0:1], "ln_mix1": ln_mix[1:2], "ln_ffn1": ln_ffn[1:2],
        "ln_final": ln_final.reshape(1, d),
        "w_in0": w_in0[0].astype(BF16),
        "gmlp_ln_g": gmlp_ln_g[0].reshape(1, W_A), "gmlp_ln_b": gmlp_ln_b[0].reshape(1, W_A),
        "ws_prompt": ws_masked(gmlp_chunk), "ws_sample": ws_masked(ts),
        "bs_prompt": gmlp_bs[0][:, :gmlp_chunk].T, "bs_sample": gmlp_bs[0][:, :ts].T,
        "conv_k": conv_k[0], "w_out0": w_out0[0].astype(BF16),
        "ffn_wg": ffn_wg[0].astype(BF16), "ffn_wu": ffn_wu[0].astype(BF16), "ffn_wd": ffn_wd[0].astype(BF16),
        "ffn_fb": _largest_divisor(ffn_wg.shape[2], 1408, LANES),
        "w_in1": jnp.pad(w_in1[0], ((0, 0), (0, in1_pad))).astype(BF16),
        "w_out1": w_out1[0].astype(BF16),
        "router": jnp.pad(router[0], ((0, 0), (0, LANES - N_EXP))).astype(BF16),
        "exp_wg": exp_wg[0].astype(BF16), "exp_wu": exp_wu[0].astype(BF16), "exp_wd": exp_wd[0].astype(BF16),
        "exp_fb": _largest_divisor(exp_wg.shape[3], 896, LANES),
    }
    g_sub = subln_g[0].reshape(1, 2 * DH)
    plan1, out_defs1 = _in1_plan()

    xp = x_prompt.reshape(seq, d)
    xs = x_sample.reshape(bs_ * ts, d)
    xp, p_tail = _layer0(xp, None, gmlp_chunk, False, p)
    hist = jnp.pad(state_b_conv[0], ((0, 0), (SUBLANES - 2, 0), (0, 0)))
    xs, s_tail, s_av = _layer0(xs, hist, ts, True, p)
    p_b_conv = p_tail[-1, SUBLANES - 2:, :].reshape(1, 1, 2, W_B)
    s_b_conv = s_tail[:, SUBLANES - 2:, :].reshape(1, bs_, 2, W_B)
    s_a_v = s_av.reshape(1, bs_, ts, W_A)

    lb = KEY_BLOCK
    w1 = w_in1[0]
    off = np.concatenate([[0], np.cumsum(IN1_SIZES)])
    field = lambda k: w1[:, off[k]:off[k + 1]]
    padc = lambda a: jnp.pad(a, ((0, 0), (0, LANES - a.shape[1])))
    log2e = math.log2(math.e)
    qscale = DH ** -0.5 * log2e
    w_norm = jnp.concatenate([field(1), field(4), field(5), padc(field(7))], axis=1).astype(BF16)
    w_tran = jnp.concatenate([field(0) * qscale, field(3) * qscale, field(6), field(2), field(5),
                              field(1), field(4), padc(field(8))], axis=1).T.astype(BF16)
    (vd32, ki32, kc, kd, ki, qct, qdt, qit, vct, vdt, vc32t, kc32t, kd32t, wit) = _proj1_t(
        xp, p["ln_mix1"], w_norm, w_tran, PROMPT_TQ)
    token_major = lambda a, *dims: jnp.moveaxis(a.reshape(dims + (seq,)), -1, 0).reshape((1, 1, seq) + dims)
    p_c_k, p_c_v = token_major(kc32t, H_C, DH), token_major(vc32t, H_C, DH)
    p_d_k = token_major(kd32t, H_D, 2, DH)
    pos = jnp.arange(seq, dtype=I32)
    ve = ((pos // CHUNK + 1) * CHUNK).reshape(1, seq)
    mask = _select_t(qit, wit, ve, ki, min(TOPK_MAX, seq // 4))
    e_far = -(-(PROMPT_LB - 1 + MAX_DIST) // ATTN_TQ)
    kpos = np.arange(PROMPT_LB)
    qpos = np.arange(ATTN_TQ)
    base = e_far * ATTN_TQ
    rel_near = (rel_bias - rel_bias[N_BUCKETS // 2 - 1:N_BUCKETS // 2]) * log2e
    btiles = jnp.stack([jnp.transpose(_bias_tile(rel_near, base + e * ATTN_TQ + qpos, base + kpos, base + PROMPT_LB),
                                      (0, 2, 1)) for e in range(e_far)]).astype(BF16)
    op = _attend_t(qct, qdt, kc, kd, vct, vdt, mask, btiles, lam, subln_g[0].reshape(2 * DH, 1), lam_init)
    y_prompt = _layer1_tail(xp, op, p).reshape(1, seq, d)

    sr = _rms_proj(xs, p["ln_mix1"], p["w_in1"], plan1, out_defs1, _largest_divisor(bs_ * ts, 256, 16))
    sqc, skc32, skc, svc32, svc, sqd, skd32, skd, svd32, svd, sqi, ski32, ski, swi = sr
    nk = past + ts
    lps = -(-nk // lb) * lb

    kis = jnp.pad(jnp.concatenate([cache_idx_k[0].astype(BF16), ski.reshape(bs_, ts, D_I)], axis=1),
                  ((0, 0), (0, lps - nk), (0, 0)))
    ves = jnp.full((bs_ * ts, 1), nk, I32)
    smask = _select(sqi, swi, ves, kis, bs_, ts, ts, min(TOPK_MAX, nk // 4), lambda i: lps // lb)
    sq_pos = past + np.arange(ts)
    sbt = jnp.stack([_bias_tile(rel_bias, sq_pos, j * lb + np.arange(lb), nk) for j in range(lps // lb)])
    sbias = jnp.concatenate([sbt[:, :H_C], jnp.repeat(sbt[:, H_C:], 2, axis=1)], axis=1).reshape(lps // lb, N_MAPS * ts, lb)

    def block_diag(q):
        qb = jnp.transpose(q.reshape(H_C, bs_, ts, DH), (1, 0, 2, 3))
        eye = jnp.eye(H_C, dtype=q.dtype)
        return (qb[:, :, :, None, :] * eye[None, :, None, :, None]).reshape(bs_, H_C * ts, W_C)

    def cache_rows(c, feature_major):
        if not feature_major:
            return c
        return jnp.transpose(c.reshape(bs_, past, W_C), (0, 2, 1))

    def new_rows(a, feature_major):
        a = jnp.pad(a.reshape(bs_, ts, W_C), ((0, 0), (0, lb - ts), (0, 0)))
        return jnp.transpose(a, (0, 2, 1)) if feature_major else a.reshape(bs_, lb, H_D, 2 * DH)
    os_ = _attend_s(block_diag(sqc), block_diag(sqd),
                    [cache_rows(c[0], fm) for c, fm in ((cache_c_k, True), (cache_c_v, True), (cache_d_k, True), (cache_d_v, False))],
                    [new_rows(a, fm) for a, fm in ((skc32, True), (svc32, True), (skd32, True), (svd32, False))],
                    smask, sbias, lam, g_sub, lam_init)
    y_sample = _layer1_tail(xs, os_, p).reshape(bs_, ts, d)

    r5 = lambda a, n, t, *tail: a.reshape((1, n, t) + tail)
    return (y_prompt, y_sample, p_b_conv,
            p_c_k, p_c_v, r5(ki32, 1, seq, D_I), p_d_k, r5(vd32, 1, seq, H_D, 2 * DH),
            s_a_v, s_b_conv,
            r5(skc32, bs_, ts, H_C, DH), r5(svc32, bs_, ts, H_C, DH), r5(ski32, bs_, ts, D_I),
            r5(skd32, bs_, ts, H_D, 2, DH), r5(svd32, bs_, ts, H_D, 2 * DH))
```
